```python
import jax, jax.numpy as jnp
from jax import lax
import numpy as np

D_MODEL = 2048
BATCH = 2
SEQ = 4096
DEPTH = 1

ROPE_THETA = 500000.0
EPS = 1e-6
A_GROUPS = 8
A_GROUP_DIM = 128
A_WIDTH = A_GROUPS * A_GROUP_DIM
CHUNK = 128
B_HEADS = 8
B_KV_HEADS = 2
B_HEAD_DIM = 128
B_WIDTH = B_HEADS * B_HEAD_DIM
IDX_HEADS = 16
IDX_DIM = 64
TOPK_MAX = 256
Q_BLOCK = 128
M_HEADS = 4
M_HEAD_DIM = 256
M_WIDTH = M_HEADS * M_HEAD_DIM
MEM_LEN = 256

SPLIT_SIZES = (
    A_WIDTH, A_WIDTH, A_WIDTH,
    B_WIDTH, B_KV_HEADS * B_HEAD_DIM, B_KV_HEADS * B_HEAD_DIM,
    B_WIDTH,
    IDX_HEADS * IDX_DIM, IDX_DIM, IDX_HEADS,
    M_WIDTH, M_WIDTH,
    D_MODEL, D_MODEL, D_MODEL,
)
D_IN = sum(SPLIT_SIZES)
SPLIT_OFFSETS = tuple(int(o) for o in np.cumsum(SPLIT_SIZES)[:-1])

kernel_name = 'hybrid_gmlp_dsa_memxattn_gated_block'


def rms_norm(x, g):
    xf = x.astype(jnp.float32)
    y = xf * lax.rsqrt(jnp.mean(xf * xf, axis=-1, keepdims=True) + EPS)
    return (y * g.astype(jnp.float32)).astype(x.dtype)


def layer_norm(x, g, b):
    xf = x.astype(jnp.float32)
    mu = jnp.mean(xf, axis=-1, keepdims=True)
    var = jnp.mean(jnp.square(xf - mu), axis=-1, keepdims=True)
    y = (xf - mu) * lax.rsqrt(var + EPS)
    return (y * g.astype(jnp.float32) + b.astype(jnp.float32)).astype(x.dtype)


def partial_rope(x, pos):
    rot = x.shape[-1] // 4
    half = rot // 2
    inv_freq = ROPE_THETA ** (-jnp.arange(half, dtype=jnp.float32) / half)
    ang = pos.astype(jnp.float32)[..., None] * inv_freq
    cos = jnp.cos(ang)[:, :, None, :]
    sin = jnp.sin(ang)[:, :, None, :]
    x1 = x[..., :half].astype(jnp.float32)
    x2 = x[..., half:rot].astype(jnp.float32)
    r1 = (x1 * cos - x2 * sin).astype(x.dtype)
    r2 = (x2 * cos + x1 * sin).astype(x.dtype)
    return jnp.concatenate([r1, r2, x[..., rot:]], axis=-1)


def gmlp_branch(a_u, a_v, a_z, ln_g, ln_b, spatial_w, spatial_b):
    bsz, s, _ = a_u.shape
    u = jax.nn.gelu(a_u)
    v = layer_norm(jax.nn.gelu(a_v), ln_g, ln_b)
    vr = v.reshape(bsz, s // CHUNK, CHUNK, A_GROUPS, A_GROUP_DIM)
    mask = jnp.tril(jnp.ones((CHUNK, CHUNK), dtype=bool))
    ws = jnp.where(mask[None], spatial_w, jnp.zeros_like(spatial_w))
    sg = jnp.einsum('gts,bcsgd->bctgd', ws, vr) + spatial_b.T[None, None, :, :, None]
    return u * sg.reshape(bsz, s, A_WIDTH) * jax.nn.silu(a_z)


def dsa_branch(b_q, b_k, b_v, b_z, i_q, i_k, i_w, positions,
               q_norm_gain, k_norm_gain, idx_k_ln_gain, idx_k_ln_bias):
    bsz, s, _ = b_q.shape
    grp = B_HEADS // B_KV_HEADS
    q = partial_rope(rms_norm(b_q.reshape(bsz, s, B_HEADS, B_HEAD_DIM), q_norm_gain), positions)
    k = partial_rope(rms_norm(b_k.reshape(bsz, s, B_KV_HEADS, B_HEAD_DIM), k_norm_gain), positions)
    v = b_v.reshape(bsz, s, B_KV_HEADS, B_HEAD_DIM)
    kv = jnp.stack([k, v], axis=2)
    iq = partial_rope(i_q.reshape(bsz, s, IDX_HEADS, IDX_DIM), positions)
    ik = partial_rope(layer_norm(i_k, idx_k_ln_gain, idx_k_ln_bias)[:, :, None, :], positions)[:, :, 0]
    iw = i_w * (IDX_HEADS ** -0.5)
    topk = min(TOPK_MAX, s // 4)
    nb = s // Q_BLOCK
    key_pos = jnp.arange(s)

    def to_blocks(t):
        return t.reshape((bsz, nb, Q_BLOCK) + t.shape[2:]).swapaxes(0, 1)

    def block(args):
        qb, iqb, iwb, bid = args
        t_idx = bid * Q_BLOCK + jnp.arange(Q_BLOCK)
        rel = jax.nn.relu(jnp.einsum('bthd,bsd->bths', iqb, ik).astype(jnp.float32) * (IDX_DIM ** -0.5))
        score = jnp.einsum('bth,bths->bts', iwb.astype(jnp.float32), rel)
        causal = key_pos[None, :] <= t_idx[:, None]
        score = jnp.where(causal[None], score, -jnp.inf)
        _, sel = lax.top_k(score, topk)
        valid = sel <= t_idx[None, :, None]
        kv_sel = jax.vmap(lambda kvb, ib: kvb[ib])(kv, sel)
        k_sel = kv_sel[:, :, :, 0]
        v_sel = kv_sel[:, :, :, 1]
        qg = qb.reshape(bsz, Q_BLOCK, B_KV_HEADS, grp, B_HEAD_DIM)
        logits = jnp.einsum('btngd,btsnd->btngs', qg, k_sel).astype(jnp.float32) * (B_HEAD_DIM ** -0.5)
        logits = jnp.where(valid[:, :, None, None, :], logits, -jnp.inf)
        p = jax.nn.softmax(logits, axis=-1).astype(v_sel.dtype)
        o = jnp.einsum('btngs,btsnd->btngd', p, v_sel)
        return o.reshape(bsz, Q_BLOCK, B_WIDTH)

    out = lax.map(block, (to_blocks(q), to_blocks(iq), to_blocks(iw), jnp.arange(nb)))
    out = out.swapaxes(0, 1).reshape(bsz, s, B_WIDTH)
    return out * jax.nn.silu(b_z)


def memory_branch(m_q, m_z, mem, mem_norm_gain, w_mem_kv, mem_q_norm_gain, mem_k_norm_gain):
    bsz, s, _ = m_q.shape
    qm = rms_norm(m_q.reshape(bsz, s, M_HEADS, M_HEAD_DIM), mem_q_norm_gain)
    memn = rms_norm(mem, mem_norm_gain)
    kvm = (memn @ w_mem_kv).reshape(bsz, mem.shape[1], 2, M_HEADS, M_HEAD_DIM)
    km = rms_norm(kvm[:, :, 0], mem_k_norm_gain)
    vm = kvm[:, :, 1]
    logits = jnp.einsum('bshd,bmhd->bhsm', qm, km).astype(jnp.float32) * (M_HEAD_DIM ** -0.5)
    p = jax.nn.softmax(logits, axis=-1).astype(vm.dtype)
    o = jnp.einsum('bhsm,bmhd->bshd', p, vm).reshape(bsz, s, M_WIDTH)
    return o * jax.nn.silu(m_z)


def setup_inputs(seed: int = 0) -> dict:
    key = jax.random.key(seed)
    ks = jax.random.split(key, 24)
    f32 = jnp.float32
    nrm = lambda k, shape, scale: jax.random.normal(k, shape, f32) * scale
    L = DEPTH
    return {
        'x': jax.random.normal(ks[0], (BATCH, SEQ, D_MODEL), f32),
        'mem': jax.random.normal(ks[1], (BATCH, MEM_LEN, D_MODEL), f32),
        'positions': jnp.arange(SEQ, dtype=jnp.int32)[None, :]
                     + jax.random.randint(ks[2], (BATCH, 1), 0, 1024, dtype=jnp.int32),
        'norm_gain': 1.0 + nrm(ks[3], (L, D_MODEL), 0.01),
        'w_in': nrm(ks[4], (L, D_MODEL, D_IN), D_MODEL ** -0.5),
        'gmlp_ln_gain': 1.0 + nrm(ks[5], (L, A_WIDTH), 0.01),
        'gmlp_ln_bias': nrm(ks[6], (L, A_WIDTH), 0.01),
        'spatial_w': nrm(ks[7], (L, A_GROUPS, CHUNK, CHUNK), CHUNK ** -0.5),
        'spatial_b': 1.0 + nrm(ks[8], (L, A_GROUPS, CHUNK), 0.1),
        'w_branch_a': nrm(ks[9], (L, A_WIDTH, D_MODEL), A_WIDTH ** -0.5),
        'q_norm_gain': 1.0 + nrm(ks[10], (L, B_HEAD_DIM), 0.01),
        'k_norm_gain': 1.0 + nrm(ks[11], (L, B_HEAD_DIM), 0.01),
        'idx_k_ln_gain': 1.0 + nrm(ks[12], (L, IDX_DIM), 0.01),
        'idx_k_ln_bias': nrm(ks[13], (L, IDX_DIM), 0.01),
        'w_branch_b': nrm(ks[14], (L, B_WIDTH, D_MODEL), B_WIDTH ** -0.5),
        'mem_norm_gain': 1.0 + nrm(ks[15], (L, D_MODEL), 0.01),
        'w_mem_kv': nrm(ks[16], (L, D_MODEL, 2 * M_WIDTH), D_MODEL ** -0.5),
        'mem_q_norm_gain': 1.0 + nrm(ks[17], (L, M_HEAD_DIM), 0.01),
        'mem_k_norm_gain': 1.0 + nrm(ks[18], (L, M_HEAD_DIM), 0.01),
        'w_branch_m': nrm(ks[19], (L, M_WIDTH, D_MODEL), M_WIDTH ** -0.5),
        'w_out': nrm(ks[20], (L, D_MODEL, D_MODEL), D_MODEL ** -0.5),
    }


def reference(x, mem, positions, norm_gain, w_in, gmlp_ln_gain, gmlp_ln_bias, spatial_w,
              spatial_b, w_branch_a, q_norm_gain, k_norm_gain, idx_k_ln_gain, idx_k_ln_bias,
              w_branch_b, mem_norm_gain, w_mem_kv, mem_q_norm_gain, mem_k_norm_gain,
              w_branch_m, w_out):
    for l in range(DEPTH):
        h = rms_norm(x, norm_gain[l])
        proj = h @ w_in[l]
        (a_u, a_v, a_z, b_q, b_k, b_v, b_z, i_q, i_k, i_w,
         m_q, m_z, g_a, g_b, g_m) = jnp.split(proj, SPLIT_OFFSETS, axis=-1)
        y_a = gmlp_branch(a_u, a_v, a_z, gmlp_ln_gain[l], gmlp_ln_bias[l],
                          spatial_w[l], spatial_b[l]) @ w_branch_a[l]
        y_b = dsa_branch(b_q, b_k, b_v, b_z, i_q, i_k, i_w, positions,
                         q_norm_gain[l], k_norm_gain[l], idx_k_ln_gain[l],
                         idx_k_ln_bias[l]) @ w_branch_b[l]
        y_m = memory_branch(m_q, m_z, mem, mem_norm_gain[l], w_mem_kv[l],
                            mem_q_norm_gain[l], mem_k_norm_gain[l]) @ w_branch_m[l]
        merged = (jax.nn.sigmoid(g_a) * y_a + jax.nn.sigmoid(g_b) * y_b
                  + jax.nn.sigmoid(g_m) * y_m)
        x = x + merged @ w_out[l]
    return x
```

```python
import functools

import numpy as np
import jax
import jax.numpy as jnp
from jax import lax
from jax.experimental import pallas as pl
from jax.experimental.pallas import tpu as pltpu

F32 = jnp.float32
BF16 = jnp.bfloat16

D_MODEL = 2048
ROPE_THETA = 500000.0
EPS = 1e-6
A_GROUPS = 8
A_GROUP_DIM = 128
A_WIDTH = A_GROUPS * A_GROUP_DIM
CHUNK = 128
B_HEADS = 8
B_KV_HEADS = 2
B_GROUP = B_HEADS // B_KV_HEADS
B_HEAD_DIM = 128
B_WIDTH = B_HEADS * B_HEAD_DIM
IDX_HEADS = 16
IDX_DIM = 64
TOPK_MAX = 256
M_HEADS = 4
M_HEAD_DIM = 256
M_WIDTH = M_HEADS * M_HEAD_DIM

SPLIT_SIZES = (
    A_WIDTH, A_WIDTH, A_WIDTH,
    B_WIDTH, B_KV_HEADS * B_HEAD_DIM, B_KV_HEADS * B_HEAD_DIM, B_WIDTH,
    IDX_HEADS * IDX_DIM, IDX_DIM, IDX_HEADS,
    M_WIDTH, M_WIDTH,
    D_MODEL, D_MODEL, D_MODEL,
)

LANES = 128
BLK = 1024
COL_AU, COL_AV, COL_AZ, COL_BQ, COL_BZ, COL_IQ, COL_MQ, COL_MZ = range(8)
COL_GA, COL_GB, COL_GM = 8, 10, 12
NB_MAIN = 14
SM_K, SM_V, SM_IK, SM_IW = 0, 256, 512, 640

VMEM_LIMIT = 56 * 1024 * 1024
LOG2E = 1.4426950408889634
NEG_BIG = -1e30
F32_MAX = 3.4028234663852886e38

TQ = 128
CK = 256
MAX_SEARCH_ITERS = 400


def _cparams(sem):
    return pltpu.CompilerParams(dimension_semantics=sem, vmem_limit_bytes=VMEM_LIMIT)


def _dot_nt(a, b):
    return lax.dot_general(a, b, (((1,), (1,)), ((), ())), preferred_element_type=F32)


def _rope(x, cos_t, sin_t, half, period):
    lane = lax.broadcasted_iota(jnp.int32, x.shape, 1) % period
    rolled = jnp.where(lane < half, pltpu.roll(x, LANES - half, 1), pltpu.roll(x, half, 1))
    return x * cos_t + rolled * sin_t


def _inproj_kernel(x_ref, g_ref, w_ref, proj_ref, small_ref, h_ref):
    n = pl.program_id(1)

    @pl.when(n == 0)
    def _():
        x = x_ref[...]
        ms = jnp.mean(x * x, axis=-1, keepdims=True)
        h_ref[...] = (x * lax.rsqrt(ms + EPS) * g_ref[...]).astype(BF16)

    acc = jnp.dot(h_ref[...], w_ref[...], preferred_element_type=F32)

    @pl.when(n < NB_MAIN)
    def _():
        proj_ref[...] = acc.astype(BF16)

    @pl.when(n == NB_MAIN)
    def _():
        small_ref[...] = acc


def _inproj(x2, gain, w_packed):
    m = x2.shape[0]
    tm = min(1024, m)
    nb = w_packed.shape[1] // BLK
    return pl.pallas_call(
        _inproj_kernel,
        grid=(m // tm, nb),
        in_specs=[
            pl.BlockSpec((tm, D_MODEL), lambda i, n: (i, 0)),
            pl.BlockSpec((1, D_MODEL), lambda i, n: (0, 0)),
            pl.BlockSpec((D_MODEL, BLK), lambda i, n: (0, n)),
        ],
        out_specs=[
            pl.BlockSpec((tm, BLK), lambda i, n: (i, jnp.minimum(n, NB_MAIN - 1))),
            pl.BlockSpec((tm, BLK), lambda i, n: (i, 0)),
        ],
        out_shape=[
            jax.ShapeDtypeStruct((m, NB_MAIN * BLK), BF16),
            jax.ShapeDtypeStruct((m, BLK), F32),
        ],
        scratch_shapes=[pltpu.VMEM((tm, D_MODEL), BF16)],
        compiler_params=_cparams(("parallel", "arbitrary")),
        name="inproj",
    )(x2, gain, w_packed)


def _kvprep_kernel(small_ref, pos_ref, fb_ref, sgb_ref, fi_ref, sgi_ref, gk_ref, lng_ref, lnb_ref,
                   k_ref, v_ref, iklo_ref, ikhi_ref, cb_ref, sb_ref, ci_ref, si_ref, iw_ref):
    pos = pos_ref[0].astype(F32)
    ang_b = pos * fb_ref[...]
    cos_b = jnp.cos(ang_b)
    sin_b = jnp.sin(ang_b) * sgb_ref[...]
    ang_i = pos * fi_ref[...]
    cos_i = jnp.cos(ang_i)
    sin_i = jnp.sin(ang_i) * sgi_ref[...]
    cb_ref[0] = cos_b
    sb_ref[0] = sin_b
    ci_ref[0] = cos_i
    si_ref[0] = sin_i

    for n in range(B_KV_HEADS):
        kh = small_ref[:, SM_K + n * B_HEAD_DIM:SM_K + (n + 1) * B_HEAD_DIM]
        r = lax.rsqrt(jnp.mean(kh * kh, axis=-1, keepdims=True) + EPS)
        kn = kh * r * gk_ref[...]
        k_ref[0, n] = _rope(kn, cos_b, sin_b, B_HEAD_DIM // 8, LANES).astype(BF16)
        v_ref[0, n] = small_ref[:, SM_V + n * B_HEAD_DIM:SM_V + (n + 1) * B_HEAD_DIM].astype(BF16)

    ikp = small_ref[:, SM_IK:SM_IK + LANES]
    lane = lax.broadcasted_iota(jnp.int32, ikp.shape, 1)
    live = lane < IDX_DIM
    mu = jnp.sum(jnp.where(live, ikp, 0.0), axis=-1, keepdims=True) * (1.0 / IDX_DIM)
    d = jnp.where(live, ikp - mu, 0.0)
    var = jnp.sum(d * d, axis=-1, keepdims=True) * (1.0 / IDX_DIM)
    y = d * lax.rsqrt(var + EPS) * lng_ref[...] + lnb_ref[...]
    yr = jnp.where(live, _rope(y, cos_i, sin_i, IDX_DIM // 8, IDX_DIM), 0.0)
    iklo_ref[0] = yr.astype(BF16)
    ikhi_ref[0] = pltpu.roll(yr, IDX_DIM, 1).astype(BF16)

    iw_ref[0] = small_ref[:, SM_IW:SM_IW + LANES] * (IDX_DIM ** -0.5 * IDX_HEADS ** -0.5)


def _kvprep(small3, pos3, tabs, gk, lng, lnb):
    bsz, s, _ = small3.shape
    tp = min(512, s)
    row = lambda b, i: (b, i, 0)
    const = lambda b, i: (0, 0)
    tab_spec = pl.BlockSpec((1, LANES), const)
    out_tok = pl.BlockSpec((1, tp, LANES), row)
    out_kv = pl.BlockSpec((1, B_KV_HEADS, tp, B_HEAD_DIM), lambda b, i: (b, 0, i, 0))
    return pl.pallas_call(
        _kvprep_kernel,
        grid=(bsz, s // tp),
        in_specs=[pl.BlockSpec((None, tp, BLK), row), pl.BlockSpec((1, tp, 1), row),
                  tab_spec, tab_spec, tab_spec, tab_spec, tab_spec, tab_spec, tab_spec],
        out_specs=[out_kv, out_kv, out_tok, out_tok, out_tok, out_tok, out_tok, out_tok, out_tok],
        out_shape=[
            jax.ShapeDtypeStruct((bsz, B_KV_HEADS, s, B_HEAD_DIM), BF16),
            jax.ShapeDtypeStruct((bsz, B_KV_HEADS, s, B_HEAD_DIM), BF16),
            jax.ShapeDtypeStruct((bsz, s, LANES), BF16),
            jax.ShapeDtypeStruct((bsz, s, LANES), BF16),
            jax.ShapeDtypeStruct((bsz, s, LANES), F32),
            jax.ShapeDtypeStruct((bsz, s, LANES), F32),
            jax.ShapeDtypeStruct((bsz, s, LANES), F32),
            jax.ShapeDtypeStruct((bsz, s, LANES), F32),
            jax.ShapeDtypeStruct((bsz, s, LANES), F32),
        ],
        compiler_params=_cparams(("parallel", "parallel")),
        name="kvprep",
    )(small3, pos3, *tabs, gk, lng, lnb)


def _gmlp_kernel(u_ref, v_ref, z_ref, lng_ref, lnb_ref, ws_ref, sbt_ref, o_ref):
    tm = u_ref.shape[0]
    u = jax.nn.gelu(u_ref[...].astype(F32))
    v = jax.nn.gelu(v_ref[...].astype(F32))
    mu = jnp.mean(v, axis=-1, keepdims=True)
    d = v - mu
    var = jnp.mean(d * d, axis=-1, keepdims=True)
    vn = (d * lax.rsqrt(var + EPS) * lng_ref[...] + lnb_ref[...]).astype(BF16)
    gate = u * jax.nn.silu(z_ref[...].astype(F32))
    tri = (lax.broadcasted_iota(jnp.int32, (CHUNK, CHUNK), 1)
           <= lax.broadcasted_iota(jnp.int32, (CHUNK, CHUNK), 0))
    for g in range(A_GROUPS):
        wg = jnp.where(tri, ws_ref[g], 0.0).astype(BF16)
        bias = sbt_ref[:, g:g + 1]
        cols = slice(g * A_GROUP_DIM, (g + 1) * A_GROUP_DIM)
        for c in range(tm // CHUNK):
            rows = slice(c * CHUNK, (c + 1) * CHUNK)
            sg = jnp.dot(wg, vn[rows, cols], preferred_element_type=F32) + bias
            o_ref[rows, cols] = (gate[rows, cols] * sg).astype(BF16)


def _gmlp(proj, lng, lnb, ws, sbt):
    m = proj.shape[0]
    tm = min(512, m)
    col = lambda c: pl.BlockSpec((tm, BLK), lambda i: (i, c))
    full = lambda shape: pl.BlockSpec(shape, lambda i: (0,) * len(shape))
    return pl.pallas_call(
        _gmlp_kernel,
        grid=(m // tm,),
        in_specs=[col(COL_AU), col(COL_AV), col(COL_AZ), full((1, A_WIDTH)), full((1, A_WIDTH)),
                  full((A_GROUPS, CHUNK, CHUNK)), full((CHUNK, A_GROUPS))],
        out_specs=pl.BlockSpec((tm, A_WIDTH), lambda i: (i, 0)),
        out_shape=jax.ShapeDtypeStruct((m, A_WIDTH), BF16),
        compiler_params=_cparams(("parallel",)),
        name="gmlp",
    )(proj, proj, proj, lng, lnb, ws, sbt)


def _memkv_kernel(mem_ref, g_ref, w_ref, gk_ref, km_ref, vm_ref):
    x = mem_ref[0]
    ms = jnp.mean(x * x, axis=-1, keepdims=True)
    h = (x * lax.rsqrt(ms + EPS) * g_ref[...]).astype(BF16)
    kv = jnp.dot(h, w_ref[...], preferred_element_type=F32)
    for hd in range(M_HEADS):
        kh = kv[:, hd * M_HEAD_DIM:(hd + 1) * M_HEAD_DIM]
        r = lax.rsqrt(jnp.mean(kh * kh, axis=-1, keepdims=True) + EPS)
        km_ref[0, hd] = (kh * r * gk_ref[...]).astype(BF16)
        vm_ref[0, hd] = kv[:, M_WIDTH + hd * M_HEAD_DIM:M_WIDTH + (hd + 1) * M_HEAD_DIM].astype(BF16)


def _memkv(mem, gain, w_kv, gk):
    bsz, ml, _ = mem.shape
    out = pl.BlockSpec((1, M_HEADS, ml, M_HEAD_DIM), lambda b: (b, 0, 0, 0))
    shp = jax.ShapeDtypeStruct((bsz, M_HEADS, ml, M_HEAD_DIM), BF16)
    return pl.pallas_call(
        _memkv_kernel,
        grid=(bsz,),
        in_specs=[pl.BlockSpec((1, ml, D_MODEL), lambda b: (b, 0, 0)),
                  pl.BlockSpec((1, D_MODEL), lambda b: (0, 0)),
                  pl.BlockSpec((D_MODEL, 2 * M_WIDTH), lambda b: (0, 0)),
                  pl.BlockSpec((1, M_HEAD_DIM), lambda b: (0, 0))],
        out_specs=[out, out],
        out_shape=[shp, shp],
        compiler_params=_cparams(("parallel",)),
        name="memkv",
    )(mem, gain, w_kv, gk)


def _memattn_kernel(q_ref, z_ref, km_ref, vm_ref, gq_ref, o_ref):
    qscale = M_HEAD_DIM ** -0.5 * LOG2E
    for hd in range(M_HEADS):
        cols = slice(hd * M_HEAD_DIM, (hd + 1) * M_HEAD_DIM)
        q = q_ref[:, cols].astype(F32)
        r = lax.rsqrt(jnp.mean(q * q, axis=-1, keepdims=True) + EPS)
        qn = (q * r * gq_ref[...] * qscale).astype(BF16)
        lg = _dot_nt(qn, km_ref[0, hd])
        p = jnp.exp2(lg - jnp.max(lg, axis=-1, keepdims=True))
        l = jnp.sum(p, axis=-1, keepdims=True)
        o = jnp.dot(p.astype(BF16), vm_ref[0, hd], preferred_element_type=F32) / l
        o_ref[:, cols] = (o * jax.nn.silu(z_ref[:, cols].astype(F32))).astype(BF16)


def _memattn(proj, km, vm, gq, s):
    m = proj.shape[0]
    tm = min(512, s)
    per_b = s // tm
    ml = km.shape[2]
    kv_spec = pl.BlockSpec((1, M_HEADS, ml, M_HEAD_DIM), lambda i: (i // per_b, 0, 0, 0))
    return pl.pallas_call(
        _memattn_kernel,
        grid=(m // tm,),
        in_specs=[pl.BlockSpec((tm, BLK), lambda i: (i, COL_MQ)),
                  pl.BlockSpec((tm, BLK), lambda i: (i, COL_MZ)),
                  kv_spec, kv_spec,
                  pl.BlockSpec((1, M_HEAD_DIM), lambda i: (0, 0))],
        out_specs=pl.BlockSpec((tm, M_WIDTH), lambda i: (i, 0)),
        out_shape=jax.ShapeDtypeStruct((m, M_WIDTH), BF16),
        compiler_params=_cparams(("parallel",)),
        name="memattn",
    )(proj, proj, km, vm, gq)


def _dsa_kernel(topk, q_ref, z_ref, iq_ref, cb_ref, sb_ref, ci_ref, si_ref, iw_ref,
                k_ref, v_ref, iklo_ref, ikhi_ref, gq_ref, o_ref,
                sc_ref, qs_ref, iqs_ref, wb_ref, m_ref, l_ref, acc_ref):
    qb = pl.program_id(1)
    nck = (qb * TQ + TQ + CK - 1) // CK
    t_row = qb * TQ + lax.broadcasted_iota(jnp.int32, (TQ, 1), 0)
    nsub = CK // LANES

    cos_b, sin_b = cb_ref[0], sb_ref[0]
    cos_i, sin_i = ci_ref[0], si_ref[0]
    qscale = B_HEAD_DIM ** -0.5 * LOG2E
    for h in range(B_HEADS):
        slab = q_ref[:, h * B_HEAD_DIM:(h + 1) * B_HEAD_DIM].astype(F32)
        r = lax.rsqrt(jnp.mean(slab * slab, axis=-1, keepdims=True) + EPS)
        qr = _rope(slab * r * gq_ref[...], cos_b, sin_b, B_HEAD_DIM // 8, LANES) * qscale
        g = h % B_GROUP
        qs_ref[h // B_GROUP, g * TQ:(g + 1) * TQ, :] = qr.astype(BF16)
    for j in range(IDX_HEADS // 2):
        slab = iq_ref[:, j * LANES:(j + 1) * LANES].astype(F32)
        iqs_ref[j] = _rope(slab, cos_i, sin_i, IDX_DIM // 8, IDX_DIM).astype(BF16)
    w = iw_ref[0]
    for h in range(IDX_HEADS):
        wb_ref[h] = jnp.broadcast_to(w[:, h:h + 1], (TQ, LANES))

    def idx_body(c, carry):
        mn, mx = carry
        off = pl.multiple_of(c * CK, CK)
        klo = iklo_ref[0, pl.ds(off, CK), :]
        khi = ikhi_ref[0, pl.ds(off, CK), :]
        acc = [jnp.zeros((TQ, LANES), F32) for _ in range(nsub)]
        for j in range(IDX_HEADS // 2):
            lhs = iqs_ref[j]
            for half, keys in enumerate((klo, khi)):
                d = _dot_nt(lhs, keys)
                wh = wb_ref[2 * j + half]
                for u in range(nsub):
                    acc[u] = acc[u] + jnp.maximum(d[:, u * LANES:(u + 1) * LANES], 0.0) * wh
        for u in range(nsub):
            key = off + u * LANES + lax.broadcasted_iota(jnp.int32, (TQ, LANES), 1)
            causal = key <= t_row
            sc_ref[c, :, u * LANES:(u + 1) * LANES] = jnp.where(causal, acc[u], -jnp.inf)
            mn = jnp.minimum(mn, jnp.where(causal, acc[u], jnp.inf))
            mx = jnp.maximum(mx, jnp.where(causal, acc[u], -jnp.inf))
        return mn, mx

    mn, mx = lax.fori_loop(0, nck, idx_body,
                           (jnp.full((TQ, LANES), jnp.inf, F32), jnp.full((TQ, LANES), -jnp.inf, F32)))
    row_min = jnp.min(mn, axis=-1, keepdims=True)
    row_max = jnp.max(mx, axis=-1, keepdims=True)

    def count_ge(thr):
        thr_b = jnp.broadcast_to(thr, (TQ, LANES))

        def body(c, cnt):
            for u in range(nsub):
                blk = sc_ref[c, :, u * LANES:(u + 1) * LANES]
                cnt = cnt + jnp.where(blk >= thr_b, 1.0, 0.0)
            return cnt

        cnt = lax.fori_loop(0, nck, body, jnp.zeros((TQ, LANES), F32))
        return jnp.sum(cnt, axis=-1, keepdims=True)

    kf = float(topk)
    all_rows = (t_row + 1) <= topk
    fin0 = jnp.where(all_rows, 1.0, 0.0)
    thr0 = jnp.where(all_rows, -F32_MAX, row_max)

    def s_cond(st):
        it, lo, hi, thr, fin = st
        return jnp.logical_and(it < MAX_SEARCH_ITERS, jnp.min(fin) < 0.5)

    def s_body(st):
        it, lo, hi, thr, fin = st
        mid = lo * 0.5 + hi * 0.5
        stuck = jnp.logical_or(mid <= lo, mid >= hi)
        probe = jnp.where(stuck, hi, mid)
        cnt = count_ge(probe)
        active = fin < 0.5
        hit = cnt == kf
        end_thr = jnp.where(jnp.logical_and(stuck, cnt < kf), lo, probe)
        ends = jnp.logical_and(active, jnp.logical_or(hit, stuck))
        thr = jnp.where(ends, end_thr, thr)
        fin = jnp.where(ends, 1.0, fin)
        lo = jnp.where(jnp.logical_and(active, cnt > kf), probe, lo)
        hi = jnp.where(jnp.logical_and(active, cnt < kf), probe, hi)
        return it + 1, lo, hi, thr, fin

    _, _, _, thr, _ = lax.while_loop(s_cond, s_body, (jnp.int32(0), row_min, row_max, thr0, fin0))

    thr_b = jnp.broadcast_to(thr, (TQ, LANES))

    def gt_body(c, cg):
        for u in range(nsub):
            blk = sc_ref[c, :, u * LANES:(u + 1) * LANES]
            cg = cg + jnp.where(blk > thr_b, 1.0, 0.0)
        return cg

    n_gt = jnp.sum(lax.fori_loop(0, nck, gt_body, jnp.zeros((TQ, LANES), F32)), axis=-1, keepdims=True)
    need_b = jnp.broadcast_to(jnp.where(all_rows, F32_MAX, kf - n_gt), (TQ, LANES))
    r_i = lax.broadcasted_iota(jnp.int32, (LANES, 2 * LANES), 0)
    c_i = lax.broadcasted_iota(jnp.int32, (LANES, 2 * LANES), 1)
    prefix_mat = jnp.where(jnp.logical_or(c_i >= LANES, r_i <= c_i), 1.0, 0.0).astype(BF16)

    m_ref[...] = jnp.full(m_ref.shape, NEG_BIG, F32)
    l_ref[...] = jnp.zeros(l_ref.shape, F32)
    acc_ref[...] = jnp.zeros(acc_ref.shape, F32)

    def att_body(c, seen):
        off = pl.multiple_of(c * CK, CK)
        bias = []
        for u in range(nsub):
            blk = sc_ref[c, :, u * LANES:(u + 1) * LANES]
            eq = blk == thr_b
            cnt = jnp.dot(jnp.where(eq, 1.0, 0.0).astype(BF16), prefix_mat, preferred_element_type=F32)
            pref = cnt[:, :LANES] + seen
            keep = jnp.logical_or(blk > thr_b, jnp.logical_and(eq, pref <= need_b))
            seen = seen + cnt[:, LANES:]
            bias.append(jnp.where(keep, 0.0, NEG_BIG))
        bias = jnp.concatenate(bias, axis=1)
        for n in range(B_KV_HEADS):
            kc = k_ref[0, n, pl.ds(off, CK), :]
            vc = v_ref[0, n, pl.ds(off, CK), :]
            lg = _dot_nt(qs_ref[n], kc)
            lg = (lg.reshape(B_GROUP, TQ, CK) + bias[None]).reshape(B_GROUP * TQ, CK)
            m_prev = m_ref[n]
            m_new = jnp.maximum(m_prev, jnp.max(lg, axis=-1, keepdims=True))
            alpha = jnp.exp2(m_prev - m_new)
            p = jnp.exp2(lg - m_new)
            l_ref[n] = alpha * l_ref[n] + jnp.sum(p, axis=-1, keepdims=True)
            acc_ref[n] = acc_ref[n] * alpha + jnp.dot(p.astype(BF16), vc, preferred_element_type=F32)
            m_ref[n] = m_new
        return seen

    lax.fori_loop(0, nck, att_body, jnp.zeros((TQ, LANES), F32))

    for h in range(B_HEADS):
        n, g = h // B_GROUP, h % B_GROUP
        rows = slice(g * TQ, (g + 1) * TQ)
        cols = slice(h * B_HEAD_DIM, (h + 1) * B_HEAD_DIM)
        o = acc_ref[n, rows, :] / l_ref[n, rows, :]
        o_ref[:, cols] = (o * jax.nn.silu(z_ref[:, cols].astype(F32))).astype(BF16)


def _dsa(proj, tabs, iw, k, v, iklo, ikhi, gq, bsz, s):
    nq = s // TQ
    topk = min(TOPK_MAX, s // 4)
    rowblk = lambda c: pl.BlockSpec((TQ, BLK), lambda b, i: (b * nq + i, c))
    tok = pl.BlockSpec((1, TQ, LANES), lambda b, i: (b, i, 0))
    kv = pl.BlockSpec((1, B_KV_HEADS, s, B_HEAD_DIM), lambda b, i: (b, 0, 0, 0))
    ik = pl.BlockSpec((1, s, LANES), lambda b, i: (b, 0, 0))
    return pl.pallas_call(
        functools.partial(_dsa_kernel, topk),
        grid=(bsz, nq),
        in_specs=[rowblk(COL_BQ), rowblk(COL_BZ), rowblk(COL_IQ), tok, tok, tok, tok, tok,
                  kv, kv, ik, ik, pl.BlockSpec((1, B_HEAD_DIM), lambda b, i: (0, 0))],
        out_specs=pl.BlockSpec((TQ, B_WIDTH), lambda b, i: (b * nq + i, 0)),
        out_shape=jax.ShapeDtypeStruct((bsz * s, B_WIDTH), BF16),
        scratch_shapes=[
            pltpu.VMEM((s // CK, TQ, CK), F32),
            pltpu.VMEM((B_KV_HEADS, B_GROUP * TQ, B_HEAD_DIM), BF16),
            pltpu.VMEM((IDX_HEADS // 2, TQ, LANES), BF16),
            pltpu.VMEM((IDX_HEADS, TQ, LANES), F32),
            pltpu.VMEM((B_KV_HEADS, B_GROUP * TQ, 1), F32),
            pltpu.VMEM((B_KV_HEADS, B_GROUP * TQ, 1), F32),
            pltpu.VMEM((B_KV_HEADS, B_GROUP * TQ, B_HEAD_DIM), F32),
        ],
        compiler_params=_cparams(("parallel", "arbitrary")),
        name="dsa",
    )(proj, proj, proj, *tabs, iw, k, v, iklo, ikhi, gq)


def _merge_kernel(ta_ref, tb_ref, tm_ref, ga_ref, gb_ref, gm_ref, x_ref,
                  wa_ref, wb_ref, wm_ref, wo_ref, o_ref):
    def branch(t_ref, g_ref, w_ref):
        y = jnp.dot(t_ref[...], w_ref[...], preferred_element_type=F32)
        return jax.nn.sigmoid(g_ref[...].astype(F32)) * y

    merged = branch(ta_ref, ga_ref, wa_ref) + branch(tb_ref, gb_ref, wb_ref) + branch(tm_ref, gm_ref, wm_ref)
    o_ref[...] = x_ref[...] + jnp.dot(merged.astype(BF16), wo_ref[...], preferred_element_type=F32)


def _merge(ta, tb, tmem, proj, x2, wa, wb, wm, wo):
    m = x2.shape[0]
    tm = min(256, m)
    act = pl.BlockSpec((tm, BLK), lambda i: (i, 0))
    gate = lambda c: pl.BlockSpec((tm, D_MODEL), lambda i: (i, c // 2))
    wide = pl.BlockSpec((tm, D_MODEL), lambda i: (i, 0))
    wbr = pl.BlockSpec((BLK, D_MODEL), lambda i: (0, 0), pipeline_mode=pl.Buffered(1))
    wout = pl.BlockSpec((D_MODEL, D_MODEL), lambda i: (0, 0), pipeline_mode=pl.Buffered(1))
    return pl.pallas_call(
        _merge_kernel,
        grid=(m // tm,),
        in_specs=[act, act, act, gate(COL_GA), gate(COL_GB), gate(COL_GM), wide, wbr, wbr, wbr, wout],
        out_specs=wide,
        out_shape=jax.ShapeDtypeStruct((m, D_MODEL), F32),
        compiler_params=_cparams(("parallel",)),
        name="merge",
    )(ta, tb, tmem, proj, proj, proj, x2, wa, wb, wm, wo)


def _pack_w_in(w):
    offs = np.concatenate([[0], np.cumsum(SPLIT_SIZES)])
    (a_u, a_v, a_z, b_q, b_k, b_v, b_z, i_q, i_k, i_w, m_q, m_z, g_a, g_b, g_m) = [
        w[:, int(offs[i]):int(offs[i + 1])] for i in range(len(SPLIT_SIZES))]
    pad = lambda n: jnp.zeros((w.shape[0], n), w.dtype)
    small = [b_k, b_v, i_k, pad(SM_IW - SM_IK - IDX_DIM), i_w, pad(BLK - SM_IW - IDX_HEADS)]
    cols = [a_u, a_v, a_z, b_q, b_z, i_q, m_q, m_z, g_a, g_b, g_m] + small
    return jnp.concatenate(cols, axis=1).astype(BF16)


def _rope_tables():
    def build(rot, period):
        half = rot // 2
        inv = ROPE_THETA ** (-np.arange(half, dtype=np.float32) / half)
        lane = np.arange(LANES) % period
        freq = np.where(lane < rot, inv[lane % half], 0.0).astype(np.float32)
        sign = np.where(lane < half, -1.0, np.where(lane < rot, 1.0, 0.0)).astype(np.float32)
        return jnp.asarray(freq[None, :]), jnp.asarray(sign[None, :])

    fb, sgb = build(B_HEAD_DIM // 4, LANES)
    fi, sgi = build(IDX_DIM // 4, IDX_DIM)
    return fb, sgb, fi, sgi


def _layer(x, mem, positions, norm_gain, w_in, gmlp_ln_gain, gmlp_ln_bias, spatial_w, spatial_b,
           w_branch_a, q_norm_gain, k_norm_gain, idx_k_ln_gain, idx_k_ln_bias, w_branch_b,
           mem_norm_gain, w_mem_kv, mem_q_norm_gain, mem_k_norm_gain, w_branch_m, w_out):
    bsz, s, _ = x.shape
    m = bsz * s
    row = lambda a: a.reshape(1, -1).astype(F32)
    pad_lanes = lambda a: jnp.pad(a.reshape(1, -1).astype(F32), ((0, 0), (0, LANES - a.shape[-1])))
    x2 = x.reshape(m, D_MODEL)

    proj, small = _inproj(x2, row(norm_gain), _pack_w_in(w_in))
    k, v, iklo, ikhi, cb, sb, ci, si, iw = _kvprep(
        small.reshape(bsz, s, BLK), positions.reshape(bsz, s, 1).astype(jnp.int32), _rope_tables(),
        row(k_norm_gain), pad_lanes(idx_k_ln_gain), pad_lanes(idx_k_ln_bias))

    t_a = _gmlp(proj, row(gmlp_ln_gain), row(gmlp_ln_bias), spatial_w.astype(F32), spatial_b.T.astype(F32))
    km, vm = _memkv(mem, row(mem_norm_gain), w_mem_kv.astype(BF16), row(mem_k_norm_gain))
    t_m = _memattn(proj, km, vm, row(mem_q_norm_gain), s)
    t_b = _dsa(proj, (cb, sb, ci, si), iw, k, v, iklo, ikhi, row(q_norm_gain), bsz, s)

    out = _merge(t_a, t_b, t_m, proj, x2, w_branch_a.astype(BF16), w_branch_b.astype(BF16),
                 w_branch_m.astype(BF16), w_out.astype(BF16))
    return out.reshape(bsz, s, D_MODEL)


def kernel(x, mem, positions, norm_gain, w_in, gmlp_ln_gain, gmlp_ln_bias, spatial_w, spatial_b, w_branch_a, q_norm_gain, k_norm_gain, idx_k_ln_gain, idx_k_ln_bias, w_branch_b, mem_norm_gain, w_mem_kv, mem_q_norm_gain, mem_k_norm_gain, w_branch_m, w_out):
    for l in range(norm_gain.shape[0]):
        x = _layer(x, mem, positions, norm_gain[l], w_in[l], gmlp_ln_gain[l], gmlp_ln_bias[l],
                   spatial_w[l], spatial_b[l], w_branch_a[l], q_norm_gain[l], k_norm_gain[l],
                   idx_k_ln_gain[l], idx_k_ln_bias[l], w_branch_b[l], mem_norm_gain[l], w_mem_kv[l],
                   mem_q_norm_gain[l], mem_k_norm_gain[l], w_branch_m[l], w_out[l])
    return x
```

```python
import functools

import numpy as np
import jax
import jax.numpy as jnp
from jax import lax
from jax.experimental import pallas as pl
from jax.experimental.pallas import tpu as pltpu

F32 = jnp.float32
BF16 = jnp.bfloat16

D_MODEL = 2048
ROPE_THETA = 500000.0
EPS = 1e-6
A_GROUPS = 8
A_GROUP_DIM = 128
A_WIDTH = A_GROUPS * A_GROUP_DIM
CHUNK = 128
B_HEADS = 8
B_KV_HEADS = 2
B_GROUP = B_HEADS // B_KV_HEADS
B_HEAD_DIM = 128
B_WIDTH = B_HEADS * B_HEAD_DIM
IDX_HEADS = 16
IDX_DIM = 64
TOPK_MAX = 256
M_HEADS = 4
M_HEAD_DIM = 256
M_WIDTH = M_HEADS * M_HEAD_DIM

SPLIT_SIZES = (
    A_WIDTH, A_WIDTH, A_WIDTH,
    B_WIDTH, B_KV_HEADS * B_HEAD_DIM, B_KV_HEADS * B_HEAD_DIM, B_WIDTH,
    IDX_HEADS * IDX_DIM, IDX_DIM, IDX_HEADS,
    M_WIDTH, M_WIDTH,
    D_MODEL, D_MODEL, D_MODEL,
)

LANES = 128
BLK = 1024
COL_AU, COL_AV, COL_AZ, COL_BQ, COL_BZ, COL_IQ, COL_MQ, COL_MZ = range(8)
COL_GA, COL_GB, COL_GM = 8, 10, 12
NB_MAIN = 14
SM_K, SM_V, SM_IK, SM_IW = 0, 256, 512, 640

VMEM_LIMIT = 56 * 1024 * 1024
LOG2E = 1.4426950408889634
NEG_BIG = -1e30
F32_MAX = 3.4028234663852886e38

TQ = 128
CK = 256
MAX_SEARCH_ITERS = 400


def _cparams(sem):
    return pltpu.CompilerParams(dimension_semantics=sem, vmem_limit_bytes=VMEM_LIMIT)


def _dot_nt(a, b):
    return lax.dot_general(a, b, (((1,), (1,)), ((), ())), preferred_element_type=F32)


def _rope(x, cos_t, sin_t, half, period):
    lane = lax.broadcasted_iota(jnp.int32, x.shape, 1) % period
    rolled = jnp.where(lane < half, pltpu.roll(x, LANES - half, 1), pltpu.roll(x, half, 1))
    return x * cos_t + rolled * sin_t


def _inproj_kernel(x_ref, g_ref, w_ref, proj_ref, small_ref, h_ref):
    n = pl.program_id(1)

    @pl.when(n == 0)
    def _():
        x = x_ref[...]
        ms = jnp.mean(x * x, axis=-1, keepdims=True)
        h_ref[...] = (x * lax.rsqrt(ms + EPS) * g_ref[...]).astype(BF16)

    acc = jnp.dot(h_ref[...], w_ref[...], preferred_element_type=F32)

    @pl.when(n < NB_MAIN)
    def _():
        proj_ref[...] = acc.astype(BF16)

    @pl.when(n == NB_MAIN)
    def _():
        small_ref[...] = acc


def _inproj(x2, gain, w_packed):
    m = x2.shape[0]
    tm = min(1024, m)
    nb = w_packed.shape[1] // BLK
    return pl.pallas_call(
        _inproj_kernel,
        grid=(m // tm, nb),
        in_specs=[
            pl.BlockSpec((tm, D_MODEL), lambda i, n: (i, 0)),
            pl.BlockSpec((1, D_MODEL), lambda i, n: (0, 0)),
            pl.BlockSpec((D_MODEL, BLK), lambda i, n: (0, n)),
        ],
        out_specs=[
            pl.BlockSpec((tm, BLK), lambda i, n: (i, jnp.minimum(n, NB_MAIN - 1))),
            pl.BlockSpec((tm, BLK), lambda i, n: (i, 0)),
        ],
        out_shape=[
            jax.ShapeDtypeStruct((m, NB_MAIN * BLK), BF16),
            jax.ShapeDtypeStruct((m, BLK), F32),
        ],
        scratch_shapes=[pltpu.VMEM((tm, D_MODEL), BF16)],
        compiler_params=_cparams(("parallel", "arbitrary")),
        name="inproj",
    )(x2, gain, w_packed)


def _kvprep_kernel(small_ref, pos_ref, fb_ref, sgb_ref, fi_ref, sgi_ref, gk_ref, lng_ref, lnb_ref,
                   k_ref, vt_ref, iklo_ref, ikhi_ref, cb_ref, sb_ref, ci_ref, si_ref, iwt_ref):
    pos = pos_ref[0].astype(F32)
    ang_b = pos * fb_ref[...]
    cos_b = jnp.cos(ang_b)
    sin_b = jnp.sin(ang_b) * sgb_ref[...]
    ang_i = pos * fi_ref[...]
    cos_i = jnp.cos(ang_i)
    sin_i = jnp.sin(ang_i) * sgi_ref[...]
    cb_ref[0] = cos_b
    sb_ref[0] = sin_b
    ci_ref[0] = cos_i
    si_ref[0] = sin_i

    for n in range(B_KV_HEADS):
        kh = small_ref[:, SM_K + n * B_HEAD_DIM:SM_K + (n + 1) * B_HEAD_DIM]
        r = lax.rsqrt(jnp.mean(kh * kh, axis=-1, keepdims=True) + EPS)
        kn = kh * r * gk_ref[...]
        k_ref[0, n] = _rope(kn, cos_b, sin_b, B_HEAD_DIM // 8, LANES).astype(BF16)
        vt = small_ref[:, SM_V + n * B_HEAD_DIM:SM_V + (n + 1) * B_HEAD_DIM].T
        for c in range(vt.shape[1] // CK):
            vt_ref[0, n, c] = vt[:, c * CK:(c + 1) * CK].astype(BF16)

    ikp = small_ref[:, SM_IK:SM_IK + LANES]
    lane = lax.broadcasted_iota(jnp.int32, ikp.shape, 1)
    live = lane < IDX_DIM
    mu = jnp.sum(jnp.where(live, ikp, 0.0), axis=-1, keepdims=True) * (1.0 / IDX_DIM)
    d = jnp.where(live, ikp - mu, 0.0)
    var = jnp.sum(d * d, axis=-1, keepdims=True) * (1.0 / IDX_DIM)
    y = d * lax.rsqrt(var + EPS) * lng_ref[...] + lnb_ref[...]
    yr = jnp.where(live, _rope(y, cos_i, sin_i, IDX_DIM // 8, IDX_DIM), 0.0)
    iklo_ref[0] = yr.astype(BF16)
    ikhi_ref[0] = pltpu.roll(yr, IDX_DIM, 1).astype(BF16)

    iw = small_ref[:, SM_IW:SM_IW + LANES] * (IDX_DIM ** -0.5 * IDX_HEADS ** -0.5)
    iwt_ref[0] = iw.T[:IDX_HEADS, :]


def _kvprep(small3, pos3, tabs, gk, lng, lnb):
    bsz, s, _ = small3.shape
    tp = min(512, s)
    row = lambda b, i: (b, i, 0)
    const = lambda b, i: (0, 0)
    tab_spec = pl.BlockSpec((1, LANES), const)
    out_tok = pl.BlockSpec((1, tp, LANES), row)
    out_k = pl.BlockSpec((1, B_KV_HEADS, tp, B_HEAD_DIM), lambda b, i: (b, 0, i, 0))
    out_vt = pl.BlockSpec((1, B_KV_HEADS, tp // CK, B_HEAD_DIM, CK), lambda b, i: (b, 0, i, 0, 0))
    out_iwt = pl.BlockSpec((1, IDX_HEADS, tp), lambda b, i: (b, 0, i))
    return pl.pallas_call(
        _kvprep_kernel,
        grid=(bsz, s // tp),
        in_specs=[pl.BlockSpec((None, tp, BLK), row), pl.BlockSpec((1, tp, 1), row),
                  tab_spec, tab_spec, tab_spec, tab_spec, tab_spec, tab_spec, tab_spec],
        out_specs=[out_k, out_vt, out_tok, out_tok, out_tok, out_tok, out_tok, out_tok, out_iwt],
        out_shape=[
            jax.ShapeDtypeStruct((bsz, B_KV_HEADS, s, B_HEAD_DIM), BF16),
            jax.ShapeDtypeStruct((bsz, B_KV_HEADS, s // CK, B_HEAD_DIM, CK), BF16),
            jax.ShapeDtypeStruct((bsz, s, LANES), BF16),
            jax.ShapeDtypeStruct((bsz, s, LANES), BF16),
            jax.ShapeDtypeStruct((bsz, s, LANES), F32),
            jax.ShapeDtypeStruct((bsz, s, LANES), F32),
            jax.ShapeDtypeStruct((bsz, s, LANES), F32),
            jax.ShapeDtypeStruct((bsz, s, LANES), F32),
            jax.ShapeDtypeStruct((bsz, IDX_HEADS, s), F32),
        ],
        compiler_params=_cparams(("parallel", "parallel")),
        name="kvprep",
    )(small3, pos3, *tabs, gk, lng, lnb)


def _gmlp_kernel(u_ref, v_ref, z_ref, lng_ref, lnb_ref, ws_ref, sbt_ref, o_ref):
    tm = u_ref.shape[0]
    u = jax.nn.gelu(u_ref[...].astype(F32))
    v = jax.nn.gelu(v_ref[...].astype(F32))
    mu = jnp.mean(v, axis=-1, keepdims=True)
    d = v - mu
    var = jnp.mean(d * d, axis=-1, keepdims=True)
    vn = (d * lax.rsqrt(var + EPS) * lng_ref[...] + lnb_ref[...]).astype(BF16)
    gate = u * jax.nn.silu(z_ref[...].astype(F32))
    tri = (lax.broadcasted_iota(jnp.int32, (CHUNK, CHUNK), 1)
           <= lax.broadcasted_iota(jnp.int32, (CHUNK, CHUNK), 0))
    for g in range(A_GROUPS):
        wg = jnp.where(tri, ws_ref[g], 0.0).astype(BF16)
        bias = sbt_ref[:, g:g + 1]
        cols = slice(g * A_GROUP_DIM, (g + 1) * A_GROUP_DIM)
        for c in range(tm // CHUNK):
            rows = slice(c * CHUNK, (c + 1) * CHUNK)
            sg = jnp.dot(wg, vn[rows, cols], preferred_element_type=F32) + bias
            o_ref[rows, cols] = (gate[rows, cols] * sg).astype(BF16)


def _gmlp(proj, lng, lnb, ws, sbt):
    m = proj.shape[0]
    tm = min(512, m)
    col = lambda c: pl.BlockSpec((tm, BLK), lambda i: (i, c))
    full = lambda shape: pl.BlockSpec(shape, lambda i: (0,) * len(shape))
    return pl.pallas_call(
        _gmlp_kernel,
        grid=(m // tm,),
        in_specs=[col(COL_AU), col(COL_AV), col(COL_AZ), full((1, A_WIDTH)), full((1, A_WIDTH)),
                  full((A_GROUPS, CHUNK, CHUNK)), full((CHUNK, A_GROUPS))],
        out_specs=pl.BlockSpec((tm, A_WIDTH), lambda i: (i, 0)),
        out_shape=jax.ShapeDtypeStruct((m, A_WIDTH), BF16),
        compiler_params=_cparams(("parallel",)),
        name="gmlp",
    )(proj, proj, proj, lng, lnb, ws, sbt)


def _memkv_kernel(mem_ref, g_ref, w_ref, gk_ref, km_ref, vm_ref):
    x = mem_ref[0]
    ms = jnp.mean(x * x, axis=-1, keepdims=True)
    h = (x * lax.rsqrt(ms + EPS) * g_ref[...]).astype(BF16)
    kv = jnp.dot(h, w_ref[...], preferred_element_type=F32)
    for hd in range(M_HEADS):
        kh = kv[:, hd * M_HEAD_DIM:(hd + 1) * M_HEAD_DIM]
        r = lax.rsqrt(jnp.mean(kh * kh, axis=-1, keepdims=True) + EPS)
        km_ref[0, hd] = (kh * r * gk_ref[...]).astype(BF16)
        vm_ref[0, hd] = kv[:, M_WIDTH + hd * M_HEAD_DIM:M_WIDTH + (hd + 1) * M_HEAD_DIM].astype(BF16)


def _memkv(mem, gain, w_kv, gk):
    bsz, ml, _ = mem.shape
    out = pl.BlockSpec((1, M_HEADS, ml, M_HEAD_DIM), lambda b: (b, 0, 0, 0))
    shp = jax.ShapeDtypeStruct((bsz, M_HEADS, ml, M_HEAD_DIM), BF16)
    return pl.pallas_call(
        _memkv_kernel,
        grid=(bsz,),
        in_specs=[pl.BlockSpec((1, ml, D_MODEL), lambda b: (b, 0, 0)),
                  pl.BlockSpec((1, D_MODEL), lambda b: (0, 0)),
                  pl.BlockSpec((D_MODEL, 2 * M_WIDTH), lambda b: (0, 0)),
                  pl.BlockSpec((1, M_HEAD_DIM), lambda b: (0, 0))],
        out_specs=[out, out],
        out_shape=[shp, shp],
        compiler_params=_cparams(("parallel",)),
        name="memkv",
    )(mem, gain, w_kv, gk)


def _memattn_kernel(q_ref, z_ref, km_ref, vm_ref, gq_ref, o_ref):
    qscale = M_HEAD_DIM ** -0.5 * LOG2E
    for hd in range(M_HEADS):
        cols = slice(hd * M_HEAD_DIM, (hd + 1) * M_HEAD_DIM)
        q = q_ref[:, cols].astype(F32)
        r = lax.rsqrt(jnp.mean(q * q, axis=-1, keepdims=True) + EPS)
        qn = (q * r * gq_ref[...] * qscale).astype(BF16)
        lg = _dot_nt(qn, km_ref[0, hd])
        p = jnp.exp2(lg - jnp.max(lg, axis=-1, keepdims=True))
        l = jnp.sum(p, axis=-1, keepdims=True)
        o = jnp.dot(p.astype(BF16), vm_ref[0, hd], preferred_element_type=F32) / l
        o_ref[:, cols] = (o * jax.nn.silu(z_ref[:, cols].astype(F32))).astype(BF16)


def _memattn(proj, km, vm, gq, s):
    m = proj.shape[0]
    tm = min(512, s)
    per_b = s // tm
    ml = km.shape[2]
    kv_spec = pl.BlockSpec((1, M_HEADS, ml, M_HEAD_DIM), lambda i: (i // per_b, 0, 0, 0))
    return pl.pallas_call(
        _memattn_kernel,
        grid=(m // tm,),
        in_specs=[pl.BlockSpec((tm, BLK), lambda i: (i, COL_MQ)),
                  pl.BlockSpec((tm, BLK), lambda i: (i, COL_MZ)),
                  kv_spec, kv_spec,
                  pl.BlockSpec((1, M_HEAD_DIM), lambda i: (0, 0))],
        out_specs=pl.BlockSpec((tm, M_WIDTH), lambda i: (i, 0)),
        out_shape=jax.ShapeDtypeStruct((m, M_WIDTH), BF16),
        compiler_params=_cparams(("parallel",)),
        name="memattn",
    )(proj, proj, km, vm, gq)


def _dsa_kernel(topk, q_ref, z_ref, iq_ref, cb_ref, sb_ref, ci_ref, si_ref, iwt_ref,
                k_ref, vt_ref, iklo_ref, ikhi_ref, gq_ref, tri_ref, o_ref,
                sc_ref, bias_ref, lg_ref, qs_ref, iqs_ref, acc_ref):
    qb = pl.program_id(1)
    nck = (qb * TQ + TQ + CK - 1) // CK
    t_lane = qb * TQ + lax.broadcasted_iota(jnp.int32, (1, TQ), 1)
    gw = B_GROUP * TQ

    def fold8(a):
        return a.reshape(a.shape[0] // 8, 8, a.shape[1])

    cos_b, sin_b = cb_ref[0], sb_ref[0]
    cos_i, sin_i = ci_ref[0], si_ref[0]
    qscale = B_HEAD_DIM ** -0.5 * LOG2E
    for h in range(B_HEADS):
        slab = q_ref[:, h * B_HEAD_DIM:(h + 1) * B_HEAD_DIM].astype(F32)
        r = lax.rsqrt(jnp.mean(slab * slab, axis=-1, keepdims=True) + EPS)
        qr = _rope(slab * r * gq_ref[...], cos_b, sin_b, B_HEAD_DIM // 8, LANES) * qscale
        g = h % B_GROUP
        qs_ref[h // B_GROUP, g * TQ:(g + 1) * TQ, :] = qr.astype(BF16)
    for j in range(IDX_HEADS // 2):
        slab = iq_ref[:, j * LANES:(j + 1) * LANES].astype(F32)
        iqs_ref[j // 2, (j % 2) * TQ:(j % 2 + 1) * TQ, :] = (
            _rope(slab, cos_i, sin_i, IDX_DIM // 8, IDX_DIM).astype(BF16))
    wt = iwt_ref[0]

    def idx_body(c, carry):
        mn8, mx8 = carry
        off = pl.multiple_of(c * CK, CK)
        acc = jnp.zeros((CK, TQ), F32)
        for jj in range(IDX_HEADS // 4):
            rhs = iqs_ref[jj]
            for half, keys_ref in enumerate((iklo_ref, ikhi_ref)):
                d = _dot_nt(keys_ref[0, pl.ds(off, CK), :], rhs)
                ha, hb = 4 * jj + half, 4 * jj + 2 + half
                acc = (acc + jnp.maximum(d[:, :TQ], 0.0) * wt[ha:ha + 1, :]
                       + jnp.maximum(d[:, TQ:], 0.0) * wt[hb:hb + 1, :])
        key = off + lax.broadcasted_iota(jnp.int32, (CK, TQ), 0)
        causal = key <= t_lane
        sc = jnp.where(causal, acc, -jnp.inf)
        sc_ref[pl.ds(off, CK), :] = sc
        mn8 = jnp.minimum(mn8, jnp.min(fold8(jnp.where(causal, acc, jnp.inf)), axis=0))
        mx8 = jnp.maximum(mx8, jnp.max(fold8(sc), axis=0))
        return mn8, mx8

    mn8, mx8 = lax.fori_loop(0, nck, idx_body,
                             (jnp.full((8, TQ), jnp.inf, F32), jnp.full((8, TQ), -jnp.inf, F32)))
    row_min = jnp.min(mn8, axis=0, keepdims=True)
    row_max = jnp.max(mx8, axis=0, keepdims=True)

    def count_rows(pred):
        def body(c, cnt8):
            blk = sc_ref[pl.ds(pl.multiple_of(c * CK, CK), CK), :]
            return cnt8 + jnp.sum(fold8(jnp.where(pred(blk), 1.0, 0.0)), axis=0)

        return jnp.sum(lax.fori_loop(0, nck, body, jnp.zeros((8, TQ), F32)), axis=0, keepdims=True)

    def count_ge(thr):
        return count_rows(lambda blk: blk >= thr)

    kf = float(topk)
    all_rows = (t_lane + 1) <= topk
    fin0 = jnp.where(all_rows, 1.0, 0.0)
    thr0 = jnp.where(all_rows, -F32_MAX, row_max)

    def s_cond(st):
        it, lo, hi, thr, fin = st
        return jnp.logical_and(it < MAX_SEARCH_ITERS, jnp.min(fin) < 0.5)

    def s_body(st):
        it, lo, hi, thr, fin = st
        mid = lo * 0.5 + hi * 0.5
        stuck = jnp.logical_or(mid <= lo, mid >= hi)
        probe = jnp.where(stuck, hi, mid)
        cnt = count_ge(probe)
        active = fin < 0.5
        hit = cnt == kf
        end_thr = jnp.where(jnp.logical_and(stuck, cnt < kf), lo, probe)
        ends = jnp.logical_and(active, jnp.logical_or(hit, stuck))
        thr = jnp.where(ends, end_thr, thr)
        fin = jnp.where(ends, 1.0, fin)
        lo = jnp.where(jnp.logical_and(active, cnt > kf), probe, lo)
        hi = jnp.where(jnp.logical_and(active, cnt < kf), probe, hi)
        return it + 1, lo, hi, thr, fin

    _, _, _, thr, _ = lax.while_loop(s_cond, s_body, (jnp.int32(0), row_min, row_max, thr0, fin0))

    n_gt = count_rows(lambda blk: blk > thr)
    need = jnp.where(all_rows, F32_MAX, kf - n_gt)

    def mask_body(c, seen):
        off = pl.multiple_of(c * CK, CK)
        blk = sc_ref[pl.ds(off, CK), :]
        eq = blk == thr
        pref = jnp.dot(tri_ref[...], jnp.where(eq, 1.0, 0.0).astype(BF16),
                       preferred_element_type=F32) + seen
        keep = jnp.logical_or(blk > thr, jnp.logical_and(eq, pref <= need))
        bias_ref[pl.ds(off, CK), :] = jnp.where(keep, 0.0, NEG_BIG)
        return pref[CK - 1:CK, :]

    lax.fori_loop(0, nck, mask_body, jnp.zeros((1, TQ), F32))

    for n in range(B_KV_HEADS):
        def logit_body(c, mx8, n=n):
            off = pl.multiple_of(c * CK, CK)
            lg = _dot_nt(k_ref[0, n, pl.ds(off, CK), :], qs_ref[n])
            b = bias_ref[pl.ds(off, CK), :]
            lg = jnp.concatenate([lg[:, g * TQ:(g + 1) * TQ] + b for g in range(B_GROUP)], axis=1)
            lg_ref[pl.ds(off, CK), :] = lg
            return jnp.maximum(mx8, jnp.max(fold8(lg), axis=0))

        mx8 = lax.fori_loop(0, nck, logit_body, jnp.full((8, gw), NEG_BIG, F32))
        m = jnp.max(mx8, axis=0, keepdims=True)
        acc_ref[...] = jnp.zeros(acc_ref.shape, F32)

        def pv_body(c, l8, n=n, m=m):
            off = pl.multiple_of(c * CK, CK)
            p = jnp.exp2(lg_ref[pl.ds(off, CK), :] - m)
            acc_ref[...] += jnp.dot(vt_ref[0, n, c], p.astype(BF16), preferred_element_type=F32)
            return l8 + jnp.sum(fold8(p), axis=0)

        l8 = lax.fori_loop(0, nck, pv_body, jnp.zeros((8, gw), F32))
        o_t = acc_ref[...] / jnp.sum(l8, axis=0, keepdims=True)
        for g in range(B_GROUP):
            cols = slice((n * B_GROUP + g) * B_HEAD_DIM, (n * B_GROUP + g + 1) * B_HEAD_DIM)
            o = o_t[:, g * TQ:(g + 1) * TQ].T
            o_ref[:, cols] = (o * jax.nn.silu(z_ref[:, cols].astype(F32))).astype(BF16)


def _dsa(proj, tabs, iwt, k, vt, iklo, ikhi, gq, bsz, s):
    nq = s // TQ
    topk = min(TOPK_MAX, s // 4)
    gw = B_GROUP * TQ
    rowblk = lambda c: pl.BlockSpec((TQ, BLK), lambda b, i: (b * nq + i, c))
    tok = pl.BlockSpec((1, TQ, LANES), lambda b, i: (b, i, 0))
    ik = pl.BlockSpec((1, s, LANES), lambda b, i: (b, 0, 0))
    tri = jnp.tril(jnp.ones((CK, CK), BF16))
    return pl.pallas_call(
        functools.partial(_dsa_kernel, topk),
        grid=(bsz, nq),
        in_specs=[rowblk(COL_BQ), rowblk(COL_BZ), rowblk(COL_IQ), tok, tok, tok, tok,
                  pl.BlockSpec((1, IDX_HEADS, TQ), lambda b, i: (b, 0, i)),
                  pl.BlockSpec((1, B_KV_HEADS, s, B_HEAD_DIM), lambda b, i: (b, 0, 0, 0)),
                  pl.BlockSpec((1, B_KV_HEADS, s // CK, B_HEAD_DIM, CK), lambda b, i: (b, 0, 0, 0, 0)),
                  ik, ik,
                  pl.BlockSpec((1, B_HEAD_DIM), lambda b, i: (0, 0)),
                  pl.BlockSpec((CK, CK), lambda b, i: (0, 0))],
        out_specs=pl.BlockSpec((TQ, B_WIDTH), lambda b, i: (b * nq + i, 0)),
        out_shape=jax.ShapeDtypeStruct((bsz * s, B_WIDTH), BF16),
        scratch_shapes=[
            pltpu.VMEM((s, TQ), F32),
            pltpu.VMEM((s, TQ), F32),
            pltpu.VMEM((s, gw), F32),
            pltpu.VMEM((B_KV_HEADS, gw, B_HEAD_DIM), BF16),
            pltpu.VMEM((IDX_HEADS // 4, 2 * TQ, LANES), BF16),
            pltpu.VMEM((B_HEAD_DIM, gw), F32),
        ],
        compiler_params=_cparams(("parallel", "arbitrary")),
        name="dsa",
    )(proj, proj, proj, *tabs, iwt, k, vt, iklo, ikhi, gq, tri)


def _merge_kernel(ta_ref, tb_ref, tm_ref, ga_ref, gb_ref, gm_ref, x_ref,
                  wa_ref, wb_ref, wm_ref, wo_ref, o_ref):
    def branch(t_ref, g_ref, w_ref):
        y = jnp.dot(t_ref[...], w_ref[...], preferred_element_type=F32)
        return jax.nn.sigmoid(g_ref[...].astype(F32)) * y

    merged = branch(ta_ref, ga_ref, wa_ref) + branch(tb_ref, gb_ref, wb_ref) + branch(tm_ref, gm_ref, wm_ref)
    o_ref[...] = x_ref[...] + jnp.dot(merged.astype(BF16), wo_ref[...], preferred_element_type=F32)


def _merge(ta, tb, tmem, proj, x2, wa, wb, wm, wo):
    m = x2.shape[0]
    tm = min(256, m)
    act = pl.BlockSpec((tm, BLK), lambda i: (i, 0))
    gate = lambda c: pl.BlockSpec((tm, D_MODEL), lambda i: (i, c // 2))
    wide = pl.BlockSpec((tm, D_MODEL), lambda i: (i, 0))
    wbr = pl.BlockSpec((BLK, D_MODEL), lambda i: (0, 0), pipeline_mode=pl.Buffered(1))
    wout = pl.BlockSpec((D_MODEL, D_MODEL), lambda i: (0, 0), pipeline_mode=pl.Buffered(1))
    return pl.pallas_call(
        _merge_kernel,
        grid=(m // tm,),
        in_specs=[act, act, act, gate(COL_GA), gate(COL_GB), gate(COL_GM), wide, wbr, wbr, wbr, wout],
        out_specs=wide,
        out_shape=jax.ShapeDtypeStruct((m, D_MODEL), F32),
        compiler_params=_cparams(("parallel",)),
        name="merge",
    )(ta, tb, tmem, proj, proj, proj, x2, wa, wb, wm, wo)


def _pack_w_in(w):
    offs = np.concatenate([[0], np.cumsum(SPLIT_SIZES)])
    (a_u, a_v, a_z, b_q, b_k, b_v, b_z, i_q, i_k, i_w, m_q, m_z, g_a, g_b, g_m) = [
        w[:, int(offs[i]):int(offs[i + 1])] for i in range(len(SPLIT_SIZES))]
    pad = lambda n: jnp.zeros((w.shape[0], n), w.dtype)
    small = [b_k, b_v, i_k, pad(SM_IW - SM_IK - IDX_DIM), i_w, pad(BLK - SM_IW - IDX_HEADS)]
    cols = [a_u, a_v, a_z, b_q, b_z, i_q, m_q, m_z, g_a, g_b, g_m] + small
    return jnp.concatenate(cols, axis=1).astype(BF16)


def _rope_tables():
    def build(rot, period):
        half = rot // 2
        inv = ROPE_THETA ** (-np.arange(half, dtype=np.float32) / half)
        lane = np.arange(LANES) % period
        freq = np.where(lane < rot, inv[lane % half], 0.0).astype(np.float32)
        sign = np.where(lane < half, -1.0, np.where(lane < rot, 1.0, 0.0)).astype(np.float32)
        return jnp.asarray(freq[None, :]), jnp.asarray(sign[None, :])

    fb, sgb = build(B_HEAD_DIM // 4, LANES)
    fi, sgi = build(IDX_DIM // 4, IDX_DIM)
    return fb, sgb, fi, sgi


def _layer(x, mem, positions, norm_gain, w_in, gmlp_ln_gain, gmlp_ln_bias, spatial_w, spatial_b,
           w_branch_a, q_norm_gain, k_norm_gain, idx_k_ln_gain, idx_k_ln_bias, w_branch_b,
           mem_norm_gain, w_mem_kv, mem_q_norm_gain, mem_k_norm_gain, w_branch_m, w_out):
    bsz, s, _ = x.shape
    m = bsz * s
    row = lambda a: a.reshape(1, -1).astype(F32)
    pad_lanes = lambda a: jnp.pad(a.reshape(1, -1).astype(F32), ((0, 0), (0, LANES - a.shape[-1])))
    x2 = x.reshape(m, D_MODEL)

    proj, small = _inproj(x2, row(norm_gain), _pack_w_in(w_in))
    k, vt, iklo, ikhi, cb, sb, ci, si, iwt = _kvprep(
        small.reshape(bsz, s, BLK), positions.reshape(bsz, s, 1).astype(jnp.int32), _rope_tables(),
        row(k_norm_gain), pad_lanes(idx_k_ln_gain), pad_lanes(idx_k_ln_bias))

    t_a = _gmlp(proj, row(gmlp_ln_gain), row(gmlp_ln_bias), spatial_w.astype(F32), spatial_b.T.astype(F32))
    km, vm = _memkv(mem, row(mem_norm_gain), w_mem_kv.astype(BF16), row(mem_k_norm_gain))
    t_m = _memattn(proj, km, vm, row(mem_q_norm_gain), s)
    t_b = _dsa(proj, (cb, sb, ci, si), iwt, k, vt, iklo, ikhi, row(q_norm_gain), bsz, s)

    out = _merge(t_a, t_b, t_m, proj, x2, w_branch_a.astype(BF16), w_branch_b.astype(BF16),
                 w_branch_m.astype(BF16), w_out.astype(BF16))
    return out.reshape(bsz, s, D_MODEL)


def kernel(x, mem, positions, norm_gain, w_in, gmlp_ln_gain, gmlp_ln_bias, spatial_w, spatial_b, w_branch_a, q_norm_gain, k_norm_gain, idx_k_ln_gain, idx_k_ln_bias, w_branch_b, mem_norm_gain, w_mem_kv, mem_q_norm_gain, mem_k_norm_gain, w_branch_m, w_out):
    for l in range(norm_gain.shape[0]):
        x = _layer(x, mem, positions, norm_gain[l], w_in[l], gmlp_ln_gain[l], gmlp_ln_bias[l],
                   spatial_w[l], spatial_b[l], w_branch_a[l], q_norm_gain[l], k_norm_gain[l],
                   idx_k_ln_gain[l], idx_k_ln_bias[l], w_branch_b[l], mem_norm_gain[l], w_mem_kv[l],
                   mem_q_norm_gain[l], mem_k_norm_gain[l], w_branch_m[l], w_out[l])
    return x
```

```python
import functools

import numpy as np
import jax
import jax.numpy as jnp
from jax import lax
from jax.experimental import pallas as pl
from jax.experimental.pallas import tpu as pltpu

F32 = jnp.float32
BF16 = jnp.bfloat16

D_MODEL = 2048
ROPE_THETA = 500000.0
EPS = 1e-6
A_GROUPS = 8
A_GROUP_DIM = 128
A_WIDTH = A_GROUPS * A_GROUP_DIM
CHUNK = 128
B_HEADS = 8
B_KV_HEADS = 2
B_GROUP = B_HEADS // B_KV_HEADS
B_HEAD_DIM = 128
B_WIDTH = B_HEADS * B_HEAD_DIM
IDX_HEADS = 16
IDX_DIM = 64
TOPK_MAX = 256
M_HEADS = 4
M_HEAD_DIM = 256
M_WIDTH = M_HEADS * M_HEAD_DIM

SPLIT_SIZES = (
    A_WIDTH, A_WIDTH, A_WIDTH,
    B_WIDTH, B_KV_HEADS * B_HEAD_DIM, B_KV_HEADS * B_HEAD_DIM, B_WIDTH,
    IDX_HEADS * IDX_DIM, IDX_DIM, IDX_HEADS,
    M_WIDTH, M_WIDTH,
    D_MODEL, D_MODEL, D_MODEL,
)

LANES = 128
BLK = 1024
COL_AU, COL_AV, COL_AZ, COL_BQ, COL_BZ, COL_IQ, COL_MQ, COL_MZ = range(8)
COL_GA, COL_GB, COL_GM = 8, 10, 12
NB_MAIN = 14
SM_K, SM_V, SM_IK, SM_IW = 0, 256, 512, 640

VMEM_LIMIT = 56 * 1024 * 1024
LOG2E = 1.4426950408889634
NEG_BIG = -1e30
F32_MAX = 3.4028234663852886e38

TQ = 128
CK = 256
CB = 512
COUNT_WAYS = 8
SEARCH_MARGIN = 0.05
SEARCH_FIXED_PASSES = 8
SEARCH_INTERP_PASSES = 16
MAX_SEARCH_ITERS = 400


def _cparams(sem):
    return pltpu.CompilerParams(dimension_semantics=sem, vmem_limit_bytes=VMEM_LIMIT)


def _dot_nt(a, b):
    return lax.dot_general(a, b, (((1,), (1,)), ((), ())), preferred_element_type=F32)


def _rope(x, cos_t, sin_t, half, period):
    lane = lax.broadcasted_iota(jnp.int32, x.shape, 1) % period
    rolled = jnp.where(lane < half, pltpu.roll(x, LANES - half, 1), pltpu.roll(x, half, 1))
    return x * cos_t + rolled * sin_t


def _inproj_kernel(x_ref, g_ref, w_ref, proj_ref, small_ref, h_ref):
    n = pl.program_id(1)

    @pl.when(n == 0)
    def _():
        x = x_ref[...]
        ms = jnp.mean(x * x, axis=-1, keepdims=True)
        h_ref[...] = (x * lax.rsqrt(ms + EPS) * g_ref[...]).astype(BF16)

    acc = jnp.dot(h_ref[...], w_ref[...], preferred_element_type=F32)

    @pl.when(n < NB_MAIN)
    def _():
        proj_ref[...] = acc.astype(BF16)

    @pl.when(n == NB_MAIN)
    def _():
        small_ref[...] = acc


def _inproj(x2, gain, w_packed):
    m = x2.shape[0]
    tm = min(1024, m)
    nb = w_packed.shape[1] // BLK
    return pl.pallas_call(
        _inproj_kernel,
        grid=(m // tm, nb),
        in_specs=[
            pl.BlockSpec((tm, D_MODEL), lambda i, n: (i, 0)),
            pl.BlockSpec((1, D_MODEL), lambda i, n: (0, 0)),
            pl.BlockSpec((D_MODEL, BLK), lambda i, n: (0, n)),
        ],
        out_specs=[
            pl.BlockSpec((tm, BLK), lambda i, n: (i, jnp.minimum(n, NB_MAIN - 1))),
            pl.BlockSpec((tm, BLK), lambda i, n: (i, 0)),
        ],
        out_shape=[
            jax.ShapeDtypeStruct((m, NB_MAIN * BLK), BF16),
            jax.ShapeDtypeStruct((m, BLK), F32),
        ],
        scratch_shapes=[pltpu.VMEM((tm, D_MODEL), BF16)],
        compiler_params=_cparams(("parallel", "arbitrary")),
        name="inproj",
    )(x2, gain, w_packed)


def _kvprep_kernel(small_ref, pos_ref, fb_ref, sgb_ref, fi_ref, sgi_ref, gk_ref, lng_ref, lnb_ref,
                   k_ref, vt_ref, iklo_ref, ikhi_ref, cb_ref, sb_ref, ci_ref, si_ref, iwt_ref):
    pos = pos_ref[0].astype(F32)
    ang_b = pos * fb_ref[...]
    cos_b = jnp.cos(ang_b)
    sin_b = jnp.sin(ang_b) * sgb_ref[...]
    ang_i = pos * fi_ref[...]
    cos_i = jnp.cos(ang_i)
    sin_i = jnp.sin(ang_i) * sgi_ref[...]
    cb_ref[0] = cos_b
    sb_ref[0] = sin_b
    ci_ref[0] = cos_i
    si_ref[0] = sin_i

    for n in range(B_KV_HEADS):
        kh = small_ref[:, SM_K + n * B_HEAD_DIM:SM_K + (n + 1) * B_HEAD_DIM]
        r = lax.rsqrt(jnp.mean(kh * kh, axis=-1, keepdims=True) + EPS)
        kn = kh * r * gk_ref[...]
        k_ref[0, n] = _rope(kn, cos_b, sin_b, B_HEAD_DIM // 8, LANES).astype(BF16)
        vt = small_ref[:, SM_V + n * B_HEAD_DIM:SM_V + (n + 1) * B_HEAD_DIM].T
        for c in range(vt.shape[1] // CB):
            vt_ref[0, n, c] = vt[:, c * CB:(c + 1) * CB].astype(BF16)

    ikp = small_ref[:, SM_IK:SM_IK + LANES]
    lane = lax.broadcasted_iota(jnp.int32, ikp.shape, 1)
    live = lane < IDX_DIM
    mu = jnp.sum(jnp.where(live, ikp, 0.0), axis=-1, keepdims=True) * (1.0 / IDX_DIM)
    d = jnp.where(live, ikp - mu, 0.0)
    var = jnp.sum(d * d, axis=-1, keepdims=True) * (1.0 / IDX_DIM)
    y = d * lax.rsqrt(var + EPS) * lng_ref[...] + lnb_ref[...]
    yr = jnp.where(live, _rope(y, cos_i, sin_i, IDX_DIM // 8, IDX_DIM), 0.0)
    iklo_ref[0] = yr.astype(BF16)
    ikhi_ref[0] = pltpu.roll(yr, IDX_DIM, 1).astype(BF16)

    iw = small_ref[:, SM_IW:SM_IW + LANES] * (IDX_DIM ** -0.5 * IDX_HEADS ** -0.5)
    iwt_ref[0] = iw.T[:IDX_HEADS, :]


def _kvprep(small3, pos3, tabs, gk, lng, lnb):
    bsz, s, _ = small3.shape
    tp = min(512, s)
    row = lambda b, i: (b, i, 0)
    const = lambda b, i: (0, 0)
    tab_spec = pl.BlockSpec((1, LANES), const)
    out_tok = pl.BlockSpec((1, tp, LANES), row)
    out_k = pl.BlockSpec((1, B_KV_HEADS, tp, B_HEAD_DIM), lambda b, i: (b, 0, i, 0))
    out_vt = pl.BlockSpec((1, B_KV_HEADS, tp // CB, B_HEAD_DIM, CB), lambda b, i: (b, 0, i, 0, 0))
    out_iwt = pl.BlockSpec((1, IDX_HEADS, tp), lambda b, i: (b, 0, i))
    return pl.pallas_call(
        _kvprep_kernel,
        grid=(bsz, s // tp),
        in_specs=[pl.BlockSpec((None, tp, BLK), row), pl.BlockSpec((1, tp, 1), row),
                  tab_spec, tab_spec, tab_spec, tab_spec, tab_spec, tab_spec, tab_spec],
        out_specs=[out_k, out_vt, out_tok, out_tok, out_tok, out_tok, out_tok, out_tok, out_iwt],
        out_shape=[
            jax.ShapeDtypeStruct((bsz, B_KV_HEADS, s, B_HEAD_DIM), BF16),
            jax.ShapeDtypeStruct((bsz, B_KV_HEADS, s // CB, B_HEAD_DIM, CB), BF16),
            jax.ShapeDtypeStruct((bsz, s, LANES), BF16),
            jax.ShapeDtypeStruct((bsz, s, LANES), BF16),
            jax.ShapeDtypeStruct((bsz, s, LANES), F32),
            jax.ShapeDtypeStruct((bsz, s, LANES), F32),
            jax.ShapeDtypeStruct((bsz, s, LANES), F32),
            jax.ShapeDtypeStruct((bsz, s, LANES), F32),
            jax.ShapeDtypeStruct((bsz, IDX_HEADS, s), F32),
        ],
        compiler_params=_cparams(("parallel", "parallel")),
        name="kvprep",
    )(small3, pos3, *tabs, gk, lng, lnb)


def _gmlp_kernel(u_ref, v_ref, z_ref, lng_ref, lnb_ref, ws_ref, sbt_ref, o_ref):
    tm = u_ref.shape[0]
    u = jax.nn.gelu(u_ref[...].astype(F32))
    v = jax.nn.gelu(v_ref[...].astype(F32))
    mu = jnp.mean(v, axis=-1, keepdims=True)
    d = v - mu
    var = jnp.mean(d * d, axis=-1, keepdims=True)
    vn = (d * lax.rsqrt(var + EPS) * lng_ref[...] + lnb_ref[...]).astype(BF16)
    gate = u * jax.nn.silu(z_ref[...].astype(F32))
    tri = (lax.broadcasted_iota(jnp.int32, (CHUNK, CHUNK), 1)
           <= lax.broadcasted_iota(jnp.int32, (CHUNK, CHUNK), 0))
    for g in range(A_GROUPS):
        wg = jnp.where(tri, ws_ref[g], 0.0).astype(BF16)
        bias = sbt_ref[:, g:g + 1]
        cols = slice(g * A_GROUP_DIM, (g + 1) * A_GROUP_DIM)
        for c in range(tm // CHUNK):
            rows = slice(c * CHUNK, (c + 1) * CHUNK)
            sg = jnp.dot(wg, vn[rows, cols], preferred_element_type=F32) + bias
            o_ref[rows, cols] = (gate[rows, cols] * sg).astype(BF16)


def _gmlp(proj, lng, lnb, ws, sbt):
    m = proj.shape[0]
    tm = min(512, m)
    col = lambda c: pl.BlockSpec((tm, BLK), lambda i: (i, c))
    full = lambda shape: pl.BlockSpec(shape, lambda i: (0,) * len(shape))
    return pl.pallas_call(
        _gmlp_kernel,
        grid=(m // tm,),
        in_specs=[col(COL_AU), col(COL_AV), col(COL_AZ), full((1, A_WIDTH)), full((1, A_WIDTH)),
                  full((A_GROUPS, CHUNK, CHUNK)), full((CHUNK, A_GROUPS))],
        out_specs=pl.BlockSpec((tm, A_WIDTH), lambda i: (i, 0)),
        out_shape=jax.ShapeDtypeStruct((m, A_WIDTH), BF16),
        compiler_params=_cparams(("parallel",)),
        name="gmlp",
    )(proj, proj, proj, lng, lnb, ws, sbt)


def _memkv_kernel(mem_ref, g_ref, w_ref, gk_ref, km_ref, vm_ref):
    x = mem_ref[0]
    ms = jnp.mean(x * x, axis=-1, keepdims=True)
    h = (x * lax.rsqrt(ms + EPS) * g_ref[...]).astype(BF16)
    kv = jnp.dot(h, w_ref[...], preferred_element_type=F32)
    for hd in range(M_HEADS):
        kh = kv[:, hd * M_HEAD_DIM:(hd + 1) * M_HEAD_DIM]
        r = lax.rsqrt(jnp.mean(kh * kh, axis=-1, keepdims=True) + EPS)
        km_ref[0, hd] = (kh * r * gk_ref[...]).astype(BF16)
        vm_ref[0, hd] = kv[:, M_WIDTH + hd * M_HEAD_DIM:M_WIDTH + (hd + 1) * M_HEAD_DIM].astype(BF16)


def _memkv(mem, gain, w_kv, gk):
    bsz, ml, _ = mem.shape
    out = pl.BlockSpec((1, M_HEADS, ml, M_HEAD_DIM), lambda b: (b, 0, 0, 0))
    shp = jax.ShapeDtypeStruct((bsz, M_HEADS, ml, M_HEAD_DIM), BF16)
    return pl.pallas_call(
        _memkv_kernel,
        grid=(bsz,),
        in_specs=[pl.BlockSpec((1, ml, D_MODEL), lambda b: (b, 0, 0)),
                  pl.BlockSpec((1, D_MODEL), lambda b: (0, 0)),
                  pl.BlockSpec((D_MODEL, 2 * M_WIDTH), lambda b: (0, 0)),
                  pl.BlockSpec((1, M_HEAD_DIM), lambda b: (0, 0))],
        out_specs=[out, out],
        out_shape=[shp, shp],
        compiler_params=_cparams(("parallel",)),
        name="memkv",
    )(mem, gain, w_kv, gk)


def _memattn_kernel(q_ref, z_ref, km_ref, vm_ref, gq_ref, o_ref):
    qscale = M_HEAD_DIM ** -0.5 * LOG2E
    for hd in range(M_HEADS):
        cols = slice(hd * M_HEAD_DIM, (hd + 1) * M_HEAD_DIM)
        q = q_ref[:, cols].astype(F32)
        r = lax.rsqrt(jnp.mean(q * q, axis=-1, keepdims=True) + EPS)
        qn = (q * r * gq_ref[...] * qscale).astype(BF16)
        lg = _dot_nt(qn, km_ref[0, hd])
        p = jnp.exp2(lg - jnp.max(lg, axis=-1, keepdims=True))
        l = jnp.sum(p, axis=-1, keepdims=True)
        o = jnp.dot(p.astype(BF16), vm_ref[0, hd], preferred_element_type=F32) / l
        o_ref[:, cols] = (o * jax.nn.silu(z_ref[:, cols].astype(F32))).astype(BF16)


def _memattn(proj, km, vm, gq, s):
    m = proj.shape[0]
    tm = min(512, s)
    per_b = s // tm
    ml = km.shape[2]
    kv_spec = pl.BlockSpec((1, M_HEADS, ml, M_HEAD_DIM), lambda i: (i // per_b, 0, 0, 0))
    return pl.pallas_call(
        _memattn_kernel,
        grid=(m // tm,),
        in_specs=[pl.BlockSpec((tm, BLK), lambda i: (i, COL_MQ)),
                  pl.BlockSpec((tm, BLK), lambda i: (i, COL_MZ)),
                  kv_spec, kv_spec,
                  pl.BlockSpec((1, M_HEAD_DIM), lambda i: (0, 0))],
        out_specs=pl.BlockSpec((tm, M_WIDTH), lambda i: (i, 0)),
        out_shape=jax.ShapeDtypeStruct((m, M_WIDTH), BF16),
        compiler_params=_cparams(("parallel",)),
        name="memattn",
    )(proj, proj, km, vm, gq)


def _dsa_kernel(topk, q_ref, z_ref, iq_ref, cb_ref, sb_ref, ci_ref, si_ref, iwt_ref,
                k_ref, vt_ref, iklo_ref, ikhi_ref, gq_ref, tri_ref, o_ref,
                sc_ref, bias_ref, lg_ref, qs_ref, iqs_ref, acc_ref):
    qb = pl.program_id(1)
    nck = (qb * TQ + TQ + CK - 1) // CK
    nbig = (qb * TQ + TQ + CB - 1) // CB
    t_lane = qb * TQ + lax.broadcasted_iota(jnp.int32, (1, TQ), 1)
    gw = B_GROUP * TQ

    def fold8(a, ways=1):
        return a.reshape(a.shape[0] // (8 * ways), 8 * ways, a.shape[1])

    cos_b, sin_b = cb_ref[0], sb_ref[0]
    cos_i, sin_i = ci_ref[0], si_ref[0]
    qscale = B_HEAD_DIM ** -0.5 * LOG2E
    for h in range(B_HEADS):
        slab = q_ref[:, h * B_HEAD_DIM:(h + 1) * B_HEAD_DIM].astype(F32)
        r = lax.rsqrt(jnp.mean(slab * slab, axis=-1, keepdims=True) + EPS)
        qr = _rope(slab * r * gq_ref[...], cos_b, sin_b, B_HEAD_DIM // 8, LANES) * qscale
        g = h % B_GROUP
        qs_ref[h // B_GROUP, g * TQ:(g + 1) * TQ, :] = qr.astype(BF16)
    for j in range(IDX_HEADS // 2):
        slab = iq_ref[:, j * LANES:(j + 1) * LANES].astype(F32)
        iqs_ref[j // 2, (j % 2) * TQ:(j % 2 + 1) * TQ, :] = (
            _rope(slab, cos_i, sin_i, IDX_DIM // 8, IDX_DIM).astype(BF16))
    wt = iwt_ref[0]

    def idx_body(c, carry):
        mn8, mx8, s1, s2 = carry
        off = pl.multiple_of(c * CK, CK)
        acc = jnp.zeros((CK, TQ), F32)
        for jj in range(IDX_HEADS // 4):
            rhs = iqs_ref[jj]
            for half, keys_ref in enumerate((iklo_ref, ikhi_ref)):
                d = _dot_nt(keys_ref[0, pl.ds(off, CK), :], rhs)
                ha, hb = 4 * jj + half, 4 * jj + 2 + half
                acc = (acc + jnp.maximum(d[:, :TQ], 0.0) * wt[ha:ha + 1, :]
                       + jnp.maximum(d[:, TQ:], 0.0) * wt[hb:hb + 1, :])
        key = off + lax.broadcasted_iota(jnp.int32, (CK, TQ), 0)
        causal = key <= t_lane
        sc = jnp.where(causal, acc, -jnp.inf)
        sc_ref[pl.ds(off, CK), :] = sc
        live = jnp.where(causal, acc, 0.0)
        mn8 = jnp.minimum(mn8, jnp.min(fold8(jnp.where(causal, acc, jnp.inf)), axis=0))
        mx8 = jnp.maximum(mx8, jnp.max(fold8(sc), axis=0))
        s1 = s1 + jnp.sum(fold8(live), axis=0)
        s2 = s2 + jnp.sum(fold8(live * live), axis=0)
        return mn8, mx8, s1, s2

    zero8 = jnp.zeros((8, TQ), F32)
    mn8, mx8, s1, s2 = lax.fori_loop(
        0, nck, idx_body, (jnp.full((8, TQ), jnp.inf, F32), jnp.full((8, TQ), -jnp.inf, F32), zero8, zero8))
    row_min = jnp.min(mn8, axis=0, keepdims=True)
    row_max = jnp.max(mx8, axis=0, keepdims=True)

    def fill_body(c, carry):
        sc_ref[pl.ds(pl.multiple_of(c * CK, CK), CK), :] = jnp.full((CK, TQ), -jnp.inf, F32)
        return carry

    lax.fori_loop(nck, nbig * (CB // CK), fill_body, 0)

    def count_rows(pred):
        def body(c, cnt):
            blk = sc_ref[pl.ds(pl.multiple_of(c * CB, CB), CB), :]
            return cnt + jnp.sum(fold8(jnp.where(pred(blk), 1.0, 0.0), COUNT_WAYS), axis=0)

        cnt = lax.fori_loop(0, nbig, body, jnp.zeros((8 * COUNT_WAYS, TQ), F32))
        return jnp.sum(cnt, axis=0, keepdims=True)

    kf = float(topk)
    n_valid = (t_lane + 1).astype(F32)
    all_rows = (t_lane + 1) <= topk

    def search_pass(st, probe, stuck):
        lo, hi, clo, chi, thr, fin, tie = st
        cnt = count_rows(lambda blk: blk >= probe)
        active = fin < 0.5
        hit = cnt == kf
        end_thr = jnp.where(jnp.logical_and(stuck, cnt < kf), lo, probe)
        ends = jnp.logical_and(active, jnp.logical_or(hit, stuck))
        thr = jnp.where(ends, end_thr, thr)
        tie = jnp.where(jnp.logical_and(ends, jnp.logical_not(hit)), 1.0, tie)
        fin = jnp.where(ends, 1.0, fin)
        up = jnp.logical_and(active, cnt > kf)
        dn = jnp.logical_and(active, cnt < kf)
        return (jnp.where(up, probe, lo), jnp.where(dn, probe, hi), jnp.where(up, cnt, clo),
                jnp.where(dn, cnt, chi), thr, fin, tie)

    def next_probe(st, halve=False):
        lo, hi, clo, chi = st[:4]
        frac = jnp.clip((clo - kf + 0.5) / (clo - chi), SEARCH_MARGIN, 1.0 - SEARCH_MARGIN)
        probe = lo + (hi - lo) * frac
        outside = jnp.logical_or(jnp.logical_or(probe <= lo, probe >= hi), halve)
        probe = jnp.where(outside, lo * 0.5 + hi * 0.5, probe)
        stuck = jnp.logical_or(probe <= lo, probe >= hi)
        return jnp.where(stuck, hi, probe), stuck

    mean = jnp.sum(s1, axis=0, keepdims=True) / n_valid
    var = jnp.maximum(jnp.sum(s2, axis=0, keepdims=True) / n_valid - mean * mean, 0.0)
    tail = jnp.clip(kf / n_valid, 1e-6, 1.0 - 1e-6)
    tq = jnp.sqrt(-2.0 * jnp.log(jnp.minimum(tail, 1.0 - tail)))
    zq = tq - ((0.010328 * tq + 0.802853) * tq + 2.515517) / (
        ((0.001308 * tq + 0.189269) * tq + 1.432788) * tq + 1.0)
    probe0 = jnp.clip(mean + jnp.where(tail < 0.5, zq, -zq) * jnp.sqrt(var), row_min, row_max)

    ones = jnp.ones((1, TQ), F32)
    st = (row_min, row_max, n_valid, 0.0 * ones, jnp.where(all_rows, -F32_MAX, row_max),
          jnp.where(all_rows, 1.0, 0.0), 0.0 * ones)
    st = search_pass(st, probe0, probe0 < row_min)

    def fixed_body(i, st):
        return search_pass(st, *next_probe(st))

    st = lax.fori_loop(0, SEARCH_FIXED_PASSES, fixed_body, st)

    def more_cond(c):
        return jnp.logical_and(c[0] < MAX_SEARCH_ITERS, jnp.min(c[1][5]) < 0.5)

    def more_body(c):
        return c[0] + 1, search_pass(c[1], *next_probe(c[1], c[0] >= SEARCH_INTERP_PASSES))

    _, st = lax.while_loop(more_cond, more_body, (jnp.int32(0), st))
    thr, tie = st[4], st[6]

    def plain_mask():
        def body(c, carry):
            off = pl.multiple_of(c * CB, CB)
            bias_ref[pl.ds(off, CB), :] = jnp.where(sc_ref[pl.ds(off, CB), :] >= thr, 0.0, NEG_BIG)
            return carry

        lax.fori_loop(0, nbig, body, 0)

    def tied_mask():
        n_gt = count_rows(lambda blk: blk > thr)
        need = jnp.where(all_rows, F32_MAX, kf - n_gt)

        def body(c, seen):
            off = pl.multiple_of(c * CB, CB)
            blk = sc_ref[pl.ds(off, CB), :]
            eq = blk == thr
            pref = jnp.dot(tri_ref[...], jnp.where(eq, 1.0, 0.0).astype(BF16),
                           preferred_element_type=F32) + seen
            keep = jnp.logical_or(blk > thr, jnp.logical_and(eq, pref <= need))
            bias_ref[pl.ds(off, CB), :] = jnp.where(keep, 0.0, NEG_BIG)
            return pref[CB - 1:CB, :]

        lax.fori_loop(0, nbig, body, jnp.zeros((1, TQ), F32))

    lax.cond(jnp.max(tie) > 0.5, tied_mask, plain_mask)

    def logit_body(c, mx):
        off = pl.multiple_of(c * CB, CB)
        b = bias_ref[pl.ds(off, CB), :]
        out = []
        for n in range(B_KV_HEADS):
            lg = _dot_nt(k_ref[0, n, pl.ds(off, CB), :], qs_ref[n])
            lg = jnp.concatenate([lg[:, g * TQ:(g + 1) * TQ] + b for g in range(B_GROUP)], axis=1)
            lg_ref[n, pl.ds(off, CB), :] = lg
            out.append(jnp.maximum(mx[n], jnp.max(fold8(lg), axis=0)))
        return tuple(out)

    mx = lax.fori_loop(0, nbig, logit_body, (jnp.full((8, gw), NEG_BIG, F32),) * B_KV_HEADS)
    m = [jnp.max(a, axis=0, keepdims=True) for a in mx]
    acc_ref[...] = jnp.zeros(acc_ref.shape, F32)

    def pv_body(c, l8):
        off = pl.multiple_of(c * CB, CB)
        out = []
        for n in range(B_KV_HEADS):
            p = jnp.exp2(lg_ref[n, pl.ds(off, CB), :] - m[n])
            acc_ref[n] += jnp.dot(vt_ref[0, n, c], p.astype(BF16), preferred_element_type=F32)
            out.append(l8[n] + jnp.sum(fold8(p), axis=0))
        return tuple(out)

    l8 = lax.fori_loop(0, nbig, pv_body, (jnp.zeros((8, gw), F32),) * B_KV_HEADS)
    for n in range(B_KV_HEADS):
        o_t = acc_ref[n] / jnp.sum(l8[n], axis=0, keepdims=True)
        for g in range(B_GROUP):
            cols = slice((n * B_GROUP + g) * B_HEAD_DIM, (n * B_GROUP + g + 1) * B_HEAD_DIM)
            o = o_t[:, g * TQ:(g + 1) * TQ].T
            o_ref[:, cols] = (o * jax.nn.silu(z_ref[:, cols].astype(F32))).astype(BF16)


def _dsa(proj, tabs, iwt, k, vt, iklo, ikhi, gq, bsz, s):
    nq = s // TQ
    topk = min(TOPK_MAX, s // 4)
    gw = B_GROUP * TQ
    rowblk = lambda c: pl.BlockSpec((TQ, BLK), lambda b, i: (b * nq + i, c))
    tok = pl.BlockSpec((1, TQ, LANES), lambda b, i: (b, i, 0))
    ik = pl.BlockSpec((1, s, LANES), lambda b, i: (b, 0, 0))
    tri = jnp.tril(jnp.ones((CB, CB), BF16))
    return pl.pallas_call(
        functools.partial(_dsa_kernel, topk),
        grid=(bsz, nq),
        in_specs=[rowblk(COL_BQ), rowblk(COL_BZ), rowblk(COL_IQ), tok, tok, tok, tok,
                  pl.BlockSpec((1, IDX_HEADS, TQ), lambda b, i: (b, 0, i)),
                  pl.BlockSpec((1, B_KV_HEADS, s, B_HEAD_DIM), lambda b, i: (b, 0, 0, 0)),
                  pl.BlockSpec((1, B_KV_HEADS, s // CB, B_HEAD_DIM, CB), lambda b, i: (b, 0, 0, 0, 0)),
                  ik, ik,
                  pl.BlockSpec((1, B_HEAD_DIM), lambda b, i: (0, 0)),
                  pl.BlockSpec((CB, CB), lambda b, i: (0, 0))],
        out_specs=pl.BlockSpec((TQ, B_WIDTH), lambda b, i: (b * nq + i, 0)),
        out_shape=jax.ShapeDtypeStruct((bsz * s, B_WIDTH), BF16),
        scratch_shapes=[
            pltpu.VMEM((s, TQ), F32),
            pltpu.VMEM((s, TQ), F32),
            pltpu.VMEM((B_KV_HEADS, s, gw), F32),
            pltpu.VMEM((B_KV_HEADS, gw, B_HEAD_DIM), BF16),
            pltpu.VMEM((IDX_HEADS // 4, 2 * TQ, LANES), BF16),
            pltpu.VMEM((B_KV_HEADS, B_HEAD_DIM, gw), F32),
        ],
        compiler_params=_cparams(("parallel", "arbitrary")),
        name="dsa",
    )(proj, proj, proj, *tabs, iwt, k, vt, iklo, ikhi, gq, tri)


def _merge_kernel(ta_ref, tb_ref, tm_ref, ga_ref, gb_ref, gm_ref, x_ref,
                  wa_ref, wb_ref, wm_ref, wo_ref, o_ref):
    def branch(t_ref, g_ref, w_ref):
        y = jnp.dot(t_ref[...], w_ref[...], preferred_element_type=F32)
        return jax.nn.sigmoid(g_ref[...].astype(F32)) * y

    merged = branch(ta_ref, ga_ref, wa_ref) + branch(tb_ref, gb_ref, wb_ref) + branch(tm_ref, gm_ref, wm_ref)
    o_ref[...] = x_ref[...] + jnp.dot(merged.astype(BF16), wo_ref[...], preferred_element_type=F32)


def _merge(ta, tb, tmem, proj, x2, wa, wb, wm, wo):
    m = x2.shape[0]
    tm = min(256, m)
    act = pl.BlockSpec((tm, BLK), lambda i: (i, 0))
    gate = lambda c: pl.BlockSpec((tm, D_MODEL), lambda i: (i, c // 2))
    wide = pl.BlockSpec((tm, D_MODEL), lambda i: (i, 0))
    wbr = pl.BlockSpec((BLK, D_MODEL), lambda i: (0, 0), pipeline_mode=pl.Buffered(1))
    wout = pl.BlockSpec((D_MODEL, D_MODEL), lambda i: (0, 0), pipeline_mode=pl.Buffered(1))
    return pl.pallas_call(
        _merge_kernel,
        grid=(m // tm,),
        in_specs=[act, act, act, gate(COL_GA), gate(COL_GB), gate(COL_GM), wide, wbr, wbr, wbr, wout],
        out_specs=wide,
        out_shape=jax.ShapeDtypeStruct((m, D_MODEL), F32),
        compiler_params=_cparams(("parallel",)),
        name="merge",
    )(ta, tb, tmem, proj, proj, proj, x2, wa, wb, wm, wo)


def _pack_w_in(w):
    offs = np.concatenate([[0], np.cumsum(SPLIT_SIZES)])
    (a_u, a_v, a_z, b_q, b_k, b_v, b_z, i_q, i_k, i_w, m_q, m_z, g_a, g_b, g_m) = [
        w[:, int(offs[i]):int(offs[i + 1])] for i in range(len(SPLIT_SIZES))]
    pad = lambda n: jnp.zeros((w.shape[0], n), w.dtype)
    small = [b_k, b_v, i_k, pad(SM_IW - SM_IK - IDX_DIM), i_w, pad(BLK - SM_IW - IDX_HEADS)]
    cols = [a_u, a_v, a_z, b_q, b_z, i_q, m_q, m_z, g_a, g_b, g_m] + small
    return jnp.concatenate(cols, axis=1).astype(BF16)


def _rope_tables():
    def build(rot, period):
        half = rot // 2
        inv = ROPE_THETA ** (-np.arange(half, dtype=np.float32) / half)
        lane = np.arange(LANES) % period
        freq = np.where(lane < rot, inv[lane % half], 0.0).astype(np.float32)
        sign = np.where(lane < half, -1.0, np.where(lane < rot, 1.0, 0.0)).astype(np.float32)
        return jnp.asarray(freq[None, :]), jnp.asarray(sign[None, :])

    fb, sgb = build(B_HEAD_DIM // 4, LANES)
    fi, sgi = build(IDX_DIM // 4, IDX_DIM)
    return fb, sgb, fi, sgi


def _layer(x, mem, positions, norm_gain, w_in, gmlp_ln_gain, gmlp_ln_bias, spatial_w, spatial_b,
           w_branch_a, q_norm_gain, k_norm_gain, idx_k_ln_gain, idx_k_ln_bias, w_branch_b,
           mem_norm_gain, w_mem_kv, mem_q_norm_gain, mem_k_norm_gain, w_branch_m, w_out):
    bsz, s, _ = x.shape
    m = bsz * s
    row = lambda a: a.reshape(1, -1).astype(F32)
    pad_lanes = lambda a: jnp.pad(a.reshape(1, -1).astype(F32), ((0, 0), (0, LANES - a.shape[-1])))
    x2 = x.reshape(m, D_MODEL)

    proj, small = _inproj(x2, row(norm_gain), _pack_w_in(w_in))
    k, vt, iklo, ikhi, cb, sb, ci, si, iwt = _kvprep(
        small.reshape(bsz, s, BLK), positions.reshape(bsz, s, 1).astype(jnp.int32), _rope_tables(),
        row(k_norm_gain), pad_lanes(idx_k_ln_gain), pad_lanes(idx_k_ln_bias))

    t_a = _gmlp(proj, row(gmlp_ln_gain), row(gmlp_ln_bias), spatial_w.astype(F32), spatial_b.T.astype(F32))
    km, vm = _memkv(mem, row(mem_norm_gain), w_mem_kv.astype(BF16), row(mem_k_norm_gain))
    t_m = _memattn(proj, km, vm, row(mem_q_norm_gain), s)
    t_b = _dsa(proj, (cb, sb, ci, si), iwt, k, vt, iklo, ikhi, row(q_norm_gain), bsz, s)

    out = _merge(t_a, t_b, t_m, proj, x2, w_branch_a.astype(BF16), w_branch_b.astype(BF16),
                 w_branch_m.astype(BF16), w_out.astype(BF16))
    return out.reshape(bsz, s, D_MODEL)


def kernel(x, mem, positions, norm_gain, w_in, gmlp_ln_gain, gmlp_ln_bias, spatial_w, spatial_b, w_branch_a, q_norm_gain, k_norm_gain, idx_k_ln_gain, idx_k_ln_bias, w_branch_b, mem_norm_gain, w_mem_kv, mem_q_norm_gain, mem_k_norm_gain, w_branch_m, w_out):
    for l in range(norm_gain.shape[0]):
        x = _layer(x, mem, positions, norm_gain[l], w_in[l], gmlp_ln_gain[l], gmlp_ln_bias[l],
                   spatial_w[l], spatial_b[l], w_branch_a[l], q_norm_gain[l], k_norm_gain[l],
                   idx_k_ln_gain[l], idx_k_ln_bias[l], w_branch_b[l], mem_norm_gain[l], w_mem_kv[l],
                   mem_q_norm_gain[l], mem_k_norm_gain[l], w_branch_m[l], w_out[l])
    return x
```

```python
import functools

import numpy as np
import jax
import jax.numpy as jnp
from jax import lax
from jax.experimental import pallas as pl
from jax.experimental.pallas import tpu as pltpu

F32 = jnp.float32
BF16 = jnp.bfloat16

D_MODEL = 2048
ROPE_THETA = 500000.0
EPS = 1e-6
A_GROUPS = 8
A_GROUP_DIM = 128
A_WIDTH = A_GROUPS * A_GROUP_DIM
CHUNK = 128
B_HEADS = 8
B_KV_HEADS = 2
B_GROUP = B_HEADS // B_KV_HEADS
B_HEAD_DIM = 128
B_WIDTH = B_HEADS * B_HEAD_DIM
IDX_HEADS = 16
IDX_DIM = 64
TOPK_MAX = 256
M_HEADS = 4
M_HEAD_DIM = 256
M_WIDTH = M_HEADS * M_HEAD_DIM

SPLIT_SIZES = (
    A_WIDTH, A_WIDTH, A_WIDTH,
    B_WIDTH, B_KV_HEADS * B_HEAD_DIM, B_KV_HEADS * B_HEAD_DIM, B_WIDTH,
    IDX_HEADS * IDX_DIM, IDX_DIM, IDX_HEADS,
    M_WIDTH, M_WIDTH,
    D_MODEL, D_MODEL, D_MODEL,
)

LANES = 128
BLK = 1024
COL_AU, COL_AV, COL_AZ, COL_BQ, COL_BZ, COL_IQ, COL_MQ, COL_MZ = range(8)
COL_GA, COL_GB, COL_GM = 8, 10, 12
NB_MAIN = 14
_OFFS = [int(o) for o in np.concatenate([[0], np.cumsum(SPLIT_SIZES)])]
_MAIN_START = ([_OFFS[i] for i in (0, 1, 2, 3, 6, 7, 10, 11)]
               + [_OFFS[i] + d for i in (12, 13, 14) for d in (0, BLK)])
TAIL_FIRST_BLOCK = 6
TAIL_SHIFT = _MAIN_START[TAIL_FIRST_BLOCK] % LANES
assert all(s % LANES == 0 for s in _MAIN_START[:TAIL_FIRST_BLOCK])
assert all(s % LANES == TAIL_SHIFT for s in _MAIN_START[TAIL_FIRST_BLOCK:]) and 0 < TAIL_SHIFT
MAIN_WINDOW_START = [s - s % LANES for s in _MAIN_START]
SMALL_KV_START, SMALL_KV_WIDTH = _OFFS[4], _OFFS[6] - _OFFS[4]
SMALL_IDX_START = _OFFS[8]
assert SMALL_KV_START % LANES == 0 and SMALL_IDX_START % LANES == 0 and IDX_DIM + IDX_HEADS <= LANES
SM_K, SM_V, SM_IDX = 0, B_KV_HEADS * B_HEAD_DIM, SMALL_KV_WIDTH
SMALL_WIDTH = SMALL_KV_WIDTH + LANES

VMEM_LIMIT = 56 * 1024 * 1024
LOG2E = 1.4426950408889634
NEG_BIG = -1e30
F32_MAX = 3.4028234663852886e38

TQ = 128
CK = 256
CB = 512
COUNT_WAYS = 8
SEARCH_MARGIN = 0.05
SEARCH_FIXED_PASSES = 8
SEARCH_INTERP_PASSES = 16
MAX_SEARCH_ITERS = 400


def _cparams(sem):
    return pltpu.CompilerParams(dimension_semantics=sem, vmem_limit_bytes=VMEM_LIMIT)


def _dot_nt(a, b):
    return lax.dot_general(a, b, (((1,), (1,)), ((), ())), preferred_element_type=F32)


def _rope(x, cos_t, sin_t, half, period):
    lane = lax.broadcasted_iota(jnp.int32, x.shape, 1) % period
    rolled = jnp.where(lane < half, pltpu.roll(x, LANES - half, 1), pltpu.roll(x, half, 1))
    return x * cos_t + rolled * sin_t


def _wpack_kernel(tab_ref, main_ref, extra_ref, o_ref):
    del tab_ref
    n = pl.program_id(0)

    @pl.when(n < TAIL_FIRST_BLOCK)
    def _():
        o_ref[...] = main_ref[...].astype(BF16)

    @pl.when(n >= TAIL_FIRST_BLOCK)
    def _():
        nt = BLK // LANES
        keep = LANES - TAIL_SHIFT
        win = [main_ref[:, c * LANES:(c + 1) * LANES] for c in range(nt)] + [extra_ref[...]]
        rolled = [pltpu.roll(w, keep, 1) for w in win]
        lane = lax.broadcasted_iota(jnp.int32, rolled[0].shape, 1)
        for c in range(nt):
            o_ref[:, c * LANES:(c + 1) * LANES] = jnp.where(lane < keep, rolled[c], rolled[c + 1]).astype(BF16)


def _wpack(w):
    tab = jnp.asarray(np.array([[s // LANES for s in MAIN_WINDOW_START],
                                [(s + BLK) // LANES for s in MAIN_WINDOW_START]], np.int32))
    return pl.pallas_call(
        _wpack_kernel,
        grid_spec=pltpu.PrefetchScalarGridSpec(
            num_scalar_prefetch=1,
            grid=(NB_MAIN,),
            in_specs=[
                pl.BlockSpec((pl.Element(D_MODEL), pl.Element(BLK)), lambda n, tab: (0, tab[0, n] * LANES)),
                pl.BlockSpec((D_MODEL, LANES), lambda n, tab: (0, tab[1, n])),
            ],
            out_specs=pl.BlockSpec((D_MODEL, BLK), lambda n, tab: (0, n)),
        ),
        out_shape=jax.ShapeDtypeStruct((D_MODEL, NB_MAIN * BLK), BF16),
        compiler_params=_cparams(("parallel",)),
        name="wpack",
    )(tab, w, w)


def _rms_rows(x, gain):
    ms = jnp.mean(x * x, axis=-1, keepdims=True)
    return (x * lax.rsqrt(ms + EPS) * gain).astype(BF16)


def _inproj_kernel(x_ref, g_ref, w_ref, proj_ref, h_ref):
    @pl.when(pl.program_id(1) == 0)
    def _():
        h_ref[...] = _rms_rows(x_ref[...], g_ref[...])

    proj_ref[...] = jnp.dot(h_ref[...], w_ref[...], preferred_element_type=F32).astype(BF16)


def _inproj(x2, gain, w_packed):
    m = x2.shape[0]
    tm = min(1024, m)
    return pl.pallas_call(
        _inproj_kernel,
        grid=(m // tm, NB_MAIN),
        in_specs=[
            pl.BlockSpec((tm, D_MODEL), lambda i, n: (i, 0)),
            pl.BlockSpec((1, D_MODEL), lambda i, n: (0, 0)),
            pl.BlockSpec((D_MODEL, BLK), lambda i, n: (0, n)),
        ],
        out_specs=pl.BlockSpec((tm, BLK), lambda i, n: (i, n)),
        out_shape=jax.ShapeDtypeStruct((m, NB_MAIN * BLK), BF16),
        scratch_shapes=[pltpu.VMEM((tm, D_MODEL), BF16)],
        compiler_params=_cparams(("parallel", "arbitrary")),
        name="inproj",
    )(x2, gain, w_packed)


def _kvprep_kernel(x_ref, g_ref, ws_ref, pos_ref, fb_ref, sgb_ref, fi_ref, sgi_ref, gk_ref, lng_ref,
                   lnb_ref, k_ref, vt_ref, iklo_ref, ikhi_ref, cb_ref, sb_ref, ci_ref, si_ref, iwt_ref):
    small = jnp.dot(_rms_rows(x_ref[...], g_ref[...]), ws_ref[...], preferred_element_type=F32)
    pos = pos_ref[0].astype(F32)
    ang_b = pos * fb_ref[...]
    cos_b = jnp.cos(ang_b)
    sin_b = jnp.sin(ang_b) * sgb_ref[...]
    ang_i = pos * fi_ref[...]
    cos_i = jnp.cos(ang_i)
    sin_i = jnp.sin(ang_i) * sgi_ref[...]
    cb_ref[0] = cos_b
    sb_ref[0] = sin_b
    ci_ref[0] = cos_i
    si_ref[0] = sin_i

    for n in range(B_KV_HEADS):
        kh = small[:, SM_K + n * B_HEAD_DIM:SM_K + (n + 1) * B_HEAD_DIM]
        r = lax.rsqrt(jnp.mean(kh * kh, axis=-1, keepdims=True) + EPS)
        kn = kh * r * gk_ref[...]
        k_ref[0, n] = _rope(kn, cos_b, sin_b, B_HEAD_DIM // 8, LANES).astype(BF16)
        vt = small[:, SM_V + n * B_HEAD_DIM:SM_V + (n + 1) * B_HEAD_DIM].T
        for c in range(vt.shape[1] // CB):
            vt_ref[0, n, c] = vt[:, c * CB:(c + 1) * CB].astype(BF16)

    ikp = small[:, SM_IDX:SM_IDX + LANES]
    lane = lax.broadcasted_iota(jnp.int32, ikp.shape, 1)
    live = lane < IDX_DIM
    mu = jnp.sum(jnp.where(live, ikp, 0.0), axis=-1, keepdims=True) * (1.0 / IDX_DIM)
    d = jnp.where(live, ikp - mu, 0.0)
    var = jnp.sum(d * d, axis=-1, keepdims=True) * (1.0 / IDX_DIM)
    y = d * lax.rsqrt(var + EPS) * lng_ref[...] + lnb_ref[...]
    yr = jnp.where(live, _rope(y, cos_i, sin_i, IDX_DIM // 8, IDX_DIM), 0.0)
    iklo_ref[0] = yr.astype(BF16)
    ikhi_ref[0] = pltpu.roll(yr, IDX_DIM, 1).astype(BF16)

    iw = ikp * (IDX_DIM ** -0.5 * IDX_HEADS ** -0.5)
    iwt_ref[0] = iw.T[IDX_DIM:IDX_DIM + IDX_HEADS, :]


def _kvprep(x3, gain, w_small, pos3, tabs, gk, lng, lnb):
    bsz, s, _ = x3.shape
    tp = min(512, s)
    row = lambda b, i: (b, i, 0)
    const = lambda b, i: (0, 0)
    tab_spec = pl.BlockSpec((1, LANES), const)
    out_tok = pl.BlockSpec((1, tp, LANES), row)
    out_k = pl.BlockSpec((1, B_KV_HEADS, tp, B_HEAD_DIM), lambda b, i: (b, 0, i, 0))
    out_vt = pl.BlockSpec((1, B_KV_HEADS, tp // CB, B_HEAD_DIM, CB), lambda b, i: (b, 0, i, 0, 0))
    out_iwt = pl.BlockSpec((1, IDX_HEADS, tp), lambda b, i: (b, 0, i))
    return pl.pallas_call(
        _kvprep_kernel,
        grid=(bsz, s // tp),
        in_specs=[pl.BlockSpec((None, tp, D_MODEL), row), pl.BlockSpec((1, D_MODEL), const),
                  pl.BlockSpec((D_MODEL, SMALL_WIDTH), const), pl.BlockSpec((1, tp, 1), row),
                  tab_spec, tab_spec, tab_spec, tab_spec, tab_spec, tab_spec, tab_spec],
        out_specs=[out_k, out_vt, out_tok, out_tok, out_tok, out_tok, out_tok, out_tok, out_iwt],
        out_shape=[
            jax.ShapeDtypeStruct((bsz, B_KV_HEADS, s, B_HEAD_DIM), BF16),
            jax.ShapeDtypeStruct((bsz, B_KV_HEADS, s // CB, B_HEAD_DIM, CB), BF16),
            jax.ShapeDtypeStruct((bsz, s, LANES), BF16),
            jax.ShapeDtypeStruct((bsz, s, LANES), BF16),
            jax.ShapeDtypeStruct((bsz, s, LANES), F32),
            jax.ShapeDtypeStruct((bsz, s, LANES), F32),
            jax.ShapeDtypeStruct((bsz, s, LANES), F32),
            jax.ShapeDtypeStruct((bsz, s, LANES), F32),
            jax.ShapeDtypeStruct((bsz, IDX_HEADS, s), F32),
        ],
        compiler_params=_cparams(("parallel", "parallel")),
        name="kvprep",
    )(x3, gain, w_small, pos3, *tabs, gk, lng, lnb)


def _gmlp_kernel(u_ref, v_ref, z_ref, lng_ref, lnb_ref, ws_ref, sbt_ref, o_ref):
    tm = u_ref.shape[0]
    u = jax.nn.gelu(u_ref[...].astype(F32))
    v = jax.nn.gelu(v_ref[...].astype(F32))
    mu = jnp.mean(v, axis=-1, keepdims=True)
    d = v - mu
    var = jnp.mean(d * d, axis=-1, keepdims=True)
    vn = (d * lax.rsqrt(var + EPS) * lng_ref[...] + lnb_ref[...]).astype(BF16)
    gate = u * jax.nn.silu(z_ref[...].astype(F32))
    tri = (lax.broadcasted_iota(jnp.int32, (CHUNK, CHUNK), 1)
           <= lax.broadcasted_iota(jnp.int32, (CHUNK, CHUNK), 0))
    for g in range(A_GROUPS):
        wg = jnp.where(tri, ws_ref[g], 0.0).astype(BF16)
        bias = sbt_ref[:, g:g + 1]
        cols = slice(g * A_GROUP_DIM, (g + 1) * A_GROUP_DIM)
        for c in range(tm // CHUNK):
            rows = slice(c * CHUNK, (c + 1) * CHUNK)
            sg = jnp.dot(wg, vn[rows, cols], preferred_element_type=F32) + bias
            o_ref[rows, cols] = (gate[rows, cols] * sg).astype(BF16)


def _gmlp(proj, lng, lnb, ws, sbt):
    m = proj.shape[0]
    tm = min(512, m)
    col = lambda c: pl.BlockSpec((tm, BLK), lambda i: (i, c))
    full = lambda shape: pl.BlockSpec(shape, lambda i: (0,) * len(shape))
    return pl.pallas_call(
        _gmlp_kernel,
        grid=(m // tm,),
        in_specs=[col(COL_AU), col(COL_AV), col(COL_AZ), full((1, A_WIDTH)), full((1, A_WIDTH)),
                  full((A_GROUPS, CHUNK, CHUNK)), full((CHUNK, A_GROUPS))],
        out_specs=pl.BlockSpec((tm, A_WIDTH), lambda i: (i, 0)),
        out_shape=jax.ShapeDtypeStruct((m, A_WIDTH), BF16),
        compiler_params=_cparams(("parallel",)),
        name="gmlp",
    )(proj, proj, proj, lng, lnb, ws, sbt)


def _memkv_kernel(mem_ref, g_ref, w_ref, gk_ref, km_ref, vm_ref):
    x = mem_ref[0]
    ms = jnp.mean(x * x, axis=-1, keepdims=True)
    h = (x * lax.rsqrt(ms + EPS) * g_ref[...]).astype(BF16)
    kv = jnp.dot(h, w_ref[...], preferred_element_type=F32)
    for hd in range(M_HEADS):
        kh = kv[:, hd * M_HEAD_DIM:(hd + 1) * M_HEAD_DIM]
        r = lax.rsqrt(jnp.mean(kh * kh, axis=-1, keepdims=True) + EPS)
        km_ref[0, hd] = (kh * r * gk_ref[...]).astype(BF16)
        vm_ref[0, hd] = kv[:, M_WIDTH + hd * M_HEAD_DIM:M_WIDTH + (hd + 1) * M_HEAD_DIM].astype(BF16)


def _memkv(mem, gain, w_kv, gk):
    bsz, ml, _ = mem.shape
    out = pl.BlockSpec((1, M_HEADS, ml, M_HEAD_DIM), lambda b: (b, 0, 0, 0))
    shp = jax.ShapeDtypeStruct((bsz, M_HEADS, ml, M_HEAD_DIM), BF16)
    return pl.pallas_call(
        _memkv_kernel,
        grid=(bsz,),
        in_specs=[pl.BlockSpec((1, ml, D_MODEL), lambda b: (b, 0, 0)),
                  pl.BlockSpec((1, D_MODEL), lambda b: (0, 0)),
                  pl.BlockSpec((D_MODEL, 2 * M_WIDTH), lambda b: (0, 0)),
                  pl.BlockSpec((1, M_HEAD_DIM), lambda b: (0, 0))],
        out_specs=[out, out],
        out_shape=[shp, shp],
        compiler_params=_cparams(("parallel",)),
        name="memkv",
    )(mem, gain, w_kv, gk)


def _memattn_kernel(q_ref, z_ref, km_ref, vm_ref, gq_ref, o_ref):
    qscale = M_HEAD_DIM ** -0.5 * LOG2E
    for hd in range(M_HEADS):
        cols = slice(hd * M_HEAD_DIM, (hd + 1) * M_HEAD_DIM)
        q = q_ref[:, cols].astype(F32)
        r = lax.rsqrt(jnp.mean(q * q, axis=-1, keepdims=True) + EPS)
        qn = (q * r * gq_ref[...] * qscale).astype(BF16)
        lg = _dot_nt(qn, km_ref[0, hd])
        p = jnp.exp2(lg - jnp.max(lg, axis=-1, keepdims=True))
        l = jnp.sum(p, axis=-1, keepdims=True)
        o = jnp.dot(p.astype(BF16), vm_ref[0, hd], preferred_element_type=F32) / l
        o_ref[:, cols] = (o * jax.nn.silu(z_ref[:, cols].astype(F32))).astype(BF16)


def _memattn(proj, km, vm, gq, s):
    m = proj.shape[0]
    tm = min(512, s)
    per_b = s // tm
    ml = km.shape[2]
    kv_spec = pl.BlockSpec((1, M_HEADS, ml, M_HEAD_DIM), lambda i: (i // per_b, 0, 0, 0))
    return pl.pallas_call(
        _memattn_kernel,
        grid=(m // tm,),
        in_specs=[pl.BlockSpec((tm, BLK), lambda i: (i, COL_MQ)),
                  pl.BlockSpec((tm, BLK), lambda i: (i, COL_MZ)),
                  kv_spec, kv_spec,
                  pl.BlockSpec((1, M_HEAD_DIM), lambda i: (0, 0))],
        out_specs=pl.BlockSpec((tm, M_WIDTH), lambda i: (i, 0)),
        out_shape=jax.ShapeDtypeStruct((m, M_WIDTH), BF16),
        compiler_params=_cparams(("parallel",)),
        name="memattn",
    )(proj, proj, km, vm, gq)


def _dsa_kernel(topk, q_ref, z_ref, iq_ref, cb_ref, sb_ref, ci_ref, si_ref, iwt_ref,
                k_ref, vt_ref, iklo_ref, ikhi_ref, gq_ref, tri_ref, o_ref,
                sc_ref, bias_ref, lg_ref, qs_ref, iqs_ref, acc_ref):
    qb = pl.program_id(1)
    nck = (qb * TQ + TQ + CK - 1) // CK
    nbig = (qb * TQ + TQ + CB - 1) // CB
    t_lane = qb * TQ + lax.broadcasted_iota(jnp.int32, (1, TQ), 1)
    gw = B_GROUP * TQ

    def fold8(a, ways=1):
        return a.reshape(a.shape[0] // (8 * ways), 8 * ways, a.shape[1])

    def paired_loop(trips, body, init):
        carry = lax.fori_loop(0, trips // 2, lambda i, c: body(2 * i + 1, body(2 * i, c)), init)
        return lax.cond(trips % 2 == 1, lambda c: body(trips - 1, c), lambda c: c, carry)

    cos_b, sin_b = cb_ref[0], sb_ref[0]
    cos_i, sin_i = ci_ref[0], si_ref[0]
    qscale = B_HEAD_DIM ** -0.5 * LOG2E
    ones_mat = jnp.ones((B_HEAD_DIM, B_HEAD_DIM), BF16)
    for h in range(B_HEADS):
        slab = q_ref[:, h * B_HEAD_DIM:(h + 1) * B_HEAD_DIM].astype(F32)
        sq = slab * slab
        sq_hi = sq.astype(BF16)
        sq_lo = (sq - sq_hi.astype(F32)).astype(BF16)
        ssq = (jnp.dot(sq_hi, ones_mat, preferred_element_type=F32)
               + jnp.dot(sq_lo, ones_mat, preferred_element_type=F32))
        r = lax.rsqrt(ssq * (1.0 / B_HEAD_DIM) + EPS)
        qr = _rope(slab * r * gq_ref[...], cos_b, sin_b, B_HEAD_DIM // 8, LANES) * qscale
        g = h % B_GROUP
        qs_ref[h // B_GROUP, g * TQ:(g + 1) * TQ, :] = qr.astype(BF16)
    for j in range(IDX_HEADS // 2):
        slab = iq_ref[:, j * LANES:(j + 1) * LANES].astype(F32)
        iqs_ref[j // 2, (j % 2) * TQ:(j % 2 + 1) * TQ, :] = (
            _rope(slab, cos_i, sin_i, IDX_DIM // 8, IDX_DIM).astype(BF16))
    wt = iwt_ref[0]

    def idx_body(c, carry):
        mn8, mx8, s1, s2 = carry
        off = pl.multiple_of(c * CK, CK)
        acc = jnp.zeros((CK, TQ), F32)
        for jj in range(IDX_HEADS // 4):
            rhs = iqs_ref[jj]
            for half, keys_ref in enumerate((iklo_ref, ikhi_ref)):
                d = _dot_nt(keys_ref[0, pl.ds(off, CK), :], rhs)
                ha, hb = 4 * jj + half, 4 * jj + 2 + half
                acc = (acc + jnp.maximum(d[:, :TQ], 0.0) * wt[ha:ha + 1, :]
                       + jnp.maximum(d[:, TQ:], 0.0) * wt[hb:hb + 1, :])
        key = off + lax.broadcasted_iota(jnp.int32, (CK, TQ), 0)
        causal = key <= t_lane
        sc = jnp.where(causal, acc, -jnp.inf)
        sc_ref[pl.ds(off, CK), :] = sc
        live = jnp.where(causal, acc, 0.0)
        mn8 = jnp.minimum(mn8, jnp.min(fold8(jnp.where(causal, acc, jnp.inf)), axis=0))
        mx8 = jnp.maximum(mx8, jnp.max(fold8(sc), axis=0))
        s1 = s1 + jnp.sum(fold8(live), axis=0)
        s2 = s2 + jnp.sum(fold8(live * live), axis=0)
        return mn8, mx8, s1, s2

    zero8 = jnp.zeros((8, TQ), F32)
    stats = (jnp.full((8, TQ), jnp.inf, F32), jnp.full((8, TQ), -jnp.inf, F32), zero8, zero8)
    mn8, mx8, s1, s2 = paired_loop(nck, idx_body, stats)
    row_min = jnp.min(mn8, axis=0, keepdims=True)
    row_max = jnp.max(mx8, axis=0, keepdims=True)

    def fill_body(c, carry):
        sc_ref[pl.ds(pl.multiple_of(c * CK, CK), CK), :] = jnp.full((CK, TQ), -jnp.inf, F32)
        return carry

    lax.fori_loop(nck, nbig * (CB // CK), fill_body, 0)

    def count_rows(pred):
        def body(c, cnt):
            blk = sc_ref[pl.ds(pl.multiple_of(c * CB, CB), CB), :]
            return cnt + jnp.sum(fold8(jnp.where(pred(blk), 1.0, 0.0), COUNT_WAYS), axis=0)

        cnt = lax.fori_loop(0, nbig, body, jnp.zeros((8 * COUNT_WAYS, TQ), F32))
        return jnp.sum(cnt, axis=0, keepdims=True)

    kf = float(topk)
    n_valid = (t_lane + 1).astype(F32)
    all_rows = (t_lane + 1) <= topk

    def search_pass(st, probe, stuck):
        lo, hi, clo, chi, thr, fin, tie = st
        cnt = count_rows(lambda blk: blk >= probe)
        active = fin < 0.5
        hit = cnt == kf
        end_thr = jnp.where(jnp.logical_and(stuck, cnt < kf), lo, probe)
        ends = jnp.logical_and(active, jnp.logical_or(hit, stuck))
        thr = jnp.where(ends, end_thr, thr)
        tie = jnp.where(jnp.logical_and(ends, jnp.logical_not(hit)), 1.0, tie)
        fin = jnp.where(ends, 1.0, fin)
        up = jnp.logical_and(active, cnt > kf)
        dn = jnp.logical_and(active, cnt < kf)
        return (jnp.where(up, probe, lo), jnp.where(dn, probe, hi), jnp.where(up, cnt, clo),
                jnp.where(dn, cnt, chi), thr, fin, tie)

    def next_probe(st, halve=False):
        lo, hi, clo, chi = st[:4]
        frac = jnp.clip((clo - kf + 0.5) / (clo - chi), SEARCH_MARGIN, 1.0 - SEARCH_MARGIN)
        probe = lo + (hi - lo) * frac
        outside = jnp.logical_or(jnp.logical_or(probe <= lo, probe >= hi), halve)
        probe = jnp.where(outside, lo * 0.5 + hi * 0.5, probe)
        stuck = jnp.logical_or(probe <= lo, probe >= hi)
        return jnp.where(stuck, hi, probe), stuck

    mean = jnp.sum(s1, axis=0, keepdims=True) / n_valid
    var = jnp.maximum(jnp.sum(s2, axis=0, keepdims=True) / n_valid - mean * mean, 0.0)
    tail = jnp.clip(kf / n_valid, 1e-6, 1.0 - 1e-6)
    tq = jnp.sqrt(-2.0 * jnp.log(jnp.minimum(tail, 1.0 - tail)))
    zq = tq - ((0.010328 * tq + 0.802853) * tq + 2.515517) / (
        ((0.001308 * tq + 0.189269) * tq + 1.432788) * tq + 1.0)
    probe0 = jnp.clip(mean + jnp.where(tail < 0.5, zq, -zq) * jnp.sqrt(var), row_min, row_max)

    ones = jnp.ones((1, TQ), F32)
    st = (row_min, row_max, n_valid, 0.0 * ones, jnp.where(all_rows, -F32_MAX, row_max),
          jnp.where(all_rows, 1.0, 0.0), 0.0 * ones)
    st = search_pass(st, probe0, probe0 < row_min)

    def fixed_body(i, st):
        return search_pass(st, *next_probe(st))

    st = lax.fori_loop(0, SEARCH_FIXED_PASSES, fixed_body, st)

    def more_cond(c):
        return jnp.logical_and(c[0] < MAX_SEARCH_ITERS, jnp.min(c[1][5]) < 0.5)

    def more_body(c):
        return c[0] + 1, search_pass(c[1], *next_probe(c[1], c[0] >= SEARCH_INTERP_PASSES))

    _, st = lax.while_loop(more_cond, more_body, (jnp.int32(0), st))
    thr, tie = st[4], st[6]

    def plain_mask():
        def body(c, carry):
            off = pl.multiple_of(c * CB, CB)
            bias_ref[pl.ds(off, CB), :] = jnp.where(sc_ref[pl.ds(off, CB), :] >= thr, 0.0, NEG_BIG)
            return carry

        lax.fori_loop(0, nbig, body, 0)

    def tied_mask():
        n_gt = count_rows(lambda blk: blk > thr)
        need = jnp.where(all_rows, F32_MAX, kf - n_gt)

        def body(c, seen):
            off = pl.multiple_of(c * CB, CB)
            blk = sc_ref[pl.ds(off, CB), :]
            eq = blk == thr
            pref = jnp.dot(tri_ref[...], jnp.where(eq, 1.0, 0.0).astype(BF16),
                           preferred_element_type=F32) + seen
            keep = jnp.logical_or(blk > thr, jnp.logical_and(eq, pref <= need))
            bias_ref[pl.ds(off, CB), :] = jnp.where(keep, 0.0, NEG_BIG)
            return pref[CB - 1:CB, :]

        lax.fori_loop(0, nbig, body, jnp.zeros((1, TQ), F32))

    lax.cond(jnp.max(tie) > 0.5, tied_mask, plain_mask)

    def logit_body(c, mx):
        off = pl.multiple_of(c * CB, CB)
        b = bias_ref[pl.ds(off, CB), :]
        out = []
        for n in range(B_KV_HEADS):
            lg = _dot_nt(k_ref[0, n, pl.ds(off, CB), :], qs_ref[n])
            lg = jnp.concatenate([lg[:, g * TQ:(g + 1) * TQ] + b for g in range(B_GROUP)], axis=1)
            lg_ref[n, pl.ds(off, CB), :] = lg
            out.append(jnp.maximum(mx[n], jnp.max(fold8(lg), axis=0)))
        return tuple(out)

    mx = paired_loop(nbig, logit_body, (jnp.full((8, gw), NEG_BIG, F32),) * B_KV_HEADS)
    m = [jnp.max(a, axis=0, keepdims=True) for a in mx]
    acc_ref[...] = jnp.zeros(acc_ref.shape, F32)

    def pv_body(c, l8):
        off = pl.multiple_of(c * CB, CB)
        out = []
        for n in range(B_KV_HEADS):
            p = jnp.exp2(lg_ref[n, pl.ds(off, CB), :] - m[n])
            acc_ref[n] += jnp.dot(vt_ref[0, n, c], p.astype(BF16), preferred_element_type=F32)
            out.append(l8[n] + jnp.sum(fold8(p), axis=0))
        return tuple(out)

    l8 = paired_loop(nbig, pv_body, (jnp.zeros((8, gw), F32),) * B_KV_HEADS)
    for n in range(B_KV_HEADS):
        o_t = acc_ref[n] / jnp.sum(l8[n], axis=0, keepdims=True)
        for g in range(B_GROUP):
            cols = slice((n * B_GROUP + g) * B_HEAD_DIM, (n * B_GROUP + g + 1) * B_HEAD_DIM)
            o = o_t[:, g * TQ:(g + 1) * TQ].T
            o_ref[:, cols] = (o * jax.nn.silu(z_ref[:, cols].astype(F32))).astype(BF16)


def _dsa(proj, tabs, iwt, k, vt, iklo, ikhi, gq, bsz, s):
    nq = s // TQ
    topk = min(TOPK_MAX, s // 4)
    gw = B_GROUP * TQ
    rowblk = lambda c: pl.BlockSpec((TQ, BLK), lambda b, i: (b * nq + i, c))
    tok = pl.BlockSpec((1, TQ, LANES), lambda b, i: (b, i, 0))
    ik = pl.BlockSpec((1, s, LANES), lambda b, i: (b, 0, 0))
    tri = jnp.tril(jnp.ones((CB, CB), BF16))
    return pl.pallas_call(
        functools.partial(_dsa_kernel, topk),
        grid=(bsz, nq),
        in_specs=[rowblk(COL_BQ), rowblk(COL_BZ), rowblk(COL_IQ), tok, tok, tok, tok,
                  pl.BlockSpec((1, IDX_HEADS, TQ), lambda b, i: (b, 0, i)),
                  pl.BlockSpec((1, B_KV_HEADS, s, B_HEAD_DIM), lambda b, i: (b, 0, 0, 0)),
                  pl.BlockSpec((1, B_KV_HEADS, s // CB, B_HEAD_DIM, CB), lambda b, i: (b, 0, 0, 0, 0)),
                  ik, ik,
                  pl.BlockSpec((1, B_HEAD_DIM), lambda b, i: (0, 0)),
                  pl.BlockSpec((CB, CB), lambda b, i: (0, 0))],
        out_specs=pl.BlockSpec((TQ, B_WIDTH), lambda b, i: (b * nq + i, 0)),
        out_shape=jax.ShapeDtypeStruct((bsz * s, B_WIDTH), BF16),
        scratch_shapes=[
            pltpu.VMEM((s, TQ), F32),
            pltpu.VMEM((s, TQ), F32),
            pltpu.VMEM((B_KV_HEADS, s, gw), F32),
            pltpu.VMEM((B_KV_HEADS, gw, B_HEAD_DIM), BF16),
            pltpu.VMEM((IDX_HEADS // 4, 2 * TQ, LANES), BF16),
            pltpu.VMEM((B_KV_HEADS, B_HEAD_DIM, gw), F32),
        ],
        compiler_params=_cparams(("parallel", "arbitrary")),
        name="dsa",
    )(proj, proj, proj, *tabs, iwt, k, vt, iklo, ikhi, gq, tri)


def _merge_kernel(ta_ref, tb_ref, tm_ref, ga_ref, gb_ref, gm_ref, x_ref,
                  wa_ref, wb_ref, wm_ref, wo_ref, o_ref):
    def branch(t_ref, g_ref, w_ref):
        y = jnp.dot(t_ref[...], w_ref[...], preferred_element_type=F32)
        return jax.nn.sigmoid(g_ref[...].astype(F32)) * y

    merged = branch(ta_ref, ga_ref, wa_ref) + branch(tb_ref, gb_ref, wb_ref) + branch(tm_ref, gm_ref, wm_ref)
    o_ref[...] = x_ref[...] + jnp.dot(merged.astype(BF16), wo_ref[...], preferred_element_type=F32)


def _merge(ta, tb, tmem, proj, x2, wa, wb, wm, wo):
    m = x2.shape[0]
    tm = min(256, m)
    act = pl.BlockSpec((tm, BLK), lambda i: (i, 0))
    gate = lambda c: pl.BlockSpec((tm, D_MODEL), lambda i: (i, c // 2))
    wide = pl.BlockSpec((tm, D_MODEL), lambda i: (i, 0))
    wbr = pl.BlockSpec((BLK, D_MODEL), lambda i: (0, 0), pipeline_mode=pl.Buffered(1))
    wout = pl.BlockSpec((D_MODEL, D_MODEL), lambda i: (0, 0), pipeline_mode=pl.Buffered(1))
    return pl.pallas_call(
        _merge_kernel,
        grid=(m // tm,),
        in_specs=[act, act, act, gate(COL_GA), gate(COL_GB), gate(COL_GM), wide, wbr, wbr, wbr, wout],
        out_specs=wide,
        out_shape=jax.ShapeDtypeStruct((m, D_MODEL), F32),
        compiler_params=_cparams(("parallel",)),
        name="merge",
    )(ta, tb, tmem, proj, proj, proj, x2, wa, wb, wm, wo)


def _rope_tables():
    def build(rot, period):
        half = rot // 2
        inv = ROPE_THETA ** (-np.arange(half, dtype=np.float32) / half)
        lane = np.arange(LANES) % period
        freq = np.where(lane < rot, inv[lane % half], 0.0).astype(np.float32)
        sign = np.where(lane < half, -1.0, np.where(lane < rot, 1.0, 0.0)).astype(np.float32)
        return jnp.asarray(freq[None, :]), jnp.asarray(sign[None, :])

    fb, sgb = build(B_HEAD_DIM // 4, LANES)
    fi, sgi = build(IDX_DIM // 4, IDX_DIM)
    return fb, sgb, fi, sgi


def _layer(x, mem, positions, norm_gain, w_in, gmlp_ln_gain, gmlp_ln_bias, spatial_w, spatial_b,
           w_branch_a, q_norm_gain, k_norm_gain, idx_k_ln_gain, idx_k_ln_bias, w_branch_b,
           mem_norm_gain, w_mem_kv, mem_q_norm_gain, mem_k_norm_gain, w_branch_m, w_out):
    bsz, s, _ = x.shape
    m = bsz * s
    row = lambda a: a.reshape(1, -1).astype(F32)
    pad_lanes = lambda a: jnp.pad(a.reshape(1, -1).astype(F32), ((0, 0), (0, LANES - a.shape[-1])))
    x2 = x.reshape(m, D_MODEL)

    proj = _inproj(x2, row(norm_gain), _wpack(w_in))
    w_small = jnp.concatenate([w_in[:, SMALL_KV_START:SMALL_KV_START + SMALL_KV_WIDTH],
                               w_in[:, SMALL_IDX_START:SMALL_IDX_START + LANES]], axis=1).astype(BF16)
    k, vt, iklo, ikhi, cb, sb, ci, si, iwt = _kvprep(
        x, row(norm_gain), w_small, positions.reshape(bsz, s, 1).astype(jnp.int32), _rope_tables(),
        row(k_norm_gain), pad_lanes(idx_k_ln_gain), pad_lanes(idx_k_ln_bias))

    t_a = _gmlp(proj, row(gmlp_ln_gain), row(gmlp_ln_bias), spatial_w.astype(F32), spatial_b.T.astype(F32))
    km, vm = _memkv(mem, row(mem_norm_gain), w_mem_kv.astype(BF16), row(mem_k_norm_gain))
    t_m = _memattn(proj, km, vm, row(mem_q_norm_gain), s)
    t_b = _dsa(proj, (cb, sb, ci, si), iwt, k, vt, iklo, ikhi, row(q_norm_gain), bsz, s)

    out = _merge(t_a, t_b, t_m, proj, x2, w_branch_a.astype(BF16), w_branch_b.astype(BF16),
                 w_branch_m.astype(BF16), w_out.astype(BF16))
    return out.reshape(bsz, s, D_MODEL)


def kernel(x, mem, positions, norm_gain, w_in, gmlp_ln_gain, gmlp_ln_bias, spatial_w, spatial_b, w_branch_a, q_norm_gain, k_norm_gain, idx_k_ln_gain, idx_k_ln_bias, w_branch_b, mem_norm_gain, w_mem_kv, mem_q_norm_gain, mem_k_norm_gain, w_branch_m, w_out):
    for l in range(norm_gain.shape[0]):
        x = _layer(x, mem, positions, norm_gain[l], w_in[l], gmlp_ln_gain[l], gmlp_ln_bias[l],
                   spatial_w[l], spatial_b[l], w_branch_a[l], q_norm_gain[l], k_norm_gain[l],
                   idx_k_ln_gain[l], idx_k_ln_bias[l], w_branch_b[l], mem_norm_gain[l], w_mem_kv[l],
                   mem_q_norm_gain[l], mem_k_norm_gain[l], w_branch_m[l], w_out[l])
    return x
```

```python
import functools

import numpy as np
import jax
import jax.numpy as jnp
from jax import lax
from jax.experimental import pallas as pl
from jax.experimental.pallas import tpu as pltpu

F32 = jnp.float32
BF16 = jnp.bfloat16

D_MODEL = 2048
ROPE_THETA = 500000.0
EPS = 1e-6
A_GROUPS = 8
A_GROUP_DIM = 128
A_WIDTH = A_GROUPS * A_GROUP_DIM
CHUNK = 128
B_HEADS = 8
B_KV_HEADS = 2
B_GROUP = B_HEADS // B_KV_HEADS
B_HEAD_DIM = 128
B_WIDTH = B_HEADS * B_HEAD_DIM
IDX_HEADS = 16
IDX_DIM = 64
TOPK_MAX = 256
M_HEADS = 4
M_HEAD_DIM = 256
M_WIDTH = M_HEADS * M_HEAD_DIM

SPLIT_SIZES = (
    A_WIDTH, A_WIDTH, A_WIDTH,
    B_WIDTH, B_KV_HEADS * B_HEAD_DIM, B_KV_HEADS * B_HEAD_DIM, B_WIDTH,
    IDX_HEADS * IDX_DIM, IDX_DIM, IDX_HEADS,
    M_WIDTH, M_WIDTH,
    D_MODEL, D_MODEL, D_MODEL,
)

LANES = 128
BLK = 1024
COL_AU, COL_AV, COL_AZ, COL_BQ, COL_BZ, COL_IQ, COL_MQ, COL_MZ = range(8)
COL_GA, COL_GB, COL_GM = 8, 10, 12
NB_MAIN = 14
_OFFS = [int(o) for o in np.concatenate([[0], np.cumsum(SPLIT_SIZES)])]
ROW_ALIGN = 16
MAIN_START = ([_OFFS[i] for i in (0, 1, 2, 3, 6, 7, 10, 11)]
              + [_OFFS[i] + d for i in (12, 13, 14) for d in (0, BLK)])
assert all(s % ROW_ALIGN == 0 for s in MAIN_START)
SMALL_KV_START, SMALL_KV_WIDTH = _OFFS[4], _OFFS[6] - _OFFS[4]
SMALL_IDX_START = _OFFS[8]
assert SMALL_KV_START % SMALL_KV_WIDTH == 0 and SMALL_IDX_START % LANES == 0
assert IDX_DIM + IDX_HEADS <= LANES and _OFFS[9] == SMALL_IDX_START + IDX_DIM

VMEM_LIMIT = 56 * 1024 * 1024
LOG2E = 1.4426950408889634
NEG_BIG = -1e30
F32_MAX = 3.4028234663852886e38

TQ = 128
CK = 256
CB = 512
COUNT_WAYS = 8
SEARCH_MARGIN = 0.05
SEARCH_FIXED_PASSES = 8
SEARCH_INTERP_PASSES = 16
MAX_SEARCH_ITERS = 400


def _cparams(sem):
    return pltpu.CompilerParams(dimension_semantics=sem, vmem_limit_bytes=VMEM_LIMIT)


def _dot_nt(a, b):
    return lax.dot_general(a, b, (((1,), (1,)), ((), ())), preferred_element_type=F32)


def _rope(x, cos_t, sin_t, half, period):
    lane = lax.broadcasted_iota(jnp.int32, x.shape, 1) % period
    rolled = jnp.where(lane < half, pltpu.roll(x, LANES - half, 1), pltpu.roll(x, half, 1))
    return x * cos_t + rolled * sin_t


def _rms_rows(x, gain):
    ms = jnp.mean(x * x, axis=-1, keepdims=True)
    return (x * lax.rsqrt(ms + EPS) * gain).astype(BF16)


def _inproj_kernel(tab_ref, x_ref, g_ref, wt_ref, proj_ref, h_ref):
    del tab_ref
    @pl.when(pl.program_id(1) == 0)
    def _():
        h_ref[...] = _rms_rows(x_ref[...], g_ref[...])

    proj_ref[...] = _dot_nt(h_ref[...], wt_ref[...].astype(BF16)).astype(BF16)


def _inproj(x2, gain, w_t):
    m = x2.shape[0]
    tm = min(1024, m)
    tab = jnp.asarray(np.array([s // ROW_ALIGN for s in MAIN_START], np.int32))
    return pl.pallas_call(
        _inproj_kernel,
        grid_spec=pltpu.PrefetchScalarGridSpec(
            num_scalar_prefetch=1,
            grid=(m // tm, NB_MAIN),
            in_specs=[
                pl.BlockSpec((tm, D_MODEL), lambda i, n, tab: (i, 0)),
                pl.BlockSpec((1, D_MODEL), lambda i, n, tab: (0, 0)),
                pl.BlockSpec((pl.Element(BLK), pl.Element(D_MODEL)), lambda i, n, tab: (tab[n] * ROW_ALIGN, 0)),
            ],
            out_specs=pl.BlockSpec((tm, BLK), lambda i, n, tab: (i, n)),
            scratch_shapes=[pltpu.VMEM((tm, D_MODEL), BF16)],
        ),
        out_shape=jax.ShapeDtypeStruct((m, NB_MAIN * BLK), BF16),
        compiler_params=_cparams(("parallel", "arbitrary")),
        name="inproj",
    )(tab, x2, gain, w_t)


def _kvprep_kernel(x_ref, g_ref, wkv_ref, widx_ref, pos_ref, fb_ref, sgb_ref, fi_ref, sgi_ref, gk_ref,
                   lng_ref, lnb_ref, k_ref, vt_ref, iklo_ref, ikhi_ref, cb_ref, sb_ref, ci_ref, si_ref,
                   iwt_ref):
    h = _rms_rows(x_ref[...], g_ref[...])
    kv = _dot_nt(h, wkv_ref[...].astype(BF16))
    ikp = _dot_nt(h, widx_ref[...].astype(BF16))
    pos = pos_ref[0].astype(F32)
    ang_b = pos * fb_ref[...]
    cos_b = jnp.cos(ang_b)
    sin_b = jnp.sin(ang_b) * sgb_ref[...]
    ang_i = pos * fi_ref[...]
    cos_i = jnp.cos(ang_i)
    sin_i = jnp.sin(ang_i) * sgi_ref[...]
    cb_ref[0] = cos_b
    sb_ref[0] = sin_b
    ci_ref[0] = cos_i
    si_ref[0] = sin_i

    for n in range(B_KV_HEADS):
        kh = kv[:, n * B_HEAD_DIM:(n + 1) * B_HEAD_DIM]
        r = lax.rsqrt(jnp.mean(kh * kh, axis=-1, keepdims=True) + EPS)
        kn = kh * r * gk_ref[...]
        k_ref[0, n] = _rope(kn, cos_b, sin_b, B_HEAD_DIM // 8, LANES).astype(BF16)
        vt = kv[:, (B_KV_HEADS + n) * B_HEAD_DIM:(B_KV_HEADS + n + 1) * B_HEAD_DIM].T
        for c in range(vt.shape[1] // CB):
            vt_ref[0, n, c] = vt[:, c * CB:(c + 1) * CB].astype(BF16)

    lane = lax.broadcasted_iota(jnp.int32, ikp.shape, 1)
    live = lane < IDX_DIM
    mu = jnp.sum(jnp.where(live, ikp, 0.0), axis=-1, keepdims=True) * (1.0 / IDX_DIM)
    d = jnp.where(live, ikp - mu, 0.0)
    var = jnp.sum(d * d, axis=-1, keepdims=True) * (1.0 / IDX_DIM)
    y = d * lax.rsqrt(var + EPS) * lng_ref[...] + lnb_ref[...]
    yr = jnp.where(live, _rope(y, cos_i, sin_i, IDX_DIM // 8, IDX_DIM), 0.0)
    iklo_ref[0] = yr.astype(BF16)
    ikhi_ref[0] = pltpu.roll(yr, IDX_DIM, 1).astype(BF16)

    iw = ikp * (IDX_DIM ** -0.5 * IDX_HEADS ** -0.5)
    iwt_ref[0] = iw.T[IDX_DIM:IDX_DIM + IDX_HEADS, :]


def _kvprep(x3, gain, w_t, pos3, tabs, gk, lng, lnb):
    bsz, s, _ = x3.shape
    tp = min(512, s)
    row = lambda b, i: (b, i, 0)
    const = lambda b, i: (0, 0)
    tab_spec = pl.BlockSpec((1, LANES), const)
    out_tok = pl.BlockSpec((1, tp, LANES), row)
    out_k = pl.BlockSpec((1, B_KV_HEADS, tp, B_HEAD_DIM), lambda b, i: (b, 0, i, 0))
    out_vt = pl.BlockSpec((1, B_KV_HEADS, tp // CB, B_HEAD_DIM, CB), lambda b, i: (b, 0, i, 0, 0))
    out_iwt = pl.BlockSpec((1, IDX_HEADS, tp), lambda b, i: (b, 0, i))
    return pl.pallas_call(
        _kvprep_kernel,
        grid=(bsz, s // tp),
        in_specs=[pl.BlockSpec((None, tp, D_MODEL), row), pl.BlockSpec((1, D_MODEL), const),
                  pl.BlockSpec((SMALL_KV_WIDTH, D_MODEL), lambda b, i: (SMALL_KV_START // SMALL_KV_WIDTH, 0)),
                  pl.BlockSpec((LANES, D_MODEL), lambda b, i: (SMALL_IDX_START // LANES, 0)),
                  pl.BlockSpec((1, tp, 1), row),
                  tab_spec, tab_spec, tab_spec, tab_spec, tab_spec, tab_spec, tab_spec],
        out_specs=[out_k, out_vt, out_tok, out_tok, out_tok, out_tok, out_tok, out_tok, out_iwt],
        out_shape=[
            jax.ShapeDtypeStruct((bsz, B_KV_HEADS, s, B_HEAD_DIM), BF16),
            jax.ShapeDtypeStruct((bsz, B_KV_HEADS, s // CB, B_HEAD_DIM, CB), BF16),
            jax.ShapeDtypeStruct((bsz, s, LANES), BF16),
            jax.ShapeDtypeStruct((bsz, s, LANES), BF16),
            jax.ShapeDtypeStruct((bsz, s, LANES), F32),
            jax.ShapeDtypeStruct((bsz, s, LANES), F32),
            jax.ShapeDtypeStruct((bsz, s, LANES), F32),
            jax.ShapeDtypeStruct((bsz, s, LANES), F32),
            jax.ShapeDtypeStruct((bsz, IDX_HEADS, s), F32),
        ],
        compiler_params=_cparams(("parallel", "parallel")),
        name="kvprep",
    )(x3, gain, w_t, w_t, pos3, *tabs, gk, lng, lnb)


def _gmlp_kernel(u_ref, v_ref, z_ref, lng_ref, lnb_ref, ws_ref, sbt_ref, o_ref):
    tm = u_ref.shape[0]
    u = jax.nn.gelu(u_ref[...].astype(F32))
    v = jax.nn.gelu(v_ref[...].astype(F32))
    mu = jnp.mean(v, axis=-1, keepdims=True)
    d = v - mu
    var = jnp.mean(d * d, axis=-1, keepdims=True)
    vn = (d * lax.rsqrt(var + EPS) * lng_ref[...] + lnb_ref[...]).astype(BF16)
    gate = u * jax.nn.silu(z_ref[...].astype(F32))
    tri = (lax.broadcasted_iota(jnp.int32, (CHUNK, CHUNK), 1)
           <= lax.broadcasted_iota(jnp.int32, (CHUNK, CHUNK), 0))
    for g in range(A_GROUPS):
        wg = jnp.where(tri, ws_ref[g], 0.0).astype(BF16)
        bias = sbt_ref[:, g:g + 1]
        cols = slice(g * A_GROUP_DIM, (g + 1) * A_GROUP_DIM)
        for c in range(tm // CHUNK):
            rows = slice(c * CHUNK, (c + 1) * CHUNK)
            sg = jnp.dot(wg, vn[rows, cols], preferred_element_type=F32) + bias
            o_ref[rows, cols] = (gate[rows, cols] * sg).astype(BF16)


def _gmlp(proj, lng, lnb, ws, sbt):
    m = proj.shape[0]
    tm = min(512, m)
    col = lambda c: pl.BlockSpec((tm, BLK), lambda i: (i, c))
    full = lambda shape: pl.BlockSpec(shape, lambda i: (0,) * len(shape))
    return pl.pallas_call(
        _gmlp_kernel,
        grid=(m // tm,),
        in_specs=[col(COL_AU), col(COL_AV), col(COL_AZ), full((1, A_WIDTH)), full((1, A_WIDTH)),
                  full((A_GROUPS, CHUNK, CHUNK)), full((CHUNK, A_GROUPS))],
        out_specs=pl.BlockSpec((tm, A_WIDTH), lambda i: (i, 0)),
        out_shape=jax.ShapeDtypeStruct((m, A_WIDTH), BF16),
        compiler_params=_cparams(("parallel",)),
        name="gmlp",
    )(proj, proj, proj, lng, lnb, ws, sbt)


def _memkv_kernel(mem_ref, g_ref, w_ref, gk_ref, km_ref, vm_ref):
    x = mem_ref[0]
    ms = jnp.mean(x * x, axis=-1, keepdims=True)
    h = (x * lax.rsqrt(ms + EPS) * g_ref[...]).astype(BF16)
    kv = jnp.dot(h, w_ref[...], preferred_element_type=F32)
    for hd in range(M_HEADS):
        kh = kv[:, hd * M_HEAD_DIM:(hd + 1) * M_HEAD_DIM]
        r = lax.rsqrt(jnp.mean(kh * kh, axis=-1, keepdims=True) + EPS)
        km_ref[0, hd] = (kh * r * gk_ref[...]).astype(BF16)
        vm_ref[0, hd] = kv[:, M_WIDTH + hd * M_HEAD_DIM:M_WIDTH + (hd + 1) * M_HEAD_DIM].astype(BF16)


def _memkv(mem, gain, w_kv, gk):
    bsz, ml, _ = mem.shape
    out = pl.BlockSpec((1, M_HEADS, ml, M_HEAD_DIM), lambda b: (b, 0, 0, 0))
    shp = jax.ShapeDtypeStruct((bsz, M_HEADS, ml, M_HEAD_DIM), BF16)
    return pl.pallas_call(
        _memkv_kernel,
        grid=(bsz,),
        in_specs=[pl.BlockSpec((1, ml, D_MODEL), lambda b: (b, 0, 0)),
                  pl.BlockSpec((1, D_MODEL), lambda b: (0, 0)),
                  pl.BlockSpec((D_MODEL, 2 * M_WIDTH), lambda b: (0, 0)),
                  pl.BlockSpec((1, M_HEAD_DIM), lambda b: (0, 0))],
        out_specs=[out, out],
        out_shape=[shp, shp],
        compiler_params=_cparams(("parallel",)),
        name="memkv",
    )(mem, gain, w_kv, gk)


def _memattn_kernel(q_ref, z_ref, km_ref, vm_ref, gq_ref, o_ref):
    qscale = M_HEAD_DIM ** -0.5 * LOG2E
    for hd in range(M_HEADS):
        cols = slice(hd * M_HEAD_DIM, (hd + 1) * M_HEAD_DIM)
        q = q_ref[:, cols].astype(F32)
        r = lax.rsqrt(jnp.mean(q * q, axis=-1, keepdims=True) + EPS)
        qn = (q * r * gq_ref[...] * qscale).astype(BF16)
        lg = _dot_nt(qn, km_ref[0, hd])
        p = jnp.exp2(lg - jnp.max(lg, axis=-1, keepdims=True))
        l = jnp.sum(p, axis=-1, keepdims=True)
        o = jnp.dot(p.astype(BF16), vm_ref[0, hd], preferred_element_type=F32) / l
        o_ref[:, cols] = (o * jax.nn.silu(z_ref[:, cols].astype(F32))).astype(BF16)


def _memattn(proj, km, vm, gq, s):
    m = proj.shape[0]
    tm = min(512, s)
    per_b = s // tm
    ml = km.shape[2]
    kv_spec = pl.BlockSpec((1, M_HEADS, ml, M_HEAD_DIM), lambda i: (i // per_b, 0, 0, 0))
    return pl.pallas_call(
        _memattn_kernel,
        grid=(m // tm,),
        in_specs=[pl.BlockSpec((tm, BLK), lambda i: (i, COL_MQ)),
                  pl.BlockSpec((tm, BLK), lambda i: (i, COL_MZ)),
                  kv_spec, kv_spec,
                  pl.BlockSpec((1, M_HEAD_DIM), lambda i: (0, 0))],
        out_specs=pl.BlockSpec((tm, M_WIDTH), lambda i: (i, 0)),
        out_shape=jax.ShapeDtypeStruct((m, M_WIDTH), BF16),
        compiler_params=_cparams(("parallel",)),
        name="memattn",
    )(proj, proj, km, vm, gq)


def _dsa_kernel(topk, q_ref, z_ref, iq_ref, cb_ref, sb_ref, ci_ref, si_ref, iwt_ref,
                k_ref, vt_ref, iklo_ref, ikhi_ref, gq_ref, tri_ref, o_ref,
                sc_ref, bias_ref, lg_ref, qs_ref, iqs_ref, acc_ref):
    qb = pl.program_id(1)
    nck = (qb * TQ + TQ + CK - 1) // CK
    nbig = (qb * TQ + TQ + CB - 1) // CB
    t_lane = qb * TQ + lax.broadcasted_iota(jnp.int32, (1, TQ), 1)
    gw = B_GROUP * TQ

    def fold8(a, ways=1):
        return a.reshape(a.shape[0] // (8 * ways), 8 * ways, a.shape[1])

    def paired_loop(trips, body, init):
        carry = lax.fori_loop(0, trips // 2, lambda i, c: body(2 * i + 1, body(2 * i, c)), init)
        return lax.cond(trips % 2 == 1, lambda c: body(trips - 1, c), lambda c: c, carry)

    cos_b, sin_b = cb_ref[0], sb_ref[0]
    cos_i, sin_i = ci_ref[0], si_ref[0]
    qscale = B_HEAD_DIM ** -0.5 * LOG2E
    ones_mat = jnp.ones((B_HEAD_DIM, B_HEAD_DIM), BF16)
    for h in range(B_HEADS):
        slab = q_ref[:, h * B_HEAD_DIM:(h + 1) * B_HEAD_DIM].astype(F32)
        sq = slab * slab
        sq_hi = sq.astype(BF16)
        sq_lo = (sq - sq_hi.astype(F32)).astype(BF16)
        ssq = (jnp.dot(sq_hi, ones_mat, preferred_element_type=F32)
               + jnp.dot(sq_lo, ones_mat, preferred_element_type=F32))
        r = lax.rsqrt(ssq * (1.0 / B_HEAD_DIM) + EPS)
        qr = _rope(slab * r * gq_ref[...], cos_b, sin_b, B_HEAD_DIM // 8, LANES) * qscale
        g = h % B_GROUP
        qs_ref[h // B_GROUP, g * TQ:(g + 1) * TQ, :] = qr.astype(BF16)
    for j in range(IDX_HEADS // 2):
        slab = iq_ref[:, j * LANES:(j + 1) * LANES].astype(F32)
        iqs_ref[j // 2, (j % 2) * TQ:(j % 2 + 1) * TQ, :] = (
            _rope(slab, cos_i, sin_i, IDX_DIM // 8, IDX_DIM).astype(BF16))
    wt = iwt_ref[0]

    def idx_body(c, carry):
        mn8, mx8, s1, s2 = carry
        off = pl.multiple_of(c * CK, CK)
        acc = jnp.zeros((CK, TQ), F32)
        for jj in range(IDX_HEADS // 4):
            rhs = iqs_ref[jj]
            for half, keys_ref in enumerate((iklo_ref, ikhi_ref)):
                d = _dot_nt(keys_ref[0, pl.ds(off, CK), :], rhs)
                ha, hb = 4 * jj + half, 4 * jj + 2 + half
                acc = (acc + jnp.maximum(d[:, :TQ], 0.0) * wt[ha:ha + 1, :]
                       + jnp.maximum(d[:, TQ:], 0.0) * wt[hb:hb + 1, :])
        key = off + lax.broadcasted_iota(jnp.int32, (CK, TQ), 0)
        causal = key <= t_lane
        sc = jnp.where(causal, acc, -jnp.inf)
        sc_ref[pl.ds(off, CK), :] = sc
        live = jnp.where(causal, acc, 0.0)
        mn8 = jnp.minimum(mn8, jnp.min(fold8(jnp.where(causal, acc, jnp.inf)), axis=0))
        mx8 = jnp.maximum(mx8, jnp.max(fold8(sc), axis=0))
        s1 = s1 + jnp.sum(fold8(live), axis=0)
        s2 = s2 + jnp.sum(fold8(live * live), axis=0)
        return mn8, mx8, s1, s2

    zero8 = jnp.zeros((8, TQ), F32)
    stats = (jnp.full((8, TQ), jnp.inf, F32), jnp.full((8, TQ), -jnp.inf, F32), zero8, zero8)
    mn8, mx8, s1, s2 = paired_loop(nck, idx_body, stats)
    row_min = jnp.min(mn8, axis=0, keepdims=True)
    row_max = jnp.max(mx8, axis=0, keepdims=True)

    def fill_body(c, carry):
        sc_ref[pl.ds(pl.multiple_of(c * CK, CK), CK), :] = jnp.full((CK, TQ), -jnp.inf, F32)
        return carry

    lax.fori_loop(nck, nbig * (CB // CK), fill_body, 0)

    def count_rows(pred):
        def body(c, cnt):
            blk = sc_ref[pl.ds(pl.multiple_of(c * CB, CB), CB), :]
            return cnt + jnp.sum(fold8(jnp.where(pred(blk), 1.0, 0.0), COUNT_WAYS), axis=0)

        cnt = lax.fori_loop(0, nbig, body, jnp.zeros((8 * COUNT_WAYS, TQ), F32))
        return jnp.sum(cnt, axis=0, keepdims=True)

    kf = float(topk)
    n_valid = (t_lane + 1).astype(F32)
    all_rows = (t_lane + 1) <= topk

    def search_pass(st, probe, stuck):
        lo, hi, clo, chi, thr, fin, tie = st
        cnt = count_rows(lambda blk: blk >= probe)
        active = fin < 0.5
        hit = cnt == kf
        end_thr = jnp.where(jnp.logical_and(stuck, cnt < kf), lo, probe)
        ends = jnp.logical_and(active, jnp.logical_or(hit, stuck))
        thr = jnp.where(ends, end_thr, thr)
        tie = jnp.where(jnp.logical_and(ends, jnp.logical_not(hit)), 1.0, tie)
        fin = jnp.where(ends, 1.0, fin)
        up = jnp.logical_and(active, cnt > kf)
        dn = jnp.logical_and(active, cnt < kf)
        return (jnp.where(up, probe, lo), jnp.where(dn, probe, hi), jnp.where(up, cnt, clo),
                jnp.where(dn, cnt, chi), thr, fin, tie)

    def next_probe(st, halve=False):
        lo, hi, clo, chi = st[:4]
        frac = jnp.clip((clo - kf + 0.5) / (clo - chi), SEARCH_MARGIN, 1.0 - SEARCH_MARGIN)
        probe = lo + (hi - lo) * frac
        outside = jnp.logical_or(jnp.logical_or(probe <= lo, probe >= hi), halve)
        probe = jnp.where(outside, lo * 0.5 + hi * 0.5, probe)
        stuck = jnp.logical_or(probe <= lo, probe >= hi)
        return jnp.where(stuck, hi, probe), stuck

    mean = jnp.sum(s1, axis=0, keepdims=True) / n_valid
    var = jnp.maximum(jnp.sum(s2, axis=0, keepdims=True) / n_valid - mean * mean, 0.0)
    tail = jnp.clip(kf / n_valid, 1e-6, 1.0 - 1e-6)
    tq = jnp.sqrt(-2.0 * jnp.log(jnp.minimum(tail, 1.0 - tail)))
    zq = tq - ((0.010328 * tq + 0.802853) * tq + 2.515517) / (
        ((0.001308 * tq + 0.189269) * tq + 1.432788) * tq + 1.0)
    probe0 = jnp.clip(mean + jnp.where(tail < 0.5, zq, -zq) * jnp.sqrt(var), row_min, row_max)

    ones = jnp.ones((1, TQ), F32)
    st = (row_min, row_max, n_valid, 0.0 * ones, jnp.where(all_rows, -F32_MAX, row_max),
          jnp.where(all_rows, 1.0, 0.0), 0.0 * ones)
    st = search_pass(st, probe0, probe0 < row_min)

    def fixed_body(i, st):
        return search_pass(st, *next_probe(st))

    st = lax.fori_loop(0, SEARCH_FIXED_PASSES, fixed_body, st)

    def more_cond(c):
        return jnp.logical_and(c[0] < MAX_SEARCH_ITERS, jnp.min(c[1][5]) < 0.5)

    def more_body(c):
        return c[0] + 1, search_pass(c[1], *next_probe(c[1], c[0] >= SEARCH_INTERP_PASSES))

    _, st = lax.while_loop(more_cond, more_body, (jnp.int32(0), st))
    thr, tie = st[4], st[6]

    def plain_mask():
        def body(c, carry):
            off = pl.multiple_of(c * CB, CB)
            bias_ref[pl.ds(off, CB), :] = jnp.where(sc_ref[pl.ds(off, CB), :] >= thr, 0.0, NEG_BIG)
            return carry

        lax.fori_loop(0, nbig, body, 0)

    def tied_mask():
        n_gt = count_rows(lambda blk: blk > thr)
        need = jnp.where(all_rows, F32_MAX, kf - n_gt)

        def body(c, seen):
            off = pl.multiple_of(c * CB, CB)
            blk = sc_ref[pl.ds(off, CB), :]
            eq = blk == thr
            pref = jnp.dot(tri_ref[...], jnp.where(eq, 1.0, 0.0).astype(BF16),
                           preferred_element_type=F32) + seen
            keep = jnp.logical_or(blk > thr, jnp.logical_and(eq, pref <= need))
            bias_ref[pl.ds(off, CB), :] = jnp.where(keep, 0.0, NEG_BIG)
            return pref[CB - 1:CB, :]

        lax.fori_loop(0, nbig, body, jnp.zeros((1, TQ), F32))

    lax.cond(jnp.max(tie) > 0.5, tied_mask, plain_mask)

    def logit_body(c, mx):
        off = pl.multiple_of(c * CB, CB)
        b = bias_ref[pl.ds(off, CB), :]
        out = []
        for n in range(B_KV_HEADS):
            lg = _dot_nt(k_ref[0, n, pl.ds(off, CB), :], qs_ref[n])
            lg = jnp.concatenate([lg[:, g * TQ:(g + 1) * TQ] + b for g in range(B_GROUP)], axis=1)
            lg_ref[n, pl.ds(off, CB), :] = lg
            out.append(jnp.maximum(mx[n], jnp.max(fold8(lg), axis=0)))
        return tuple(out)

    mx = paired_loop(nbig, logit_body, (jnp.full((8, gw), NEG_BIG, F32),) * B_KV_HEADS)
    m = [jnp.max(a, axis=0, keepdims=True) for a in mx]
    acc_ref[...] = jnp.zeros(acc_ref.shape, F32)

    def pv_body(c, l8):
        off = pl.multiple_of(c * CB, CB)
        out = []
        for n in range(B_KV_HEADS):
            p = jnp.exp2(lg_ref[n, pl.ds(off, CB), :] - m[n])
            acc_ref[n] += jnp.dot(vt_ref[0, n, c], p.astype(BF16), preferred_element_type=F32)
            out.append(l8[n] + jnp.sum(fold8(p), axis=0))
        return tuple(out)

    l8 = paired_loop(nbig, pv_body, (jnp.zeros((8, gw), F32),) * B_KV_HEADS)
    for n in range(B_KV_HEADS):
        o_t = acc_ref[n] / jnp.sum(l8[n], axis=0, keepdims=True)
        for g in range(B_GROUP):
            cols = slice((n * B_GROUP + g) * B_HEAD_DIM, (n * B_GROUP + g + 1) * B_HEAD_DIM)
            o = o_t[:, g * TQ:(g + 1) * TQ].T
            o_ref[:, cols] = (o * jax.nn.silu(z_ref[:, cols].astype(F32))).astype(BF16)


def _dsa(proj, tabs, iwt, k, vt, iklo, ikhi, gq, bsz, s):
    nq = s // TQ
    topk = min(TOPK_MAX, s // 4)
    gw = B_GROUP * TQ
    rowblk = lambda c: pl.BlockSpec((TQ, BLK), lambda b, i: (b * nq + i, c))
    tok = pl.BlockSpec((1, TQ, LANES), lambda b, i: (b, i, 0))
    ik = pl.BlockSpec((1, s, LANES), lambda b, i: (b, 0, 0))
    tri = jnp.tril(jnp.ones((CB, CB), BF16))
    return pl.pallas_call(
        functools.partial(_dsa_kernel, topk),
        grid=(bsz, nq),
        in_specs=[rowblk(COL_BQ), rowblk(COL_BZ), rowblk(COL_IQ), tok, tok, tok, tok,
                  pl.BlockSpec((1, IDX_HEADS, TQ), lambda b, i: (b, 0, i)),
                  pl.BlockSpec((1, B_KV_HEADS, s, B_HEAD_DIM), lambda b, i: (b, 0, 0, 0)),
                  pl.BlockSpec((1, B_KV_HEADS, s // CB, B_HEAD_DIM, CB), lambda b, i: (b, 0, 0, 0, 0)),
                  ik, ik,
                  pl.BlockSpec((1, B_HEAD_DIM), lambda b, i: (0, 0)),
                  pl.BlockSpec((CB, CB), lambda b, i: (0, 0))],
        out_specs=pl.BlockSpec((TQ, B_WIDTH), lambda b, i: (b * nq + i, 0)),
        out_shape=jax.ShapeDtypeStruct((bsz * s, B_WIDTH), BF16),
        scratch_shapes=[
            pltpu.VMEM((s, TQ), F32),
            pltpu.VMEM((s, TQ), F32),
            pltpu.VMEM((B_KV_HEADS, s, gw), F32),
            pltpu.VMEM((B_KV_HEADS, gw, B_HEAD_DIM), BF16),
            pltpu.VMEM((IDX_HEADS // 4, 2 * TQ, LANES), BF16),
            pltpu.VMEM((B_KV_HEADS, B_HEAD_DIM, gw), F32),
        ],
        compiler_params=_cparams(("parallel", "arbitrary")),
        name="dsa",
    )(proj, proj, proj, *tabs, iwt, k, vt, iklo, ikhi, gq, tri)


def _merge_kernel(ta_ref, tb_ref, tm_ref, ga_ref, gb_ref, gm_ref, x_ref,
                  wa_ref, wb_ref, wm_ref, wo_ref, o_ref):
    def branch(t_ref, g_ref, w_ref):
        y = jnp.dot(t_ref[...], w_ref[...], preferred_element_type=F32)
        return jax.nn.sigmoid(g_ref[...].astype(F32)) * y

    merged = branch(ta_ref, ga_ref, wa_ref) + branch(tb_ref, gb_ref, wb_ref) + branch(tm_ref, gm_ref, wm_ref)
    o_ref[...] = x_ref[...] + jnp.dot(merged.astype(BF16), wo_ref[...], preferred_element_type=F32)


def _merge(ta, tb, tmem, proj, x2, wa, wb, wm, wo):
    m = x2.shape[0]
    tm = min(256, m)
    act = pl.BlockSpec((tm, BLK), lambda i: (i, 0))
    gate = lambda c: pl.BlockSpec((tm, D_MODEL), lambda i: (i, c // 2))
    wide = pl.BlockSpec((tm, D_MODEL), lambda i: (i, 0))
    wbr = pl.BlockSpec((BLK, D_MODEL), lambda i: (0, 0), pipeline_mode=pl.Buffered(1))
    wout = pl.BlockSpec((D_MODEL, D_MODEL), lambda i: (0, 0), pipeline_mode=pl.Buffered(1))
    return pl.pallas_call(
        _merge_kernel,
        grid=(m // tm,),
        in_specs=[act, act, act, gate(COL_GA), gate(COL_GB), gate(COL_GM), wide, wbr, wbr, wbr, wout],
        out_specs=wide,
        out_shape=jax.ShapeDtypeStruct((m, D_MODEL), F32),
        compiler_params=_cparams(("parallel",)),
        name="merge",
    )(ta, tb, tmem, proj, proj, proj, x2, wa, wb, wm, wo)


def _rope_tables():
    def build(rot, period):
        half = rot // 2
        inv = ROPE_THETA ** (-np.arange(half, dtype=np.float32) / half)
        lane = np.arange(LANES) % period
        freq = np.where(lane < rot, inv[lane % half], 0.0).astype(np.float32)
        sign = np.where(lane < half, -1.0, np.where(lane < rot, 1.0, 0.0)).astype(np.float32)
        return jnp.asarray(freq[None, :]), jnp.asarray(sign[None, :])

    fb, sgb = build(B_HEAD_DIM // 4, LANES)
    fi, sgi = build(IDX_DIM // 4, IDX_DIM)
    return fb, sgb, fi, sgi


def _layer(x, mem, positions, norm_gain, w_in, gmlp_ln_gain, gmlp_ln_bias, spatial_w, spatial_b,
           w_branch_a, q_norm_gain, k_norm_gain, idx_k_ln_gain, idx_k_ln_bias, w_branch_b,
           mem_norm_gain, w_mem_kv, mem_q_norm_gain, mem_k_norm_gain, w_branch_m, w_out):
    bsz, s, _ = x.shape
    m = bsz * s
    row = lambda a: a.reshape(1, -1).astype(F32)
    pad_lanes = lambda a: jnp.pad(a.reshape(1, -1).astype(F32), ((0, 0), (0, LANES - a.shape[-1])))
    x2 = x.reshape(m, D_MODEL)

    w_t = w_in.T
    proj = _inproj(x2, row(norm_gain), w_t)
    k, vt, iklo, ikhi, cb, sb, ci, si, iwt = _kvprep(
        x, row(norm_gain), w_t, positions.reshape(bsz, s, 1).astype(jnp.int32), _rope_tables(),
        row(k_norm_gain), pad_lanes(idx_k_ln_gain), pad_lanes(idx_k_ln_bias))

    t_a = _gmlp(proj, row(gmlp_ln_gain), row(gmlp_ln_bias), spatial_w.astype(F32), spatial_b.T.astype(F32))
    km, vm = _memkv(mem, row(mem_norm_gain), w_mem_kv.astype(BF16), row(mem_k_norm_gain))
    t_m = _memattn(proj, km, vm, row(mem_q_norm_gain), s)
    t_b = _dsa(proj, (cb, sb, ci, si), iwt, k, vt, iklo, ikhi, row(q_norm_gain), bsz, s)

    out = _merge(t_a, t_b, t_m, proj, x2, w_branch_a.astype(BF16), w_branch_b.astype(BF16),
                 w_branch_m.astype(BF16), w_out.astype(BF16))
    return out.reshape(bsz, s, D_MODEL)


def kernel(x, mem, positions, norm_gain, w_in, gmlp_ln_gain, gmlp_ln_bias, spatial_w, spatial_b, w_branch_a, q_norm_gain, k_norm_gain, idx_k_ln_gain, idx_k_ln_bias, w_branch_b, mem_norm_gain, w_mem_kv, mem_q_norm_gain, mem_k_norm_gain, w_branch_m, w_out):
    for l in range(norm_gain.shape[0]):
        x = _layer(x, mem, positions, norm_gain[l], w_in[l], gmlp_ln_gain[l], gmlp_ln_bias[l],
                   spatial_w[l], spatial_b[l], w_branch_a[l], q_norm_gain[l], k_norm_gain[l],
                   idx_k_ln_gain[l], idx_k_ln_bias[l], w_branch_b[l], mem_norm_gain[l], w_mem_kv[l],
                   mem_q_norm_gain[l], mem_k_norm_gain[l], w_branch_m[l], w_out[l])
    return x
```

```python
import functools

import numpy as np
import jax
import jax.numpy as jnp
from jax import lax
from jax.experimental import pallas as pl
from jax.experimental.pallas import tpu as pltpu

F32 = jnp.float32
BF16 = jnp.bfloat16

D_MODEL = 2048
ROPE_THETA = 500000.0
EPS = 1e-6
A_GROUPS = 8
A_GROUP_DIM = 128
A_WIDTH = A_GROUPS * A_GROUP_DIM
CHUNK = 128
B_HEADS = 8
B_KV_HEADS = 2
B_GROUP = B_HEADS // B_KV_HEADS
B_HEAD_DIM = 128
B_WIDTH = B_HEADS * B_HEAD_DIM
IDX_HEADS = 16
IDX_DIM = 64
TOPK_MAX = 256
M_HEADS = 4
M_HEAD_DIM = 256
M_WIDTH = M_HEADS * M_HEAD_DIM

SPLIT_SIZES = (
    A_WIDTH, A_WIDTH, A_WIDTH,
    B_WIDTH, B_KV_HEADS * B_HEAD_DIM, B_KV_HEADS * B_HEAD_DIM, B_WIDTH,
    IDX_HEADS * IDX_DIM, IDX_DIM, IDX_HEADS,
    M_WIDTH, M_WIDTH,
    D_MODEL, D_MODEL, D_MODEL,
)

LANES = 128
BLK = 1024
COL_AU, COL_AV, COL_AZ, COL_BQ, COL_BZ, COL_IQ, COL_MQ, COL_MZ = range(8)
COL_GA, COL_GB, COL_GM = 8, 10, 12
NB_MAIN = 14
_OFFS = [int(o) for o in np.concatenate([[0], np.cumsum(SPLIT_SIZES)])]
ROW_ALIGN = 16
MAIN_START = ([_OFFS[i] for i in (0, 1, 2, 3, 6, 7, 10, 11)]
              + [_OFFS[i] + d for i in (12, 13, 14) for d in (0, BLK)])
assert all(s % ROW_ALIGN == 0 for s in MAIN_START)
SMALL_KV_START, SMALL_KV_WIDTH = _OFFS[4], _OFFS[6] - _OFFS[4]
SMALL_IDX_START = _OFFS[8]
assert SMALL_KV_START % SMALL_KV_WIDTH == 0 and SMALL_IDX_START % LANES == 0
assert IDX_DIM + IDX_HEADS <= LANES and _OFFS[9] == SMALL_IDX_START + IDX_DIM

VMEM_LIMIT = 56 * 1024 * 1024
LOG2E = 1.4426950408889634
NEG_BIG = -1e30
F32_MAX = 3.4028234663852886e38

TQ = 256
CK = 256
CB = 512
COUNT_WAYS = 8
SEARCH_MARGIN = 0.05
SEARCH_FIXED_PASSES = 8
SEARCH_INTERP_PASSES = 16
MAX_SEARCH_ITERS = 400


def _cparams(sem):
    return pltpu.CompilerParams(dimension_semantics=sem, vmem_limit_bytes=VMEM_LIMIT)


def _dot_nt(a, b):
    return lax.dot_general(a, b, (((1,), (1,)), ((), ())), preferred_element_type=F32)


def _rope(x, cos_t, sin_t, half, period):
    lane = lax.broadcasted_iota(jnp.int32, x.shape, 1) % period
    rolled = jnp.where(lane < half, pltpu.roll(x, LANES - half, 1), pltpu.roll(x, half, 1))
    return x * cos_t + rolled * sin_t


def _rms_rows(x, gain):
    ms = jnp.mean(x * x, axis=-1, keepdims=True)
    return (x * lax.rsqrt(ms + EPS) * gain).astype(BF16)


def _inproj_kernel(tab_ref, x_ref, g_ref, wt_ref, proj_ref, h_ref):
    del tab_ref
    @pl.when(pl.program_id(1) == 0)
    def _():
        h_ref[...] = _rms_rows(x_ref[...], g_ref[...])

    proj_ref[...] = _dot_nt(h_ref[...], wt_ref[...].astype(BF16)).astype(BF16)


def _inproj(x2, gain, w_t):
    m = x2.shape[0]
    tm = min(1024, m)
    tab = jnp.asarray(np.array([s // ROW_ALIGN for s in MAIN_START], np.int32))
    return pl.pallas_call(
        _inproj_kernel,
        grid_spec=pltpu.PrefetchScalarGridSpec(
            num_scalar_prefetch=1,
            grid=(m // tm, NB_MAIN),
            in_specs=[
                pl.BlockSpec((tm, D_MODEL), lambda i, n, tab: (i, 0)),
                pl.BlockSpec((1, D_MODEL), lambda i, n, tab: (0, 0)),
                pl.BlockSpec((pl.Element(BLK), pl.Element(D_MODEL)), lambda i, n, tab: (tab[n] * ROW_ALIGN, 0)),
            ],
            out_specs=pl.BlockSpec((tm, BLK), lambda i, n, tab: (i, n)),
            scratch_shapes=[pltpu.VMEM((tm, D_MODEL), BF16)],
        ),
        out_shape=jax.ShapeDtypeStruct((m, NB_MAIN * BLK), BF16),
        compiler_params=_cparams(("parallel", "arbitrary")),
        name="inproj",
    )(tab, x2, gain, w_t)


def _kvprep_kernel(x_ref, g_ref, wkv_ref, widx_ref, pos_ref, fb_ref, sgb_ref, fi_ref, sgi_ref, gk_ref,
                   lng_ref, lnb_ref, k_ref, vt_ref, iklo_ref, ikhi_ref, cb_ref, sb_ref, ci_ref, si_ref,
                   iwt_ref):
    h = _rms_rows(x_ref[...], g_ref[...])
    kv = _dot_nt(h, wkv_ref[...].astype(BF16))
    ikp = _dot_nt(h, widx_ref[...].astype(BF16))
    pos = pos_ref[0].astype(F32)
    ang_b = pos * fb_ref[...]
    cos_b = jnp.cos(ang_b)
    sin_b = jnp.sin(ang_b) * sgb_ref[...]
    ang_i = pos * fi_ref[...]
    cos_i = jnp.cos(ang_i)
    sin_i = jnp.sin(ang_i) * sgi_ref[...]
    cb_ref[0] = cos_b
    sb_ref[0] = sin_b
    ci_ref[0] = cos_i
    si_ref[0] = sin_i

    for n in range(B_KV_HEADS):
        kh = kv[:, n * B_HEAD_DIM:(n + 1) * B_HEAD_DIM]
        r = lax.rsqrt(jnp.mean(kh * kh, axis=-1, keepdims=True) + EPS)
        kn = kh * r * gk_ref[...]
        k_ref[0, n] = _rope(kn, cos_b, sin_b, B_HEAD_DIM // 8, LANES).astype(BF16)
        vt = kv[:, (B_KV_HEADS + n) * B_HEAD_DIM:(B_KV_HEADS + n + 1) * B_HEAD_DIM].T
        for c in range(vt.shape[1] // CB):
            vt_ref[0, n, c] = vt[:, c * CB:(c + 1) * CB].astype(BF16)

    lane = lax.broadcasted_iota(jnp.int32, ikp.shape, 1)
    live = lane < IDX_DIM
    mu = jnp.sum(jnp.where(live, ikp, 0.0), axis=-1, keepdims=True) * (1.0 / IDX_DIM)
    d = jnp.where(live, ikp - mu, 0.0)
    var = jnp.sum(d * d, axis=-1, keepdims=True) * (1.0 / IDX_DIM)
    y = d * lax.rsqrt(var + EPS) * lng_ref[...] + lnb_ref[...]
    yr = jnp.where(live, _rope(y, cos_i, sin_i, IDX_DIM // 8, IDX_DIM), 0.0)
    iklo_ref[0] = yr.astype(BF16)
    ikhi_ref[0] = pltpu.roll(yr, IDX_DIM, 1).astype(BF16)

    iw = ikp * (IDX_DIM ** -0.5 * IDX_HEADS ** -0.5)
    iwt_ref[0] = iw.T[IDX_DIM:IDX_DIM + IDX_HEADS, :]


def _kvprep(x3, gain, w_t, pos3, tabs, gk, lng, lnb):
    bsz, s, _ = x3.shape
    tp = min(512, s)
    row = lambda b, i: (b, i, 0)
    const = lambda b, i: (0, 0)
    tab_spec = pl.BlockSpec((1, LANES), const)
    out_tok = pl.BlockSpec((1, tp, LANES), row)
    out_k = pl.BlockSpec((1, B_KV_HEADS, tp, B_HEAD_DIM), lambda b, i: (b, 0, i, 0))
    out_vt = pl.BlockSpec((1, B_KV_HEADS, tp // CB, B_HEAD_DIM, CB), lambda b, i: (b, 0, i, 0, 0))
    out_iwt = pl.BlockSpec((1, IDX_HEADS, tp), lambda b, i: (b, 0, i))
    return pl.pallas_call(
        _kvprep_kernel,
        grid=(bsz, s // tp),
        in_specs=[pl.BlockSpec((None, tp, D_MODEL), row), pl.BlockSpec((1, D_MODEL), const),
                  pl.BlockSpec((SMALL_KV_WIDTH, D_MODEL), lambda b, i: (SMALL_KV_START // SMALL_KV_WIDTH, 0)),
                  pl.BlockSpec((LANES, D_MODEL), lambda b, i: (SMALL_IDX_START // LANES, 0)),
                  pl.BlockSpec((1, tp, 1), row),
                  tab_spec, tab_spec, tab_spec, tab_spec, tab_spec, tab_spec, tab_spec],
        out_specs=[out_k, out_vt, out_tok, out_tok, out_tok, out_tok, out_tok, out_tok, out_iwt],
        out_shape=[
            jax.ShapeDtypeStruct((bsz, B_KV_HEADS, s, B_HEAD_DIM), BF16),
            jax.ShapeDtypeStruct((bsz, B_KV_HEADS, s // CB, B_HEAD_DIM, CB), BF16),
            jax.ShapeDtypeStruct((bsz, s, LANES), BF16),
            jax.ShapeDtypeStruct((bsz, s, LANES), BF16),
            jax.ShapeDtypeStruct((bsz, s, LANES), F32),
            jax.ShapeDtypeStruct((bsz, s, LANES), F32),
            jax.ShapeDtypeStruct((bsz, s, LANES), F32),
            jax.ShapeDtypeStruct((bsz, s, LANES), F32),
            jax.ShapeDtypeStruct((bsz, IDX_HEADS, s), F32),
        ],
        compiler_params=_cparams(("parallel", "parallel")),
        name="kvprep",
    )(x3, gain, w_t, w_t, pos3, *tabs, gk, lng, lnb)


def _gmlp_kernel(u_ref, v_ref, z_ref, lng_ref, lnb_ref, ws_ref, sbt_ref, o_ref):
    tm = u_ref.shape[0]
    u = jax.nn.gelu(u_ref[...].astype(F32))
    v = jax.nn.gelu(v_ref[...].astype(F32))
    mu = jnp.mean(v, axis=-1, keepdims=True)
    d = v - mu
    var = jnp.mean(d * d, axis=-1, keepdims=True)
    vn = (d * lax.rsqrt(var + EPS) * lng_ref[...] + lnb_ref[...]).astype(BF16)
    gate = u * jax.nn.silu(z_ref[...].astype(F32))
    tri = (lax.broadcasted_iota(jnp.int32, (CHUNK, CHUNK), 1)
           <= lax.broadcasted_iota(jnp.int32, (CHUNK, CHUNK), 0))
    for g in range(A_GROUPS):
        wg = jnp.where(tri, ws_ref[g], 0.0).astype(BF16)
        bias = sbt_ref[:, g:g + 1]
        cols = slice(g * A_GROUP_DIM, (g + 1) * A_GROUP_DIM)
        for c in range(tm // CHUNK):
            rows = slice(c * CHUNK, (c + 1) * CHUNK)
            sg = jnp.dot(wg, vn[rows, cols], preferred_element_type=F32) + bias
            o_ref[rows, cols] = (gate[rows, cols] * sg).astype(BF16)


def _gmlp(proj, lng, lnb, ws, sbt):
    m = proj.shape[0]
    tm = min(512, m)
    col = lambda c: pl.BlockSpec((tm, BLK), lambda i: (i, c))
    full = lambda shape: pl.BlockSpec(shape, lambda i: (0,) * len(shape))
    return pl.pallas_call(
        _gmlp_kernel,
        grid=(m // tm,),
        in_specs=[col(COL_AU), col(COL_AV), col(COL_AZ), full((1, A_WIDTH)), full((1, A_WIDTH)),
                  full((A_GROUPS, CHUNK, CHUNK)), full((CHUNK, A_GROUPS))],
        out_specs=pl.BlockSpec((tm, A_WIDTH), lambda i: (i, 0)),
        out_shape=jax.ShapeDtypeStruct((m, A_WIDTH), BF16),
        compiler_params=_cparams(("parallel",)),
        name="gmlp",
    )(proj, proj, proj, lng, lnb, ws, sbt)


def _memkv_kernel(mem_ref, g_ref, w_ref, gk_ref, km_ref, vm_ref):
    x = mem_ref[0]
    ms = jnp.mean(x * x, axis=-1, keepdims=True)
    h = (x * lax.rsqrt(ms + EPS) * g_ref[...]).astype(BF16)
    kv = jnp.dot(h, w_ref[...], preferred_element_type=F32)
    for hd in range(M_HEADS):
        kh = kv[:, hd * M_HEAD_DIM:(hd + 1) * M_HEAD_DIM]
        r = lax.rsqrt(jnp.mean(kh * kh, axis=-1, keepdims=True) + EPS)
        km_ref[0, hd] = (kh * r * gk_ref[...]).astype(BF16)
        vm_ref[0, hd] = kv[:, M_WIDTH + hd * M_HEAD_DIM:M_WIDTH + (hd + 1) * M_HEAD_DIM].astype(BF16)


def _memkv(mem, gain, w_kv, gk):
    bsz, ml, _ = mem.shape
    out = pl.BlockSpec((1, M_HEADS, ml, M_HEAD_DIM), lambda b: (b, 0, 0, 0))
    shp = jax.ShapeDtypeStruct((bsz, M_HEADS, ml, M_HEAD_DIM), BF16)
    return pl.pallas_call(
        _memkv_kernel,
        grid=(bsz,),
        in_specs=[pl.BlockSpec((1, ml, D_MODEL), lambda b: (b, 0, 0)),
                  pl.BlockSpec((1, D_MODEL), lambda b: (0, 0)),
                  pl.BlockSpec((D_MODEL, 2 * M_WIDTH), lambda b: (0, 0)),
                  pl.BlockSpec((1, M_HEAD_DIM), lambda b: (0, 0))],
        out_specs=[out, out],
        out_shape=[shp, shp],
        compiler_params=_cparams(("parallel",)),
        name="memkv",
    )(mem, gain, w_kv, gk)


def _memattn_kernel(q_ref, z_ref, km_ref, vm_ref, gq_ref, o_ref):
    qscale = M_HEAD_DIM ** -0.5 * LOG2E
    for hd in range(M_HEADS):
        cols = slice(hd * M_HEAD_DIM, (hd + 1) * M_HEAD_DIM)
        q = q_ref[:, cols].astype(F32)
        r = lax.rsqrt(jnp.mean(q * q, axis=-1, keepdims=True) + EPS)
        qn = (q * r * gq_ref[...] * qscale).astype(BF16)
        lg = _dot_nt(qn, km_ref[0, hd])
        p = jnp.exp2(lg - jnp.max(lg, axis=-1, keepdims=True))
        l = jnp.sum(p, axis=-1, keepdims=True)
        o = jnp.dot(p.astype(BF16), vm_ref[0, hd], preferred_element_type=F32) / l
        o_ref[:, cols] = (o * jax.nn.silu(z_ref[:, cols].astype(F32))).astype(BF16)


def _memattn(proj, km, vm, gq, s):
    m = proj.shape[0]
    tm = min(512, s)
    per_b = s // tm
    ml = km.shape[2]
    kv_spec = pl.BlockSpec((1, M_HEADS, ml, M_HEAD_DIM), lambda i: (i // per_b, 0, 0, 0))
    return pl.pallas_call(
        _memattn_kernel,
        grid=(m // tm,),
        in_specs=[pl.BlockSpec((tm, BLK), lambda i: (i, COL_MQ)),
                  pl.BlockSpec((tm, BLK), lambda i: (i, COL_MZ)),
                  kv_spec, kv_spec,
                  pl.BlockSpec((1, M_HEAD_DIM), lambda i: (0, 0))],
        out_specs=pl.BlockSpec((tm, M_WIDTH), lambda i: (i, 0)),
        out_shape=jax.ShapeDtypeStruct((m, M_WIDTH), BF16),
        compiler_params=_cparams(("parallel",)),
        name="memattn",
    )(proj, proj, km, vm, gq)


def _dsa_kernel(topk, q_ref, z_ref, iq_ref, cb_ref, sb_ref, ci_ref, si_ref, iwt_ref,
                k_ref, vt_ref, iklo_ref, ikhi_ref, gq_ref, tri_ref, o_ref,
                sc_ref, bias_ref, lg_ref, qs_ref, iqs_ref, acc_ref):
    qb = pl.program_id(1)
    nck = (qb * TQ + TQ + CK - 1) // CK
    nbig = (qb * TQ + TQ + CB - 1) // CB
    t_lane = qb * TQ + lax.broadcasted_iota(jnp.int32, (1, TQ), 1)
    gw = B_GROUP * TQ

    def fold8(a, ways=1):
        return a.reshape(a.shape[0] // (8 * ways), 8 * ways, a.shape[1])

    def paired_loop(trips, body, init):
        carry = lax.fori_loop(0, trips // 2, lambda i, c: body(2 * i + 1, body(2 * i, c)), init)
        return lax.cond(trips % 2 == 1, lambda c: body(trips - 1, c), lambda c: c, carry)

    cos_b, sin_b = cb_ref[0], sb_ref[0]
    cos_i, sin_i = ci_ref[0], si_ref[0]
    qscale = B_HEAD_DIM ** -0.5 * LOG2E
    ones_mat = jnp.ones((B_HEAD_DIM, B_HEAD_DIM), BF16)
    for h in range(B_HEADS):
        slab = q_ref[:, h * B_HEAD_DIM:(h + 1) * B_HEAD_DIM].astype(F32)
        sq = slab * slab
        sq_hi = sq.astype(BF16)
        sq_lo = (sq - sq_hi.astype(F32)).astype(BF16)
        ssq = (jnp.dot(sq_hi, ones_mat, preferred_element_type=F32)
               + jnp.dot(sq_lo, ones_mat, preferred_element_type=F32))
        r = lax.rsqrt(ssq * (1.0 / B_HEAD_DIM) + EPS)
        qr = _rope(slab * r * gq_ref[...], cos_b, sin_b, B_HEAD_DIM // 8, LANES) * qscale
        g = h % B_GROUP
        qs_ref[h // B_GROUP, g * TQ:(g + 1) * TQ, :] = qr.astype(BF16)
    for j in range(IDX_HEADS // 2):
        slab = iq_ref[:, j * LANES:(j + 1) * LANES].astype(F32)
        iqs_ref[j // 2, (j % 2) * TQ:(j % 2 + 1) * TQ, :] = (
            _rope(slab, cos_i, sin_i, IDX_DIM // 8, IDX_DIM).astype(BF16))
    wt = iwt_ref[0]

    def idx_body(c, carry):
        mn8, mx8, s1, s2 = carry
        off = pl.multiple_of(c * CK, CK)
        acc = jnp.zeros((CK, TQ), F32)
        for jj in range(IDX_HEADS // 4):
            rhs = iqs_ref[jj]
            for half, keys_ref in enumerate((iklo_ref, ikhi_ref)):
                d = _dot_nt(keys_ref[0, pl.ds(off, CK), :], rhs)
                ha, hb = 4 * jj + half, 4 * jj + 2 + half
                acc = (acc + jnp.maximum(d[:, :TQ], 0.0) * wt[ha:ha + 1, :]
                       + jnp.maximum(d[:, TQ:], 0.0) * wt[hb:hb + 1, :])
        key = off + lax.broadcasted_iota(jnp.int32, (CK, TQ), 0)
        causal = key <= t_lane
        sc = jnp.where(causal, acc, -jnp.inf)
        sc_ref[pl.ds(off, CK), :] = sc
        live = jnp.where(causal, acc, 0.0)
        mn8 = jnp.minimum(mn8, jnp.min(fold8(jnp.where(causal, acc, jnp.inf)), axis=0))
        mx8 = jnp.maximum(mx8, jnp.max(fold8(sc), axis=0))
        s1 = s1 + jnp.sum(fold8(live), axis=0)
        s2 = s2 + jnp.sum(fold8(live * live), axis=0)
        return mn8, mx8, s1, s2

    zero8 = jnp.zeros((8, TQ), F32)
    stats = (jnp.full((8, TQ), jnp.inf, F32), jnp.full((8, TQ), -jnp.inf, F32), zero8, zero8)
    mn8, mx8, s1, s2 = paired_loop(nck, idx_body, stats)
    row_min = jnp.min(mn8, axis=0, keepdims=True)
    row_max = jnp.max(mx8, axis=0, keepdims=True)

    def fill_body(c, carry):
        sc_ref[pl.ds(pl.multiple_of(c * CK, CK), CK), :] = jnp.full((CK, TQ), -jnp.inf, F32)
        return carry

    lax.fori_loop(nck, nbig * (CB // CK), fill_body, 0)

    def count_rows(pred):
        def body(c, cnt):
            blk = sc_ref[pl.ds(pl.multiple_of(c * CB, CB), CB), :]
            return cnt + jnp.sum(fold8(jnp.where(pred(blk), 1.0, 0.0), COUNT_WAYS), axis=0)

        cnt = lax.fori_loop(0, nbig, body, jnp.zeros((8 * COUNT_WAYS, TQ), F32))
        return jnp.sum(cnt, axis=0, keepdims=True)

    kf = float(topk)
    n_valid = (t_lane + 1).astype(F32)
    all_rows = (t_lane + 1) <= topk

    def search_pass(st, probe, stuck):
        lo, hi, clo, chi, thr, fin, tie = st
        cnt = count_rows(lambda blk: blk >= probe)
        active = fin < 0.5
        hit = cnt == kf
        end_thr = jnp.where(jnp.logical_and(stuck, cnt < kf), lo, probe)
        ends = jnp.logical_and(active, jnp.logical_or(hit, stuck))
        thr = jnp.where(ends, end_thr, thr)
        tie = jnp.where(jnp.logical_and(ends, jnp.logical_not(hit)), 1.0, tie)
        fin = jnp.where(ends, 1.0, fin)
        up = jnp.logical_and(active, cnt > kf)
        dn = jnp.logical_and(active, cnt < kf)
        return (jnp.where(up, probe, lo), jnp.where(dn, probe, hi), jnp.where(up, cnt, clo),
                jnp.where(dn, cnt, chi), thr, fin, tie)

    def next_probe(st, halve=False):
        lo, hi, clo, chi = st[:4]
        frac = jnp.clip((clo - kf + 0.5) / (clo - chi), SEARCH_MARGIN, 1.0 - SEARCH_MARGIN)
        probe = lo + (hi - lo) * frac
        outside = jnp.logical_or(jnp.logical_or(probe <= lo, probe >= hi), halve)
        probe = jnp.where(outside, lo * 0.5 + hi * 0.5, probe)
        stuck = jnp.logical_or(probe <= lo, probe >= hi)
        return jnp.where(stuck, hi, probe), stuck

    mean = jnp.sum(s1, axis=0, keepdims=True) / n_valid
    var = jnp.maximum(jnp.sum(s2, axis=0, keepdims=True) / n_valid - mean * mean, 0.0)
    tail = jnp.clip(kf / n_valid, 1e-6, 1.0 - 1e-6)
    tq = jnp.sqrt(-2.0 * jnp.log(jnp.minimum(tail, 1.0 - tail)))
    zq = tq - ((0.010328 * tq + 0.802853) * tq + 2.515517) / (
        ((0.001308 * tq + 0.189269) * tq + 1.432788) * tq + 1.0)
    probe0 = jnp.clip(mean + jnp.where(tail < 0.5, zq, -zq) * jnp.sqrt(var), row_min, row_max)

    ones = jnp.ones((1, TQ), F32)
    st = (row_min, row_max, n_valid, 0.0 * ones, jnp.where(all_rows, -F32_MAX, row_max),
          jnp.where(all_rows, 1.0, 0.0), 0.0 * ones)
    st = search_pass(st, probe0, probe0 < row_min)

    def fixed_body(i, st):
        return search_pass(st, *next_probe(st))

    st = lax.fori_loop(0, SEARCH_FIXED_PASSES, fixed_body, st)

    def more_cond(c):
        return jnp.logical_and(c[0] < MAX_SEARCH_ITERS, jnp.min(c[1][5]) < 0.5)

    def more_body(c):
        return c[0] + 1, search_pass(c[1], *next_probe(c[1], c[0] >= SEARCH_INTERP_PASSES))

    _, st = lax.while_loop(more_cond, more_body, (jnp.int32(0), st))
    thr, tie = st[4], st[6]

    def plain_mask():
        def body(c, carry):
            off = pl.multiple_of(c * CB, CB)
            bias_ref[pl.ds(off, CB), :] = jnp.where(sc_ref[pl.ds(off, CB), :] >= thr, 0.0, NEG_BIG)
            return carry

        lax.fori_loop(0, nbig, body, 0)

    def tied_mask():
        n_gt = count_rows(lambda blk: blk > thr)
        need = jnp.where(all_rows, F32_MAX, kf - n_gt)

        def body(c, seen):
            off = pl.multiple_of(c * CB, CB)
            blk = sc_ref[pl.ds(off, CB), :]
            eq = blk == thr
            pref = jnp.dot(tri_ref[...], jnp.where(eq, 1.0, 0.0).astype(BF16),
                           preferred_element_type=F32) + seen
            keep = jnp.logical_or(blk > thr, jnp.logical_and(eq, pref <= need))
            bias_ref[pl.ds(off, CB), :] = jnp.where(keep, 0.0, NEG_BIG)
            return pref[CB - 1:CB, :]

        lax.fori_loop(0, nbig, body, jnp.zeros((1, TQ), F32))

    lax.cond(jnp.max(tie) > 0.5, tied_mask, plain_mask)

    for n in range(B_KV_HEADS):
        def logit_body(c, mx8, n=n):
            off = pl.multiple_of(c * CB, CB)
            b = bias_ref[pl.ds(off, CB), :]
            lg = _dot_nt(k_ref[0, n, pl.ds(off, CB), :], qs_ref[n])
            lg = jnp.concatenate([lg[:, g * TQ:(g + 1) * TQ] + b for g in range(B_GROUP)], axis=1)
            lg_ref[pl.ds(off, CB), :] = lg
            return jnp.maximum(mx8, jnp.max(fold8(lg), axis=0))

        mx8 = paired_loop(nbig, logit_body, jnp.full((8, gw), NEG_BIG, F32))
        m = jnp.max(mx8, axis=0, keepdims=True)
        acc_ref[...] = jnp.zeros(acc_ref.shape, F32)

        def pv_body(c, l8, n=n, m=m):
            off = pl.multiple_of(c * CB, CB)
            p = jnp.exp2(lg_ref[pl.ds(off, CB), :] - m)
            acc_ref[...] += jnp.dot(vt_ref[0, n, c], p.astype(BF16), preferred_element_type=F32)
            return l8 + jnp.sum(fold8(p), axis=0)

        l8 = paired_loop(nbig, pv_body, jnp.zeros((8, gw), F32))
        o_t = acc_ref[...] / jnp.sum(l8, axis=0, keepdims=True)
        for g in range(B_GROUP):
            cols = slice((n * B_GROUP + g) * B_HEAD_DIM, (n * B_GROUP + g + 1) * B_HEAD_DIM)
            o = o_t[:, g * TQ:(g + 1) * TQ].T
            o_ref[:, cols] = (o * jax.nn.silu(z_ref[:, cols].astype(F32))).astype(BF16)


def _dsa(proj, tabs, iwt, k, vt, iklo, ikhi, gq, bsz, s):
    nq = s // TQ
    topk = min(TOPK_MAX, s // 4)
    gw = B_GROUP * TQ
    rowblk = lambda c: pl.BlockSpec((TQ, BLK), lambda b, i: (b * nq + i, c))
    tok = pl.BlockSpec((1, TQ, LANES), lambda b, i: (b, i, 0))
    ik = pl.BlockSpec((1, s, LANES), lambda b, i: (b, 0, 0))
    tri = jnp.tril(jnp.ones((CB, CB), BF16))
    return pl.pallas_call(
        functools.partial(_dsa_kernel, topk),
        grid=(bsz, nq),
        in_specs=[rowblk(COL_BQ), rowblk(COL_BZ), rowblk(COL_IQ), tok, tok, tok, tok,
                  pl.BlockSpec((1, IDX_HEADS, TQ), lambda b, i: (b, 0, i)),
                  pl.BlockSpec((1, B_KV_HEADS, s, B_HEAD_DIM), lambda b, i: (b, 0, 0, 0)),
                  pl.BlockSpec((1, B_KV_HEADS, s // CB, B_HEAD_DIM, CB), lambda b, i: (b, 0, 0, 0, 0)),
                  ik, ik,
                  pl.BlockSpec((1, B_HEAD_DIM), lambda b, i: (0, 0)),
                  pl.BlockSpec((CB, CB), lambda b, i: (0, 0))],
        out_specs=pl.BlockSpec((TQ, B_WIDTH), lambda b, i: (b * nq + i, 0)),
        out_shape=jax.ShapeDtypeStruct((bsz * s, B_WIDTH), BF16),
        scratch_shapes=[
            pltpu.VMEM((s, TQ), F32),
            pltpu.VMEM((s, TQ), F32),
            pltpu.VMEM((s, gw), F32),
            pltpu.VMEM((B_KV_HEADS, gw, B_HEAD_DIM), BF16),
            pltpu.VMEM((IDX_HEADS // 4, 2 * TQ, LANES), BF16),
            pltpu.VMEM((B_HEAD_DIM, gw), F32),
        ],
        compiler_params=_cparams(("parallel", "arbitrary")),
        name="dsa",
    )(proj, proj, proj, *tabs, iwt, k, vt, iklo, ikhi, gq, tri)


def _merge_kernel(ta_ref, tb_ref, tm_ref, ga_ref, gb_ref, gm_ref, x_ref,
                  wa_ref, wb_ref, wm_ref, wo_ref, o_ref):
    def branch(t_ref, g_ref, w_ref):
        y = jnp.dot(t_ref[...], w_ref[...], preferred_element_type=F32)
        return jax.nn.sigmoid(g_ref[...].astype(F32)) * y

    merged = branch(ta_ref, ga_ref, wa_ref) + branch(tb_ref, gb_ref, wb_ref) + branch(tm_ref, gm_ref, wm_ref)
    o_ref[...] = x_ref[...] + jnp.dot(merged.astype(BF16), wo_ref[...], preferred_element_type=F32)


def _merge(ta, tb, tmem, proj, x2, wa, wb, wm, wo):
    m = x2.shape[0]
    tm = min(256, m)
    act = pl.BlockSpec((tm, BLK), lambda i: (i, 0))
    gate = lambda c: pl.BlockSpec((tm, D_MODEL), lambda i: (i, c // 2))
    wide = pl.BlockSpec((tm, D_MODEL), lambda i: (i, 0))
    wbr = pl.BlockSpec((BLK, D_MODEL), lambda i: (0, 0), pipeline_mode=pl.Buffered(1))
    wout = pl.BlockSpec((D_MODEL, D_MODEL), lambda i: (0, 0), pipeline_mode=pl.Buffered(1))
    return pl.pallas_call(
        _merge_kernel,
        grid=(m // tm,),
        in_specs=[act, act, act, gate(COL_GA), gate(COL_GB), gate(COL_GM), wide, wbr, wbr, wbr, wout],
        out_specs=wide,
        out_shape=jax.ShapeDtypeStruct((m, D_MODEL), F32),
        compiler_params=_cparams(("parallel",)),
        name="merge",
    )(ta, tb, tmem, proj, proj, proj, x2, wa, wb, wm, wo)


def _rope_tables():
    def build(rot, period):
        half = rot // 2
        inv = ROPE_THETA ** (-np.arange(half, dtype=np.float32) / half)
        lane = np.arange(LANES) % period
        freq = np.where(lane < rot, inv[lane % half], 0.0).astype(np.float32)
        sign = np.where(lane < half, -1.0, np.where(lane < rot, 1.0, 0.0)).astype(np.float32)
        return jnp.asarray(freq[None, :]), jnp.asarray(sign[None, :])

    fb, sgb = build(B_HEAD_DIM // 4, LANES)
    fi, sgi = build(IDX_DIM // 4, IDX_DIM)
    return fb, sgb, fi, sgi


def _layer(x, mem, positions, norm_gain, w_in, gmlp_ln_gain, gmlp_ln_bias, spatial_w, spatial_b,
           w_branch_a, q_norm_gain, k_norm_gain, idx_k_ln_gain, idx_k_ln_bias, w_branch_b,
           mem_norm_gain, w_mem_kv, mem_q_norm_gain, mem_k_norm_gain, w_branch_m, w_out):
    bsz, s, _ = x.shape
    m = bsz * s
    row = lambda a: a.reshape(1, -1).astype(F32)
    pad_lanes = lambda a: jnp.pad(a.reshape(1, -1).astype(F32), ((0, 0), (0, LANES - a.shape[-1])))
    x2 = x.reshape(m, D_MODEL)

    w_t = w_in.T
    proj = _inproj(x2, row(norm_gain), w_t)
    k, vt, iklo, ikhi, cb, sb, ci, si, iwt = _kvprep(
        x, row(norm_gain), w_t, positions.reshape(bsz, s, 1).astype(jnp.int32), _rope_tables(),
        row(k_norm_gain), pad_lanes(idx_k_ln_gain), pad_lanes(idx_k_ln_bias))

    t_a = _gmlp(proj, row(gmlp_ln_gain), row(gmlp_ln_bias), spatial_w.astype(F32), spatial_b.T.astype(F32))
    km, vm = _memkv(mem, row(mem_norm_gain), w_mem_kv.astype(BF16), row(mem_k_norm_gain))
    t_m = _memattn(proj, km, vm, row(mem_q_norm_gain), s)
    t_b = _dsa(proj, (cb, sb, ci, si), iwt, k, vt, iklo, ikhi, row(q_norm_gain), bsz, s)

    out = _merge(t_a, t_b, t_m, proj, x2, w_branch_a.astype(BF16), w_branch_b.astype(BF16),
                 w_branch_m.astype(BF16), w_out.astype(BF16))
    return out.reshape(bsz, s, D_MODEL)


def kernel(x, mem, positions, norm_gain, w_in, gmlp_ln_gain, gmlp_ln_bias, spatial_w, spatial_b, w_branch_a, q_norm_gain, k_norm_gain, idx_k_ln_gain, idx_k_ln_bias, w_branch_b, mem_norm_gain, w_mem_kv, mem_q_norm_gain, mem_k_norm_gain, w_branch_m, w_out):
    for l in range(norm_gain.shape[0]):
        x = _layer(x, mem, positions, norm_gain[l], w_in[l], gmlp_ln_gain[l], gmlp_ln_bias[l],
                   spatial_w[l], spatial_b[l], w_branch_a[l], q_norm_gain[l], k_norm_gain[l],
                   idx_k_ln_gain[l], idx_k_ln_bias[l], w_branch_b[l], mem_norm_gain[l], w_mem_kv[l],
                   mem_q_norm_gain[l], mem_k_norm_gain[l], w_branch_m[l], w_out[l])
    return x
```

```python
import functools

import numpy as np
import jax
import jax.numpy as jnp
from jax import lax
from jax.experimental import pallas as pl
from jax.experimental.pallas import tpu as pltpu

F32 = jnp.float32
BF16 = jnp.bfloat16

D_MODEL = 2048
ROPE_THETA = 500000.0
EPS = 1e-6
A_GROUPS = 8
A_GROUP_DIM = 128
A_WIDTH = A_GROUPS * A_GROUP_DIM
CHUNK = 128
B_HEADS = 8
B_KV_HEADS = 2
B_GROUP = B_HEADS // B_KV_HEADS
B_HEAD_DIM = 128
B_WIDTH = B_HEADS * B_HEAD_DIM
IDX_HEADS = 16
IDX_DIM = 64
TOPK_MAX = 256
M_HEADS = 4
M_HEAD_DIM = 256
M_WIDTH = M_HEADS * M_HEAD_DIM

SPLIT_SIZES = (
    A_WIDTH, A_WIDTH, A_WIDTH,
    B_WIDTH, B_KV_HEADS * B_HEAD_DIM, B_KV_HEADS * B_HEAD_DIM, B_WIDTH,
    IDX_HEADS * IDX_DIM, IDX_DIM, IDX_HEADS,
    M_WIDTH, M_WIDTH,
    D_MODEL, D_MODEL, D_MODEL,
)

LANES = 128
BLK = 1024
COL_AU, COL_AV, COL_AZ, COL_BQ, COL_BZ, COL_IQ, COL_MQ, COL_MZ = range(8)
COL_GA, COL_GB, COL_GM = 8, 10, 12
NB_MAIN = 14
_OFFS = [int(o) for o in np.concatenate([[0], np.cumsum(SPLIT_SIZES)])]
ROW_ALIGN = 16
MAIN_START = ([_OFFS[i] for i in (0, 1, 2, 3, 6, 7, 10, 11)]
              + [_OFFS[i] + d for i in (12, 13, 14) for d in (0, BLK)])
assert all(s % ROW_ALIGN == 0 for s in MAIN_START)
SMALL_KV_START, SMALL_KV_WIDTH = _OFFS[4], _OFFS[6] - _OFFS[4]
SMALL_IDX_START = _OFFS[8]
assert SMALL_KV_START % SMALL_KV_WIDTH == 0 and SMALL_IDX_START % LANES == 0
assert IDX_DIM + IDX_HEADS <= LANES and _OFFS[9] == SMALL_IDX_START + IDX_DIM

VMEM_LIMIT = 56 * 1024 * 1024
LOG2E = 1.4426950408889634
NEG_BIG = -1e30
F32_MAX = 3.4028234663852886e38

INPROJ_TM, INPROJ_TN = 2048, 512
TQ = 256
CK = 256
CB = 512
COUNT_WAYS = 8
SEARCH_MARGIN = 0.05
SEARCH_FIXED_PASSES = 8
SEARCH_INTERP_PASSES = 16
MAX_SEARCH_ITERS = 400


def _cparams(sem):
    return pltpu.CompilerParams(dimension_semantics=sem, vmem_limit_bytes=VMEM_LIMIT)


def _dot_nt(a, b):
    return lax.dot_general(a, b, (((1,), (1,)), ((), ())), preferred_element_type=F32)


def _rope(x, cos_t, sin_t, half, period):
    lane = lax.broadcasted_iota(jnp.int32, x.shape, 1) % period
    rolled = jnp.where(lane < half, pltpu.roll(x, LANES - half, 1), pltpu.roll(x, half, 1))
    return x * cos_t + rolled * sin_t


def _rms_rows(x, gain):
    ms = jnp.mean(x * x, axis=-1, keepdims=True)
    return (x * lax.rsqrt(ms + EPS) * gain).astype(BF16)


def _inproj_kernel(tab_ref, h_ref, wt_ref, proj_ref):
    del tab_ref
    proj_ref[...] = _dot_nt(h_ref[...], wt_ref[...].astype(BF16)).astype(BF16)


def _inproj(h, w_t):
    m = h.shape[0]
    tm = min(INPROJ_TM, m)
    starts = [s + d for s in MAIN_START for d in range(0, BLK, INPROJ_TN)]
    tab = jnp.asarray(np.array([s // ROW_ALIGN for s in starts], np.int32))
    return pl.pallas_call(
        _inproj_kernel,
        grid_spec=pltpu.PrefetchScalarGridSpec(
            num_scalar_prefetch=1,
            grid=(m // tm, len(starts)),
            in_specs=[
                pl.BlockSpec((tm, D_MODEL), lambda i, n, tab: (i, 0)),
                pl.BlockSpec((pl.Element(INPROJ_TN), pl.Element(D_MODEL)),
                             lambda i, n, tab: (tab[n] * ROW_ALIGN, 0)),
            ],
            out_specs=pl.BlockSpec((tm, INPROJ_TN), lambda i, n, tab: (i, n)),
        ),
        out_shape=jax.ShapeDtypeStruct((m, NB_MAIN * BLK), BF16),
        compiler_params=_cparams(("parallel", "arbitrary")),
        name="inproj",
    )(tab, h, w_t)


def _kvprep_kernel(x_ref, g_ref, wkv_ref, widx_ref, pos_ref, fc_ref, gk_ref, lng_ref, lnb_ref,
                   h_ref, k_ref, vt_ref, iklo_ref, ikhi_ref, cb_ref, sb_ref, ci_ref, si_ref, iwt_ref,
                   wkv_bf_ref, widx_bf_ref):
    @pl.when(jnp.logical_and(pl.program_id(0) == 0, pl.program_id(1) == 0))
    def _():
        wkv_bf_ref[...] = wkv_ref[...].astype(BF16)
        widx_bf_ref[...] = widx_ref[...].astype(BF16)

    h = _rms_rows(x_ref[...], g_ref[...])
    h_ref[...] = h
    kv = _dot_nt(h, wkv_bf_ref[...])
    ikp = _dot_nt(h, widx_bf_ref[...])

    hb, hi = B_HEAD_DIM // 8, IDX_DIM // 8
    ang = pos_ref[0].astype(F32) * fc_ref[...]
    cos_c, sin_c = jnp.cos(ang), jnp.sin(ang)
    lane = lax.broadcasted_iota(jnp.int32, ang.shape, 1)
    cos_b = jnp.where(lane < hb, cos_c, jnp.where(lane < 2 * hb, pltpu.roll(cos_c, hb, 1), 1.0))
    sin_b = jnp.where(lane < hb, -sin_c, jnp.where(lane < 2 * hb, pltpu.roll(sin_c, hb, 1), 0.0))
    cos_i = jnp.ones_like(cos_c)
    sin_i = jnp.zeros_like(sin_c)
    for head in range(LANES // IDX_DIM):
        for part, sign in enumerate((-1.0, 1.0)):
            first = head * IDX_DIM + part * hi
            here = jnp.logical_and(lane >= first, lane < first + hi)
            shift = (first - hb) % LANES
            cos_i = jnp.where(here, pltpu.roll(cos_c, shift, 1), cos_i)
            sin_i = jnp.where(here, sign * pltpu.roll(sin_c, shift, 1), sin_i)
    cb_ref[0] = cos_b
    sb_ref[0] = sin_b
    ci_ref[0] = cos_i
    si_ref[0] = sin_i

    for n in range(B_KV_HEADS):
        kh = kv[:, n * B_HEAD_DIM:(n + 1) * B_HEAD_DIM]
        r = lax.rsqrt(jnp.mean(kh * kh, axis=-1, keepdims=True) + EPS)
        kn = kh * r * gk_ref[...]
        k_ref[0, n] = _rope(kn, cos_b, sin_b, B_HEAD_DIM // 8, LANES).astype(BF16)
        vt = kv[:, (B_KV_HEADS + n) * B_HEAD_DIM:(B_KV_HEADS + n + 1) * B_HEAD_DIM].T
        for c in range(vt.shape[1] // CB):
            vt_ref[0, n, c] = vt[:, c * CB:(c + 1) * CB].astype(BF16)

    lane = lax.broadcasted_iota(jnp.int32, ikp.shape, 1)
    live = lane < IDX_DIM
    mu = jnp.sum(jnp.where(live, ikp, 0.0), axis=-1, keepdims=True) * (1.0 / IDX_DIM)
    d = jnp.where(live, ikp - mu, 0.0)
    var = jnp.sum(d * d, axis=-1, keepdims=True) * (1.0 / IDX_DIM)
    y = d * lax.rsqrt(var + EPS) * lng_ref[...] + lnb_ref[...]
    yr = jnp.where(live, _rope(y, cos_i, sin_i, IDX_DIM // 8, IDX_DIM), 0.0)
    iklo_ref[0] = yr.astype(BF16)
    ikhi_ref[0] = pltpu.roll(yr, IDX_DIM, 1).astype(BF16)

    iw = ikp * (IDX_DIM ** -0.5 * IDX_HEADS ** -0.5)
    iwt_ref[0] = iw.T[IDX_DIM:IDX_DIM + IDX_HEADS, :]


def _kvprep(x3, gain, w_t, pos3, freqs, gk, lng, lnb):
    bsz, s, _ = x3.shape
    tp = min(512, s)
    row = lambda b, i: (b, i, 0)
    const = lambda b, i: (0, 0)
    tab_spec = pl.BlockSpec((1, LANES), const)
    out_h = pl.BlockSpec((tp, D_MODEL), lambda b, i: (b * (s // tp) + i, 0))
    out_tok = pl.BlockSpec((1, tp, LANES), row)
    out_k = pl.BlockSpec((1, B_KV_HEADS, tp, B_HEAD_DIM), lambda b, i: (b, 0, i, 0))
    out_vt = pl.BlockSpec((1, B_KV_HEADS, tp // CB, B_HEAD_DIM, CB), lambda b, i: (b, 0, i, 0, 0))
    out_iwt = pl.BlockSpec((1, IDX_HEADS, tp), lambda b, i: (b, 0, i))
    return pl.pallas_call(
        _kvprep_kernel,
        grid=(bsz, s // tp),
        in_specs=[pl.BlockSpec((None, tp, D_MODEL), row), pl.BlockSpec((1, D_MODEL), const),
                  pl.BlockSpec((SMALL_KV_WIDTH, D_MODEL), lambda b, i: (SMALL_KV_START // SMALL_KV_WIDTH, 0)),
                  pl.BlockSpec((LANES, D_MODEL), lambda b, i: (SMALL_IDX_START // LANES, 0)),
                  pl.BlockSpec((1, tp, 1), row),
                  tab_spec, tab_spec, tab_spec, tab_spec],
        out_specs=[out_h, out_k, out_vt, out_tok, out_tok, out_tok, out_tok, out_tok, out_tok, out_iwt],
        out_shape=[
            jax.ShapeDtypeStruct((bsz * s, D_MODEL), BF16),
            jax.ShapeDtypeStruct((bsz, B_KV_HEADS, s, B_HEAD_DIM), BF16),
            jax.ShapeDtypeStruct((bsz, B_KV_HEADS, s // CB, B_HEAD_DIM, CB), BF16),
            jax.ShapeDtypeStruct((bsz, s, LANES), BF16),
            jax.ShapeDtypeStruct((bsz, s, LANES), BF16),
            jax.ShapeDtypeStruct((bsz, s, LANES), F32),
            jax.ShapeDtypeStruct((bsz, s, LANES), F32),
            jax.ShapeDtypeStruct((bsz, s, LANES), F32),
            jax.ShapeDtypeStruct((bsz, s, LANES), F32),
            jax.ShapeDtypeStruct((bsz, IDX_HEADS, s), F32),
        ],
        scratch_shapes=[pltpu.VMEM((SMALL_KV_WIDTH, D_MODEL), BF16), pltpu.VMEM((LANES, D_MODEL), BF16)],
        compiler_params=_cparams(("arbitrary", "arbitrary")),
        name="kvprep",
    )(x3, gain, w_t, w_t, pos3, freqs, gk, lng, lnb)


def _gmlp_kernel(u_ref, v_ref, z_ref, lng_ref, lnb_ref, ws_ref, sbt_ref, o_ref):
    tm = u_ref.shape[0]
    u = jax.nn.gelu(u_ref[...].astype(F32))
    v = jax.nn.gelu(v_ref[...].astype(F32))
    mu = jnp.mean(v, axis=-1, keepdims=True)
    d = v - mu
    var = jnp.mean(d * d, axis=-1, keepdims=True)
    vn = (d * lax.rsqrt(var + EPS) * lng_ref[...] + lnb_ref[...]).astype(BF16)
    gate = u * jax.nn.silu(z_ref[...].astype(F32))
    tri = (lax.broadcasted_iota(jnp.int32, (CHUNK, CHUNK), 1)
           <= lax.broadcasted_iota(jnp.int32, (CHUNK, CHUNK), 0))
    for g in range(A_GROUPS):
        wg = jnp.where(tri, ws_ref[g], 0.0).astype(BF16)
        bias = sbt_ref[:, g:g + 1]
        cols = slice(g * A_GROUP_DIM, (g + 1) * A_GROUP_DIM)
        for c in range(tm // CHUNK):
            rows = slice(c * CHUNK, (c + 1) * CHUNK)
            sg = jnp.dot(wg, vn[rows, cols], preferred_element_type=F32) + bias
            o_ref[rows, cols] = (gate[rows, cols] * sg).astype(BF16)


def _gmlp(proj, lng, lnb, ws, sbt):
    m = proj.shape[0]
    tm = min(512, m)
    col = lambda c: pl.BlockSpec((tm, BLK), lambda i: (i, c))
    full = lambda shape: pl.BlockSpec(shape, lambda i: (0,) * len(shape))
    return pl.pallas_call(
        _gmlp_kernel,
        grid=(m // tm,),
        in_specs=[col(COL_AU), col(COL_AV), col(COL_AZ), full((1, A_WIDTH)), full((1, A_WIDTH)),
                  full((A_GROUPS, CHUNK, CHUNK)), full((CHUNK, A_GROUPS))],
        out_specs=pl.BlockSpec((tm, A_WIDTH), lambda i: (i, 0)),
        out_shape=jax.ShapeDtypeStruct((m, A_WIDTH), BF16),
        compiler_params=_cparams(("parallel",)),
        name="gmlp",
    )(proj, proj, proj, lng, lnb, ws, sbt)


def _memkv_kernel(mem_ref, g_ref, w_ref, gk_ref, km_ref, vm_ref):
    x = mem_ref[0]
    ms = jnp.mean(x * x, axis=-1, keepdims=True)
    h = (x * lax.rsqrt(ms + EPS) * g_ref[...]).astype(BF16)
    kv = jnp.dot(h, w_ref[...], preferred_element_type=F32)
    for hd in range(M_HEADS):
        kh = kv[:, hd * M_HEAD_DIM:(hd + 1) * M_HEAD_DIM]
        r = lax.rsqrt(jnp.mean(kh * kh, axis=-1, keepdims=True) + EPS)
        km_ref[0, hd] = (kh * r * gk_ref[...]).astype(BF16)
        vm_ref[0, hd] = kv[:, M_WIDTH + hd * M_HEAD_DIM:M_WIDTH + (hd + 1) * M_HEAD_DIM].astype(BF16)


def _memkv(mem, gain, w_kv, gk):
    bsz, ml, _ = mem.shape
    out = pl.BlockSpec((1, M_HEADS, ml, M_HEAD_DIM), lambda b: (b, 0, 0, 0))
    shp = jax.ShapeDtypeStruct((bsz, M_HEADS, ml, M_HEAD_DIM), BF16)
    return pl.pallas_call(
        _memkv_kernel,
        grid=(bsz,),
        in_specs=[pl.BlockSpec((1, ml, D_MODEL), lambda b: (b, 0, 0)),
                  pl.BlockSpec((1, D_MODEL), lambda b: (0, 0)),
                  pl.BlockSpec((D_MODEL, 2 * M_WIDTH), lambda b: (0, 0)),
                  pl.BlockSpec((1, M_HEAD_DIM), lambda b: (0, 0))],
        out_specs=[out, out],
        out_shape=[shp, shp],
        compiler_params=_cparams(("parallel",)),
        name="memkv",
    )(mem, gain, w_kv, gk)


def _memattn_kernel(q_ref, z_ref, km_ref, vm_ref, gq_ref, o_ref):
    qscale = M_HEAD_DIM ** -0.5 * LOG2E
    for hd in range(M_HEADS):
        cols = slice(hd * M_HEAD_DIM, (hd + 1) * M_HEAD_DIM)
        q = q_ref[:, cols].astype(F32)
        r = lax.rsqrt(jnp.mean(q * q, axis=-1, keepdims=True) + EPS)
        qn = (q * r * gq_ref[...] * qscale).astype(BF16)
        lg = _dot_nt(qn, km_ref[0, hd])
        p = jnp.exp2(lg - jnp.max(lg, axis=-1, keepdims=True))
        l = jnp.sum(p, axis=-1, keepdims=True)
        o = jnp.dot(p.astype(BF16), vm_ref[0, hd], preferred_element_type=F32) / l
        o_ref[:, cols] = (o * jax.nn.silu(z_ref[:, cols].astype(F32))).astype(BF16)


def _memattn(proj, km, vm, gq, s):
    m = proj.shape[0]
    tm = min(512, s)
    per_b = s // tm
    ml = km.shape[2]
    kv_spec = pl.BlockSpec((1, M_HEADS, ml, M_HEAD_DIM), lambda i: (i // per_b, 0, 0, 0))
    return pl.pallas_call(
        _memattn_kernel,
        grid=(m // tm,),
        in_specs=[pl.BlockSpec((tm, BLK), lambda i: (i, COL_MQ)),
                  pl.BlockSpec((tm, BLK), lambda i: (i, COL_MZ)),
                  kv_spec, kv_spec,
                  pl.BlockSpec((1, M_HEAD_DIM), lambda i: (0, 0))],
        out_specs=pl.BlockSpec((tm, M_WIDTH), lambda i: (i, 0)),
        out_shape=jax.ShapeDtypeStruct((m, M_WIDTH), BF16),
        compiler_params=_cparams(("parallel",)),
        name="memattn",
    )(proj, proj, km, vm, gq)


def _dsa_kernel(topk, q_ref, z_ref, iq_ref, cb_ref, sb_ref, ci_ref, si_ref, iwt_ref,
                k_ref, vt_ref, iklo_ref, ikhi_ref, gq_ref, tri_ref, o_ref,
                sc_ref, bias_ref, lg_ref, qs_ref, iqs_ref, acc_ref):
    qb = pl.program_id(1)
    nck = (qb * TQ + TQ + CK - 1) // CK
    nbig = (qb * TQ + TQ + CB - 1) // CB
    t_lane = qb * TQ + lax.broadcasted_iota(jnp.int32, (1, TQ), 1)
    gw = B_GROUP * TQ

    def fold8(a, ways=1):
        return a.reshape(a.shape[0] // (8 * ways), 8 * ways, a.shape[1])

    def paired_loop(trips, body, init):
        carry = lax.fori_loop(0, trips // 2, lambda i, c: body(2 * i + 1, body(2 * i, c)), init)
        return lax.cond(trips % 2 == 1, lambda c: body(trips - 1, c), lambda c: c, carry)

    cos_b, sin_b = cb_ref[0], sb_ref[0]
    cos_i, sin_i = ci_ref[0], si_ref[0]
    qscale = B_HEAD_DIM ** -0.5 * LOG2E
    ones_mat = jnp.ones((B_HEAD_DIM, B_HEAD_DIM), BF16)
    for h in range(B_HEADS):
        slab = q_ref[:, h * B_HEAD_DIM:(h + 1) * B_HEAD_DIM].astype(F32)
        sq = slab * slab
        sq_hi = sq.astype(BF16)
        sq_lo = (sq - sq_hi.astype(F32)).astype(BF16)
        ssq = (jnp.dot(sq_hi, ones_mat, preferred_element_type=F32)
               + jnp.dot(sq_lo, ones_mat, preferred_element_type=F32))
        r = lax.rsqrt(ssq * (1.0 / B_HEAD_DIM) + EPS)
        qr = _rope(slab * r * gq_ref[...], cos_b, sin_b, B_HEAD_DIM // 8, LANES) * qscale
        g = h % B_GROUP
        qs_ref[h // B_GROUP, g * TQ:(g + 1) * TQ, :] = qr.astype(BF16)
    for j in range(IDX_HEADS // 2):
        slab = iq_ref[:, j * LANES:(j + 1) * LANES].astype(F32)
        iqs_ref[j // 2, (j % 2) * TQ:(j % 2 + 1) * TQ, :] = (
            _rope(slab, cos_i, sin_i, IDX_DIM // 8, IDX_DIM).astype(BF16))
    wt = iwt_ref[0]

    def idx_body(c, carry):
        mn8, mx8, s1, s2 = carry
        off = pl.multiple_of(c * CK, CK)
        acc = jnp.zeros((CK, TQ), F32)
        for jj in range(IDX_HEADS // 4):
            rhs = iqs_ref[jj]
            for half, keys_ref in enumerate((iklo_ref, ikhi_ref)):
                d = _dot_nt(keys_ref[0, pl.ds(off, CK), :], rhs)
                ha, hb = 4 * jj + half, 4 * jj + 2 + half
                acc = (acc + jnp.maximum(d[:, :TQ], 0.0) * wt[ha:ha + 1, :]
                       + jnp.maximum(d[:, TQ:], 0.0) * wt[hb:hb + 1, :])
        key = off + lax.broadcasted_iota(jnp.int32, (CK, TQ), 0)
        causal = key <= t_lane
        sc = jnp.where(causal, acc, -jnp.inf)
        sc_ref[pl.ds(off, CK), :] = sc
        live = jnp.where(causal, acc, 0.0)
        mn8 = jnp.minimum(mn8, jnp.min(fold8(jnp.where(causal, acc, jnp.inf)), axis=0))
        mx8 = jnp.maximum(mx8, jnp.max(fold8(sc), axis=0))
        s1 = s1 + jnp.sum(fold8(live), axis=0)
        s2 = s2 + jnp.sum(fold8(live * live), axis=0)
        return mn8, mx8, s1, s2

    zero8 = jnp.zeros((8, TQ), F32)
    stats = (jnp.full((8, TQ), jnp.inf, F32), jnp.full((8, TQ), -jnp.inf, F32), zero8, zero8)
    mn8, mx8, s1, s2 = paired_loop(nck, idx_body, stats)
    row_min = jnp.min(mn8, axis=0, keepdims=True)
    row_max = jnp.max(mx8, axis=0, keepdims=True)

    def fill_body(c, carry):
        sc_ref[pl.ds(pl.multiple_of(c * CK, CK), CK), :] = jnp.full((CK, TQ), -jnp.inf, F32)
        return carry

    lax.fori_loop(nck, nbig * (CB // CK), fill_body, 0)

    def count_rows(pred):
        def body(c, cnt):
            blk = sc_ref[pl.ds(pl.multiple_of(c * CB, CB), CB), :]
            return cnt + jnp.sum(fold8(jnp.where(pred(blk), 1.0, 0.0), COUNT_WAYS), axis=0)

        cnt = lax.fori_loop(0, nbig, body, jnp.zeros((8 * COUNT_WAYS, TQ), F32))
        return jnp.sum(cnt, axis=0, keepdims=True)

    kf = float(topk)
    n_valid = (t_lane + 1).astype(F32)
    all_rows = (t_lane + 1) <= topk

    def search_pass(st, probe, stuck):
        lo, hi, clo, chi, thr, fin, tie = st
        cnt = count_rows(lambda blk: blk >= probe)
        active = fin < 0.5
        hit = cnt == kf
        end_thr = jnp.where(jnp.logical_and(stuck, cnt < kf), lo, probe)
        ends = jnp.logical_and(active, jnp.logical_or(hit, stuck))
        thr = jnp.where(ends, end_thr, thr)
        tie = jnp.where(jnp.logical_and(ends, jnp.logical_not(hit)), 1.0, tie)
        fin = jnp.where(ends, 1.0, fin)
        up = jnp.logical_and(active, cnt > kf)
        dn = jnp.logical_and(active, cnt < kf)
        return (jnp.where(up, probe, lo), jnp.where(dn, probe, hi), jnp.where(up, cnt, clo),
                jnp.where(dn, cnt, chi), thr, fin, tie)

    def next_probe(st, halve=False):
        lo, hi, clo, chi = st[:4]
        frac = jnp.clip((clo - kf + 0.5) / (clo - chi), SEARCH_MARGIN, 1.0 - SEARCH_MARGIN)
        probe = lo + (hi - lo) * frac
        outside = jnp.logical_or(jnp.logical_or(probe <= lo, probe >= hi), halve)
        probe = jnp.where(outside, lo * 0.5 + hi * 0.5, probe)
        stuck = jnp.logical_or(probe <= lo, probe >= hi)
        return jnp.where(stuck, hi, probe), stuck

    mean = jnp.sum(s1, axis=0, keepdims=True) / n_valid
    var = jnp.maximum(jnp.sum(s2, axis=0, keepdims=True) / n_valid - mean * mean, 0.0)
    tail = jnp.clip(kf / n_valid, 1e-6, 1.0 - 1e-6)
    tq = jnp.sqrt(-2.0 * jnp.log(jnp.minimum(tail, 1.0 - tail)))
    zq = tq - ((0.010328 * tq + 0.802853) * tq + 2.515517) / (
        ((0.001308 * tq + 0.189269) * tq + 1.432788) * tq + 1.0)
    probe0 = jnp.clip(mean + jnp.where(tail < 0.5, zq, -zq) * jnp.sqrt(var), row_min, row_max)

    ones = jnp.ones((1, TQ), F32)
    st = (row_min, row_max, n_valid, 0.0 * ones, jnp.where(all_rows, -F32_MAX, row_max),
          jnp.where(all_rows, 1.0, 0.0), 0.0 * ones)
    st = search_pass(st, probe0, probe0 < row_min)

    def fixed_body(i, st):
        return search_pass(st, *next_probe(st))

    st = lax.fori_loop(0, SEARCH_FIXED_PASSES, fixed_body, st)

    def more_cond(c):
        return jnp.logical_and(c[0] < MAX_SEARCH_ITERS, jnp.min(c[1][5]) < 0.5)

    def more_body(c):
        return c[0] + 1, search_pass(c[1], *next_probe(c[1], c[0] >= SEARCH_INTERP_PASSES))

    _, st = lax.while_loop(more_cond, more_body, (jnp.int32(0), st))
    thr, tie = st[4], st[6]

    def plain_mask():
        def body(c, carry):
            off = pl.multiple_of(c * CB, CB)
            bias_ref[pl.ds(off, CB), :] = jnp.where(sc_ref[pl.ds(off, CB), :] >= thr, 0.0, NEG_BIG)
            return carry

        lax.fori_loop(0, nbig, body, 0)

    def tied_mask():
        n_gt = count_rows(lambda blk: blk > thr)
        need = jnp.where(all_rows, F32_MAX, kf - n_gt)

        def body(c, seen):
            off = pl.multiple_of(c * CB, CB)
            blk = sc_ref[pl.ds(off, CB), :]
            eq = blk == thr
            pref = jnp.dot(tri_ref[...], jnp.where(eq, 1.0, 0.0).astype(BF16),
                           preferred_element_type=F32) + seen
            keep = jnp.logical_or(blk > thr, jnp.logical_and(eq, pref <= need))
            bias_ref[pl.ds(off, CB), :] = jnp.where(keep, 0.0, NEG_BIG)
            return pref[CB - 1:CB, :]

        lax.fori_loop(0, nbig, body, jnp.zeros((1, TQ), F32))

    lax.cond(jnp.max(tie) > 0.5, tied_mask, plain_mask)

    for n in range(B_KV_HEADS):
        def logit_body(c, mx8, n=n):
            off = pl.multiple_of(c * CB, CB)
            b = bias_ref[pl.ds(off, CB), :]
            lg = _dot_nt(k_ref[0, n, pl.ds(off, CB), :], qs_ref[n])
            lg = jnp.concatenate([lg[:, g * TQ:(g + 1) * TQ] + b for g in range(B_GROUP)], axis=1)
            lg_ref[pl.ds(off, CB), :] = lg
            return jnp.maximum(mx8, jnp.max(fold8(lg), axis=0))

        mx8 = paired_loop(nbig, logit_body, jnp.full((8, gw), NEG_BIG, F32))
        m = jnp.max(mx8, axis=0, keepdims=True)
        acc_ref[...] = jnp.zeros(acc_ref.shape, F32)

        def pv_body(c, l8, n=n, m=m):
            off = pl.multiple_of(c * CB, CB)
            p = jnp.exp2(lg_ref[pl.ds(off, CB), :] - m)
            acc_ref[...] += jnp.dot(vt_ref[0, n, c], p.astype(BF16), preferred_element_type=F32)
            return l8 + jnp.sum(fold8(p), axis=0)

        l8 = paired_loop(nbig, pv_body, jnp.zeros((8, gw), F32))
        o_t = acc_ref[...] / jnp.sum(l8, axis=0, keepdims=True)
        for g in range(B_GROUP):
            cols = slice((n * B_GROUP + g) * B_HEAD_DIM, (n * B_GROUP + g + 1) * B_HEAD_DIM)
            o = o_t[:, g * TQ:(g + 1) * TQ].T
            o_ref[:, cols] = (o * jax.nn.silu(z_ref[:, cols].astype(F32))).astype(BF16)


def _dsa(proj, tabs, iwt, k, vt, iklo, ikhi, gq, bsz, s):
    nq = s // TQ
    topk = min(TOPK_MAX, s // 4)
    gw = B_GROUP * TQ
    rowblk = lambda c: pl.BlockSpec((TQ, BLK), lambda b, i: (b * nq + i, c))
    tok = pl.BlockSpec((1, TQ, LANES), lambda b, i: (b, i, 0))
    ik = pl.BlockSpec((1, s, LANES), lambda b, i: (b, 0, 0))
    tri = jnp.tril(jnp.ones((CB, CB), BF16))
    return pl.pallas_call(
        functools.partial(_dsa_kernel, topk),
        grid=(bsz, nq),
        in_specs=[rowblk(COL_BQ), rowblk(COL_BZ), rowblk(COL_IQ), tok, tok, tok, tok,
                  pl.BlockSpec((1, IDX_HEADS, TQ), lambda b, i: (b, 0, i)),
                  pl.BlockSpec((1, B_KV_HEADS, s, B_HEAD_DIM), lambda b, i: (b, 0, 0, 0)),
                  pl.BlockSpec((1, B_KV_HEADS, s // CB, B_HEAD_DIM, CB), lambda b, i: (b, 0, 0, 0, 0)),
                  ik, ik,
                  pl.BlockSpec((1, B_HEAD_DIM), lambda b, i: (0, 0)),
                  pl.BlockSpec((CB, CB), lambda b, i: (0, 0))],
        out_specs=pl.BlockSpec((TQ, B_WIDTH), lambda b, i: (b * nq + i, 0)),
        out_shape=jax.ShapeDtypeStruct((bsz * s, B_WIDTH), BF16),
        scratch_shapes=[
            pltpu.VMEM((s, TQ), F32),
            pltpu.VMEM((s, TQ), F32),
            pltpu.VMEM((s, gw), F32),
            pltpu.VMEM((B_KV_HEADS, gw, B_HEAD_DIM), BF16),
            pltpu.VMEM((IDX_HEADS // 4, 2 * TQ, LANES), BF16),
            pltpu.VMEM((B_HEAD_DIM, gw), F32),
        ],
        compiler_params=_cparams(("parallel", "arbitrary")),
        name="dsa",
    )(proj, proj, proj, *tabs, iwt, k, vt, iklo, ikhi, gq, tri)


def _merge_kernel(ta_ref, tb_ref, tm_ref, ga_ref, gb_ref, gm_ref, x_ref,
                  wa_ref, wb_ref, wm_ref, wo_ref, o_ref):
    def branch(t_ref, g_ref, w_ref):
        y = jnp.dot(t_ref[...], w_ref[...], preferred_element_type=F32)
        return jax.nn.sigmoid(g_ref[...].astype(F32)) * y

    merged = branch(ta_ref, ga_ref, wa_ref) + branch(tb_ref, gb_ref, wb_ref) + branch(tm_ref, gm_ref, wm_ref)
    o_ref[...] = x_ref[...] + jnp.dot(merged.astype(BF16), wo_ref[...], preferred_element_type=F32)


def _merge(ta, tb, tmem, proj, x2, wa, wb, wm, wo):
    m = x2.shape[0]
    tm = min(256, m)
    act = pl.BlockSpec((tm, BLK), lambda i: (i, 0))
    gate = lambda c: pl.BlockSpec((tm, D_MODEL), lambda i: (i, c // 2))
    wide = pl.BlockSpec((tm, D_MODEL), lambda i: (i, 0))
    wbr = pl.BlockSpec((BLK, D_MODEL), lambda i: (0, 0), pipeline_mode=pl.Buffered(1))
    wout = pl.BlockSpec((D_MODEL, D_MODEL), lambda i: (0, 0), pipeline_mode=pl.Buffered(1))
    return pl.pallas_call(
        _merge_kernel,
        grid=(m // tm,),
        in_specs=[act, act, act, gate(COL_GA), gate(COL_GB), gate(COL_GM), wide, wbr, wbr, wbr, wout],
        out_specs=wide,
        out_shape=jax.ShapeDtypeStruct((m, D_MODEL), F32),
        compiler_params=_cparams(("parallel",)),
        name="merge",
    )(ta, tb, tmem, proj, proj, proj, x2, wa, wb, wm, wo)


def _rope_freqs():
    inv = lambda half: ROPE_THETA ** (-np.arange(half, dtype=np.float32) / half)
    freq = np.zeros((1, LANES), np.float32)
    hb, hi = B_HEAD_DIM // 8, IDX_DIM // 8
    freq[0, :hb] = inv(hb)
    freq[0, hb:hb + hi] = inv(hi)
    return jnp.asarray(freq)


def _layer(x, mem, positions, norm_gain, w_in, gmlp_ln_gain, gmlp_ln_bias, spatial_w, spatial_b,
           w_branch_a, q_norm_gain, k_norm_gain, idx_k_ln_gain, idx_k_ln_bias, w_branch_b,
           mem_norm_gain, w_mem_kv, mem_q_norm_gain, mem_k_norm_gain, w_branch_m, w_out):
    bsz, s, _ = x.shape
    m = bsz * s
    row = lambda a: a.reshape(1, -1).astype(F32)
    pad_lanes = lambda a: jnp.pad(a.reshape(1, -1).astype(F32), ((0, 0), (0, LANES - a.shape[-1])))
    x2 = x.reshape(m, D_MODEL)

    w_t = w_in.T
    h, k, vt, iklo, ikhi, cb, sb, ci, si, iwt = _kvprep(
        x, row(norm_gain), w_t, positions.reshape(bsz, s, 1).astype(jnp.int32), _rope_freqs(),
        row(k_norm_gain), pad_lanes(idx_k_ln_gain), pad_lanes(idx_k_ln_bias))
    proj = _inproj(h, w_t)

    t_a = _gmlp(proj, row(gmlp_ln_gain), row(gmlp_ln_bias), spatial_w.astype(F32), spatial_b.T.astype(F32))
    km, vm = _memkv(mem, row(mem_norm_gain), w_mem_kv.astype(BF16), row(mem_k_norm_gain))
    t_m = _memattn(proj, km, vm, row(mem_q_norm_gain), s)
    t_b = _dsa(proj, (cb, sb, ci, si), iwt, k, vt, iklo, ikhi, row(q_norm_gain), bsz, s)

    out = _merge(t_a, t_b, t_m, proj, x2, w_branch_a.astype(BF16), w_branch_b.astype(BF16),
                 w_branch_m.astype(BF16), w_out.astype(BF16))
    return out.reshape(bsz, s, D_MODEL)


def kernel(x, mem, positions, norm_gain, w_in, gmlp_ln_gain, gmlp_ln_bias, spatial_w, spatial_b, w_branch_a, q_norm_gain, k_norm_gain, idx_k_ln_gain, idx_k_ln_bias, w_branch_b, mem_norm_gain, w_mem_kv, mem_q_norm_gain, mem_k_norm_gain, w_branch_m, w_out):
    for l in range(norm_gain.shape[0]):
        x = _layer(x, mem, positions, norm_gain[l], w_in[l], gmlp_ln_gain[l], gmlp_ln_bias[l],
                   spatial_w[l], spatial_b[l], w_branch_a[l], q_norm_gain[l], k_norm_gain[l],
                   idx_k_ln_gain[l], idx_k_ln_bias[l], w_branch_b[l], mem_norm_gain[l], w_mem_kv[l],
                   mem_q_norm_gain[l], mem_k_norm_gain[l], w_branch_m[l], w_out[l])
    return x
```

```python
import functools

import numpy as np
import jax
import jax.numpy as jnp
from jax import lax
from jax.experimental import pallas as pl
from jax.experimental.pallas import tpu as pltpu

F32 = jnp.float32
BF16 = jnp.bfloat16

D_MODEL = 2048
ROPE_THETA = 500000.0
EPS = 1e-6
A_GROUPS = 8
A_GROUP_DIM = 128
A_WIDTH = A_GROUPS * A_GROUP_DIM
CHUNK = 128
B_HEADS = 8
B_KV_HEADS = 2
B_GROUP = B_HEADS // B_KV_HEADS
B_HEAD_DIM = 128
B_WIDTH = B_HEADS * B_HEAD_DIM
IDX_HEADS = 16
IDX_DIM = 64
TOPK_MAX = 256
M_HEADS = 4
M_HEAD_DIM = 256
M_WIDTH = M_HEADS * M_HEAD_DIM

SPLIT_SIZES = (
    A_WIDTH, A_WIDTH, A_WIDTH,
    B_WIDTH, B_KV_HEADS * B_HEAD_DIM, B_KV_HEADS * B_HEAD_DIM, B_WIDTH,
    IDX_HEADS * IDX_DIM, IDX_DIM, IDX_HEADS,
    M_WIDTH, M_WIDTH,
    D_MODEL, D_MODEL, D_MODEL,
)

LANES = 128
BLK = 1024
COL_AU, COL_AV, COL_AZ, COL_BQ, COL_BZ, COL_IQ, COL_MQ, COL_MZ = range(8)
COL_GA, COL_GB, COL_GM = 8, 10, 12
NB_MAIN = 14
_OFFS = [int(o) for o in np.concatenate([[0], np.cumsum(SPLIT_SIZES)])]
ROW_ALIGN = 16
MAIN_START = ([_OFFS[i] for i in (0, 1, 2, 3, 6, 7, 10, 11)]
              + [_OFFS[i] + d for i in (12, 13, 14) for d in (0, BLK)])
assert all(s % ROW_ALIGN == 0 for s in MAIN_START)
SMALL_KV_START, SMALL_KV_WIDTH = _OFFS[4], _OFFS[6] - _OFFS[4]
SMALL_IDX_START = _OFFS[8]
assert SMALL_KV_START % SMALL_KV_WIDTH == 0 and SMALL_IDX_START % LANES == 0
assert IDX_DIM + IDX_HEADS <= LANES and _OFFS[9] == SMALL_IDX_START + IDX_DIM

VMEM_LIMIT = 56 * 1024 * 1024
LOG2E = 1.4426950408889634
NEG_BIG = -1e30
F32_MAX = 3.4028234663852886e38

INPROJ_TM, INPROJ_TN = 2048, 512
TQ = 256
CK = 256
CB = 512
COUNT_WAYS = 8
SEARCH_MARGIN = 0.05
SEARCH_FIXED_PASSES = 11
SEARCH_INTERP_PASSES = 13
MAX_SEARCH_ITERS = 400


def _cparams(sem):
    return pltpu.CompilerParams(dimension_semantics=sem, vmem_limit_bytes=VMEM_LIMIT)


def _dot_nt(a, b):
    return lax.dot_general(a, b, (((1,), (1,)), ((), ())), preferred_element_type=F32)


def _rope(x, cos_t, sin_t, half, period):
    lane = lax.broadcasted_iota(jnp.int32, x.shape, 1) % period
    rolled = jnp.where(lane < half, pltpu.roll(x, LANES - half, 1), pltpu.roll(x, half, 1))
    return x * cos_t + rolled * sin_t


def _rms_rows(x, gain):
    ms = jnp.mean(x * x, axis=-1, keepdims=True)
    return (x * lax.rsqrt(ms + EPS) * gain).astype(BF16)


def _inproj_kernel(tab_ref, h_ref, wt_ref, proj_ref):
    del tab_ref
    proj_ref[...] = _dot_nt(h_ref[...], wt_ref[...].astype(BF16)).astype(BF16)


def _inproj(h, w_t):
    m = h.shape[0]
    tm = min(INPROJ_TM, m)
    starts = [s + d for s in MAIN_START for d in range(0, BLK, INPROJ_TN)]
    tab = jnp.asarray(np.array([s // ROW_ALIGN for s in starts], np.int32))
    return pl.pallas_call(
        _inproj_kernel,
        grid_spec=pltpu.PrefetchScalarGridSpec(
            num_scalar_prefetch=1,
            grid=(m // tm, len(starts)),
            in_specs=[
                pl.BlockSpec((tm, D_MODEL), lambda i, n, tab: (i, 0)),
                pl.BlockSpec((pl.Element(INPROJ_TN), pl.Element(D_MODEL)),
                             lambda i, n, tab: (tab[n] * ROW_ALIGN, 0)),
            ],
            out_specs=pl.BlockSpec((tm, INPROJ_TN), lambda i, n, tab: (i, n)),
        ),
        out_shape=jax.ShapeDtypeStruct((m, NB_MAIN * BLK), BF16),
        compiler_params=_cparams(("parallel", "arbitrary")),
        name="inproj",
    )(tab, h, w_t)


def _kvprep_kernel(x_ref, g_ref, wkv_ref, widx_ref, pos_ref, fc_ref, gk_ref, lng_ref, lnb_ref,
                   h_ref, k_ref, vt_ref, iklo_ref, ikhi_ref, cb_ref, sb_ref, ci_ref, si_ref, iwt_ref,
                   wkv_bf_ref, widx_bf_ref):
    @pl.when(jnp.logical_and(pl.program_id(0) == 0, pl.program_id(1) == 0))
    def _():
        wkv_bf_ref[...] = wkv_ref[...].astype(BF16)
        widx_bf_ref[...] = widx_ref[...].astype(BF16)

    h = _rms_rows(x_ref[...], g_ref[...])
    h_ref[...] = h
    kv = _dot_nt(h, wkv_bf_ref[...])
    ikp = _dot_nt(h, widx_bf_ref[...])

    hb, hi = B_HEAD_DIM // 8, IDX_DIM // 8
    ang = pos_ref[0].astype(F32) * fc_ref[...]
    cos_c, sin_c = jnp.cos(ang), jnp.sin(ang)
    lane = lax.broadcasted_iota(jnp.int32, ang.shape, 1)
    cos_b = jnp.where(lane < hb, cos_c, jnp.where(lane < 2 * hb, pltpu.roll(cos_c, hb, 1), 1.0))
    sin_b = jnp.where(lane < hb, -sin_c, jnp.where(lane < 2 * hb, pltpu.roll(sin_c, hb, 1), 0.0))
    cos_i = jnp.ones_like(cos_c)
    sin_i = jnp.zeros_like(sin_c)
    for head in range(LANES // IDX_DIM):
        for part, sign in enumerate((-1.0, 1.0)):
            first = head * IDX_DIM + part * hi
            here = jnp.logical_and(lane >= first, lane < first + hi)
            shift = (first - hb) % LANES
            cos_i = jnp.where(here, pltpu.roll(cos_c, shift, 1), cos_i)
            sin_i = jnp.where(here, sign * pltpu.roll(sin_c, shift, 1), sin_i)
    cb_ref[0] = cos_b
    sb_ref[0] = sin_b
    ci_ref[0] = cos_i
    si_ref[0] = sin_i

    for n in range(B_KV_HEADS):
        kh = kv[:, n * B_HEAD_DIM:(n + 1) * B_HEAD_DIM]
        r = lax.rsqrt(jnp.mean(kh * kh, axis=-1, keepdims=True) + EPS)
        kn = kh * r * gk_ref[...]
        k_ref[0, n] = _rope(kn, cos_b, sin_b, B_HEAD_DIM // 8, LANES).astype(BF16)
        vt = kv[:, (B_KV_HEADS + n) * B_HEAD_DIM:(B_KV_HEADS + n + 1) * B_HEAD_DIM].T
        for c in range(vt.shape[1] // CB):
            vt_ref[0, n, c] = vt[:, c * CB:(c + 1) * CB].astype(BF16)

    lane = lax.broadcasted_iota(jnp.int32, ikp.shape, 1)
    live = lane < IDX_DIM
    mu = jnp.sum(jnp.where(live, ikp, 0.0), axis=-1, keepdims=True) * (1.0 / IDX_DIM)
    d = jnp.where(live, ikp - mu, 0.0)
    var = jnp.sum(d * d, axis=-1, keepdims=True) * (1.0 / IDX_DIM)
    y = d * lax.rsqrt(var + EPS) * lng_ref[...] + lnb_ref[...]
    yr = jnp.where(live, _rope(y, cos_i, sin_i, IDX_DIM // 8, IDX_DIM), 0.0)
    iklo_ref[0] = yr.astype(BF16)
    ikhi_ref[0] = pltpu.roll(yr, IDX_DIM, 1).astype(BF16)

    iw = ikp * (IDX_DIM ** -0.5 * IDX_HEADS ** -0.5)
    iwt_ref[0] = iw.T[IDX_DIM:IDX_DIM + IDX_HEADS, :]


def _kvprep(x3, gain, w_t, pos3, freqs, gk, lng, lnb):
    bsz, s, _ = x3.shape
    tp = min(512, s)
    row = lambda b, i: (b, i, 0)
    const = lambda b, i: (0, 0)
    tab_spec = pl.BlockSpec((1, LANES), const)
    out_h = pl.BlockSpec((tp, D_MODEL), lambda b, i: (b * (s // tp) + i, 0))
    out_tok = pl.BlockSpec((1, tp, LANES), row)
    out_k = pl.BlockSpec((1, B_KV_HEADS, tp, B_HEAD_DIM), lambda b, i: (b, 0, i, 0))
    out_vt = pl.BlockSpec((1, B_KV_HEADS, tp // CB, B_HEAD_DIM, CB), lambda b, i: (b, 0, i, 0, 0))
    out_iwt = pl.BlockSpec((1, IDX_HEADS, tp), lambda b, i: (b, 0, i))
    return pl.pallas_call(
        _kvprep_kernel,
        grid=(bsz, s // tp),
        in_specs=[pl.BlockSpec((None, tp, D_MODEL), row), pl.BlockSpec((1, D_MODEL), const),
                  pl.BlockSpec((SMALL_KV_WIDTH, D_MODEL), lambda b, i: (SMALL_KV_START // SMALL_KV_WIDTH, 0)),
                  pl.BlockSpec((LANES, D_MODEL), lambda b, i: (SMALL_IDX_START // LANES, 0)),
                  pl.BlockSpec((1, tp, 1), row),
                  tab_spec, tab_spec, tab_spec, tab_spec],
        out_specs=[out_h, out_k, out_vt, out_tok, out_tok, out_tok, out_tok, out_tok, out_tok, out_iwt],
        out_shape=[
            jax.ShapeDtypeStruct((bsz * s, D_MODEL), BF16),
            jax.ShapeDtypeStruct((bsz, B_KV_HEADS, s, B_HEAD_DIM), BF16),
            jax.ShapeDtypeStruct((bsz, B_KV_HEADS, s // CB, B_HEAD_DIM, CB), BF16),
            jax.ShapeDtypeStruct((bsz, s, LANES), BF16),
            jax.ShapeDtypeStruct((bsz, s, LANES), BF16),
            jax.ShapeDtypeStruct((bsz, s, LANES), F32),
            jax.ShapeDtypeStruct((bsz, s, LANES), F32),
            jax.ShapeDtypeStruct((bsz, s, LANES), F32),
            jax.ShapeDtypeStruct((bsz, s, LANES), F32),
            jax.ShapeDtypeStruct((bsz, IDX_HEADS, s), F32),
        ],
        scratch_shapes=[pltpu.VMEM((SMALL_KV_WIDTH, D_MODEL), BF16), pltpu.VMEM((LANES, D_MODEL), BF16)],
        compiler_params=_cparams(("arbitrary", "arbitrary")),
        name="kvprep",
    )(x3, gain, w_t, w_t, pos3, freqs, gk, lng, lnb)


def _gmlp_kernel(u_ref, v_ref, z_ref, lng_ref, lnb_ref, ws_ref, sbt_ref, o_ref):
    tm = u_ref.shape[0]
    u = jax.nn.gelu(u_ref[...].astype(F32))
    v = jax.nn.gelu(v_ref[...].astype(F32))
    mu = jnp.mean(v, axis=-1, keepdims=True)
    d = v - mu
    var = jnp.mean(d * d, axis=-1, keepdims=True)
    vn = (d * lax.rsqrt(var + EPS) * lng_ref[...] + lnb_ref[...]).astype(BF16)
    gate = u * jax.nn.silu(z_ref[...].astype(F32))
    tri = (lax.broadcasted_iota(jnp.int32, (CHUNK, CHUNK), 1)
           <= lax.broadcasted_iota(jnp.int32, (CHUNK, CHUNK), 0))
    for g in range(A_GROUPS):
        wg = jnp.where(tri, ws_ref[g], 0.0).astype(BF16)
        bias = sbt_ref[:, g:g + 1]
        cols = slice(g * A_GROUP_DIM, (g + 1) * A_GROUP_DIM)
        for c in range(tm // CHUNK):
            rows = slice(c * CHUNK, (c + 1) * CHUNK)
            sg = jnp.dot(wg, vn[rows, cols], preferred_element_type=F32) + bias
            o_ref[rows, cols] = (gate[rows, cols] * sg).astype(BF16)


def _gmlp(proj, lng, lnb, ws, sbt):
    m = proj.shape[0]
    tm = min(512, m)
    col = lambda c: pl.BlockSpec((tm, BLK), lambda i: (i, c))
    full = lambda shape: pl.BlockSpec(shape, lambda i: (0,) * len(shape))
    return pl.pallas_call(
        _gmlp_kernel,
        grid=(m // tm,),
        in_specs=[col(COL_AU), col(COL_AV), col(COL_AZ), full((1, A_WIDTH)), full((1, A_WIDTH)),
                  full((A_GROUPS, CHUNK, CHUNK)), full((CHUNK, A_GROUPS))],
        out_specs=pl.BlockSpec((tm, A_WIDTH), lambda i: (i, 0)),
        out_shape=jax.ShapeDtypeStruct((m, A_WIDTH), BF16),
        compiler_params=_cparams(("parallel",)),
        name="gmlp",
    )(proj, proj, proj, lng, lnb, ws, sbt)


def _memkv_kernel(mem_ref, g_ref, w_ref, gk_ref, km_ref, vm_ref):
    x = mem_ref[0]
    ms = jnp.mean(x * x, axis=-1, keepdims=True)
    h = (x * lax.rsqrt(ms + EPS) * g_ref[...]).astype(BF16)
    kv = jnp.dot(h, w_ref[...], preferred_element_type=F32)
    for hd in range(M_HEADS):
        kh = kv[:, hd * M_HEAD_DIM:(hd + 1) * M_HEAD_DIM]
        r = lax.rsqrt(jnp.mean(kh * kh, axis=-1, keepdims=True) + EPS)
        km_ref[0, hd] = (kh * r * gk_ref[...]).astype(BF16)
        vm_ref[0, hd] = kv[:, M_WIDTH + hd * M_HEAD_DIM:M_WIDTH + (hd + 1) * M_HEAD_DIM].astype(BF16)


def _memkv(mem, gain, w_kv, gk):
    bsz, ml, _ = mem.shape
    out = pl.BlockSpec((1, M_HEADS, ml, M_HEAD_DIM), lambda b: (b, 0, 0, 0))
    shp = jax.ShapeDtypeStruct((bsz, M_HEADS, ml, M_HEAD_DIM), BF16)
    return pl.pallas_call(
        _memkv_kernel,
        grid=(bsz,),
        in_specs=[pl.BlockSpec((1, ml, D_MODEL), lambda b: (b, 0, 0)),
                  pl.BlockSpec((1, D_MODEL), lambda b: (0, 0)),
                  pl.BlockSpec((D_MODEL, 2 * M_WIDTH), lambda b: (0, 0)),
                  pl.BlockSpec((1, M_HEAD_DIM), lambda b: (0, 0))],
        out_specs=[out, out],
        out_shape=[shp, shp],
        compiler_params=_cparams(("parallel",)),
        name="memkv",
    )(mem, gain, w_kv, gk)


def _memattn_kernel(q_ref, z_ref, km_ref, vm_ref, gq_ref, o_ref):
    qscale = M_HEAD_DIM ** -0.5 * LOG2E
    for hd in range(M_HEADS):
        cols = slice(hd * M_HEAD_DIM, (hd + 1) * M_HEAD_DIM)
        q = q_ref[:, cols].astype(F32)
        r = lax.rsqrt(jnp.mean(q * q, axis=-1, keepdims=True) + EPS)
        qn = (q * r * gq_ref[...] * qscale).astype(BF16)
        lg = _dot_nt(qn, km_ref[0, hd])
        p = jnp.exp2(lg - jnp.max(lg, axis=-1, keepdims=True))
        l = jnp.sum(p, axis=-1, keepdims=True)
        o = jnp.dot(p.astype(BF16), vm_ref[0, hd], preferred_element_type=F32) / l
        o_ref[:, cols] = (o * jax.nn.silu(z_ref[:, cols].astype(F32))).astype(BF16)


def _memattn(proj, km, vm, gq, s):
    m = proj.shape[0]
    tm = min(512, s)
    per_b = s // tm
    ml = km.shape[2]
    kv_spec = pl.BlockSpec((1, M_HEADS, ml, M_HEAD_DIM), lambda i: (i // per_b, 0, 0, 0))
    return pl.pallas_call(
        _memattn_kernel,
        grid=(m // tm,),
        in_specs=[pl.BlockSpec((tm, BLK), lambda i: (i, COL_MQ)),
                  pl.BlockSpec((tm, BLK), lambda i: (i, COL_MZ)),
                  kv_spec, kv_spec,
                  pl.BlockSpec((1, M_HEAD_DIM), lambda i: (0, 0))],
        out_specs=pl.BlockSpec((tm, M_WIDTH), lambda i: (i, 0)),
        out_shape=jax.ShapeDtypeStruct((m, M_WIDTH), BF16),
        compiler_params=_cparams(("parallel",)),
        name="memattn",
    )(proj, proj, km, vm, gq)


def _dsa_kernel(topk, q_ref, z_ref, iq_ref, cb_ref, sb_ref, ci_ref, si_ref, iwt_ref,
                k_ref, vt_ref, iklo_ref, ikhi_ref, gq_ref, tri_ref, o_ref,
                sc_ref, bias_ref, lg_ref, qs_ref, iqs_ref, acc_ref):
    qb = pl.program_id(1)
    nck = (qb * TQ + TQ + CK - 1) // CK
    nbig = (qb * TQ + TQ + CB - 1) // CB
    t_lane = qb * TQ + lax.broadcasted_iota(jnp.int32, (1, TQ), 1)
    gw = B_GROUP * TQ

    def fold8(a, ways=1):
        return a.reshape(a.shape[0] // (8 * ways), 8 * ways, a.shape[1])

    def paired_loop(trips, body, init):
        carry = lax.fori_loop(0, trips // 2, lambda i, c: body(2 * i + 1, body(2 * i, c)), init)
        return lax.cond(trips % 2 == 1, lambda c: body(trips - 1, c), lambda c: c, carry)

    cos_b, sin_b = cb_ref[0], sb_ref[0]
    cos_i, sin_i = ci_ref[0], si_ref[0]
    qscale = B_HEAD_DIM ** -0.5 * LOG2E
    ones_mat = jnp.ones((B_HEAD_DIM, B_HEAD_DIM), BF16)
    for h in range(B_HEADS):
        slab = q_ref[:, h * B_HEAD_DIM:(h + 1) * B_HEAD_DIM].astype(F32)
        sq = slab * slab
        sq_hi = sq.astype(BF16)
        sq_lo = (sq - sq_hi.astype(F32)).astype(BF16)
        ssq = (jnp.dot(sq_hi, ones_mat, preferred_element_type=F32)
               + jnp.dot(sq_lo, ones_mat, preferred_element_type=F32))
        r = lax.rsqrt(ssq * (1.0 / B_HEAD_DIM) + EPS)
        qr = _rope(slab * r * gq_ref[...], cos_b, sin_b, B_HEAD_DIM // 8, LANES) * qscale
        g = h % B_GROUP
        qs_ref[h // B_GROUP, g * TQ:(g + 1) * TQ, :] = qr.astype(BF16)
    for j in range(IDX_HEADS // 2):
        slab = iq_ref[:, j * LANES:(j + 1) * LANES].astype(F32)
        iqs_ref[j // 2, (j % 2) * TQ:(j % 2 + 1) * TQ, :] = (
            _rope(slab, cos_i, sin_i, IDX_DIM // 8, IDX_DIM).astype(BF16))
    wt = iwt_ref[0]

    def idx_body(c, carry):
        mn8, mx8, s1, s2 = carry
        off = pl.multiple_of(c * CK, CK)
        acc = jnp.zeros((CK, TQ), F32)
        for jj in range(IDX_HEADS // 4):
            rhs = iqs_ref[jj]
            for half, keys_ref in enumerate((iklo_ref, ikhi_ref)):
                d = _dot_nt(keys_ref[0, pl.ds(off, CK), :], rhs)
                ha, hb = 4 * jj + half, 4 * jj + 2 + half
                acc = (acc + jnp.maximum(d[:, :TQ], 0.0) * wt[ha:ha + 1, :]
                       + jnp.maximum(d[:, TQ:], 0.0) * wt[hb:hb + 1, :])
        key = off + lax.broadcasted_iota(jnp.int32, (CK, TQ), 0)
        causal = key <= t_lane
        sc = jnp.where(causal, acc, -jnp.inf)
        sc_ref[pl.ds(off, CK), :] = sc
        live = jnp.where(causal, acc, 0.0)
        mn8 = jnp.minimum(mn8, jnp.min(fold8(jnp.where(causal, acc, jnp.inf)), axis=0))
        mx8 = jnp.maximum(mx8, jnp.max(fold8(sc), axis=0))
        s1 = s1 + jnp.sum(fold8(live), axis=0)
        s2 = s2 + jnp.sum(fold8(live * live), axis=0)
        return mn8, mx8, s1, s2

    zero8 = jnp.zeros((8, TQ), F32)
    stats = (jnp.full((8, TQ), jnp.inf, F32), jnp.full((8, TQ), -jnp.inf, F32), zero8, zero8)
    mn8, mx8, s1, s2 = paired_loop(nck, idx_body, stats)
    row_min = jnp.min(mn8, axis=0, keepdims=True)
    row_max = jnp.max(mx8, axis=0, keepdims=True)

    def fill_body(c, carry):
        sc_ref[pl.ds(pl.multiple_of(c * CK, CK), CK), :] = jnp.full((CK, TQ), -jnp.inf, F32)
        return carry

    lax.fori_loop(nck, nbig * (CB // CK), fill_body, 0)

    def count_rows(pred):
        rows = 8 * COUNT_WAYS

        def body(c, cnt):
            off = pl.multiple_of(c * CB, CB)
            for r in range(CB // rows):
                cnt = cnt + jnp.where(pred(sc_ref[pl.ds(off + r * rows, rows), :]), 1.0, 0.0)
            return cnt

        cnt = lax.fori_loop(0, nbig, body, jnp.zeros((rows, TQ), F32))
        return jnp.sum(cnt, axis=0, keepdims=True)

    kf = float(topk)
    n_valid = (t_lane + 1).astype(F32)
    all_rows = (t_lane + 1) <= topk

    def search_pass(st, probe, stuck):
        lo, hi, clo, chi, thr, fin, tie = st
        cnt = count_rows(lambda blk: blk >= probe)
        active = fin < 0.5
        hit = cnt == kf
        end_thr = jnp.where(jnp.logical_and(stuck, cnt < kf), lo, probe)
        ends = jnp.logical_and(active, jnp.logical_or(hit, stuck))
        thr = jnp.where(ends, end_thr, thr)
        tie = jnp.where(jnp.logical_and(ends, jnp.logical_not(hit)), 1.0, tie)
        fin = jnp.where(ends, 1.0, fin)
        up = jnp.logical_and(active, cnt > kf)
        dn = jnp.logical_and(active, cnt < kf)
        return (jnp.where(up, probe, lo), jnp.where(dn, probe, hi), jnp.where(up, cnt, clo),
                jnp.where(dn, cnt, chi), thr, fin, tie)

    def next_probe(st, halve=False):
        lo, hi, clo, chi = st[:4]
        frac = jnp.clip((clo - kf + 0.5) / (clo - chi + 1.0), SEARCH_MARGIN, 1.0 - SEARCH_MARGIN)
        probe = lo + (hi - lo) * frac
        outside = jnp.logical_or(jnp.logical_or(probe <= lo, probe >= hi), halve)
        probe = jnp.where(outside, lo * 0.5 + hi * 0.5, probe)
        stuck = jnp.logical_or(probe <= lo, probe >= hi)
        return jnp.where(stuck, hi, probe), stuck

    mean = jnp.sum(s1, axis=0, keepdims=True) / n_valid
    var = jnp.maximum(jnp.sum(s2, axis=0, keepdims=True) / n_valid - mean * mean, 0.0)
    tail = jnp.clip(kf / n_valid, 1e-6, 1.0 - 1e-6)
    tq = jnp.sqrt(-2.0 * jnp.log(jnp.minimum(tail, 1.0 - tail)))
    zq = tq - ((0.010328 * tq + 0.802853) * tq + 2.515517) / (
        ((0.001308 * tq + 0.189269) * tq + 1.432788) * tq + 1.0)
    probe0 = jnp.clip(mean + jnp.where(tail < 0.5, zq, -zq) * jnp.sqrt(var), row_min, row_max)

    ones = jnp.ones((1, TQ), F32)
    st = (row_min, row_max, n_valid, 0.0 * ones, jnp.where(all_rows, -F32_MAX, row_max),
          jnp.where(all_rows, 1.0, 0.0), 0.0 * ones)
    st = search_pass(st, probe0, probe0 < row_min)

    def fixed_body(i, st):
        return search_pass(st, *next_probe(st))

    st = lax.fori_loop(0, SEARCH_FIXED_PASSES, fixed_body, st)

    def more_cond(c):
        return jnp.logical_and(c[0] < MAX_SEARCH_ITERS, jnp.min(c[1][5]) < 0.5)

    def more_body(c):
        return c[0] + 1, search_pass(c[1], *next_probe(c[1], c[0] >= SEARCH_INTERP_PASSES))

    _, st = lax.while_loop(more_cond, more_body, (jnp.int32(0), st))
    thr, tie = st[4], st[6]

    def plain_mask():
        def body(c, carry):
            off = pl.multiple_of(c * CB, CB)
            bias_ref[pl.ds(off, CB), :] = jnp.where(sc_ref[pl.ds(off, CB), :] >= thr, 0.0, NEG_BIG)
            return carry

        lax.fori_loop(0, nbig, body, 0)

    def tied_mask():
        n_gt = count_rows(lambda blk: blk > thr)
        need = jnp.where(all_rows, F32_MAX, kf - n_gt)

        def body(c, seen):
            off = pl.multiple_of(c * CB, CB)
            blk = sc_ref[pl.ds(off, CB), :]
            eq = blk == thr
            pref = jnp.dot(tri_ref[...], jnp.where(eq, 1.0, 0.0).astype(BF16),
                           preferred_element_type=F32) + seen
            keep = jnp.logical_or(blk > thr, jnp.logical_and(eq, pref <= need))
            bias_ref[pl.ds(off, CB), :] = jnp.where(keep, 0.0, NEG_BIG)
            return pref[CB - 1:CB, :]

        lax.fori_loop(0, nbig, body, jnp.zeros((1, TQ), F32))

    lax.cond(jnp.max(tie) > 0.5, tied_mask, plain_mask)

    for n in range(B_KV_HEADS):
        def logit_body(c, mx8, n=n):
            off = pl.multiple_of(c * CB, CB)
            b = bias_ref[pl.ds(off, CB), :]
            lg = _dot_nt(k_ref[0, n, pl.ds(off, CB), :], qs_ref[n])
            lg = jnp.concatenate([lg[:, g * TQ:(g + 1) * TQ] + b for g in range(B_GROUP)], axis=1)
            lg_ref[pl.ds(off, CB), :] = lg
            return jnp.maximum(mx8, jnp.max(fold8(lg), axis=0))

        mx8 = paired_loop(nbig, logit_body, jnp.full((8, gw), NEG_BIG, F32))
        m = jnp.max(mx8, axis=0, keepdims=True)
        acc_ref[...] = jnp.zeros(acc_ref.shape, F32)

        def pv_body(c, l8, n=n, m=m):
            off = pl.multiple_of(c * CB, CB)
            p = jnp.exp2(lg_ref[pl.ds(off, CB), :] - m)
            acc_ref[...] += jnp.dot(vt_ref[0, n, c], p.astype(BF16), preferred_element_type=F32)
            return l8 + jnp.sum(fold8(p), axis=0)

        l8 = paired_loop(nbig, pv_body, jnp.zeros((8, gw), F32))
        o_t = acc_ref[...] / jnp.sum(l8, axis=0, keepdims=True)
        for g in range(B_GROUP):
            cols = slice((n * B_GROUP + g) * B_HEAD_DIM, (n * B_GROUP + g + 1) * B_HEAD_DIM)
            o = o_t[:, g * TQ:(g + 1) * TQ].T
            o_ref[:, cols] = (o * jax.nn.silu(z_ref[:, cols].astype(F32))).astype(BF16)


def _dsa(proj, tabs, iwt, k, vt, iklo, ikhi, gq, bsz, s):
    nq = s // TQ
    topk = min(TOPK_MAX, s // 4)
    gw = B_GROUP * TQ
    rowblk = lambda c: pl.BlockSpec((TQ, BLK), lambda b, i: (b * nq + i, c))
    tok = pl.BlockSpec((1, TQ, LANES), lambda b, i: (b, i, 0))
    ik = pl.BlockSpec((1, s, LANES), lambda b, i: (b, 0, 0))
    tri = jnp.tril(jnp.ones((CB, CB), BF16))
    return pl.pallas_call(
        functools.partial(_dsa_kernel, topk),
        grid=(bsz, nq),
        in_specs=[rowblk(COL_BQ), rowblk(COL_BZ), rowblk(COL_IQ), tok, tok, tok, tok,
                  pl.BlockSpec((1, IDX_HEADS, TQ), lambda b, i: (b, 0, i)),
                  pl.BlockSpec((1, B_KV_HEADS, s, B_HEAD_DIM), lambda b, i: (b, 0, 0, 0)),
                  pl.BlockSpec((1, B_KV_HEADS, s // CB, B_HEAD_DIM, CB), lambda b, i: (b, 0, 0, 0, 0)),
                  ik, ik,
                  pl.BlockSpec((1, B_HEAD_DIM), lambda b, i: (0, 0)),
                  pl.BlockSpec((CB, CB), lambda b, i: (0, 0))],
        out_specs=pl.BlockSpec((TQ, B_WIDTH), lambda b, i: (b * nq + i, 0)),
        out_shape=jax.ShapeDtypeStruct((bsz * s, B_WIDTH), BF16),
        scratch_shapes=[
            pltpu.VMEM((s, TQ), F32),
            pltpu.VMEM((s, TQ), F32),
            pltpu.VMEM((s, gw), F32),
            pltpu.VMEM((B_KV_HEADS, gw, B_HEAD_DIM), BF16),
            pltpu.VMEM((IDX_HEADS // 4, 2 * TQ, LANES), BF16),
            pltpu.VMEM((B_HEAD_DIM, gw), F32),
        ],
        compiler_params=_cparams(("parallel", "arbitrary")),
        name="dsa",
    )(proj, proj, proj, *tabs, iwt, k, vt, iklo, ikhi, gq, tri)


def _merge_kernel(ta_ref, tb_ref, tm_ref, ga_ref, gb_ref, gm_ref, x_ref,
                  wa_ref, wb_ref, wm_ref, wo_ref, o_ref):
    def branch(t_ref, g_ref, w_ref):
        y = jnp.dot(t_ref[...], w_ref[...], preferred_element_type=F32)
        return jax.nn.sigmoid(g_ref[...].astype(F32)) * y

    merged = branch(ta_ref, ga_ref, wa_ref) + branch(tb_ref, gb_ref, wb_ref) + branch(tm_ref, gm_ref, wm_ref)
    o_ref[...] = x_ref[...] + jnp.dot(merged.astype(BF16), wo_ref[...], preferred_element_type=F32)


def _merge(ta, tb, tmem, proj, x2, wa, wb, wm, wo):
    m = x2.shape[0]
    tm = min(256, m)
    act = pl.BlockSpec((tm, BLK), lambda i: (i, 0))
    gate = lambda c: pl.BlockSpec((tm, D_MODEL), lambda i: (i, c // 2))
    wide = pl.BlockSpec((tm, D_MODEL), lambda i: (i, 0))
    wbr = pl.BlockSpec((BLK, D_MODEL), lambda i: (0, 0), pipeline_mode=pl.Buffered(1))
    wout = pl.BlockSpec((D_MODEL, D_MODEL), lambda i: (0, 0), pipeline_mode=pl.Buffered(1))
    return pl.pallas_call(
        _merge_kernel,
        grid=(m // tm,),
        in_specs=[act, act, act, gate(COL_GA), gate(COL_GB), gate(COL_GM), wide, wbr, wbr, wbr, wout],
        out_specs=wide,
        out_shape=jax.ShapeDtypeStruct((m, D_MODEL), F32),
        compiler_params=_cparams(("parallel",)),
        name="merge",
    )(ta, tb, tmem, proj, proj, proj, x2, wa, wb, wm, wo)


def _rope_freqs():
    inv = lambda half: ROPE_THETA ** (-np.arange(half, dtype=np.float32) / half)
    freq = np.zeros((1, LANES), np.float32)
    hb, hi = B_HEAD_DIM // 8, IDX_DIM // 8
    freq[0, :hb] = inv(hb)
    freq[0, hb:hb + hi] = inv(hi)
    return jnp.asarray(freq)


def _layer(x, mem, positions, norm_gain, w_in, gmlp_ln_gain, gmlp_ln_bias, spatial_w, spatial_b,
           w_branch_a, q_norm_gain, k_norm_gain, idx_k_ln_gain, idx_k_ln_bias, w_branch_b,
           mem_norm_gain, w_mem_kv, mem_q_norm_gain, mem_k_norm_gain, w_branch_m, w_out):
    bsz, s, _ = x.shape
    m = bsz * s
    row = lambda a: a.reshape(1, -1).astype(F32)
    pad_lanes = lambda a: jnp.pad(a.reshape(1, -1).astype(F32), ((0, 0), (0, LANES - a.shape[-1])))
    x2 = x.reshape(m, D_MODEL)

    w_t = w_in.T
    h, k, vt, iklo, ikhi, cb, sb, ci, si, iwt = _kvprep(
        x, row(norm_gain), w_t, positions.reshape(bsz, s, 1).astype(jnp.int32), _rope_freqs(),
        row(k_norm_gain), pad_lanes(idx_k_ln_gain), pad_lanes(idx_k_ln_bias))
    proj = _inproj(h, w_t)

    t_a = _gmlp(proj, row(gmlp_ln_gain), row(gmlp_ln_bias), spatial_w.astype(F32), spatial_b.T.astype(F32))
    km, vm = _memkv(mem, row(mem_norm_gain), w_mem_kv.astype(BF16), row(mem_k_norm_gain))
    t_m = _memattn(proj, km, vm, row(mem_q_norm_gain), s)
    t_b = _dsa(proj, (cb, sb, ci, si), iwt, k, vt, iklo, ikhi, row(q_norm_gain), bsz, s)

    out = _merge(t_a, t_b, t_m, proj, x2, w_branch_a.astype(BF16), w_branch_b.astype(BF16),
                 w_branch_m.astype(BF16), w_out.astype(BF16))
    return out.reshape(bsz, s, D_MODEL)


def kernel(x, mem, positions, norm_gain, w_in, gmlp_ln_gain, gmlp_ln_bias, spatial_w, spatial_b, w_branch_a, q_norm_gain, k_norm_gain, idx_k_ln_gain, idx_k_ln_bias, w_branch_b, mem_norm_gain, w_mem_kv, mem_q_norm_gain, mem_k_norm_gain, w_branch_m, w_out):
    for l in range(norm_gain.shape[0]):
        x = _layer(x, mem, positions, norm_gain[l], w_in[l], gmlp_ln_gain[l], gmlp_ln_bias[l],
                   spatial_w[l], spatial_b[l], w_branch_a[l], q_norm_gain[l], k_norm_gain[l],
                   idx_k_ln_gain[l], idx_k_ln_bias[l], w_branch_b[l], mem_norm_gain[l], w_mem_kv[l],
                   mem_q_norm_gain[l], mem_k_norm_gain[l], w_branch_m[l], w_out[l])
    return x
```

```python
import functools

import numpy as np
import jax
import jax.numpy as jnp
from jax import lax
from jax.experimental import pallas as pl
from jax.experimental.pallas import tpu as pltpu

F32 = jnp.float32
BF16 = jnp.bfloat16

D_MODEL = 2048
ROPE_THETA = 500000.0
EPS = 1e-6
A_GROUPS = 8
A_GROUP_DIM = 128
A_WIDTH = A_GROUPS * A_GROUP_DIM
CHUNK = 128
B_HEADS = 8
B_KV_HEADS = 2
B_GROUP = B_HEADS // B_KV_HEADS
B_HEAD_DIM = 128
B_WIDTH = B_HEADS * B_HEAD_DIM
IDX_HEADS = 16
IDX_DIM = 64
TOPK_MAX = 256
M_HEADS = 4
M_HEAD_DIM = 256
M_WIDTH = M_HEADS * M_HEAD_DIM

SPLIT_SIZES = (
    A_WIDTH, A_WIDTH, A_WIDTH,
    B_WIDTH, B_KV_HEADS * B_HEAD_DIM, B_KV_HEADS * B_HEAD_DIM, B_WIDTH,
    IDX_HEADS * IDX_DIM, IDX_DIM, IDX_HEADS,
    M_WIDTH, M_WIDTH,
    D_MODEL, D_MODEL, D_MODEL,
)

LANES = 128
BLK = 1024
COL_AU, COL_AV, COL_AZ, COL_BQ, COL_BZ, COL_IQ, COL_MQ, COL_MZ = range(8)
COL_GA, COL_GB, COL_GM = 8, 10, 12
NB_MAIN = 14
_OFFS = [int(o) for o in np.concatenate([[0], np.cumsum(SPLIT_SIZES)])]
ROW_ALIGN = 16
MAIN_START = ([_OFFS[i] for i in (0, 1, 2, 3, 6, 7, 10, 11)]
              + [_OFFS[i] + d for i in (12, 13, 14) for d in (0, BLK)])
assert all(s % ROW_ALIGN == 0 for s in MAIN_START)
SMALL_KV_START, SMALL_KV_WIDTH = _OFFS[4], _OFFS[6] - _OFFS[4]
SMALL_IDX_START = _OFFS[8]
assert SMALL_KV_START % SMALL_KV_WIDTH == 0 and SMALL_IDX_START % LANES == 0
assert IDX_DIM + IDX_HEADS <= LANES and _OFFS[9] == SMALL_IDX_START + IDX_DIM

VMEM_LIMIT = 56 * 1024 * 1024
LOG2E = 1.4426950408889634
NEG_BIG = -1e30
F32_MAX = 3.4028234663852886e38

INPROJ_TM, INPROJ_TN = 2048, 1024
TQ = 256
CK = 256
CB = 512
COUNT_WAYS = 8
SEARCH_MARGIN = 0.05
SEARCH_FIXED_PASSES = 11
SEARCH_INTERP_PASSES = 13
MAX_SEARCH_ITERS = 400


def _cparams(sem):
    return pltpu.CompilerParams(dimension_semantics=sem, vmem_limit_bytes=VMEM_LIMIT)


def _dot_nt(a, b):
    return lax.dot_general(a, b, (((1,), (1,)), ((), ())), preferred_element_type=F32)


def _rope(x, cos_t, sin_t, half, period):
    lane = lax.broadcasted_iota(jnp.int32, x.shape, 1) % period
    rolled = jnp.where(lane < half, pltpu.roll(x, LANES - half, 1), pltpu.roll(x, half, 1))
    return x * cos_t + rolled * sin_t


def _rms_rows(x, gain):
    ms = jnp.mean(x * x, axis=-1, keepdims=True)
    return (x * lax.rsqrt(ms + EPS) * gain).astype(BF16)


def _inproj_kernel(tab_ref, h_ref, wt_ref, proj_ref):
    del tab_ref
    proj_ref[...] = _dot_nt(h_ref[...], wt_ref[...].astype(BF16)).astype(BF16)


def _inproj(h, w_t):
    m = h.shape[0]
    tm = min(INPROJ_TM, m)
    starts = [s + d for s in MAIN_START for d in range(0, BLK, INPROJ_TN)]
    tab = jnp.asarray(np.array([s // ROW_ALIGN for s in starts], np.int32))
    return pl.pallas_call(
        _inproj_kernel,
        grid_spec=pltpu.PrefetchScalarGridSpec(
            num_scalar_prefetch=1,
            grid=(m // tm, len(starts)),
            in_specs=[
                pl.BlockSpec((tm, D_MODEL), lambda i, n, tab: (i, 0)),
                pl.BlockSpec((pl.Element(INPROJ_TN), pl.Element(D_MODEL)),
                             lambda i, n, tab: (tab[n] * ROW_ALIGN, 0)),
            ],
            out_specs=pl.BlockSpec((tm, INPROJ_TN), lambda i, n, tab: (i, n)),
        ),
        out_shape=jax.ShapeDtypeStruct((m, NB_MAIN * BLK), BF16),
        compiler_params=_cparams(("parallel", "arbitrary")),
        name="inproj",
    )(tab, h, w_t)


def _kvprep_kernel(x_ref, g_ref, wkv_ref, widx_ref, pos_ref, fc_ref, gk_ref, lng_ref, lnb_ref,
                   h_ref, k_ref, vt_ref, iklo_ref, ikhi_ref, cb_ref, sb_ref, ci_ref, si_ref, iwt_ref,
                   wkv_bf_ref, widx_bf_ref):
    @pl.when(jnp.logical_and(pl.program_id(0) == 0, pl.program_id(1) == 0))
    def _():
        wkv_bf_ref[...] = wkv_ref[...].astype(BF16)
        widx_bf_ref[...] = widx_ref[...].astype(BF16)

    h = _rms_rows(x_ref[...], g_ref[...])
    h_ref[...] = h
    kv = _dot_nt(h, wkv_bf_ref[...])
    ikp = _dot_nt(h, widx_bf_ref[...])

    hb, hi = B_HEAD_DIM // 8, IDX_DIM // 8
    ang = pos_ref[0].astype(F32) * fc_ref[...]
    cos_c, sin_c = jnp.cos(ang), jnp.sin(ang)
    lane = lax.broadcasted_iota(jnp.int32, ang.shape, 1)
    cos_b = jnp.where(lane < hb, cos_c, jnp.where(lane < 2 * hb, pltpu.roll(cos_c, hb, 1), 1.0))
    sin_b = jnp.where(lane < hb, -sin_c, jnp.where(lane < 2 * hb, pltpu.roll(sin_c, hb, 1), 0.0))
    cos_i = jnp.ones_like(cos_c)
    sin_i = jnp.zeros_like(sin_c)
    for head in range(LANES // IDX_DIM):
        for part, sign in enumerate((-1.0, 1.0)):
            first = head * IDX_DIM + part * hi
            here = jnp.logical_and(lane >= first, lane < first + hi)
            shift = (first - hb) % LANES
            cos_i = jnp.where(here, pltpu.roll(cos_c, shift, 1), cos_i)
            sin_i = jnp.where(here, sign * pltpu.roll(sin_c, shift, 1), sin_i)
    cb_ref[0] = cos_b
    sb_ref[0] = sin_b
    ci_ref[0] = cos_i
    si_ref[0] = sin_i

    for n in range(B_KV_HEADS):
        kh = kv[:, n * B_HEAD_DIM:(n + 1) * B_HEAD_DIM]
        r = lax.rsqrt(jnp.mean(kh * kh, axis=-1, keepdims=True) + EPS)
        kn = kh * r * gk_ref[...]
        k_ref[0, n] = _rope(kn, cos_b, sin_b, B_HEAD_DIM // 8, LANES).astype(BF16)
        vt = kv[:, (B_KV_HEADS + n) * B_HEAD_DIM:(B_KV_HEADS + n + 1) * B_HEAD_DIM].T
        for c in range(vt.shape[1] // CB):
            vt_ref[0, n, c] = vt[:, c * CB:(c + 1) * CB].astype(BF16)

    lane = lax.broadcasted_iota(jnp.int32, ikp.shape, 1)
    live = lane < IDX_DIM
    mu = jnp.sum(jnp.where(live, ikp, 0.0), axis=-1, keepdims=True) * (1.0 / IDX_DIM)
    d = jnp.where(live, ikp - mu, 0.0)
    var = jnp.sum(d * d, axis=-1, keepdims=True) * (1.0 / IDX_DIM)
    y = d * lax.rsqrt(var + EPS) * lng_ref[...] + lnb_ref[...]
    yr = jnp.where(live, _rope(y, cos_i, sin_i, IDX_DIM // 8, IDX_DIM), 0.0)
    iklo_ref[0] = yr.astype(BF16)
    ikhi_ref[0] = pltpu.roll(yr, IDX_DIM, 1).astype(BF16)

    iw = ikp * (IDX_DIM ** -0.5 * IDX_HEADS ** -0.5)
    iwt_ref[0] = iw.T[IDX_DIM:IDX_DIM + IDX_HEADS, :]


def _kvprep(x3, gain, w_t, pos3, freqs, gk, lng, lnb):
    bsz, s, _ = x3.shape
    tp = min(512, s)
    row = lambda b, i: (b, i, 0)
    const = lambda b, i: (0, 0)
    tab_spec = pl.BlockSpec((1, LANES), const)
    out_h = pl.BlockSpec((tp, D_MODEL), lambda b, i: (b * (s // tp) + i, 0))
    out_tok = pl.BlockSpec((1, tp, LANES), row)
    out_k = pl.BlockSpec((1, B_KV_HEADS, tp, B_HEAD_DIM), lambda b, i: (b, 0, i, 0))
    out_vt = pl.BlockSpec((1, B_KV_HEADS, tp // CB, B_HEAD_DIM, CB), lambda b, i: (b, 0, i, 0, 0))
    out_iwt = pl.BlockSpec((1, IDX_HEADS, tp), lambda b, i: (b, 0, i))
    return pl.pallas_call(
        _kvprep_kernel,
        grid=(bsz, s // tp),
        in_specs=[pl.BlockSpec((None, tp, D_MODEL), row), pl.BlockSpec((1, D_MODEL), const),
                  pl.BlockSpec((SMALL_KV_WIDTH, D_MODEL), lambda b, i: (SMALL_KV_START // SMALL_KV_WIDTH, 0)),
                  pl.BlockSpec((LANES, D_MODEL), lambda b, i: (SMALL_IDX_START // LANES, 0)),
                  pl.BlockSpec((1, tp, 1), row),
                  tab_spec, tab_spec, tab_spec, tab_spec],
        out_specs=[out_h, out_k, out_vt, out_tok, out_tok, out_tok, out_tok, out_tok, out_tok, out_iwt],
        out_shape=[
            jax.ShapeDtypeStruct((bsz * s, D_MODEL), BF16),
            jax.ShapeDtypeStruct((bsz, B_KV_HEADS, s, B_HEAD_DIM), BF16),
            jax.ShapeDtypeStruct((bsz, B_KV_HEADS, s // CB, B_HEAD_DIM, CB), BF16),
            jax.ShapeDtypeStruct((bsz, s, LANES), BF16),
            jax.ShapeDtypeStruct((bsz, s, LANES), BF16),
            jax.ShapeDtypeStruct((bsz, s, LANES), F32),
            jax.ShapeDtypeStruct((bsz, s, LANES), F32),
            jax.ShapeDtypeStruct((bsz, s, LANES), F32),
            jax.ShapeDtypeStruct((bsz, s, LANES), F32),
            jax.ShapeDtypeStruct((bsz, IDX_HEADS, s), F32),
        ],
        scratch_shapes=[pltpu.VMEM((SMALL_KV_WIDTH, D_MODEL), BF16), pltpu.VMEM((LANES, D_MODEL), BF16)],
        compiler_params=_cparams(("arbitrary", "arbitrary")),
        name="kvprep",
    )(x3, gain, w_t, w_t, pos3, freqs, gk, lng, lnb)


def _gelu_tanh(x):
    c = -2.0 * (2.0 / np.pi) ** 0.5 * LOG2E
    return x / (1.0 + jnp.exp2(x * (x * x * (0.044715 * c) + c)))


def _gmlp_kernel(u_ref, v_ref, z_ref, lng_ref, lnb_ref, ws_ref, sbt_ref, o_ref):
    tm = u_ref.shape[0]
    u = _gelu_tanh(u_ref[...].astype(F32))
    v = _gelu_tanh(v_ref[...].astype(F32))
    mu = jnp.mean(v, axis=-1, keepdims=True)
    d = v - mu
    var = jnp.mean(d * d, axis=-1, keepdims=True)
    vn = (d * lax.rsqrt(var + EPS) * lng_ref[...] + lnb_ref[...]).astype(BF16)
    gate = u * jax.nn.silu(z_ref[...].astype(F32))
    tri = (lax.broadcasted_iota(jnp.int32, (CHUNK, CHUNK), 1)
           <= lax.broadcasted_iota(jnp.int32, (CHUNK, CHUNK), 0))
    for g in range(A_GROUPS):
        wg = jnp.where(tri, ws_ref[g], 0.0).astype(BF16)
        bias = sbt_ref[:, g:g + 1]
        cols = slice(g * A_GROUP_DIM, (g + 1) * A_GROUP_DIM)
        for c in range(tm // CHUNK):
            rows = slice(c * CHUNK, (c + 1) * CHUNK)
            sg = jnp.dot(wg, vn[rows, cols], preferred_element_type=F32) + bias
            o_ref[rows, cols] = (gate[rows, cols] * sg).astype(BF16)


def _gmlp(proj, lng, lnb, ws, sbt):
    m = proj.shape[0]
    tm = min(512, m)
    col = lambda c: pl.BlockSpec((tm, BLK), lambda i: (i, c))
    full = lambda shape: pl.BlockSpec(shape, lambda i: (0,) * len(shape))
    return pl.pallas_call(
        _gmlp_kernel,
        grid=(m // tm,),
        in_specs=[col(COL_AU), col(COL_AV), col(COL_AZ), full((1, A_WIDTH)), full((1, A_WIDTH)),
                  full((A_GROUPS, CHUNK, CHUNK)), full((CHUNK, A_GROUPS))],
        out_specs=pl.BlockSpec((tm, A_WIDTH), lambda i: (i, 0)),
        out_shape=jax.ShapeDtypeStruct((m, A_WIDTH), BF16),
        compiler_params=_cparams(("parallel",)),
        name="gmlp",
    )(proj, proj, proj, lng, lnb, ws, sbt)


def _memkv_kernel(mem_ref, g_ref, w_ref, gk_ref, km_ref, vm_ref):
    x = mem_ref[0]
    ms = jnp.mean(x * x, axis=-1, keepdims=True)
    h = (x * lax.rsqrt(ms + EPS) * g_ref[...]).astype(BF16)
    kv = jnp.dot(h, w_ref[...], preferred_element_type=F32)
    for hd in range(M_HEADS):
        kh = kv[:, hd * M_HEAD_DIM:(hd + 1) * M_HEAD_DIM]
        r = lax.rsqrt(jnp.mean(kh * kh, axis=-1, keepdims=True) + EPS)
        km_ref[0, hd] = (kh * r * gk_ref[...]).astype(BF16)
        vm_ref[0, hd] = kv[:, M_WIDTH + hd * M_HEAD_DIM:M_WIDTH + (hd + 1) * M_HEAD_DIM].astype(BF16)


def _memkv(mem, gain, w_kv, gk):
    bsz, ml, _ = mem.shape
    out = pl.BlockSpec((1, M_HEADS, ml, M_HEAD_DIM), lambda b: (b, 0, 0, 0))
    shp = jax.ShapeDtypeStruct((bsz, M_HEADS, ml, M_HEAD_DIM), BF16)
    return pl.pallas_call(
        _memkv_kernel,
        grid=(bsz,),
        in_specs=[pl.BlockSpec((1, ml, D_MODEL), lambda b: (b, 0, 0)),
                  pl.BlockSpec((1, D_MODEL), lambda b: (0, 0)),
                  pl.BlockSpec((D_MODEL, 2 * M_WIDTH), lambda b: (0, 0)),
                  pl.BlockSpec((1, M_HEAD_DIM), lambda b: (0, 0))],
        out_specs=[out, out],
        out_shape=[shp, shp],
        compiler_params=_cparams(("parallel",)),
        name="memkv",
    )(mem, gain, w_kv, gk)


def _memattn_kernel(q_ref, z_ref, km_ref, vm_ref, gq_ref, o_ref):
    qscale = M_HEAD_DIM ** -0.5 * LOG2E
    for hd in range(M_HEADS):
        cols = slice(hd * M_HEAD_DIM, (hd + 1) * M_HEAD_DIM)
        q = q_ref[:, cols].astype(F32)
        r = lax.rsqrt(jnp.mean(q * q, axis=-1, keepdims=True) + EPS)
        qn = (q * r * gq_ref[...] * qscale).astype(BF16)
        lg = _dot_nt(qn, km_ref[0, hd])
        p = jnp.exp2(lg - jnp.max(lg, axis=-1, keepdims=True))
        l = jnp.sum(p, axis=-1, keepdims=True)
        o = jnp.dot(p.astype(BF16), vm_ref[0, hd], preferred_element_type=F32) / l
        o_ref[:, cols] = (o * jax.nn.silu(z_ref[:, cols].astype(F32))).astype(BF16)


def _memattn(proj, km, vm, gq, s):
    m = proj.shape[0]
    tm = min(512, s)
    per_b = s // tm
    ml = km.shape[2]
    kv_spec = pl.BlockSpec((1, M_HEADS, ml, M_HEAD_DIM), lambda i: (i // per_b, 0, 0, 0))
    return pl.pallas_call(
        _memattn_kernel,
        grid=(m // tm,),
        in_specs=[pl.BlockSpec((tm, BLK), lambda i: (i, COL_MQ)),
                  pl.BlockSpec((tm, BLK), lambda i: (i, COL_MZ)),
                  kv_spec, kv_spec,
                  pl.BlockSpec((1, M_HEAD_DIM), lambda i: (0, 0))],
        out_specs=pl.BlockSpec((tm, M_WIDTH), lambda i: (i, 0)),
        out_shape=jax.ShapeDtypeStruct((m, M_WIDTH), BF16),
        compiler_params=_cparams(("parallel",)),
        name="memattn",
    )(proj, proj, km, vm, gq)


def _dsa_kernel(topk, q_ref, z_ref, iq_ref, cb_ref, sb_ref, ci_ref, si_ref, iwt_ref,
                k_ref, vt_ref, iklo_ref, ikhi_ref, gq_ref, tri_ref, o_ref,
                sc_ref, bias_ref, lg_ref, qs_ref, iqs_ref, acc_ref):
    qb = pl.program_id(1)
    nck = (qb * TQ + TQ + CK - 1) // CK
    nbig = (qb * TQ + TQ + CB - 1) // CB
    t_lane = qb * TQ + lax.broadcasted_iota(jnp.int32, (1, TQ), 1)
    gw = B_GROUP * TQ

    def fold8(a, ways=1):
        return a.reshape(a.shape[0] // (8 * ways), 8 * ways, a.shape[1])

    def paired_loop(trips, body, init):
        carry = lax.fori_loop(0, trips // 2, lambda i, c: body(2 * i + 1, body(2 * i, c)), init)
        return lax.cond(trips % 2 == 1, lambda c: body(trips - 1, c), lambda c: c, carry)

    cos_b, sin_b = cb_ref[0], sb_ref[0]
    cos_i, sin_i = ci_ref[0], si_ref[0]
    qscale = B_HEAD_DIM ** -0.5 * LOG2E
    ones_mat = jnp.ones((B_HEAD_DIM, B_HEAD_DIM), BF16)
    for h in range(B_HEADS):
        slab = q_ref[:, h * B_HEAD_DIM:(h + 1) * B_HEAD_DIM].astype(F32)
        sq = slab * slab
        sq_hi = sq.astype(BF16)
        sq_lo = (sq - sq_hi.astype(F32)).astype(BF16)
        ssq = (jnp.dot(sq_hi, ones_mat, preferred_element_type=F32)
               + jnp.dot(sq_lo, ones_mat, preferred_element_type=F32))
        r = lax.rsqrt(ssq * (1.0 / B_HEAD_DIM) + EPS)
        qr = _rope(slab * r * gq_ref[...], cos_b, sin_b, B_HEAD_DIM // 8, LANES) * qscale
        g = h % B_GROUP
        qs_ref[h // B_GROUP, g * TQ:(g + 1) * TQ, :] = qr.astype(BF16)
    for j in range(IDX_HEADS // 2):
        slab = iq_ref[:, j * LANES:(j + 1) * LANES].astype(F32)
        iqs_ref[j // 2, (j % 2) * TQ:(j % 2 + 1) * TQ, :] = (
            _rope(slab, cos_i, sin_i, IDX_DIM // 8, IDX_DIM).astype(BF16))
    wt = iwt_ref[0]

    def idx_body(c, carry):
        mn8, mx8, s1, s2 = carry
        off = pl.multiple_of(c * CK, CK)
        acc = jnp.zeros((CK, TQ), F32)
        for jj in range(IDX_HEADS // 4):
            rhs = iqs_ref[jj]
            for half, keys_ref in enumerate((iklo_ref, ikhi_ref)):
                d = _dot_nt(keys_ref[0, pl.ds(off, CK), :], rhs)
                ha, hb = 4 * jj + half, 4 * jj + 2 + half
                acc = (acc + jnp.maximum(d[:, :TQ], 0.0) * wt[ha:ha + 1, :]
                       + jnp.maximum(d[:, TQ:], 0.0) * wt[hb:hb + 1, :])
        key = off + lax.broadcasted_iota(jnp.int32, (CK, TQ), 0)
        causal = key <= t_lane
        sc = jnp.where(causal, acc, -jnp.inf)
        sc_ref[pl.ds(off, CK), :] = sc
        live = jnp.where(causal, acc, 0.0)
        mn8 = jnp.minimum(mn8, jnp.min(fold8(jnp.where(causal, acc, jnp.inf)), axis=0))
        mx8 = jnp.maximum(mx8, jnp.max(fold8(sc), axis=0))
        s1 = s1 + jnp.sum(fold8(live), axis=0)
        s2 = s2 + jnp.sum(fold8(live * live), axis=0)
        return mn8, mx8, s1, s2

    zero8 = jnp.zeros((8, TQ), F32)
    stats = (jnp.full((8, TQ), jnp.inf, F32), jnp.full((8, TQ), -jnp.inf, F32), zero8, zero8)
    mn8, mx8, s1, s2 = paired_loop(nck, idx_body, stats)
    row_min = jnp.min(mn8, axis=0, keepdims=True)
    row_max = jnp.max(mx8, axis=0, keepdims=True)

    def fill_body(c, carry):
        sc_ref[pl.ds(pl.multiple_of(c * CK, CK), CK), :] = jnp.full((CK, TQ), -jnp.inf, F32)
        return carry

    lax.fori_loop(nck, nbig * (CB // CK), fill_body, 0)

    def count_rows(pred):
        rows = 8 * COUNT_WAYS

        def body(c, cnt):
            off = pl.multiple_of(c * CB, CB)
            for r in range(CB // rows):
                cnt = cnt + jnp.where(pred(sc_ref[pl.ds(off + r * rows, rows), :]), 1.0, 0.0)
            return cnt

        cnt = lax.fori_loop(0, nbig, body, jnp.zeros((rows, TQ), F32))
        return jnp.sum(cnt, axis=0, keepdims=True)

    kf = float(topk)
    n_valid = (t_lane + 1).astype(F32)
    all_rows = (t_lane + 1) <= topk

    def search_pass(st, probe, stuck):
        lo, hi, clo, chi, thr, fin, tie = st
        cnt = count_rows(lambda blk: blk >= probe)
        active = fin < 0.5
        hit = cnt == kf
        end_thr = jnp.where(jnp.logical_and(stuck, cnt < kf), lo, probe)
        ends = jnp.logical_and(active, jnp.logical_or(hit, stuck))
        thr = jnp.where(ends, end_thr, thr)
        tie = jnp.where(jnp.logical_and(ends, jnp.logical_not(hit)), 1.0, tie)
        fin = jnp.where(ends, 1.0, fin)
        up = jnp.logical_and(active, cnt > kf)
        dn = jnp.logical_and(active, cnt < kf)
        return (jnp.where(up, probe, lo), jnp.where(dn, probe, hi), jnp.where(up, cnt, clo),
                jnp.where(dn, cnt, chi), thr, fin, tie)

    def next_probe(st, halve=False):
        lo, hi, clo, chi = st[:4]
        frac = jnp.clip((clo - kf + 0.5) / (clo - chi + 1.0), SEARCH_MARGIN, 1.0 - SEARCH_MARGIN)
        probe = lo + (hi - lo) * frac
        outside = jnp.logical_or(jnp.logical_or(probe <= lo, probe >= hi), halve)
        probe = jnp.where(outside, lo * 0.5 + hi * 0.5, probe)
        stuck = jnp.logical_or(probe <= lo, probe >= hi)
        return jnp.where(stuck, hi, probe), stuck

    mean = jnp.sum(s1, axis=0, keepdims=True) / n_valid
    var = jnp.maximum(jnp.sum(s2, axis=0, keepdims=True) / n_valid - mean * mean, 0.0)
    tail = jnp.clip(kf / n_valid, 1e-6, 1.0 - 1e-6)
    tq = jnp.sqrt(-2.0 * jnp.log(jnp.minimum(tail, 1.0 - tail)))
    zq = tq - ((0.010328 * tq + 0.802853) * tq + 2.515517) / (
        ((0.001308 * tq + 0.189269) * tq + 1.432788) * tq + 1.0)
    probe0 = jnp.clip(mean + jnp.where(tail < 0.5, zq, -zq) * jnp.sqrt(var), row_min, row_max)

    ones = jnp.ones((1, TQ), F32)
    st = (row_min, row_max, n_valid, 0.0 * ones, jnp.where(all_rows, -F32_MAX, row_max),
          jnp.where(all_rows, 1.0, 0.0), 0.0 * ones)
    st = search_pass(st, probe0, probe0 < row_min)

    def fixed_body(i, st):
        return search_pass(st, *next_probe(st))

    st = lax.fori_loop(0, SEARCH_FIXED_PASSES, fixed_body, st)

    def more_cond(c):
        return jnp.logical_and(c[0] < MAX_SEARCH_ITERS, jnp.min(c[1][5]) < 0.5)

    def more_body(c):
        return c[0] + 1, search_pass(c[1], *next_probe(c[1], c[0] >= SEARCH_INTERP_PASSES))

    _, st = lax.while_loop(more_cond, more_body, (jnp.int32(0), st))
    thr, tie = st[4], st[6]

    def plain_mask():
        def body(c, carry):
            off = pl.multiple_of(c * CB, CB)
            bias_ref[pl.ds(off, CB), :] = jnp.where(sc_ref[pl.ds(off, CB), :] >= thr, 0.0, NEG_BIG)
            return carry

        lax.fori_loop(0, nbig, body, 0)

    def tied_mask():
        n_gt = count_rows(lambda blk: blk > thr)
        need = jnp.where(all_rows, F32_MAX, kf - n_gt)

        def body(c, seen):
            off = pl.multiple_of(c * CB, CB)
            blk = sc_ref[pl.ds(off, CB), :]
            eq = blk == thr
            pref = jnp.dot(tri_ref[...], jnp.where(eq, 1.0, 0.0).astype(BF16),
                           preferred_element_type=F32) + seen
            keep = jnp.logical_or(blk > thr, jnp.logical_and(eq, pref <= need))
            bias_ref[pl.ds(off, CB), :] = jnp.where(keep, 0.0, NEG_BIG)
            return pref[CB - 1:CB, :]

        lax.fori_loop(0, nbig, body, jnp.zeros((1, TQ), F32))

    lax.cond(jnp.max(tie) > 0.5, tied_mask, plain_mask)

    for n in range(B_KV_HEADS):
        def logit_body(c, mx8, n=n):
            off = pl.multiple_of(c * CB, CB)
            b = bias_ref[pl.ds(off, CB), :]
            lg = _dot_nt(k_ref[0, n, pl.ds(off, CB), :], qs_ref[n])
            lg = jnp.concatenate([lg[:, g * TQ:(g + 1) * TQ] + b for g in range(B_GROUP)], axis=1)
            lg_ref[pl.ds(off, CB), :] = lg
            return jnp.maximum(mx8, jnp.max(fold8(lg), axis=0))

        mx8 = paired_loop(nbig, logit_body, jnp.full((8, gw), NEG_BIG, F32))
        m = jnp.max(mx8, axis=0, keepdims=True)
        acc_ref[...] = jnp.zeros(acc_ref.shape, F32)

        def pv_body(c, l8, n=n, m=m):
            off = pl.multiple_of(c * CB, CB)
            p = jnp.exp2(lg_ref[pl.ds(off, CB), :] - m)
            acc_ref[...] += jnp.dot(vt_ref[0, n, c], p.astype(BF16), preferred_element_type=F32)
            return l8 + jnp.sum(fold8(p), axis=0)

        l8 = paired_loop(nbig, pv_body, jnp.zeros((8, gw), F32))
        o_t = acc_ref[...] / jnp.sum(l8, axis=0, keepdims=True)
        for g in range(B_GROUP):
            cols = slice((n * B_GROUP + g) * B_HEAD_DIM, (n * B_GROUP + g + 1) * B_HEAD_DIM)
            o = o_t[:, g * TQ:(g + 1) * TQ].T
            o_ref[:, cols] = (o * jax.nn.silu(z_ref[:, cols].astype(F32))).astype(BF16)


def _dsa(proj, tabs, iwt, k, vt, iklo, ikhi, gq, bsz, s):
    nq = s // TQ
    topk = min(TOPK_MAX, s // 4)
    gw = B_GROUP * TQ
    rowblk = lambda c: pl.BlockSpec((TQ, BLK), lambda b, i: (b * nq + i, c))
    tok = pl.BlockSpec((1, TQ, LANES), lambda b, i: (b, i, 0))
    ik = pl.BlockSpec((1, s, LANES), lambda b, i: (b, 0, 0))
    tri = jnp.tril(jnp.ones((CB, CB), BF16))
    return pl.pallas_call(
        functools.partial(_dsa_kernel, topk),
        grid=(bsz, nq),
        in_specs=[rowblk(COL_BQ), rowblk(COL_BZ), rowblk(COL_IQ), tok, tok, tok, tok,
                  pl.BlockSpec((1, IDX_HEADS, TQ), lambda b, i: (b, 0, i)),
                  pl.BlockSpec((1, B_KV_HEADS, s, B_HEAD_DIM), lambda b, i: (b, 0, 0, 0)),
                  pl.BlockSpec((1, B_KV_HEADS, s // CB, B_HEAD_DIM, CB), lambda b, i: (b, 0, 0, 0, 0)),
                  ik, ik,
                  pl.BlockSpec((1, B_HEAD_DIM), lambda b, i: (0, 0)),
                  pl.BlockSpec((CB, CB), lambda b, i: (0, 0))],
        out_specs=pl.BlockSpec((TQ, B_WIDTH), lambda b, i: (b * nq + i, 0)),
        out_shape=jax.ShapeDtypeStruct((bsz * s, B_WIDTH), BF16),
        scratch_shapes=[
            pltpu.VMEM((s, TQ), F32),
            pltpu.VMEM((s, TQ), F32),
            pltpu.VMEM((s, gw), F32),
            pltpu.VMEM((B_KV_HEADS, gw, B_HEAD_DIM), BF16),
            pltpu.VMEM((IDX_HEADS // 4, 2 * TQ, LANES), BF16),
            pltpu.VMEM((B_HEAD_DIM, gw), F32),
        ],
        compiler_params=_cparams(("parallel", "arbitrary")),
        name="dsa",
    )(proj, proj, proj, *tabs, iwt, k, vt, iklo, ikhi, gq, tri)


def _merge_kernel(ta_ref, tb_ref, tm_ref, ga_ref, gb_ref, gm_ref, x_ref,
                  wa_ref, wb_ref, wm_ref, wo_ref, o_ref):
    def branch(t_ref, g_ref, w_ref):
        y = jnp.dot(t_ref[...], w_ref[...], preferred_element_type=F32)
        return jax.nn.sigmoid(g_ref[...].astype(F32)) * y

    merged = branch(ta_ref, ga_ref, wa_ref) + branch(tb_ref, gb_ref, wb_ref) + branch(tm_ref, gm_ref, wm_ref)
    o_ref[...] = x_ref[...] + jnp.dot(merged.astype(BF16), wo_ref[...], preferred_element_type=F32)


def _merge(ta, tb, tmem, proj, x2, wa, wb, wm, wo):
    m = x2.shape[0]
    tm = min(256, m)
    act = pl.BlockSpec((tm, BLK), lambda i: (i, 0))
    gate = lambda c: pl.BlockSpec((tm, D_MODEL), lambda i: (i, c // 2))
    wide = pl.BlockSpec((tm, D_MODEL), lambda i: (i, 0))
    wbr = pl.BlockSpec((BLK, D_MODEL), lambda i: (0, 0), pipeline_mode=pl.Buffered(1))
    wout = pl.BlockSpec((D_MODEL, D_MODEL), lambda i: (0, 0), pipeline_mode=pl.Buffered(1))
    return pl.pallas_call(
        _merge_kernel,
        grid=(m // tm,),
        in_specs=[act, act, act, gate(COL_GA), gate(COL_GB), gate(COL_GM), wide, wbr, wbr, wbr, wout],
        out_specs=wide,
        out_shape=jax.ShapeDtypeStruct((m, D_MODEL), F32),
        compiler_params=_cparams(("parallel",)),
        name="merge",
    )(ta, tb, tmem, proj, proj, proj, x2, wa, wb, wm, wo)


def _rope_freqs():
    inv = lambda half: ROPE_THETA ** (-np.arange(half, dtype=np.float32) / half)
    freq = np.zeros((1, LANES), np.float32)
    hb, hi = B_HEAD_DIM // 8, IDX_DIM // 8
    freq[0, :hb] = inv(hb)
    freq[0, hb:hb + hi] = inv(hi)
    return jnp.asarray(freq)


def _layer(x, mem, positions, norm_gain, w_in, gmlp_ln_gain, gmlp_ln_bias, spatial_w, spatial_b,
           w_branch_a, q_norm_gain, k_norm_gain, idx_k_ln_gain, idx_k_ln_bias, w_branch_b,
           mem_norm_gain, w_mem_kv, mem_q_norm_gain, mem_k_norm_gain, w_branch_m, w_out):
    bsz, s, _ = x.shape
    m = bsz * s
    row = lambda a: a.reshape(1, -1).astype(F32)
    pad_lanes = lambda a: jnp.pad(a.reshape(1, -1).astype(F32), ((0, 0), (0, LANES - a.shape[-1])))
    x2 = x.reshape(m, D_MODEL)

    w_t = w_in.T
    h, k, vt, iklo, ikhi, cb, sb, ci, si, iwt = _kvprep(
        x, row(norm_gain), w_t, positions.reshape(bsz, s, 1).astype(jnp.int32), _rope_freqs(),
        row(k_norm_gain), pad_lanes(idx_k_ln_gain), pad_lanes(idx_k_ln_bias))
    proj = _inproj(h, w_t)

    t_a = _gmlp(proj, row(gmlp_ln_gain), row(gmlp_ln_bias), spatial_w.astype(F32), spatial_b.T.astype(F32))
    km, vm = _memkv(mem, row(mem_norm_gain), w_mem_kv.astype(BF16), row(mem_k_norm_gain))
    t_m = _memattn(proj, km, vm, row(mem_q_norm_gain), s)
    t_b = _dsa(proj, (cb, sb, ci, si), iwt, k, vt, iklo, ikhi, row(q_norm_gain), bsz, s)

    out = _merge(t_a, t_b, t_m, proj, x2, w_branch_a.astype(BF16), w_branch_b.astype(BF16),
                 w_branch_m.astype(BF16), w_out.astype(BF16))
    return out.reshape(bsz, s, D_MODEL)


def kernel(x, mem, positions, norm_gain, w_in, gmlp_ln_gain, gmlp_ln_bias, spatial_w, spatial_b, w_branch_a, q_norm_gain, k_norm_gain, idx_k_ln_gain, idx_k_ln_bias, w_branch_b, mem_norm_gain, w_mem_kv, mem_q_norm_gain, mem_k_norm_gain, w_branch_m, w_out):
    for l in range(norm_gain.shape[0]):
        x = _layer(x, mem, positions, norm_gain[l], w_in[l], gmlp_ln_gain[l], gmlp_ln_bias[l],
                   spatial_w[l], spatial_b[l], w_branch_a[l], q_norm_gain[l], k_norm_gain[l],
                   idx_k_ln_gain[l], idx_k_ln_bias[l], w_branch_b[l], mem_norm_gain[l], w_mem_kv[l],
                   mem_q_norm_gain[l], mem_k_norm_gain[l], w_branch_m[l], w_out[l])
    return x
```

```python
import functools

import numpy as np
import jax
import jax.numpy as jnp
from jax import lax
from jax.experimental import pallas as pl
from jax.experimental.pallas import tpu as pltpu

F32 = jnp.float32
BF16 = jnp.bfloat16

D_MODEL = 2048
ROPE_THETA = 500000.0
EPS = 1e-6
A_GROUPS = 8
A_GROUP_DIM = 128
A_WIDTH = A_GROUPS * A_GROUP_DIM
CHUNK = 128
B_HEADS = 8
B_KV_HEADS = 2
B_GROUP = B_HEADS // B_KV_HEADS
B_HEAD_DIM = 128
B_WIDTH = B_HEADS * B_HEAD_DIM
IDX_HEADS = 16
IDX_DIM = 64
TOPK_MAX = 256
M_HEADS = 4
M_HEAD_DIM = 256
M_WIDTH = M_HEADS * M_HEAD_DIM

SPLIT_SIZES = (
    A_WIDTH, A_WIDTH, A_WIDTH,
    B_WIDTH, B_KV_HEADS * B_HEAD_DIM, B_KV_HEADS * B_HEAD_DIM, B_WIDTH,
    IDX_HEADS * IDX_DIM, IDX_DIM, IDX_HEADS,
    M_WIDTH, M_WIDTH,
    D_MODEL, D_MODEL, D_MODEL,
)

LANES = 128
BLK = 1024
COL_AU, COL_AV, COL_AZ, COL_BQ, COL_BZ, COL_IQ, COL_MQ, COL_MZ = range(8)
COL_GA, COL_GB, COL_GM = 8, 10, 12
NB_MAIN = 14
_OFFS = [int(o) for o in np.concatenate([[0], np.cumsum(SPLIT_SIZES)])]
ROW_ALIGN = 16
MAIN_START = ([_OFFS[i] for i in (0, 1, 2, 3, 6, 7, 10, 11)]
              + [_OFFS[i] + d for i in (12, 13, 14) for d in (0, BLK)])
assert all(s % ROW_ALIGN == 0 for s in MAIN_START)
SMALL_KV_START, SMALL_KV_WIDTH = _OFFS[4], _OFFS[6] - _OFFS[4]
SMALL_IDX_START = _OFFS[8]
assert SMALL_KV_START % SMALL_KV_WIDTH == 0 and SMALL_IDX_START % LANES == 0
assert IDX_DIM + IDX_HEADS <= LANES and _OFFS[9] == SMALL_IDX_START + IDX_DIM

VMEM_LIMIT = 56 * 1024 * 1024
LOG2E = 1.4426950408889634
NEG_BIG = -1e30
F32_MAX = 3.4028234663852886e38

INPROJ_TM, INPROJ_TN = 2048, 1024
TQ = 256
CK = 256
CB = 512
COUNT_WAYS = 8
SEARCH_MARGIN = 0.05
SEARCH_FIXED_PASSES = 11
SEARCH_INTERP_PASSES = 13
MAX_SEARCH_ITERS = 400
MAX_SHIFT_BOUND = 48.0


def _cparams(sem):
    return pltpu.CompilerParams(dimension_semantics=sem, vmem_limit_bytes=VMEM_LIMIT)


def _dot_nt(a, b):
    return lax.dot_general(a, b, (((1,), (1,)), ((), ())), preferred_element_type=F32)


def _rope(x, cos_t, sin_t, half, period):
    lane = lax.broadcasted_iota(jnp.int32, x.shape, 1) % period
    rolled = jnp.where(lane < half, pltpu.roll(x, LANES - half, 1), pltpu.roll(x, half, 1))
    return x * cos_t + rolled * sin_t


def _rms_rows(x, gain):
    ms = jnp.mean(x * x, axis=-1, keepdims=True)
    return (x * lax.rsqrt(ms + EPS) * gain).astype(BF16)


def _inproj_kernel(tab_ref, h_ref, wt_ref, proj_ref):
    del tab_ref
    proj_ref[...] = _dot_nt(h_ref[...], wt_ref[...].astype(BF16)).astype(BF16)


def _inproj(h, w_t):
    m = h.shape[0]
    tm = min(INPROJ_TM, m)
    starts = [s + d for s in MAIN_START for d in range(0, BLK, INPROJ_TN)]
    tab = jnp.asarray(np.array([s // ROW_ALIGN for s in starts], np.int32))
    return pl.pallas_call(
        _inproj_kernel,
        grid_spec=pltpu.PrefetchScalarGridSpec(
            num_scalar_prefetch=1,
            grid=(m // tm, len(starts)),
            in_specs=[
                pl.BlockSpec((tm, D_MODEL), lambda i, n, tab: (i, 0)),
                pl.BlockSpec((pl.Element(INPROJ_TN), pl.Element(D_MODEL)),
                             lambda i, n, tab: (tab[n] * ROW_ALIGN, 0)),
            ],
            out_specs=pl.BlockSpec((tm, INPROJ_TN), lambda i, n, tab: (i, n)),
        ),
        out_shape=jax.ShapeDtypeStruct((m, NB_MAIN * BLK), BF16),
        compiler_params=_cparams(("parallel", "arbitrary")),
        name="inproj",
    )(tab, h, w_t)


def _kvprep_kernel(x_ref, g_ref, wkv_ref, widx_ref, pos_ref, fc_ref, gk_ref, lng_ref, lnb_ref,
                   h_ref, k_ref, vt_ref, iklo_ref, ikhi_ref, cb_ref, sb_ref, ci_ref, si_ref, iwt_ref,
                   wkv_bf_ref, widx_bf_ref):
    @pl.when(jnp.logical_and(pl.program_id(0) == 0, pl.program_id(1) == 0))
    def _():
        wkv_bf_ref[...] = wkv_ref[...].astype(BF16)
        widx_bf_ref[...] = widx_ref[...].astype(BF16)

    h = _rms_rows(x_ref[...], g_ref[...])
    h_ref[...] = h
    kv = _dot_nt(h, wkv_bf_ref[...])
    ikp = _dot_nt(h, widx_bf_ref[...])

    hb, hi = B_HEAD_DIM // 8, IDX_DIM // 8
    ang = pos_ref[0].astype(F32) * fc_ref[...]
    cos_c, sin_c = jnp.cos(ang), jnp.sin(ang)
    lane = lax.broadcasted_iota(jnp.int32, ang.shape, 1)
    cos_b = jnp.where(lane < hb, cos_c, jnp.where(lane < 2 * hb, pltpu.roll(cos_c, hb, 1), 1.0))
    sin_b = jnp.where(lane < hb, -sin_c, jnp.where(lane < 2 * hb, pltpu.roll(sin_c, hb, 1), 0.0))
    cos_i = jnp.ones_like(cos_c)
    sin_i = jnp.zeros_like(sin_c)
    for head in range(LANES // IDX_DIM):
        for part, sign in enumerate((-1.0, 1.0)):
            first = head * IDX_DIM + part * hi
            here = jnp.logical_and(lane >= first, lane < first + hi)
            shift = (first - hb) % LANES
            cos_i = jnp.where(here, pltpu.roll(cos_c, shift, 1), cos_i)
            sin_i = jnp.where(here, sign * pltpu.roll(sin_c, shift, 1), sin_i)
    cb_ref[0] = cos_b
    sb_ref[0] = sin_b
    ci_ref[0] = cos_i
    si_ref[0] = sin_i

    for n in range(B_KV_HEADS):
        kh = kv[:, n * B_HEAD_DIM:(n + 1) * B_HEAD_DIM]
        r = lax.rsqrt(jnp.mean(kh * kh, axis=-1, keepdims=True) + EPS)
        kn = kh * r * gk_ref[...]
        k_ref[0, n] = _rope(kn, cos_b, sin_b, B_HEAD_DIM // 8, LANES).astype(BF16)
        vt = kv[:, (B_KV_HEADS + n) * B_HEAD_DIM:(B_KV_HEADS + n + 1) * B_HEAD_DIM].T
        for c in range(vt.shape[1] // CB):
            vt_ref[0, n, c] = vt[:, c * CB:(c + 1) * CB].astype(BF16)

    lane = lax.broadcasted_iota(jnp.int32, ikp.shape, 1)
    live = lane < IDX_DIM
    mu = jnp.sum(jnp.where(live, ikp, 0.0), axis=-1, keepdims=True) * (1.0 / IDX_DIM)
    d = jnp.where(live, ikp - mu, 0.0)
    var = jnp.sum(d * d, axis=-1, keepdims=True) * (1.0 / IDX_DIM)
    y = d * lax.rsqrt(var + EPS) * lng_ref[...] + lnb_ref[...]
    yr = jnp.where(live, _rope(y, cos_i, sin_i, IDX_DIM // 8, IDX_DIM), 0.0)
    iklo_ref[0] = yr.astype(BF16)
    ikhi_ref[0] = pltpu.roll(yr, IDX_DIM, 1).astype(BF16)

    iw = ikp * (IDX_DIM ** -0.5 * IDX_HEADS ** -0.5)
    iwt_ref[0] = iw.T[IDX_DIM:IDX_DIM + IDX_HEADS, :]


def _kvprep(x3, gain, w_t, pos3, freqs, gk, lng, lnb):
    bsz, s, _ = x3.shape
    tp = min(512, s)
    row = lambda b, i: (b, i, 0)
    const = lambda b, i: (0, 0)
    tab_spec = pl.BlockSpec((1, LANES), const)
    out_h = pl.BlockSpec((tp, D_MODEL), lambda b, i: (b * (s // tp) + i, 0))
    out_tok = pl.BlockSpec((1, tp, LANES), row)
    out_k = pl.BlockSpec((1, B_KV_HEADS, tp, B_HEAD_DIM), lambda b, i: (b, 0, i, 0))
    out_vt = pl.BlockSpec((1, B_KV_HEADS, tp // CB, B_HEAD_DIM, CB), lambda b, i: (b, 0, i, 0, 0))
    out_iwt = pl.BlockSpec((1, IDX_HEADS, tp), lambda b, i: (b, 0, i))
    return pl.pallas_call(
        _kvprep_kernel,
        grid=(bsz, s // tp),
        in_specs=[pl.BlockSpec((None, tp, D_MODEL), row), pl.BlockSpec((1, D_MODEL), const),
                  pl.BlockSpec((SMALL_KV_WIDTH, D_MODEL), lambda b, i: (SMALL_KV_START // SMALL_KV_WIDTH, 0)),
                  pl.BlockSpec((LANES, D_MODEL), lambda b, i: (SMALL_IDX_START // LANES, 0)),
                  pl.BlockSpec((1, tp, 1), row),
                  tab_spec, tab_spec, tab_spec, tab_spec],
        out_specs=[out_h, out_k, out_vt, out_tok, out_tok, out_tok, out_tok, out_tok, out_tok, out_iwt],
        out_shape=[
            jax.ShapeDtypeStruct((bsz * s, D_MODEL), BF16),
            jax.ShapeDtypeStruct((bsz, B_KV_HEADS, s, B_HEAD_DIM), BF16),
            jax.ShapeDtypeStruct((bsz, B_KV_HEADS, s // CB, B_HEAD_DIM, CB), BF16),
            jax.ShapeDtypeStruct((bsz, s, LANES), BF16),
            jax.ShapeDtypeStruct((bsz, s, LANES), BF16),
            jax.ShapeDtypeStruct((bsz, s, LANES), F32),
            jax.ShapeDtypeStruct((bsz, s, LANES), F32),
            jax.ShapeDtypeStruct((bsz, s, LANES), F32),
            jax.ShapeDtypeStruct((bsz, s, LANES), F32),
            jax.ShapeDtypeStruct((bsz, IDX_HEADS, s), F32),
        ],
        scratch_shapes=[pltpu.VMEM((SMALL_KV_WIDTH, D_MODEL), BF16), pltpu.VMEM((LANES, D_MODEL), BF16)],
        compiler_params=_cparams(("arbitrary", "arbitrary")),
        name="kvprep",
    )(x3, gain, w_t, w_t, pos3, freqs, gk, lng, lnb)


def _gelu_tanh(x):
    c = -2.0 * (2.0 / np.pi) ** 0.5 * LOG2E
    return x / (1.0 + jnp.exp2(x * (x * x * (0.044715 * c) + c)))


def _gmlp_kernel(u_ref, v_ref, z_ref, lng_ref, lnb_ref, ws_ref, sbt_ref, o_ref):
    tm = u_ref.shape[0]
    u = _gelu_tanh(u_ref[...].astype(F32))
    v = _gelu_tanh(v_ref[...].astype(F32))
    mu = jnp.mean(v, axis=-1, keepdims=True)
    d = v - mu
    var = jnp.mean(d * d, axis=-1, keepdims=True)
    vn = (d * lax.rsqrt(var + EPS) * lng_ref[...] + lnb_ref[...]).astype(BF16)
    gate = u * jax.nn.silu(z_ref[...].astype(F32))
    tri = (lax.broadcasted_iota(jnp.int32, (CHUNK, CHUNK), 1)
           <= lax.broadcasted_iota(jnp.int32, (CHUNK, CHUNK), 0))
    for g in range(A_GROUPS):
        wg = jnp.where(tri, ws_ref[g], 0.0).astype(BF16)
        bias = sbt_ref[:, g:g + 1]
        cols = slice(g * A_GROUP_DIM, (g + 1) * A_GROUP_DIM)
        for c in range(tm // CHUNK):
            rows = slice(c * CHUNK, (c + 1) * CHUNK)
            sg = jnp.dot(wg, vn[rows, cols], preferred_element_type=F32) + bias
            o_ref[rows, cols] = (gate[rows, cols] * sg).astype(BF16)


def _gmlp(proj, lng, lnb, ws, sbt):
    m = proj.shape[0]
    tm = min(512, m)
    col = lambda c: pl.BlockSpec((tm, BLK), lambda i: (i, c))
    full = lambda shape: pl.BlockSpec(shape, lambda i: (0,) * len(shape))
    return pl.pallas_call(
        _gmlp_kernel,
        grid=(m // tm,),
        in_specs=[col(COL_AU), col(COL_AV), col(COL_AZ), full((1, A_WIDTH)), full((1, A_WIDTH)),
                  full((A_GROUPS, CHUNK, CHUNK)), full((CHUNK, A_GROUPS))],
        out_specs=pl.BlockSpec((tm, A_WIDTH), lambda i: (i, 0)),
        out_shape=jax.ShapeDtypeStruct((m, A_WIDTH), BF16),
        compiler_params=_cparams(("parallel",)),
        name="gmlp",
    )(proj, proj, proj, lng, lnb, ws, sbt)


def _memkv_kernel(mem_ref, g_ref, w_ref, gk_ref, km_ref, vm_ref):
    x = mem_ref[0]
    ms = jnp.mean(x * x, axis=-1, keepdims=True)
    h = (x * lax.rsqrt(ms + EPS) * g_ref[...]).astype(BF16)
    kv = jnp.dot(h, w_ref[...], preferred_element_type=F32)
    for hd in range(M_HEADS):
        kh = kv[:, hd * M_HEAD_DIM:(hd + 1) * M_HEAD_DIM]
        r = lax.rsqrt(jnp.mean(kh * kh, axis=-1, keepdims=True) + EPS)
        km_ref[0, hd] = (kh * r * gk_ref[...]).astype(BF16)
        vm_ref[0, hd] = kv[:, M_WIDTH + hd * M_HEAD_DIM:M_WIDTH + (hd + 1) * M_HEAD_DIM].astype(BF16)


def _memkv(mem, gain, w_kv, gk):
    bsz, ml, _ = mem.shape
    out = pl.BlockSpec((1, M_HEADS, ml, M_HEAD_DIM), lambda b: (b, 0, 0, 0))
    shp = jax.ShapeDtypeStruct((bsz, M_HEADS, ml, M_HEAD_DIM), BF16)
    return pl.pallas_call(
        _memkv_kernel,
        grid=(bsz,),
        in_specs=[pl.BlockSpec((1, ml, D_MODEL), lambda b: (b, 0, 0)),
                  pl.BlockSpec((1, D_MODEL), lambda b: (0, 0)),
                  pl.BlockSpec((D_MODEL, 2 * M_WIDTH), lambda b: (0, 0)),
                  pl.BlockSpec((1, M_HEAD_DIM), lambda b: (0, 0))],
        out_specs=[out, out],
        out_shape=[shp, shp],
        compiler_params=_cparams(("parallel",)),
        name="memkv",
    )(mem, gain, w_kv, gk)


def _memattn_kernel(q_ref, z_ref, km_ref, vm_ref, gq_ref, o_ref):
    qscale = M_HEAD_DIM ** -0.5 * LOG2E
    for hd in range(M_HEADS):
        cols = slice(hd * M_HEAD_DIM, (hd + 1) * M_HEAD_DIM)
        q = q_ref[:, cols].astype(F32)
        r = lax.rsqrt(jnp.mean(q * q, axis=-1, keepdims=True) + EPS)
        qn = (q * r * gq_ref[...] * qscale).astype(BF16)
        lg = _dot_nt(qn, km_ref[0, hd])
        p = jnp.exp2(lg - jnp.max(lg, axis=-1, keepdims=True))
        l = jnp.sum(p, axis=-1, keepdims=True)
        o = jnp.dot(p.astype(BF16), vm_ref[0, hd], preferred_element_type=F32) / l
        o_ref[:, cols] = (o * jax.nn.silu(z_ref[:, cols].astype(F32))).astype(BF16)


def _memattn(proj, km, vm, gq, s):
    m = proj.shape[0]
    tm = min(512, s)
    per_b = s // tm
    ml = km.shape[2]
    kv_spec = pl.BlockSpec((1, M_HEADS, ml, M_HEAD_DIM), lambda i: (i // per_b, 0, 0, 0))
    return pl.pallas_call(
        _memattn_kernel,
        grid=(m // tm,),
        in_specs=[pl.BlockSpec((tm, BLK), lambda i: (i, COL_MQ)),
                  pl.BlockSpec((tm, BLK), lambda i: (i, COL_MZ)),
                  kv_spec, kv_spec,
                  pl.BlockSpec((1, M_HEAD_DIM), lambda i: (0, 0))],
        out_specs=pl.BlockSpec((tm, M_WIDTH), lambda i: (i, 0)),
        out_shape=jax.ShapeDtypeStruct((m, M_WIDTH), BF16),
        compiler_params=_cparams(("parallel",)),
        name="memattn",
    )(proj, proj, km, vm, gq)


def _dsa_kernel(topk, q_ref, z_ref, iq_ref, cb_ref, sb_ref, ci_ref, si_ref, iwt_ref,
                k_ref, vt_ref, iklo_ref, ikhi_ref, gq_ref, tri_ref, bound_ref, o_ref,
                sc_ref, bias_ref, lg_ref, qs_ref, iqs_ref, acc_ref):
    qb = pl.program_id(1)
    nck = (qb * TQ + TQ + CK - 1) // CK
    nbig = (qb * TQ + TQ + CB - 1) // CB
    t_lane = qb * TQ + lax.broadcasted_iota(jnp.int32, (1, TQ), 1)
    gw = B_GROUP * TQ

    def fold8(a, ways=1):
        return a.reshape(a.shape[0] // (8 * ways), 8 * ways, a.shape[1])

    def paired_loop(trips, body, init):
        carry = lax.fori_loop(0, trips // 2, lambda i, c: body(2 * i + 1, body(2 * i, c)), init)
        return lax.cond(trips % 2 == 1, lambda c: body(trips - 1, c), lambda c: c, carry)

    cos_b, sin_b = cb_ref[0], sb_ref[0]
    cos_i, sin_i = ci_ref[0], si_ref[0]
    qscale = B_HEAD_DIM ** -0.5 * LOG2E
    ones_mat = jnp.ones((B_HEAD_DIM, B_HEAD_DIM), BF16)
    for h in range(B_HEADS):
        slab = q_ref[:, h * B_HEAD_DIM:(h + 1) * B_HEAD_DIM].astype(F32)
        sq = slab * slab
        sq_hi = sq.astype(BF16)
        sq_lo = (sq - sq_hi.astype(F32)).astype(BF16)
        ssq = (jnp.dot(sq_hi, ones_mat, preferred_element_type=F32)
               + jnp.dot(sq_lo, ones_mat, preferred_element_type=F32))
        r = lax.rsqrt(ssq * (1.0 / B_HEAD_DIM) + EPS)
        qr = _rope(slab * r * gq_ref[...], cos_b, sin_b, B_HEAD_DIM // 8, LANES) * qscale
        g = h % B_GROUP
        qs_ref[h // B_GROUP, g * TQ:(g + 1) * TQ, :] = qr.astype(BF16)
    for j in range(IDX_HEADS // 2):
        slab = iq_ref[:, j * LANES:(j + 1) * LANES].astype(F32)
        iqs_ref[j // 2, (j % 2) * TQ:(j % 2 + 1) * TQ, :] = (
            _rope(slab, cos_i, sin_i, IDX_DIM // 8, IDX_DIM).astype(BF16))
    wt = iwt_ref[0]

    def idx_body(c, carry):
        mn8, mx8, s1, s2 = carry
        off = pl.multiple_of(c * CK, CK)
        acc = jnp.zeros((CK, TQ), F32)
        for jj in range(IDX_HEADS // 4):
            rhs = iqs_ref[jj]
            for half, keys_ref in enumerate((iklo_ref, ikhi_ref)):
                d = _dot_nt(keys_ref[0, pl.ds(off, CK), :], rhs)
                ha, hb = 4 * jj + half, 4 * jj + 2 + half
                acc = (acc + jnp.maximum(d[:, :TQ], 0.0) * wt[ha:ha + 1, :]
                       + jnp.maximum(d[:, TQ:], 0.0) * wt[hb:hb + 1, :])
        key = off + lax.broadcasted_iota(jnp.int32, (CK, TQ), 0)
        causal = key <= t_lane
        sc = jnp.where(causal, acc, -jnp.inf)
        sc_ref[pl.ds(off, CK), :] = sc
        live = jnp.where(causal, acc, 0.0)
        mn8 = jnp.minimum(mn8, jnp.min(fold8(jnp.where(causal, acc, jnp.inf)), axis=0))
        mx8 = jnp.maximum(mx8, jnp.max(fold8(sc), axis=0))
        s1 = s1 + jnp.sum(fold8(live), axis=0)
        s2 = s2 + jnp.sum(fold8(live * live), axis=0)
        return mn8, mx8, s1, s2

    zero8 = jnp.zeros((8, TQ), F32)
    stats = (jnp.full((8, TQ), jnp.inf, F32), jnp.full((8, TQ), -jnp.inf, F32), zero8, zero8)
    mn8, mx8, s1, s2 = paired_loop(nck, idx_body, stats)
    row_min = jnp.min(mn8, axis=0, keepdims=True)
    row_max = jnp.max(mx8, axis=0, keepdims=True)

    def fill_body(c, carry):
        sc_ref[pl.ds(pl.multiple_of(c * CK, CK), CK), :] = jnp.full((CK, TQ), -jnp.inf, F32)
        return carry

    lax.fori_loop(nck, nbig * (CB // CK), fill_body, 0)

    def count_rows(pred):
        rows = 8 * COUNT_WAYS

        def body(c, cnt):
            off = pl.multiple_of(c * CB, CB)
            for r in range(CB // rows):
                cnt = cnt + jnp.where(pred(sc_ref[pl.ds(off + r * rows, rows), :]), 1.0, 0.0)
            return cnt

        cnt = lax.fori_loop(0, nbig, body, jnp.zeros((rows, TQ), F32))
        return jnp.sum(cnt, axis=0, keepdims=True)

    kf = float(topk)
    n_valid = (t_lane + 1).astype(F32)
    all_rows = (t_lane + 1) <= topk

    def search_pass(st, probe, stuck):
        lo, hi, clo, chi, thr, fin, tie = st
        cnt = count_rows(lambda blk: blk >= probe)
        active = fin < 0.5
        hit = cnt == kf
        end_thr = jnp.where(jnp.logical_and(stuck, cnt < kf), lo, probe)
        ends = jnp.logical_and(active, jnp.logical_or(hit, stuck))
        thr = jnp.where(ends, end_thr, thr)
        tie = jnp.where(jnp.logical_and(ends, jnp.logical_not(hit)), 1.0, tie)
        fin = jnp.where(ends, 1.0, fin)
        up = jnp.logical_and(active, cnt > kf)
        dn = jnp.logical_and(active, cnt < kf)
        return (jnp.where(up, probe, lo), jnp.where(dn, probe, hi), jnp.where(up, cnt, clo),
                jnp.where(dn, cnt, chi), thr, fin, tie)

    def next_probe(st, halve=False):
        lo, hi, clo, chi = st[:4]
        frac = jnp.clip((clo - kf + 0.5) / (clo - chi + 1.0), SEARCH_MARGIN, 1.0 - SEARCH_MARGIN)
        probe = lo + (hi - lo) * frac
        outside = jnp.logical_or(jnp.logical_or(probe <= lo, probe >= hi), halve)
        probe = jnp.where(outside, lo * 0.5 + hi * 0.5, probe)
        stuck = jnp.logical_or(probe <= lo, probe >= hi)
        return jnp.where(stuck, hi, probe), stuck

    mean = jnp.sum(s1, axis=0, keepdims=True) / n_valid
    var = jnp.maximum(jnp.sum(s2, axis=0, keepdims=True) / n_valid - mean * mean, 0.0)
    tail = jnp.clip(kf / n_valid, 1e-6, 1.0 - 1e-6)
    tq = jnp.sqrt(-2.0 * jnp.log(jnp.minimum(tail, 1.0 - tail)))
    zq = tq - ((0.010328 * tq + 0.802853) * tq + 2.515517) / (
        ((0.001308 * tq + 0.189269) * tq + 1.432788) * tq + 1.0)
    probe0 = jnp.clip(mean + jnp.where(tail < 0.5, zq, -zq) * jnp.sqrt(var), row_min, row_max)

    ones = jnp.ones((1, TQ), F32)
    st = (row_min, row_max, n_valid, 0.0 * ones, jnp.where(all_rows, -F32_MAX, row_max),
          jnp.where(all_rows, 1.0, 0.0), 0.0 * ones)
    st = search_pass(st, probe0, probe0 < row_min)

    def fixed_body(i, st):
        return search_pass(st, *next_probe(st))

    st = lax.fori_loop(0, SEARCH_FIXED_PASSES, fixed_body, st)

    def more_cond(c):
        return jnp.logical_and(c[0] < MAX_SEARCH_ITERS, jnp.min(c[1][5]) < 0.5)

    def more_body(c):
        return c[0] + 1, search_pass(c[1], *next_probe(c[1], c[0] >= SEARCH_INTERP_PASSES))

    _, st = lax.while_loop(more_cond, more_body, (jnp.int32(0), st))
    thr, tie = st[4], st[6]

    def plain_mask():
        def body(c, carry):
            off = pl.multiple_of(c * CB, CB)
            bias_ref[pl.ds(off, CB), :] = jnp.where(sc_ref[pl.ds(off, CB), :] >= thr, 0.0, NEG_BIG)
            return carry

        lax.fori_loop(0, nbig, body, 0)

    def tied_mask():
        n_gt = count_rows(lambda blk: blk > thr)
        need = jnp.where(all_rows, F32_MAX, kf - n_gt)

        def body(c, seen):
            off = pl.multiple_of(c * CB, CB)
            blk = sc_ref[pl.ds(off, CB), :]
            eq = blk == thr
            pref = jnp.dot(tri_ref[...], jnp.where(eq, 1.0, 0.0).astype(BF16),
                           preferred_element_type=F32) + seen
            keep = jnp.logical_or(blk > thr, jnp.logical_and(eq, pref <= need))
            bias_ref[pl.ds(off, CB), :] = jnp.where(keep, 0.0, NEG_BIG)
            return pref[CB - 1:CB, :]

        lax.fori_loop(0, nbig, body, jnp.zeros((1, TQ), F32))

    lax.cond(jnp.max(tie) > 0.5, tied_mask, plain_mask)

    logit_bound = bound_ref[0, 0]

    def masked_logits(n, off):
        b = bias_ref[pl.ds(off, CB), :]
        lg = _dot_nt(k_ref[0, n, pl.ds(off, CB), :], qs_ref[n])
        return jnp.concatenate([lg[:, g * TQ:(g + 1) * TQ] + b for g in range(B_GROUP)], axis=1)

    for n in range(B_KV_HEADS):
        acc_ref[...] = jnp.zeros(acc_ref.shape, F32)

        def one_pass(n=n):
            def body(c, l8):
                off = pl.multiple_of(c * CB, CB)
                p = jnp.exp2(masked_logits(n, off) - logit_bound)
                acc_ref[...] += jnp.dot(vt_ref[0, n, c], p.astype(BF16), preferred_element_type=F32)
                return l8 + jnp.sum(fold8(p), axis=0)

            return paired_loop(nbig, body, jnp.zeros((8, gw), F32))

        def two_pass(n=n):
            def logit_body(c, mx8):
                off = pl.multiple_of(c * CB, CB)
                lg = masked_logits(n, off)
                lg_ref[pl.ds(off, CB), :] = lg
                return jnp.maximum(mx8, jnp.max(fold8(lg), axis=0))

            mx8 = paired_loop(nbig, logit_body, jnp.full((8, gw), NEG_BIG, F32))
            m = jnp.max(mx8, axis=0, keepdims=True)

            def pv_body(c, l8):
                off = pl.multiple_of(c * CB, CB)
                p = jnp.exp2(lg_ref[pl.ds(off, CB), :] - m)
                acc_ref[...] += jnp.dot(vt_ref[0, n, c], p.astype(BF16), preferred_element_type=F32)
                return l8 + jnp.sum(fold8(p), axis=0)

            return paired_loop(nbig, pv_body, jnp.zeros((8, gw), F32))

        l8 = lax.cond(logit_bound <= MAX_SHIFT_BOUND, one_pass, two_pass)
        o_t = acc_ref[...] / jnp.sum(l8, axis=0, keepdims=True)
        for g in range(B_GROUP):
            cols = slice((n * B_GROUP + g) * B_HEAD_DIM, (n * B_GROUP + g + 1) * B_HEAD_DIM)
            o = o_t[:, g * TQ:(g + 1) * TQ].T
            o_ref[:, cols] = (o * jax.nn.silu(z_ref[:, cols].astype(F32))).astype(BF16)


def _dsa(proj, tabs, iwt, k, vt, iklo, ikhi, gq, gk, bsz, s):
    nq = s // TQ
    bound = (1.02 * B_HEAD_DIM ** 0.5 * LOG2E * jnp.max(jnp.abs(gq)) * jnp.max(jnp.abs(gk))).reshape(1, 1)
    topk = min(TOPK_MAX, s // 4)
    gw = B_GROUP * TQ
    rowblk = lambda c: pl.BlockSpec((TQ, BLK), lambda b, i: (b * nq + i, c))
    tok = pl.BlockSpec((1, TQ, LANES), lambda b, i: (b, i, 0))
    ik = pl.BlockSpec((1, s, LANES), lambda b, i: (b, 0, 0))
    tri = jnp.tril(jnp.ones((CB, CB), BF16))
    return pl.pallas_call(
        functools.partial(_dsa_kernel, topk),
        grid=(bsz, nq),
        in_specs=[rowblk(COL_BQ), rowblk(COL_BZ), rowblk(COL_IQ), tok, tok, tok, tok,
                  pl.BlockSpec((1, IDX_HEADS, TQ), lambda b, i: (b, 0, i)),
                  pl.BlockSpec((1, B_KV_HEADS, s, B_HEAD_DIM), lambda b, i: (b, 0, 0, 0)),
                  pl.BlockSpec((1, B_KV_HEADS, s // CB, B_HEAD_DIM, CB), lambda b, i: (b, 0, 0, 0, 0)),
                  ik, ik,
                  pl.BlockSpec((1, B_HEAD_DIM), lambda b, i: (0, 0)),
                  pl.BlockSpec((CB, CB), lambda b, i: (0, 0)),
                  pl.BlockSpec(memory_space=pltpu.SMEM)],
        out_specs=pl.BlockSpec((TQ, B_WIDTH), lambda b, i: (b * nq + i, 0)),
        out_shape=jax.ShapeDtypeStruct((bsz * s, B_WIDTH), BF16),
        scratch_shapes=[
            pltpu.VMEM((s, TQ), F32),
            pltpu.VMEM((s, TQ), F32),
            pltpu.VMEM((s, gw), F32),
            pltpu.VMEM((B_KV_HEADS, gw, B_HEAD_DIM), BF16),
            pltpu.VMEM((IDX_HEADS // 4, 2 * TQ, LANES), BF16),
            pltpu.VMEM((B_HEAD_DIM, gw), F32),
        ],
        compiler_params=_cparams(("parallel", "arbitrary")),
        name="dsa",
    )(proj, proj, proj, *tabs, iwt, k, vt, iklo, ikhi, gq, tri, bound)


def _merge_kernel(ta_ref, tb_ref, tm_ref, ga_ref, gb_ref, gm_ref, x_ref,
                  wa_ref, wb_ref, wm_ref, wo_ref, o_ref):
    def branch(t_ref, g_ref, w_ref):
        y = jnp.dot(t_ref[...], w_ref[...], preferred_element_type=F32)
        return jax.nn.sigmoid(g_ref[...].astype(F32)) * y

    merged = branch(ta_ref, ga_ref, wa_ref) + branch(tb_ref, gb_ref, wb_ref) + branch(tm_ref, gm_ref, wm_ref)
    o_ref[...] = x_ref[...] + jnp.dot(merged.astype(BF16), wo_ref[...], preferred_element_type=F32)


def _merge(ta, tb, tmem, proj, x2, wa, wb, wm, wo):
    m = x2.shape[0]
    tm = min(256, m)
    act = pl.BlockSpec((tm, BLK), lambda i: (i, 0))
    gate = lambda c: pl.BlockSpec((tm, D_MODEL), lambda i: (i, c // 2))
    wide = pl.BlockSpec((tm, D_MODEL), lambda i: (i, 0))
    wbr = pl.BlockSpec((BLK, D_MODEL), lambda i: (0, 0), pipeline_mode=pl.Buffered(1))
    wout = pl.BlockSpec((D_MODEL, D_MODEL), lambda i: (0, 0), pipeline_mode=pl.Buffered(1))
    return pl.pallas_call(
        _merge_kernel,
        grid=(m // tm,),
        in_specs=[act, act, act, gate(COL_GA), gate(COL_GB), gate(COL_GM), wide, wbr, wbr, wbr, wout],
        out_specs=wide,
        out_shape=jax.ShapeDtypeStruct((m, D_MODEL), F32),
        compiler_params=_cparams(("parallel",)),
        name="merge",
    )(ta, tb, tmem, proj, proj, proj, x2, wa, wb, wm, wo)


def _rope_freqs():
    inv = lambda half: ROPE_THETA ** (-np.arange(half, dtype=np.float32) / half)
    freq = np.zeros((1, LANES), np.float32)
    hb, hi = B_HEAD_DIM // 8, IDX_DIM // 8
    freq[0, :hb] = inv(hb)
    freq[0, hb:hb + hi] = inv(hi)
    return jnp.asarray(freq)


def _layer(x, mem, positions, norm_gain, w_in, gmlp_ln_gain, gmlp_ln_bias, spatial_w, spatial_b,
           w_branch_a, q_norm_gain, k_norm_gain, idx_k_ln_gain, idx_k_ln_bias, w_branch_b,
           mem_norm_gain, w_mem_kv, mem_q_norm_gain, mem_k_norm_gain, w_branch_m, w_out):
    bsz, s, _ = x.shape
    m = bsz * s
    row = lambda a: a.reshape(1, -1).astype(F32)
    pad_lanes = lambda a: jnp.pad(a.reshape(1, -1).astype(F32), ((0, 0), (0, LANES - a.shape[-1])))
    x2 = x.reshape(m, D_MODEL)

    w_t = w_in.T
    h, k, vt, iklo, ikhi, cb, sb, ci, si, iwt = _kvprep(
        x, row(norm_gain), w_t, positions.reshape(bsz, s, 1).astype(jnp.int32), _rope_freqs(),
        row(k_norm_gain), pad_lanes(idx_k_ln_gain), pad_lanes(idx_k_ln_bias))
    proj = _inproj(h, w_t)

    t_a = _gmlp(proj, row(gmlp_ln_gain), row(gmlp_ln_bias), spatial_w.astype(F32), spatial_b.T.astype(F32))
    km, vm = _memkv(mem, row(mem_norm_gain), w_mem_kv.astype(BF16), row(mem_k_norm_gain))
    t_m = _memattn(proj, km, vm, row(mem_q_norm_gain), s)
    t_b = _dsa(proj, (cb, sb, ci, si), iwt, k, vt, iklo, ikhi, row(q_norm_gain), row(k_norm_gain), bsz, s)

    out = _merge(t_a, t_b, t_m, proj, x2, w_branch_a.astype(BF16), w_branch_b.astype(BF16),
                 w_branch_m.astype(BF16), w_out.astype(BF16))
    return out.reshape(bsz, s, D_MODEL)


def kernel(x, mem, positions, norm_gain, w_in, gmlp_ln_gain, gmlp_ln_bias, spatial_w, spatial_b, w_branch_a, q_norm_gain, k_norm_gain, idx_k_ln_gain, idx_k_ln_bias, w_branch_b, mem_norm_gain, w_mem_kv, mem_q_norm_gain, mem_k_norm_gain, w_branch_m, w_out):
    for l in range(norm_gain.shape[0]):
        x = _layer(x, mem, positions, norm_gain[l], w_in[l], gmlp_ln_gain[l], gmlp_ln_bias[l],
                   spatial_w[l], spatial_b[l], w_branch_a[l], q_norm_gain[l], k_norm_gain[l],
                   idx_k_ln_gain[l], idx_k_ln_bias[l], w_branch_b[l], mem_norm_gain[l], w_mem_kv[l],
                   mem_q_norm_gain[l], mem_k_norm_gain[l], w_branch_m[l], w_out[l])
    return x
```

```python
import functools

import numpy as np
import jax
import jax.numpy as jnp
from jax import lax
from jax.experimental import pallas as pl
from jax.experimental.pallas import tpu as pltpu

F32 = jnp.float32
BF16 = jnp.bfloat16

D_MODEL = 2048
ROPE_THETA = 500000.0
EPS = 1e-6
A_GROUPS = 8
A_GROUP_DIM = 128
A_WIDTH = A_GROUPS * A_GROUP_DIM
CHUNK = 128
B_HEADS = 8
B_KV_HEADS = 2
B_GROUP = B_HEADS // B_KV_HEADS
B_HEAD_DIM = 128
B_WIDTH = B_HEADS * B_HEAD_DIM
IDX_HEADS = 16
IDX_DIM = 64
TOPK_MAX = 256
M_HEADS = 4
M_HEAD_DIM = 256
M_WIDTH = M_HEADS * M_HEAD_DIM

SPLIT_SIZES = (
    A_WIDTH, A_WIDTH, A_WIDTH,
    B_WIDTH, B_KV_HEADS * B_HEAD_DIM, B_KV_HEADS * B_HEAD_DIM, B_WIDTH,
    IDX_HEADS * IDX_DIM, IDX_DIM, IDX_HEADS,
    M_WIDTH, M_WIDTH,
    D_MODEL, D_MODEL, D_MODEL,
)

LANES = 128
BLK = 1024
COL_AU, COL_AV, COL_AZ, COL_BQ, COL_BZ, COL_IQ, COL_MQ, COL_MZ = range(8)
COL_GA, COL_GB, COL_GM = 8, 10, 12
NB_MAIN = 14
_OFFS = [int(o) for o in np.concatenate([[0], np.cumsum(SPLIT_SIZES)])]
ROW_ALIGN = 16
MAIN_START = ([_OFFS[i] for i in (0, 1, 2, 3, 6, 7, 10, 11)]
              + [_OFFS[i] + d for i in (12, 13, 14) for d in (0, BLK)])
assert all(s % ROW_ALIGN == 0 for s in MAIN_START)
SMALL_KV_START, SMALL_KV_WIDTH = _OFFS[4], _OFFS[6] - _OFFS[4]
SMALL_IDX_START = _OFFS[8]
assert SMALL_KV_START % SMALL_KV_WIDTH == 0 and SMALL_IDX_START % LANES == 0
assert IDX_DIM + IDX_HEADS <= LANES and _OFFS[9] == SMALL_IDX_START + IDX_DIM

VMEM_LIMIT = 56 * 1024 * 1024
LOG2E = 1.4426950408889634
NEG_BIG = -1e30
F32_MAX = 3.4028234663852886e38

INPROJ_TM, INPROJ_TN = 2048, 1024
TQ = 256
CK = 256
CB = 512
COUNT_WAYS = 8
SEARCH_MARGIN = 0.05
SEARCH_FIXED_PASSES = 11
SEARCH_INTERP_PASSES = 13
MAX_SEARCH_ITERS = 400
MERGE_STAGE_ROWS = 256
MAX_SHIFT_BOUND = 48.0


def _cparams(sem):
    return pltpu.CompilerParams(dimension_semantics=sem, vmem_limit_bytes=VMEM_LIMIT)


def _dot_nt(a, b):
    return lax.dot_general(a, b, (((1,), (1,)), ((), ())), preferred_element_type=F32)


def _rope(x, cos_t, sin_t, half, period):
    lane = lax.broadcasted_iota(jnp.int32, x.shape, 1) % period
    rolled = jnp.where(lane < half, pltpu.roll(x, LANES - half, 1), pltpu.roll(x, half, 1))
    return x * cos_t + rolled * sin_t


def _rms_rows(x, gain):
    ms = jnp.mean(x * x, axis=-1, keepdims=True)
    return (x * lax.rsqrt(ms + EPS) * gain).astype(BF16)


def _inproj_kernel(tab_ref, h_ref, wt_ref, proj_ref):
    del tab_ref
    proj_ref[...] = _dot_nt(h_ref[...], wt_ref[...].astype(BF16)).astype(BF16)


def _inproj(h, w_t):
    m = h.shape[0]
    tm = min(INPROJ_TM, m)
    starts = [s + d for s in MAIN_START for d in range(0, BLK, INPROJ_TN)]
    tab = jnp.asarray(np.array([s // ROW_ALIGN for s in starts], np.int32))
    return pl.pallas_call(
        _inproj_kernel,
        grid_spec=pltpu.PrefetchScalarGridSpec(
            num_scalar_prefetch=1,
            grid=(m // tm, len(starts)),
            in_specs=[
                pl.BlockSpec((tm, D_MODEL), lambda i, n, tab: (i, 0)),
                pl.BlockSpec((pl.Element(INPROJ_TN), pl.Element(D_MODEL)),
                             lambda i, n, tab: (tab[n] * ROW_ALIGN, 0)),
            ],
            out_specs=pl.BlockSpec((tm, INPROJ_TN), lambda i, n, tab: (i, n)),
        ),
        out_shape=jax.ShapeDtypeStruct((m, NB_MAIN * BLK), BF16),
        compiler_params=_cparams(("parallel", "arbitrary")),
        name="inproj",
    )(tab, h, w_t)


def _kvprep_kernel(x_ref, g_ref, wkv_ref, widx_ref, pos_ref, fc_ref, gk_ref, lng_ref, lnb_ref,
                   h_ref, k_ref, vt_ref, iklo_ref, ikhi_ref, cb_ref, sb_ref, ci_ref, si_ref, iwt_ref,
                   wkv_bf_ref, widx_bf_ref):
    @pl.when(jnp.logical_and(pl.program_id(0) == 0, pl.program_id(1) == 0))
    def _():
        wkv_bf_ref[...] = wkv_ref[...].astype(BF16)
        widx_bf_ref[...] = widx_ref[...].astype(BF16)

    h = _rms_rows(x_ref[...], g_ref[...])
    h_ref[...] = h
    kv = _dot_nt(h, wkv_bf_ref[...])
    ikp = _dot_nt(h, widx_bf_ref[...])

    hb, hi = B_HEAD_DIM // 8, IDX_DIM // 8
    ang = pos_ref[0].astype(F32) * fc_ref[...]
    cos_c, sin_c = jnp.cos(ang), jnp.sin(ang)
    lane = lax.broadcasted_iota(jnp.int32, ang.shape, 1)
    cos_b = jnp.where(lane < hb, cos_c, jnp.where(lane < 2 * hb, pltpu.roll(cos_c, hb, 1), 1.0))
    sin_b = jnp.where(lane < hb, -sin_c, jnp.where(lane < 2 * hb, pltpu.roll(sin_c, hb, 1), 0.0))
    cos_i = jnp.ones_like(cos_c)
    sin_i = jnp.zeros_like(sin_c)
    for head in range(LANES // IDX_DIM):
        for part, sign in enumerate((-1.0, 1.0)):
            first = head * IDX_DIM + part * hi
            here = jnp.logical_and(lane >= first, lane < first + hi)
            shift = (first - hb) % LANES
            cos_i = jnp.where(here, pltpu.roll(cos_c, shift, 1), cos_i)
            sin_i = jnp.where(here, sign * pltpu.roll(sin_c, shift, 1), sin_i)
    cb_ref[0] = cos_b
    sb_ref[0] = sin_b
    ci_ref[0] = cos_i
    si_ref[0] = sin_i

    for n in range(B_KV_HEADS):
        kh = kv[:, n * B_HEAD_DIM:(n + 1) * B_HEAD_DIM]
        r = lax.rsqrt(jnp.mean(kh * kh, axis=-1, keepdims=True) + EPS)
        kn = kh * r * gk_ref[...]
        k_ref[0, n] = _rope(kn, cos_b, sin_b, B_HEAD_DIM // 8, LANES).astype(BF16)
        vt = kv[:, (B_KV_HEADS + n) * B_HEAD_DIM:(B_KV_HEADS + n + 1) * B_HEAD_DIM].T
        for c in range(vt.shape[1] // CB):
            vt_ref[0, n, c] = vt[:, c * CB:(c + 1) * CB].astype(BF16)

    lane = lax.broadcasted_iota(jnp.int32, ikp.shape, 1)
    live = lane < IDX_DIM
    mu = jnp.sum(jnp.where(live, ikp, 0.0), axis=-1, keepdims=True) * (1.0 / IDX_DIM)
    d = jnp.where(live, ikp - mu, 0.0)
    var = jnp.sum(d * d, axis=-1, keepdims=True) * (1.0 / IDX_DIM)
    y = d * lax.rsqrt(var + EPS) * lng_ref[...] + lnb_ref[...]
    yr = jnp.where(live, _rope(y, cos_i, sin_i, IDX_DIM // 8, IDX_DIM), 0.0)
    iklo_ref[0] = yr.astype(BF16)
    ikhi_ref[0] = pltpu.roll(yr, IDX_DIM, 1).astype(BF16)

    iw = ikp * (IDX_DIM ** -0.5 * IDX_HEADS ** -0.5)
    iwt_ref[0] = iw.T[IDX_DIM:IDX_DIM + IDX_HEADS, :]


def _kvprep(x3, gain, w_t, pos3, freqs, gk, lng, lnb):
    bsz, s, _ = x3.shape
    tp = min(512, s)
    row = lambda b, i: (b, i, 0)
    const = lambda b, i: (0, 0)
    tab_spec = pl.BlockSpec((1, LANES), const)
    out_h = pl.BlockSpec((tp, D_MODEL), lambda b, i: (b * (s // tp) + i, 0))
    out_tok = pl.BlockSpec((1, tp, LANES), row)
    out_k = pl.BlockSpec((1, B_KV_HEADS, tp, B_HEAD_DIM), lambda b, i: (b, 0, i, 0))
    out_vt = pl.BlockSpec((1, B_KV_HEADS, tp // CB, B_HEAD_DIM, CB), lambda b, i: (b, 0, i, 0, 0))
    out_iwt = pl.BlockSpec((1, IDX_HEADS, tp), lambda b, i: (b, 0, i))
    return pl.pallas_call(
        _kvprep_kernel,
        grid=(bsz, s // tp),
        in_specs=[pl.BlockSpec((None, tp, D_MODEL), row), pl.BlockSpec((1, D_MODEL), const),
                  pl.BlockSpec((SMALL_KV_WIDTH, D_MODEL), lambda b, i: (SMALL_KV_START // SMALL_KV_WIDTH, 0)),
                  pl.BlockSpec((LANES, D_MODEL), lambda b, i: (SMALL_IDX_START // LANES, 0)),
                  pl.BlockSpec((1, tp, 1), row),
                  tab_spec, tab_spec, tab_spec, tab_spec],
        out_specs=[out_h, out_k, out_vt, out_tok, out_tok, out_tok, out_tok, out_tok, out_tok, out_iwt],
        out_shape=[
            jax.ShapeDtypeStruct((bsz * s, D_MODEL), BF16),
            jax.ShapeDtypeStruct((bsz, B_KV_HEADS, s, B_HEAD_DIM), BF16),
            jax.ShapeDtypeStruct((bsz, B_KV_HEADS, s // CB, B_HEAD_DIM, CB), BF16),
            jax.ShapeDtypeStruct((bsz, s, LANES), BF16),
            jax.ShapeDtypeStruct((bsz, s, LANES), BF16),
            jax.ShapeDtypeStruct((bsz, s, LANES), F32),
            jax.ShapeDtypeStruct((bsz, s, LANES), F32),
            jax.ShapeDtypeStruct((bsz, s, LANES), F32),
            jax.ShapeDtypeStruct((bsz, s, LANES), F32),
            jax.ShapeDtypeStruct((bsz, IDX_HEADS, s), F32),
        ],
        scratch_shapes=[pltpu.VMEM((SMALL_KV_WIDTH, D_MODEL), BF16), pltpu.VMEM((LANES, D_MODEL), BF16)],
        compiler_params=_cparams(("arbitrary", "arbitrary")),
        name="kvprep",
    )(x3, gain, w_t, w_t, pos3, freqs, gk, lng, lnb)


def _gelu_tanh(x):
    c = -2.0 * (2.0 / np.pi) ** 0.5 * LOG2E
    return x / (1.0 + jnp.exp2(x * (x * x * (0.044715 * c) + c)))


def _gmlp_kernel(u_ref, v_ref, z_ref, lng_ref, lnb_ref, ws_ref, sbt_ref, o_ref):
    tm = u_ref.shape[0]
    u = _gelu_tanh(u_ref[...].astype(F32))
    v = _gelu_tanh(v_ref[...].astype(F32))
    mu = jnp.mean(v, axis=-1, keepdims=True)
    d = v - mu
    var = jnp.mean(d * d, axis=-1, keepdims=True)
    vn = (d * lax.rsqrt(var + EPS) * lng_ref[...] + lnb_ref[...]).astype(BF16)
    gate = u * jax.nn.silu(z_ref[...].astype(F32))
    tri = (lax.broadcasted_iota(jnp.int32, (CHUNK, CHUNK), 1)
           <= lax.broadcasted_iota(jnp.int32, (CHUNK, CHUNK), 0))
    for g in range(A_GROUPS):
        wg = jnp.where(tri, ws_ref[g], 0.0).astype(BF16)
        bias = sbt_ref[:, g:g + 1]
        cols = slice(g * A_GROUP_DIM, (g + 1) * A_GROUP_DIM)
        for c in range(tm // CHUNK):
            rows = slice(c * CHUNK, (c + 1) * CHUNK)
            sg = jnp.dot(wg, vn[rows, cols], preferred_element_type=F32) + bias
            o_ref[rows, cols] = (gate[rows, cols] * sg).astype(BF16)


def _gmlp(proj, lng, lnb, ws, sbt):
    m = proj.shape[0]
    tm = min(512, m)
    col = lambda c: pl.BlockSpec((tm, BLK), lambda i: (i, c))
    full = lambda shape: pl.BlockSpec(shape, lambda i: (0,) * len(shape))
    return pl.pallas_call(
        _gmlp_kernel,
        grid=(m // tm,),
        in_specs=[col(COL_AU), col(COL_AV), col(COL_AZ), full((1, A_WIDTH)), full((1, A_WIDTH)),
                  full((A_GROUPS, CHUNK, CHUNK)), full((CHUNK, A_GROUPS))],
        out_specs=pl.BlockSpec((tm, A_WIDTH), lambda i: (i, 0)),
        out_shape=jax.ShapeDtypeStruct((m, A_WIDTH), BF16),
        compiler_params=_cparams(("parallel",)),
        name="gmlp",
    )(proj, proj, proj, lng, lnb, ws, sbt)


def _memkv_kernel(mem_ref, g_ref, w_ref, gk_ref, km_ref, vm_ref, wbf_ref):
    @pl.when(pl.program_id(0) == 0)
    def _():
        wbf_ref[...] = w_ref[...].astype(BF16)

    h = _rms_rows(mem_ref[0], g_ref[...])
    kv = jnp.dot(h, wbf_ref[...], preferred_element_type=F32)
    for hd in range(M_HEADS):
        kh = kv[:, hd * M_HEAD_DIM:(hd + 1) * M_HEAD_DIM]
        r = lax.rsqrt(jnp.mean(kh * kh, axis=-1, keepdims=True) + EPS)
        km_ref[0, hd] = (kh * r * gk_ref[...]).astype(BF16)
        vm_ref[0, hd] = kv[:, M_WIDTH + hd * M_HEAD_DIM:M_WIDTH + (hd + 1) * M_HEAD_DIM].astype(BF16)


def _memkv(mem, gain, w_kv, gk):
    bsz, ml, _ = mem.shape
    out = pl.BlockSpec((1, M_HEADS, ml, M_HEAD_DIM), lambda b: (b, 0, 0, 0))
    shp = jax.ShapeDtypeStruct((bsz, M_HEADS, ml, M_HEAD_DIM), BF16)
    return pl.pallas_call(
        _memkv_kernel,
        grid=(bsz,),
        in_specs=[pl.BlockSpec((1, ml, D_MODEL), lambda b: (b, 0, 0)),
                  pl.BlockSpec((1, D_MODEL), lambda b: (0, 0)),
                  pl.BlockSpec((D_MODEL, 2 * M_WIDTH), lambda b: (0, 0), pipeline_mode=pl.Buffered(1)),
                  pl.BlockSpec((1, M_HEAD_DIM), lambda b: (0, 0))],
        out_specs=[out, out],
        out_shape=[shp, shp],
        scratch_shapes=[pltpu.VMEM((D_MODEL, 2 * M_WIDTH), BF16)],
        compiler_params=_cparams(("arbitrary",)),
        name="memkv",
    )(mem, gain, w_kv, gk)


def _memattn_kernel(q_ref, z_ref, km_ref, vm_ref, gq_ref, o_ref):
    qscale = M_HEAD_DIM ** -0.5 * LOG2E
    for hd in range(M_HEADS):
        cols = slice(hd * M_HEAD_DIM, (hd + 1) * M_HEAD_DIM)
        q = q_ref[:, cols].astype(F32)
        r = lax.rsqrt(jnp.mean(q * q, axis=-1, keepdims=True) + EPS)
        qn = (q * r * gq_ref[...] * qscale).astype(BF16)
        lg = _dot_nt(qn, km_ref[0, hd])
        p = jnp.exp2(lg - jnp.max(lg, axis=-1, keepdims=True))
        l = jnp.sum(p, axis=-1, keepdims=True)
        o = jnp.dot(p.astype(BF16), vm_ref[0, hd], preferred_element_type=F32) / l
        o_ref[:, cols] = (o * jax.nn.silu(z_ref[:, cols].astype(F32))).astype(BF16)


def _memattn(proj, km, vm, gq, s):
    m = proj.shape[0]
    tm = min(512, s)
    per_b = s // tm
    ml = km.shape[2]
    kv_spec = pl.BlockSpec((1, M_HEADS, ml, M_HEAD_DIM), lambda i: (i // per_b, 0, 0, 0))
    return pl.pallas_call(
        _memattn_kernel,
        grid=(m // tm,),
        in_specs=[pl.BlockSpec((tm, BLK), lambda i: (i, COL_MQ)),
                  pl.BlockSpec((tm, BLK), lambda i: (i, COL_MZ)),
                  kv_spec, kv_spec,
                  pl.BlockSpec((1, M_HEAD_DIM), lambda i: (0, 0))],
        out_specs=pl.BlockSpec((tm, M_WIDTH), lambda i: (i, 0)),
        out_shape=jax.ShapeDtypeStruct((m, M_WIDTH), BF16),
        compiler_params=_cparams(("parallel",)),
        name="memattn",
    )(proj, proj, km, vm, gq)


def _dsa_kernel(topk, q_ref, z_ref, iq_ref, cb_ref, sb_ref, ci_ref, si_ref, iwt_ref,
                k_ref, vt_ref, iklo_ref, ikhi_ref, gq_ref, tri_ref, bound_ref, o_ref,
                sc_ref, bias_ref, lg_ref, qs_ref, iqs_ref, acc_ref):
    qb = pl.program_id(1)
    nck = (qb * TQ + TQ + CK - 1) // CK
    nbig = (qb * TQ + TQ + CB - 1) // CB
    t_lane = qb * TQ + lax.broadcasted_iota(jnp.int32, (1, TQ), 1)
    gw = B_GROUP * TQ

    def fold8(a, ways=1):
        return a.reshape(a.shape[0] // (8 * ways), 8 * ways, a.shape[1])

    def paired_loop(trips, body, init, per_step=2):
        def step(i, c):
            for u in range(per_step):
                c = body(per_step * i + u, c)
            return c

        carry = lax.fori_loop(0, trips // per_step, step, init)
        return lax.fori_loop(trips - trips % per_step, trips, body, carry)

    cos_b, sin_b = cb_ref[0], sb_ref[0]
    cos_i, sin_i = ci_ref[0], si_ref[0]
    qscale = B_HEAD_DIM ** -0.5 * LOG2E
    ones_mat = jnp.ones((B_HEAD_DIM, B_HEAD_DIM), BF16)
    for h in range(B_HEADS):
        slab = q_ref[:, h * B_HEAD_DIM:(h + 1) * B_HEAD_DIM].astype(F32)
        sq = slab * slab
        sq_hi = sq.astype(BF16)
        sq_lo = (sq - sq_hi.astype(F32)).astype(BF16)
        ssq = (jnp.dot(sq_hi, ones_mat, preferred_element_type=F32)
               + jnp.dot(sq_lo, ones_mat, preferred_element_type=F32))
        r = lax.rsqrt(ssq * (1.0 / B_HEAD_DIM) + EPS)
        qr = _rope(slab * r * gq_ref[...], cos_b, sin_b, B_HEAD_DIM // 8, LANES) * qscale
        g = h % B_GROUP
        qs_ref[h // B_GROUP, g * TQ:(g + 1) * TQ, :] = qr.astype(BF16)
    for j in range(IDX_HEADS // 2):
        slab = iq_ref[:, j * LANES:(j + 1) * LANES].astype(F32)
        iqs_ref[j // 2, (j % 2) * TQ:(j % 2 + 1) * TQ, :] = (
            _rope(slab, cos_i, sin_i, IDX_DIM // 8, IDX_DIM).astype(BF16))
    wt = iwt_ref[0]

    def idx_body(c, carry):
        mn8, mx8, s1, s2 = carry
        off = pl.multiple_of(c * CK, CK)
        acc = jnp.zeros((CK, TQ), F32)
        for jj in range(IDX_HEADS // 4):
            rhs = iqs_ref[jj]
            for half, keys_ref in enumerate((iklo_ref, ikhi_ref)):
                d = _dot_nt(keys_ref[0, pl.ds(off, CK), :], rhs)
                ha, hb = 4 * jj + half, 4 * jj + 2 + half
                acc = (acc + jnp.maximum(d[:, :TQ], 0.0) * wt[ha:ha + 1, :]
                       + jnp.maximum(d[:, TQ:], 0.0) * wt[hb:hb + 1, :])
        key = off + lax.broadcasted_iota(jnp.int32, (CK, TQ), 0)
        causal = key <= t_lane
        sc = jnp.where(causal, acc, -jnp.inf)
        sc_ref[pl.ds(off, CK), :] = sc
        live = jnp.where(causal, acc, 0.0)
        mn8 = jnp.minimum(mn8, jnp.min(fold8(jnp.where(causal, acc, jnp.inf)), axis=0))
        mx8 = jnp.maximum(mx8, jnp.max(fold8(sc), axis=0))
        s1 = s1 + jnp.sum(fold8(live), axis=0)
        s2 = s2 + jnp.sum(fold8(live * live), axis=0)
        return mn8, mx8, s1, s2

    zero8 = jnp.zeros((8, TQ), F32)
    stats = (jnp.full((8, TQ), jnp.inf, F32), jnp.full((8, TQ), -jnp.inf, F32), zero8, zero8)
    mn8, mx8, s1, s2 = paired_loop(nck, idx_body, stats, per_step=4)
    row_min = jnp.min(mn8, axis=0, keepdims=True)
    row_max = jnp.max(mx8, axis=0, keepdims=True)

    def fill_body(c, carry):
        sc_ref[pl.ds(pl.multiple_of(c * CK, CK), CK), :] = jnp.full((CK, TQ), -jnp.inf, F32)
        return carry

    lax.fori_loop(nck, nbig * (CB // CK), fill_body, 0)

    def count_rows(pred):
        rows = 8 * COUNT_WAYS

        def body(c, cnt):
            off = pl.multiple_of(c * CB, CB)
            for r in range(CB // rows):
                cnt = cnt + jnp.where(pred(sc_ref[pl.ds(off + r * rows, rows), :]), 1.0, 0.0)
            return cnt

        cnt = lax.fori_loop(0, nbig, body, jnp.zeros((rows, TQ), F32))
        return jnp.sum(cnt, axis=0, keepdims=True)

    kf = float(topk)
    n_valid = (t_lane + 1).astype(F32)
    all_rows = (t_lane + 1) <= topk

    def search_pass(st, probe, stuck):
        lo, hi, clo, chi, thr, fin, tie = st
        cnt = count_rows(lambda blk: blk >= probe)
        active = fin < 0.5
        hit = cnt == kf
        end_thr = jnp.where(jnp.logical_and(stuck, cnt < kf), lo, probe)
        ends = jnp.logical_and(active, jnp.logical_or(hit, stuck))
        thr = jnp.where(ends, end_thr, thr)
        tie = jnp.where(jnp.logical_and(ends, jnp.logical_not(hit)), 1.0, tie)
        fin = jnp.where(ends, 1.0, fin)
        up = jnp.logical_and(active, cnt > kf)
        dn = jnp.logical_and(active, cnt < kf)
        return (jnp.where(up, probe, lo), jnp.where(dn, probe, hi), jnp.where(up, cnt, clo),
                jnp.where(dn, cnt, chi), thr, fin, tie)

    def next_probe(st, halve=False):
        lo, hi, clo, chi = st[:4]
        frac = jnp.clip((clo - kf + 0.5) / (clo - chi + 1.0), SEARCH_MARGIN, 1.0 - SEARCH_MARGIN)
        probe = lo + (hi - lo) * frac
        outside = jnp.logical_or(jnp.logical_or(probe <= lo, probe >= hi), halve)
        probe = jnp.where(outside, lo * 0.5 + hi * 0.5, probe)
        stuck = jnp.logical_or(probe <= lo, probe >= hi)
        return jnp.where(stuck, hi, probe), stuck

    mean = jnp.sum(s1, axis=0, keepdims=True) / n_valid
    var = jnp.maximum(jnp.sum(s2, axis=0, keepdims=True) / n_valid - mean * mean, 0.0)
    tail = jnp.clip(kf / n_valid, 1e-6, 1.0 - 1e-6)
    tq = jnp.sqrt(-2.0 * jnp.log(jnp.minimum(tail, 1.0 - tail)))
    zq = tq - ((0.010328 * tq + 0.802853) * tq + 2.515517) / (
        ((0.001308 * tq + 0.189269) * tq + 1.432788) * tq + 1.0)
    probe0 = jnp.clip(mean + jnp.where(tail < 0.5, zq, -zq) * jnp.sqrt(var), row_min, row_max)

    ones = jnp.ones((1, TQ), F32)
    st = (row_min, row_max, n_valid, 0.0 * ones, jnp.where(all_rows, -F32_MAX, row_max),
          jnp.where(all_rows, 1.0, 0.0), 0.0 * ones)
    st = search_pass(st, probe0, probe0 < row_min)

    def fixed_body(i, st):
        return search_pass(st, *next_probe(st))

    st = lax.fori_loop(0, SEARCH_FIXED_PASSES, fixed_body, st)

    def more_cond(c):
        return jnp.logical_and(c[0] < MAX_SEARCH_ITERS, jnp.min(c[1][5]) < 0.5)

    def more_body(c):
        return c[0] + 1, search_pass(c[1], *next_probe(c[1], c[0] >= SEARCH_INTERP_PASSES))

    _, st = lax.while_loop(more_cond, more_body, (jnp.int32(0), st))
    thr, tie = st[4], st[6]

    def plain_mask():
        def body(c, carry):
            off = pl.multiple_of(c * CB, CB)
            bias_ref[pl.ds(off, CB), :] = jnp.where(sc_ref[pl.ds(off, CB), :] >= thr, 0.0, NEG_BIG)
            return carry

        lax.fori_loop(0, nbig, body, 0)

    def tied_mask():
        n_gt = count_rows(lambda blk: blk > thr)
        need = jnp.where(all_rows, F32_MAX, kf - n_gt)

        def body(c, seen):
            off = pl.multiple_of(c * CB, CB)
            blk = sc_ref[pl.ds(off, CB), :]
            eq = blk == thr
            pref = jnp.dot(tri_ref[...], jnp.where(eq, 1.0, 0.0).astype(BF16),
                           preferred_element_type=F32) + seen
            keep = jnp.logical_or(blk > thr, jnp.logical_and(eq, pref <= need))
            bias_ref[pl.ds(off, CB), :] = jnp.where(keep, 0.0, NEG_BIG)
            return pref[CB - 1:CB, :]

        lax.fori_loop(0, nbig, body, jnp.zeros((1, TQ), F32))

    lax.cond(jnp.max(tie) > 0.5, tied_mask, plain_mask)

    logit_bound = bound_ref[0, 0]

    def masked_logits(n, off):
        b = bias_ref[pl.ds(off, CB), :]
        lg = _dot_nt(k_ref[0, n, pl.ds(off, CB), :], qs_ref[n])
        return jnp.concatenate([lg[:, g * TQ:(g + 1) * TQ] + b for g in range(B_GROUP)], axis=1)

    for n in range(B_KV_HEADS):
        acc_ref[...] = jnp.zeros(acc_ref.shape, F32)

        def one_pass(n=n):
            def body(c, l8):
                off = pl.multiple_of(c * CB, CB)
                p = jnp.exp2(masked_logits(n, off) - logit_bound)
                acc_ref[...] += jnp.dot(vt_ref[0, n, c], p.astype(BF16), preferred_element_type=F32)
                return l8 + jnp.sum(fold8(p), axis=0)

            return paired_loop(nbig, body, jnp.zeros((8, gw), F32))

        def two_pass(n=n):
            def logit_body(c, mx8):
                off = pl.multiple_of(c * CB, CB)
                lg = masked_logits(n, off)
                lg_ref[pl.ds(off, CB), :] = lg
                return jnp.maximum(mx8, jnp.max(fold8(lg), axis=0))

            mx8 = paired_loop(nbig, logit_body, jnp.full((8, gw), NEG_BIG, F32))
            m = jnp.max(mx8, axis=0, keepdims=True)

            def pv_body(c, l8):
                off = pl.multiple_of(c * CB, CB)
                p = jnp.exp2(lg_ref[pl.ds(off, CB), :] - m)
                acc_ref[...] += jnp.dot(vt_ref[0, n, c], p.astype(BF16), preferred_element_type=F32)
                return l8 + jnp.sum(fold8(p), axis=0)

            return paired_loop(nbig, pv_body, jnp.zeros((8, gw), F32))

        l8 = lax.cond(logit_bound <= MAX_SHIFT_BOUND, one_pass, two_pass)
        o_t = acc_ref[...] / jnp.sum(l8, axis=0, keepdims=True)
        for g in range(B_GROUP):
            cols = slice((n * B_GROUP + g) * B_HEAD_DIM, (n * B_GROUP + g + 1) * B_HEAD_DIM)
            o = o_t[:, g * TQ:(g + 1) * TQ].T
            o_ref[:, cols] = (o * jax.nn.silu(z_ref[:, cols].astype(F32))).astype(BF16)


def _dsa(proj, tabs, iwt, k, vt, iklo, ikhi, gq, gk, bsz, s):
    nq = s // TQ
    bound = (1.02 * B_HEAD_DIM ** 0.5 * LOG2E * jnp.max(jnp.abs(gq)) * jnp.max(jnp.abs(gk))).reshape(1, 1)
    topk = min(TOPK_MAX, s // 4)
    gw = B_GROUP * TQ
    rowblk = lambda c: pl.BlockSpec((TQ, BLK), lambda b, i: (b * nq + i, c))
    tok = pl.BlockSpec((1, TQ, LANES), lambda b, i: (b, i, 0))
    ik = pl.BlockSpec((1, s, LANES), lambda b, i: (b, 0, 0))
    tri = jnp.tril(jnp.ones((CB, CB), BF16))
    return pl.pallas_call(
        functools.partial(_dsa_kernel, topk),
        grid=(bsz, nq),
        in_specs=[rowblk(COL_BQ), rowblk(COL_BZ), rowblk(COL_IQ), tok, tok, tok, tok,
                  pl.BlockSpec((1, IDX_HEADS, TQ), lambda b, i: (b, 0, i)),
                  pl.BlockSpec((1, B_KV_HEADS, s, B_HEAD_DIM), lambda b, i: (b, 0, 0, 0)),
                  pl.BlockSpec((1, B_KV_HEADS, s // CB, B_HEAD_DIM, CB), lambda b, i: (b, 0, 0, 0, 0)),
                  ik, ik,
                  pl.BlockSpec((1, B_HEAD_DIM), lambda b, i: (0, 0)),
                  pl.BlockSpec((CB, CB), lambda b, i: (0, 0)),
                  pl.BlockSpec(memory_space=pltpu.SMEM)],
        out_specs=pl.BlockSpec((TQ, B_WIDTH), lambda b, i: (b * nq + i, 0)),
        out_shape=jax.ShapeDtypeStruct((bsz * s, B_WIDTH), BF16),
        scratch_shapes=[
            pltpu.VMEM((s, TQ), F32),
            pltpu.VMEM((s, TQ), F32),
            pltpu.VMEM((s, gw), F32),
            pltpu.VMEM((B_KV_HEADS, gw, B_HEAD_DIM), BF16),
            pltpu.VMEM((IDX_HEADS // 4, 2 * TQ, LANES), BF16),
            pltpu.VMEM((B_HEAD_DIM, gw), F32),
        ],
        compiler_params=_cparams(("parallel", "arbitrary")),
        name="dsa",
    )(proj, proj, proj, *tabs, iwt, k, vt, iklo, ikhi, gq, tri, bound)


def _merge_kernel(ta_ref, tb_ref, tm_ref, ga_ref, gb_ref, gm_ref, x_ref,
                  wa_hbm, wb_hbm, wm_hbm, wo_hbm, o_ref,
                  wa_ref, wb_ref, wm_ref, wo_ref, stage_ref, sem):
    @pl.when(pl.program_id(0) == 0)
    def _():
        chunks = [(src, dst, r) for src, dst in ((wa_hbm, wa_ref), (wb_hbm, wb_ref), (wm_hbm, wm_ref),
                                                 (wo_hbm, wo_ref))
                  for r in range(0, src.shape[0], MERGE_STAGE_ROWS)]

        def copy(j):
            src, _, r = chunks[j]
            return pltpu.make_async_copy(src.at[pl.ds(r, MERGE_STAGE_ROWS), :], stage_ref.at[j % 2],
                                         sem.at[j % 2])

        copy(0).start()
        for j, (_, dst, r) in enumerate(chunks):
            if j + 1 < len(chunks):
                copy(j + 1).start()
            copy(j).wait()
            dst[pl.ds(r, MERGE_STAGE_ROWS), :] = stage_ref[j % 2].astype(BF16)

    def branch(t_ref, g_ref, w_ref):
        y = jnp.dot(t_ref[...], w_ref[...], preferred_element_type=F32)
        return jax.nn.sigmoid(g_ref[...].astype(F32)) * y

    merged = branch(ta_ref, ga_ref, wa_ref) + branch(tb_ref, gb_ref, wb_ref) + branch(tm_ref, gm_ref, wm_ref)
    o_ref[...] = x_ref[...] + jnp.dot(merged.astype(BF16), wo_ref[...], preferred_element_type=F32)


def _merge(ta, tb, tmem, proj, x2, wa, wb, wm, wo):
    m = x2.shape[0]
    tm = min(256, m)
    act = pl.BlockSpec((tm, BLK), lambda i: (i, 0))
    gate = lambda c: pl.BlockSpec((tm, D_MODEL), lambda i: (i, c // 2))
    wide = pl.BlockSpec((tm, D_MODEL), lambda i: (i, 0))
    hbm = pl.BlockSpec(memory_space=pl.ANY)
    return pl.pallas_call(
        _merge_kernel,
        grid=(m // tm,),
        in_specs=[act, act, act, gate(COL_GA), gate(COL_GB), gate(COL_GM), wide, hbm, hbm, hbm, hbm],
        out_specs=wide,
        out_shape=jax.ShapeDtypeStruct((m, D_MODEL), F32),
        scratch_shapes=[pltpu.VMEM(w.shape, BF16) for w in (wa, wb, wm, wo)]
        + [pltpu.VMEM((2, MERGE_STAGE_ROWS, D_MODEL), F32), pltpu.SemaphoreType.DMA((2,))],
        compiler_params=_cparams(("arbitrary",)),
        name="merge",
    )(ta, tb, tmem, proj, proj, proj, x2, wa, wb, wm, wo)


def _rope_freqs():
    inv = lambda half: ROPE_THETA ** (-np.arange(half, dtype=np.float32) / half)
    freq = np.zeros((1, LANES), np.float32)
    hb, hi = B_HEAD_DIM // 8, IDX_DIM // 8
    freq[0, :hb] = inv(hb)
    freq[0, hb:hb + hi] = inv(hi)
    return jnp.asarray(freq)


def _layer(x, mem, positions, norm_gain, w_in, gmlp_ln_gain, gmlp_ln_bias, spatial_w, spatial_b,
           w_branch_a, q_norm_gain, k_norm_gain, idx_k_ln_gain, idx_k_ln_bias, w_branch_b,
           mem_norm_gain, w_mem_kv, mem_q_norm_gain, mem_k_norm_gain, w_branch_m, w_out):
    bsz, s, _ = x.shape
    m = bsz * s
    row = lambda a: a.reshape(1, -1).astype(F32)
    pad_lanes = lambda a: jnp.pad(a.reshape(1, -1).astype(F32), ((0, 0), (0, LANES - a.shape[-1])))
    x2 = x.reshape(m, D_MODEL)

    w_t = w_in.T
    h, k, vt, iklo, ikhi, cb, sb, ci, si, iwt = _kvprep(
        x, row(norm_gain), w_t, positions.reshape(bsz, s, 1).astype(jnp.int32), _rope_freqs(),
        row(k_norm_gain), pad_lanes(idx_k_ln_gain), pad_lanes(idx_k_ln_bias))
    proj = _inproj(h, w_t)

    t_a = _gmlp(proj, row(gmlp_ln_gain), row(gmlp_ln_bias), spatial_w.astype(F32), spatial_b.T.astype(F32))
    km, vm = _memkv(mem, row(mem_norm_gain), w_mem_kv, row(mem_k_norm_gain))
    t_m = _memattn(proj, km, vm, row(mem_q_norm_gain), s)
    t_b = _dsa(proj, (cb, sb, ci, si), iwt, k, vt, iklo, ikhi, row(q_norm_gain), row(k_norm_gain), bsz, s)

    out = _merge(t_a, t_b, t_m, proj, x2, w_branch_a, w_branch_b, w_branch_m, w_out)
    return out.reshape(bsz, s, D_MODEL)


def kernel(x, mem, positions, norm_gain, w_in, gmlp_ln_gain, gmlp_ln_bias, spatial_w, spatial_b, w_branch_a, q_norm_gain, k_norm_gain, idx_k_ln_gain, idx_k_ln_bias, w_branch_b, mem_norm_gain, w_mem_kv, mem_q_norm_gain, mem_k_norm_gain, w_branch_m, w_out):
    for l in range(norm_gain.shape[0]):
        x = _layer(x, mem, positions, norm_gain[l], w_in[l], gmlp_ln_gain[l], gmlp_ln_bias[l],
                   spatial_w[l], spatial_b[l], w_branch_a[l], q_norm_gain[l], k_norm_gain[l],
                   idx_k_ln_gain[l], idx_k_ln_bias[l], w_branch_b[l], mem_norm_gain[l], w_mem_kv[l],
                   mem_q_norm_gain[l], mem_k_norm_gain[l], w_branch_m[l], w_out[l])
    return x
```

```python
import functools

import numpy as np
import jax
import jax.numpy as jnp
from jax import lax
from jax.experimental import pallas as pl
from jax.experimental.pallas import tpu as pltpu

F32 = jnp.float32
BF16 = jnp.bfloat16

D_MODEL = 2048
ROPE_THETA = 500000.0
EPS = 1e-6
A_GROUPS = 8
A_GROUP_DIM = 128
A_WIDTH = A_GROUPS * A_GROUP_DIM
CHUNK = 128
B_HEADS = 8
B_KV_HEADS = 2
B_GROUP = B_HEADS // B_KV_HEADS
B_HEAD_DIM = 128
B_WIDTH = B_HEADS * B_HEAD_DIM
IDX_HEADS = 16
IDX_DIM = 64
TOPK_MAX = 256
M_HEADS = 4
M_HEAD_DIM = 256
M_WIDTH = M_HEADS * M_HEAD_DIM

SPLIT_SIZES = (
    A_WIDTH, A_WIDTH, A_WIDTH,
    B_WIDTH, B_KV_HEADS * B_HEAD_DIM, B_KV_HEADS * B_HEAD_DIM, B_WIDTH,
    IDX_HEADS * IDX_DIM, IDX_DIM, IDX_HEADS,
    M_WIDTH, M_WIDTH,
    D_MODEL, D_MODEL, D_MODEL,
)

LANES = 128
BLK = 1024
COL_AU, COL_AV, COL_AZ, COL_BQ, COL_BZ, COL_IQ, COL_MQ, COL_MZ = range(8)
COL_GA, COL_GB, COL_GM = 8, 10, 12
NB_MAIN = 14
_OFFS = [int(o) for o in np.concatenate([[0], np.cumsum(SPLIT_SIZES)])]
ROW_ALIGN = 16
MAIN_START = ([_OFFS[i] for i in (0, 1, 2, 3, 6, 7, 10, 11)]
              + [_OFFS[i] + d for i in (12, 13, 14) for d in (0, BLK)])
assert all(s % ROW_ALIGN == 0 for s in MAIN_START)
SMALL_KV_START, SMALL_KV_WIDTH = _OFFS[4], _OFFS[6] - _OFFS[4]
SMALL_IDX_START = _OFFS[8]
assert SMALL_KV_START % SMALL_KV_WIDTH == 0 and SMALL_IDX_START % LANES == 0
assert IDX_DIM + IDX_HEADS <= LANES and _OFFS[9] == SMALL_IDX_START + IDX_DIM

VMEM_LIMIT = 56 * 1024 * 1024
LOG2E = 1.4426950408889634
NEG_BIG = -1e30
F32_MAX = 3.4028234663852886e38

INPROJ_TM, INPROJ_TN = 2048, 1024
KVPREP_TP = 512
TQ = 256
CK = 256
CB = 512
COUNT_WAYS = 8
SEARCH_MARGIN = 0.05
SEARCH_FIXED_PASSES = 11
SEARCH_INTERP_PASSES = 13
MAX_SEARCH_ITERS = 400
MERGE_STAGE_ROWS = 512
MAX_SHIFT_BOUND = 48.0


def _cparams(sem):
    return pltpu.CompilerParams(dimension_semantics=sem, vmem_limit_bytes=VMEM_LIMIT)


def _dot_nt(a, b):
    return lax.dot_general(a, b, (((1,), (1,)), ((), ())), preferred_element_type=F32)


def _rope(x, cos_t, sin_t, half, period):
    lane = lax.broadcasted_iota(jnp.int32, x.shape, 1) % period
    rolled = jnp.where(lane < half, pltpu.roll(x, LANES - half, 1), pltpu.roll(x, half, 1))
    return x * cos_t + rolled * sin_t


def _rms_rows(x, gain):
    ms = jnp.mean(x * x, axis=-1, keepdims=True)
    return (x * lax.rsqrt(ms + EPS) * gain).astype(BF16)


def _inproj_kernel(tab_ref, h_ref, wt_ref, proj_ref):
    del tab_ref
    proj_ref[...] = _dot_nt(h_ref[...], wt_ref[...].astype(BF16)).astype(BF16)


def _inproj(h, w_t):
    m = h.shape[0]
    tm = min(INPROJ_TM, m)
    starts = [s + d for s in MAIN_START for d in range(0, BLK, INPROJ_TN)]
    tab = jnp.asarray(np.array([s // ROW_ALIGN for s in starts], np.int32))
    return pl.pallas_call(
        _inproj_kernel,
        grid_spec=pltpu.PrefetchScalarGridSpec(
            num_scalar_prefetch=1,
            grid=(m // tm, len(starts)),
            in_specs=[
                pl.BlockSpec((tm, D_MODEL), lambda i, n, tab: (i, 0)),
                pl.BlockSpec((pl.Element(INPROJ_TN), pl.Element(D_MODEL)),
                             lambda i, n, tab: (tab[n] * ROW_ALIGN, 0)),
            ],
            out_specs=pl.BlockSpec((tm, INPROJ_TN), lambda i, n, tab: (i, n)),
        ),
        out_shape=jax.ShapeDtypeStruct((m, NB_MAIN * BLK), BF16),
        compiler_params=_cparams(("parallel", "arbitrary")),
        name="inproj",
    )(tab, h, w_t)


def _kvprep_kernel(x_ref, g_ref, wkv_ref, widx_ref, pos_ref, fc_ref, gk_ref, lng_ref, lnb_ref,
                   h_ref, k_ref, vt_ref, iklo_ref, ikhi_ref, cb_ref, sb_ref, ci_ref, si_ref, iwt_ref,
                   wkv_bf_ref, widx_bf_ref):
    @pl.when(jnp.logical_and(pl.program_id(0) == 0, pl.program_id(1) == 0))
    def _():
        wkv_bf_ref[...] = wkv_ref[...].astype(BF16)
        widx_bf_ref[...] = widx_ref[...].astype(BF16)

    h = _rms_rows(x_ref[...], g_ref[...])
    h_ref[...] = h
    kv = _dot_nt(h, wkv_bf_ref[...])
    ikp = _dot_nt(h, widx_bf_ref[...])

    hb, hi = B_HEAD_DIM // 8, IDX_DIM // 8
    ang = pos_ref[0].astype(F32) * fc_ref[...]
    cos_c, sin_c = jnp.cos(ang), jnp.sin(ang)
    lane = lax.broadcasted_iota(jnp.int32, ang.shape, 1)
    cos_b = jnp.where(lane < hb, cos_c, jnp.where(lane < 2 * hb, pltpu.roll(cos_c, hb, 1), 1.0))
    sin_b = jnp.where(lane < hb, -sin_c, jnp.where(lane < 2 * hb, pltpu.roll(sin_c, hb, 1), 0.0))
    cos_i = jnp.ones_like(cos_c)
    sin_i = jnp.zeros_like(sin_c)
    for head in range(LANES // IDX_DIM):
        for part, sign in enumerate((-1.0, 1.0)):
            first = head * IDX_DIM + part * hi
            here = jnp.logical_and(lane >= first, lane < first + hi)
            shift = (first - hb) % LANES
            cos_i = jnp.where(here, pltpu.roll(cos_c, shift, 1), cos_i)
            sin_i = jnp.where(here, sign * pltpu.roll(sin_c, shift, 1), sin_i)
    cb_ref[0] = cos_b
    sb_ref[0] = sin_b
    ci_ref[0] = cos_i
    si_ref[0] = sin_i

    for n in range(B_KV_HEADS):
        kh = kv[:, n * B_HEAD_DIM:(n + 1) * B_HEAD_DIM]
        r = lax.rsqrt(jnp.mean(kh * kh, axis=-1, keepdims=True) + EPS)
        kn = kh * r * gk_ref[...]
        k_ref[0, n] = _rope(kn, cos_b, sin_b, B_HEAD_DIM // 8, LANES).astype(BF16)
        vt = kv[:, (B_KV_HEADS + n) * B_HEAD_DIM:(B_KV_HEADS + n + 1) * B_HEAD_DIM].T
        for c in range(vt.shape[1] // CB):
            vt_ref[0, n, c] = vt[:, c * CB:(c + 1) * CB].astype(BF16)

    lane = lax.broadcasted_iota(jnp.int32, ikp.shape, 1)
    live = lane < IDX_DIM
    mu = jnp.sum(jnp.where(live, ikp, 0.0), axis=-1, keepdims=True) * (1.0 / IDX_DIM)
    d = jnp.where(live, ikp - mu, 0.0)
    var = jnp.sum(d * d, axis=-1, keepdims=True) * (1.0 / IDX_DIM)
    y = d * lax.rsqrt(var + EPS) * lng_ref[...] + lnb_ref[...]
    yr = jnp.where(live, _rope(y, cos_i, sin_i, IDX_DIM // 8, IDX_DIM), 0.0)
    iklo_ref[0] = yr.astype(BF16)
    ikhi_ref[0] = pltpu.roll(yr, IDX_DIM, 1).astype(BF16)

    iw = ikp * (IDX_DIM ** -0.5 * IDX_HEADS ** -0.5)
    iwt_ref[0] = iw.T[IDX_DIM:IDX_DIM + IDX_HEADS, :]


def _kvprep(x3, gain, w_t, pos3, freqs, gk, lng, lnb):
    bsz, s, _ = x3.shape
    tp = min(KVPREP_TP, s)
    row = lambda b, i: (b, i, 0)
    const = lambda b, i: (0, 0)
    tab_spec = pl.BlockSpec((1, LANES), const)
    out_h = pl.BlockSpec((tp, D_MODEL), lambda b, i: (b * (s // tp) + i, 0))
    out_tok = pl.BlockSpec((1, tp, LANES), row)
    out_k = pl.BlockSpec((1, B_KV_HEADS, tp, B_HEAD_DIM), lambda b, i: (b, 0, i, 0))
    out_vt = pl.BlockSpec((1, B_KV_HEADS, tp // CB, B_HEAD_DIM, CB), lambda b, i: (b, 0, i, 0, 0))
    out_iwt = pl.BlockSpec((1, IDX_HEADS, tp), lambda b, i: (b, 0, i))
    return pl.pallas_call(
        _kvprep_kernel,
        grid=(bsz, s // tp),
        in_specs=[pl.BlockSpec((None, tp, D_MODEL), row), pl.BlockSpec((1, D_MODEL), const),
                  pl.BlockSpec((SMALL_KV_WIDTH, D_MODEL), lambda b, i: (SMALL_KV_START // SMALL_KV_WIDTH, 0)),
                  pl.BlockSpec((LANES, D_MODEL), lambda b, i: (SMALL_IDX_START // LANES, 0)),
                  pl.BlockSpec((1, tp, 1), row),
                  tab_spec, tab_spec, tab_spec, tab_spec],
        out_specs=[out_h, out_k, out_vt, out_tok, out_tok, out_tok, out_tok, out_tok, out_tok, out_iwt],
        out_shape=[
            jax.ShapeDtypeStruct((bsz * s, D_MODEL), BF16),
            jax.ShapeDtypeStruct((bsz, B_KV_HEADS, s, B_HEAD_DIM), BF16),
            jax.ShapeDtypeStruct((bsz, B_KV_HEADS, s // CB, B_HEAD_DIM, CB), BF16),
            jax.ShapeDtypeStruct((bsz, s, LANES), BF16),
            jax.ShapeDtypeStruct((bsz, s, LANES), BF16),
            jax.ShapeDtypeStruct((bsz, s, LANES), F32),
            jax.ShapeDtypeStruct((bsz, s, LANES), F32),
            jax.ShapeDtypeStruct((bsz, s, LANES), F32),
            jax.ShapeDtypeStruct((bsz, s, LANES), F32),
            jax.ShapeDtypeStruct((bsz, IDX_HEADS, s), F32),
        ],
        scratch_shapes=[pltpu.VMEM((SMALL_KV_WIDTH, D_MODEL), BF16), pltpu.VMEM((LANES, D_MODEL), BF16)],
        compiler_params=_cparams(("arbitrary", "arbitrary")),
        name="kvprep",
    )(x3, gain, w_t, w_t, pos3, freqs, gk, lng, lnb)


def _gelu_tanh(x):
    c = -2.0 * (2.0 / np.pi) ** 0.5 * LOG2E
    return x / (1.0 + jnp.exp2(x * (x * x * (0.044715 * c) + c)))


def _gmlp_kernel(u_ref, v_ref, z_ref, lng_ref, lnb_ref, ws_ref, sbt_ref, o_ref):
    tm = u_ref.shape[0]
    u = _gelu_tanh(u_ref[...].astype(F32))
    v = _gelu_tanh(v_ref[...].astype(F32))
    mu = jnp.mean(v, axis=-1, keepdims=True)
    d = v - mu
    var = jnp.mean(d * d, axis=-1, keepdims=True)
    vn = (d * lax.rsqrt(var + EPS) * lng_ref[...] + lnb_ref[...]).astype(BF16)
    gate = u * jax.nn.silu(z_ref[...].astype(F32))
    tri = (lax.broadcasted_iota(jnp.int32, (CHUNK, CHUNK), 1)
           <= lax.broadcasted_iota(jnp.int32, (CHUNK, CHUNK), 0))
    for g in range(A_GROUPS):
        wg = jnp.where(tri, ws_ref[g], 0.0).astype(BF16)
        bias = sbt_ref[:, g:g + 1]
        cols = slice(g * A_GROUP_DIM, (g + 1) * A_GROUP_DIM)
        for c in range(tm // CHUNK):
            rows = slice(c * CHUNK, (c + 1) * CHUNK)
            sg = jnp.dot(wg, vn[rows, cols], preferred_element_type=F32) + bias
            o_ref[rows, cols] = (gate[rows, cols] * sg).astype(BF16)


def _gmlp(proj, lng, lnb, ws, sbt):
    m = proj.shape[0]
    tm = min(512, m)
    col = lambda c: pl.BlockSpec((tm, BLK), lambda i: (i, c))
    full = lambda shape: pl.BlockSpec(shape, lambda i: (0,) * len(shape))
    return pl.pallas_call(
        _gmlp_kernel,
        grid=(m // tm,),
        in_specs=[col(COL_AU), col(COL_AV), col(COL_AZ), full((1, A_WIDTH)), full((1, A_WIDTH)),
                  full((A_GROUPS, CHUNK, CHUNK)), full((CHUNK, A_GROUPS))],
        out_specs=pl.BlockSpec((tm, A_WIDTH), lambda i: (i, 0)),
        out_shape=jax.ShapeDtypeStruct((m, A_WIDTH), BF16),
        compiler_params=_cparams(("parallel",)),
        name="gmlp",
    )(proj, proj, proj, lng, lnb, ws, sbt)


def _memkv_kernel(mem_ref, g_ref, w_ref, gk_ref, km_ref, vm_ref, wbf_ref):
    @pl.when(pl.program_id(0) == 0)
    def _():
        wbf_ref[...] = w_ref[...].astype(BF16)

    h = _rms_rows(mem_ref[0], g_ref[...])
    kv = jnp.dot(h, wbf_ref[...], preferred_element_type=F32)
    for hd in range(M_HEADS):
        kh = kv[:, hd * M_HEAD_DIM:(hd + 1) * M_HEAD_DIM]
        r = lax.rsqrt(jnp.mean(kh * kh, axis=-1, keepdims=True) + EPS)
        km_ref[0, hd] = (kh * r * gk_ref[...]).astype(BF16)
        vm_ref[0, hd] = kv[:, M_WIDTH + hd * M_HEAD_DIM:M_WIDTH + (hd + 1) * M_HEAD_DIM].astype(BF16)


def _memkv(mem, gain, w_kv, gk):
    bsz, ml, _ = mem.shape
    out = pl.BlockSpec((1, M_HEADS, ml, M_HEAD_DIM), lambda b: (b, 0, 0, 0))
    shp = jax.ShapeDtypeStruct((bsz, M_HEADS, ml, M_HEAD_DIM), BF16)
    return pl.pallas_call(
        _memkv_kernel,
        grid=(bsz,),
        in_specs=[pl.BlockSpec((1, ml, D_MODEL), lambda b: (b, 0, 0)),
                  pl.BlockSpec((1, D_MODEL), lambda b: (0, 0)),
                  pl.BlockSpec((D_MODEL, 2 * M_WIDTH), lambda b: (0, 0), pipeline_mode=pl.Buffered(1)),
                  pl.BlockSpec((1, M_HEAD_DIM), lambda b: (0, 0))],
        out_specs=[out, out],
        out_shape=[shp, shp],
        scratch_shapes=[pltpu.VMEM((D_MODEL, 2 * M_WIDTH), BF16)],
        compiler_params=_cparams(("arbitrary",)),
        name="memkv",
    )(mem, gain, w_kv, gk)


def _memattn_kernel(q_ref, z_ref, km_ref, vm_ref, gq_ref, o_ref):
    qscale = M_HEAD_DIM ** -0.5 * LOG2E
    for hd in range(M_HEADS):
        cols = slice(hd * M_HEAD_DIM, (hd + 1) * M_HEAD_DIM)
        q = q_ref[:, cols].astype(F32)
        r = lax.rsqrt(jnp.mean(q * q, axis=-1, keepdims=True) + EPS)
        qn = (q * r * gq_ref[...] * qscale).astype(BF16)
        lg = _dot_nt(qn, km_ref[0, hd])
        p = jnp.exp2(lg - jnp.max(lg, axis=-1, keepdims=True))
        l = jnp.sum(p, axis=-1, keepdims=True)
        o = jnp.dot(p.astype(BF16), vm_ref[0, hd], preferred_element_type=F32) / l
        o_ref[:, cols] = (o * jax.nn.silu(z_ref[:, cols].astype(F32))).astype(BF16)


def _memattn(proj, km, vm, gq, s):
    m = proj.shape[0]
    tm = min(512, s)
    per_b = s // tm
    ml = km.shape[2]
    kv_spec = pl.BlockSpec((1, M_HEADS, ml, M_HEAD_DIM), lambda i: (i // per_b, 0, 0, 0))
    return pl.pallas_call(
        _memattn_kernel,
        grid=(m // tm,),
        in_specs=[pl.BlockSpec((tm, BLK), lambda i: (i, COL_MQ)),
                  pl.BlockSpec((tm, BLK), lambda i: (i, COL_MZ)),
                  kv_spec, kv_spec,
                  pl.BlockSpec((1, M_HEAD_DIM), lambda i: (0, 0))],
        out_specs=pl.BlockSpec((tm, M_WIDTH), lambda i: (i, 0)),
        out_shape=jax.ShapeDtypeStruct((m, M_WIDTH), BF16),
        compiler_params=_cparams(("parallel",)),
        name="memattn",
    )(proj, proj, km, vm, gq)


def _dsa_kernel(topk, q_ref, z_ref, iq_ref, cb_ref, sb_ref, ci_ref, si_ref, iwt_ref,
                k_ref, vt_ref, iklo_ref, ikhi_ref, gq_ref, tri_ref, bound_ref, o_ref,
                sc_ref, bias_ref, lg_ref, qs_ref, iqs_ref, acc_ref):
    qb = pl.program_id(1)
    nck = (qb * TQ + TQ + CK - 1) // CK
    nbig = (qb * TQ + TQ + CB - 1) // CB
    t_lane = qb * TQ + lax.broadcasted_iota(jnp.int32, (1, TQ), 1)
    gw = B_GROUP * TQ

    def fold8(a):
        return a.reshape(a.shape[0] // 8, 8, a.shape[1])

    def grouped_loop(trips, body, init, per_step=2):
        carry, done = init, 0
        while per_step >= 1:
            def step(i, c, first=done, n=per_step):
                for u in range(n):
                    c = body(first + n * i + u, c)
                return c

            steps = (trips - done) // per_step
            carry = lax.fori_loop(0, steps, step, carry)
            done = done + steps * per_step
            per_step //= 2
        return carry

    cos_b, sin_b = cb_ref[0], sb_ref[0]
    cos_i, sin_i = ci_ref[0], si_ref[0]
    qscale = B_HEAD_DIM ** -0.5 * LOG2E
    mean_mat = jnp.full((B_HEAD_DIM, B_HEAD_DIM), 1.0 / B_HEAD_DIM, BF16)
    gain = gq_ref[...] * qscale
    for h in range(B_HEADS):
        slab = q_ref[:, h * B_HEAD_DIM:(h + 1) * B_HEAD_DIM].astype(F32)
        sq = slab * slab
        sq_hi = sq.astype(BF16)
        sq_lo = (sq - sq_hi.astype(F32)).astype(BF16)
        msq = (jnp.dot(sq_hi, mean_mat, preferred_element_type=F32)
               + jnp.dot(sq_lo, mean_mat, preferred_element_type=F32))
        qr = _rope(slab * lax.rsqrt(msq + EPS) * gain, cos_b, sin_b, B_HEAD_DIM // 8, LANES)
        g = h % B_GROUP
        qs_ref[h // B_GROUP, g * TQ:(g + 1) * TQ, :] = qr.astype(BF16)
    for j in range(IDX_HEADS // 2):
        slab = iq_ref[:, j * LANES:(j + 1) * LANES].astype(F32)
        iqs_ref[j // 2, (j % 2) * TQ:(j % 2 + 1) * TQ, :] = (
            _rope(slab, cos_i, sin_i, IDX_DIM // 8, IDX_DIM).astype(BF16))
    wt = iwt_ref[0]

    def idx_body(c, carry):
        mn8, mx8, s1, s2 = carry
        off = pl.multiple_of(c * CK, CK)
        acc = jnp.zeros((CK, TQ), F32)
        for jj in range(IDX_HEADS // 4):
            rhs = iqs_ref[jj]
            for half, keys_ref in enumerate((iklo_ref, ikhi_ref)):
                d = _dot_nt(keys_ref[0, pl.ds(off, CK), :], rhs)
                ha, hb = 4 * jj + half, 4 * jj + 2 + half
                acc = (acc + jnp.maximum(d[:, :TQ], 0.0) * wt[ha:ha + 1, :]
                       + jnp.maximum(d[:, TQ:], 0.0) * wt[hb:hb + 1, :])
        key = off + lax.broadcasted_iota(jnp.int32, (CK, TQ), 0)
        causal = key <= t_lane
        sc = jnp.where(causal, acc, -jnp.inf)
        sc_ref[pl.ds(off, CK), :] = sc
        live = jnp.where(causal, acc, 0.0)
        mn8 = jnp.minimum(mn8, jnp.min(fold8(jnp.where(causal, acc, jnp.inf)), axis=0))
        mx8 = jnp.maximum(mx8, jnp.max(fold8(sc), axis=0))
        s1 = s1 + jnp.sum(fold8(live), axis=0)
        s2 = s2 + jnp.sum(fold8(live * live), axis=0)
        return mn8, mx8, s1, s2

    zero8 = jnp.zeros((8, TQ), F32)
    stats = (jnp.full((8, TQ), jnp.inf, F32), jnp.full((8, TQ), -jnp.inf, F32), zero8, zero8)
    mn8, mx8, s1, s2 = grouped_loop(nck, idx_body, stats, per_step=4)
    row_min = jnp.min(mn8, axis=0, keepdims=True)
    row_max = jnp.max(mx8, axis=0, keepdims=True)

    def fill_body(c, carry):
        sc_ref[pl.ds(pl.multiple_of(c * CK, CK), CK), :] = jnp.full((CK, TQ), -jnp.inf, F32)
        return carry

    lax.fori_loop(nck, nbig * (CB // CK), fill_body, 0)

    def count_rows(pred):
        rows = 8 * COUNT_WAYS

        def body(c, cnt):
            off = pl.multiple_of(c * CB, CB)
            for r in range(CB // rows):
                cnt = cnt + jnp.where(pred(sc_ref[pl.ds(off + r * rows, rows), :]), 1.0, 0.0)
            return cnt

        cnt = lax.fori_loop(0, nbig, body, jnp.zeros((rows, TQ), F32))
        return jnp.sum(cnt, axis=0, keepdims=True)

    kf = float(topk)
    n_valid = (t_lane + 1).astype(F32)
    all_rows = (t_lane + 1) <= topk

    def search_pass(st, probe, stuck):
        lo, hi, clo, chi, thr, fin, tie = st
        cnt = count_rows(lambda blk: blk >= probe)
        active = fin < 0.5
        hit = cnt == kf
        end_thr = jnp.where(jnp.logical_and(stuck, cnt < kf), lo, probe)
        ends = jnp.logical_and(active, jnp.logical_or(hit, stuck))
        thr = jnp.where(ends, end_thr, thr)
        tie = jnp.where(jnp.logical_and(ends, jnp.logical_not(hit)), 1.0, tie)
        fin = jnp.where(ends, 1.0, fin)
        up = jnp.logical_and(active, cnt > kf)
        dn = jnp.logical_and(active, cnt < kf)
        return (jnp.where(up, probe, lo), jnp.where(dn, probe, hi), jnp.where(up, cnt, clo),
                jnp.where(dn, cnt, chi), thr, fin, tie)

    def next_probe(st, halve=False):
        lo, hi, clo, chi = st[:4]
        frac = jnp.clip((clo - kf + 0.5) / (clo - chi + 1.0), SEARCH_MARGIN, 1.0 - SEARCH_MARGIN)
        probe = lo + (hi - lo) * frac
        outside = jnp.logical_or(jnp.logical_or(probe <= lo, probe >= hi), halve)
        probe = jnp.where(outside, lo * 0.5 + hi * 0.5, probe)
        stuck = jnp.logical_or(probe <= lo, probe >= hi)
        return jnp.where(stuck, hi, probe), stuck

    mean = jnp.sum(s1, axis=0, keepdims=True) / n_valid
    var = jnp.maximum(jnp.sum(s2, axis=0, keepdims=True) / n_valid - mean * mean, 0.0)
    tail = jnp.clip(kf / n_valid, 1e-6, 1.0 - 1e-6)
    tq = jnp.sqrt(-2.0 * jnp.log(jnp.minimum(tail, 1.0 - tail)))
    zq = tq - ((0.010328 * tq + 0.802853) * tq + 2.515517) / (
        ((0.001308 * tq + 0.189269) * tq + 1.432788) * tq + 1.0)
    guess = mean + jnp.where(tail < 0.5, zq, -zq) * jnp.sqrt(var)
    inside = jnp.logical_and(guess >= row_min, guess <= row_max)
    probe0 = jnp.where(inside, guess, row_min * 0.5 + row_max * 0.5)

    ones = jnp.ones((1, TQ), F32)
    st = (row_min, row_max, n_valid, 0.0 * ones, jnp.where(all_rows, -F32_MAX, row_max),
          jnp.where(all_rows, 1.0, 0.0), 0.0 * ones)
    st = search_pass(st, probe0, probe0 < row_min)

    def fixed_body(i, st):
        return search_pass(st, *next_probe(st))

    st = lax.fori_loop(0, SEARCH_FIXED_PASSES, fixed_body, st)

    def more_cond(c):
        return jnp.logical_and(c[0] < MAX_SEARCH_ITERS, jnp.min(c[1][5]) < 0.5)

    def more_body(c):
        return c[0] + 1, search_pass(c[1], *next_probe(c[1], c[0] >= SEARCH_INTERP_PASSES))

    _, st = lax.while_loop(more_cond, more_body, (jnp.int32(0), st))
    thr, tie = st[4], st[6]

    def plain_mask():
        def body(c, carry):
            off = pl.multiple_of(c * CB, CB)
            bias_ref[pl.ds(off, CB), :] = jnp.where(sc_ref[pl.ds(off, CB), :] >= thr, 0.0, NEG_BIG)
            return carry

        lax.fori_loop(0, nbig, body, 0)

    def tied_mask():
        n_gt = count_rows(lambda blk: blk > thr)
        need = jnp.where(all_rows, F32_MAX, kf - n_gt)

        def body(c, seen):
            off = pl.multiple_of(c * CB, CB)
            blk = sc_ref[pl.ds(off, CB), :]
            eq = blk == thr
            pref = jnp.dot(tri_ref[...], jnp.where(eq, 1.0, 0.0).astype(BF16),
                           preferred_element_type=F32) + seen
            keep = jnp.logical_or(blk > thr, jnp.logical_and(eq, pref <= need))
            bias_ref[pl.ds(off, CB), :] = jnp.where(keep, 0.0, NEG_BIG)
            return pref[CB - 1:CB, :]

        lax.fori_loop(0, nbig, body, jnp.zeros((1, TQ), F32))

    lax.cond(jnp.max(tie) > 0.5, tied_mask, plain_mask)

    logit_bound = bound_ref[0, 0]

    def masked_logits(n, off):
        b = bias_ref[pl.ds(off, CB), :]
        lg = _dot_nt(k_ref[0, n, pl.ds(off, CB), :], qs_ref[n])
        return jnp.concatenate([lg[:, g * TQ:(g + 1) * TQ] + b for g in range(B_GROUP)], axis=1)

    for n in range(B_KV_HEADS):
        acc_ref[...] = jnp.zeros(acc_ref.shape, F32)

        def one_pass(n=n):
            def body(c, l8):
                off = pl.multiple_of(c * CB, CB)
                p = jnp.exp2(masked_logits(n, off) - logit_bound)
                acc_ref[...] += jnp.dot(vt_ref[0, n, c], p.astype(BF16), preferred_element_type=F32)
                return l8 + jnp.sum(fold8(p), axis=0)

            return grouped_loop(nbig, body, jnp.zeros((8, gw), F32), per_step=4)

        def two_pass(n=n):
            def logit_body(c, mx8):
                off = pl.multiple_of(c * CB, CB)
                lg = masked_logits(n, off)
                lg_ref[pl.ds(off, CB), :] = lg
                return jnp.maximum(mx8, jnp.max(fold8(lg), axis=0))

            mx8 = grouped_loop(nbig, logit_body, jnp.full((8, gw), NEG_BIG, F32))
            m = jnp.max(mx8, axis=0, keepdims=True)

            def pv_body(c, l8):
                off = pl.multiple_of(c * CB, CB)
                p = jnp.exp2(lg_ref[pl.ds(off, CB), :] - m)
                acc_ref[...] += jnp.dot(vt_ref[0, n, c], p.astype(BF16), preferred_element_type=F32)
                return l8 + jnp.sum(fold8(p), axis=0)

            return grouped_loop(nbig, pv_body, jnp.zeros((8, gw), F32))

        l8 = lax.cond(logit_bound <= MAX_SHIFT_BOUND, one_pass, two_pass)
        o_t = acc_ref[...] / jnp.sum(l8, axis=0, keepdims=True)
        for g in range(B_GROUP):
            cols = slice((n * B_GROUP + g) * B_HEAD_DIM, (n * B_GROUP + g + 1) * B_HEAD_DIM)
            o = o_t[:, g * TQ:(g + 1) * TQ].T
            o_ref[:, cols] = (o * jax.nn.silu(z_ref[:, cols].astype(F32))).astype(BF16)


def _dsa(proj, tabs, iwt, k, vt, iklo, ikhi, gq, gk, bsz, s):
    nq = s // TQ
    bound = (1.02 * B_HEAD_DIM ** 0.5 * LOG2E * jnp.max(jnp.abs(gq)) * jnp.max(jnp.abs(gk))).reshape(1, 1)
    topk = min(TOPK_MAX, s // 4)
    gw = B_GROUP * TQ
    rowblk = lambda c: pl.BlockSpec((TQ, BLK), lambda b, i: (b * nq + i, c))
    tok = pl.BlockSpec((1, TQ, LANES), lambda b, i: (b, i, 0))
    ik = pl.BlockSpec((1, s, LANES), lambda b, i: (b, 0, 0))
    tri = jnp.tril(jnp.ones((CB, CB), BF16))
    return pl.pallas_call(
        functools.partial(_dsa_kernel, topk),
        grid=(bsz, nq),
        in_specs=[rowblk(COL_BQ), rowblk(COL_BZ), rowblk(COL_IQ), tok, tok, tok, tok,
                  pl.BlockSpec((1, IDX_HEADS, TQ), lambda b, i: (b, 0, i)),
                  pl.BlockSpec((1, B_KV_HEADS, s, B_HEAD_DIM), lambda b, i: (b, 0, 0, 0)),
                  pl.BlockSpec((1, B_KV_HEADS, s // CB, B_HEAD_DIM, CB), lambda b, i: (b, 0, 0, 0, 0)),
                  ik, ik,
                  pl.BlockSpec((1, B_HEAD_DIM), lambda b, i: (0, 0)),
                  pl.BlockSpec((CB, CB), lambda b, i: (0, 0)),
                  pl.BlockSpec(memory_space=pltpu.SMEM)],
        out_specs=pl.BlockSpec((TQ, B_WIDTH), lambda b, i: (b * nq + i, 0)),
        out_shape=jax.ShapeDtypeStruct((bsz * s, B_WIDTH), BF16),
        scratch_shapes=[
            pltpu.VMEM((s, TQ), F32),
            pltpu.VMEM((s, TQ), F32),
            pltpu.VMEM((s, gw), F32),
            pltpu.VMEM((B_KV_HEADS, gw, B_HEAD_DIM), BF16),
            pltpu.VMEM((IDX_HEADS // 4, 2 * TQ, LANES), BF16),
            pltpu.VMEM((B_HEAD_DIM, gw), F32),
        ],
        compiler_params=_cparams(("parallel", "arbitrary")),
        name="dsa",
    )(proj, proj, proj, *tabs, iwt, k, vt, iklo, ikhi, gq, tri, bound)


def _merge_kernel(ta_ref, tb_ref, tm_ref, ga_ref, gb_ref, gm_ref, x_ref,
                  wa_hbm, wb_hbm, wm_hbm, wo_hbm, o_ref,
                  wa_ref, wb_ref, wm_ref, wo_ref, stage_ref, sem):
    @pl.when(pl.program_id(0) == 0)
    def _():
        chunks = [(src, dst, r) for src, dst in ((wa_hbm, wa_ref), (wb_hbm, wb_ref), (wm_hbm, wm_ref),
                                                 (wo_hbm, wo_ref))
                  for r in range(0, src.shape[0], MERGE_STAGE_ROWS)]

        def copy(j):
            src, _, r = chunks[j]
            return pltpu.make_async_copy(src.at[pl.ds(r, MERGE_STAGE_ROWS), :], stage_ref.at[j % 2],
                                         sem.at[j % 2])

        copy(0).start()
        for j, (_, dst, r) in enumerate(chunks):
            if j + 1 < len(chunks):
                copy(j + 1).start()
            copy(j).wait()
            dst[pl.ds(r, MERGE_STAGE_ROWS), :] = stage_ref[j % 2].astype(BF16)

    def branch(t_ref, g_ref, w_ref):
        y = jnp.dot(t_ref[...], w_ref[...], preferred_element_type=F32)
        return jax.nn.sigmoid(g_ref[...].astype(F32)) * y

    merged = branch(ta_ref, ga_ref, wa_ref) + branch(tb_ref, gb_ref, wb_ref) + branch(tm_ref, gm_ref, wm_ref)
    o_ref[...] = x_ref[...] + jnp.dot(merged.astype(BF16), wo_ref[...], preferred_element_type=F32)


def _merge(ta, tb, tmem, proj, x2, wa, wb, wm, wo):
    m = x2.shape[0]
    tm = min(256, m)
    act = pl.BlockSpec((tm, BLK), lambda i: (i, 0))
    gate = lambda c: pl.BlockSpec((tm, D_MODEL), lambda i: (i, c // 2))
    wide = pl.BlockSpec((tm, D_MODEL), lambda i: (i, 0))
    hbm = pl.BlockSpec(memory_space=pl.ANY)
    return pl.pallas_call(
        _merge_kernel,
        grid=(m // tm,),
        in_specs=[act, act, act, gate(COL_GA), gate(COL_GB), gate(COL_GM), wide, hbm, hbm, hbm, hbm],
        out_specs=wide,
        out_shape=jax.ShapeDtypeStruct((m, D_MODEL), F32),
        scratch_shapes=[pltpu.VMEM(w.shape, BF16) for w in (wa, wb, wm, wo)]
        + [pltpu.VMEM((2, MERGE_STAGE_ROWS, D_MODEL), F32), pltpu.SemaphoreType.DMA((2,))],
        compiler_params=_cparams(("arbitrary",)),
        name="merge",
    )(ta, tb, tmem, proj, proj, proj, x2, wa, wb, wm, wo)


def _rope_freqs():
    inv = lambda half: ROPE_THETA ** (-np.arange(half, dtype=np.float32) / half)
    freq = np.zeros((1, LANES), np.float32)
    hb, hi = B_HEAD_DIM // 8, IDX_DIM // 8
    freq[0, :hb] = inv(hb)
    freq[0, hb:hb + hi] = inv(hi)
    return jnp.asarray(freq)


def _layer(x, mem, positions, norm_gain, w_in, gmlp_ln_gain, gmlp_ln_bias, spatial_w, spatial_b,
           w_branch_a, q_norm_gain, k_norm_gain, idx_k_ln_gain, idx_k_ln_bias, w_branch_b,
           mem_norm_gain, w_mem_kv, mem_q_norm_gain, mem_k_norm_gain, w_branch_m, w_out):
    bsz, s, _ = x.shape
    m = bsz * s
    row = lambda a: a.reshape(1, -1).astype(F32)
    pad_lanes = lambda a: jnp.pad(a.reshape(1, -1).astype(F32), ((0, 0), (0, LANES - a.shape[-1])))
    x2 = x.reshape(m, D_MODEL)

    w_t = w_in.T
    h, k, vt, iklo, ikhi, cb, sb, ci, si, iwt = _kvprep(
        x, row(norm_gain), w_t, positions.reshape(bsz, s, 1).astype(jnp.int32), _rope_freqs(),
        row(k_norm_gain), pad_lanes(idx_k_ln_gain), pad_lanes(idx_k_ln_bias))
    proj = _inproj(h, w_t)

    t_a = _gmlp(proj, row(gmlp_ln_gain), row(gmlp_ln_bias), spatial_w.astype(F32), spatial_b.T.astype(F32))
    km, vm = _memkv(mem, row(mem_norm_gain), w_mem_kv, row(mem_k_norm_gain))
    t_m = _memattn(proj, km, vm, row(mem_q_norm_gain), s)
    t_b = _dsa(proj, (cb, sb, ci, si), iwt, k, vt, iklo, ikhi, row(q_norm_gain), row(k_norm_gain), bsz, s)

    out = _merge(t_a, t_b, t_m, proj, x2, w_branch_a, w_branch_b, w_branch_m, w_out)
    return out.reshape(bsz, s, D_MODEL)


def kernel(x, mem, positions, norm_gain, w_in, gmlp_ln_gain, gmlp_ln_bias, spatial_w, spatial_b, w_branch_a, q_norm_gain, k_norm_gain, idx_k_ln_gain, idx_k_ln_bias, w_branch_b, mem_norm_gain, w_mem_kv, mem_q_norm_gain, mem_k_norm_gain, w_branch_m, w_out):
    for l in range(norm_gain.shape[0]):
        x = _layer(x, mem, positions, norm_gain[l], w_in[l], gmlp_ln_gain[l], gmlp_ln_bias[l],
                   spatial_w[l], spatial_b[l], w_branch_a[l], q_norm_gain[l], k_norm_gain[l],
                   idx_k_ln_gain[l], idx_k_ln_bias[l], w_branch_b[l], mem_norm_gain[l], w_mem_kv[l],
                   mem_q_norm_gain[l], mem_k_norm_gain[l], w_branch_m[l], w_out[l])
    return x
```

```python
import functools

import numpy as np
import jax
import jax.numpy as jnp
from jax import lax
from jax.experimental import pallas as pl
from jax.experimental.pallas import tpu as pltpu

F32 = jnp.float32
BF16 = jnp.bfloat16

D_MODEL = 2048
ROPE_THETA = 500000.0
EPS = 1e-6
A_GROUPS = 8
A_GROUP_DIM = 128
A_WIDTH = A_GROUPS * A_GROUP_DIM
CHUNK = 128
B_HEADS = 8
B_KV_HEADS = 2
B_GROUP = B_HEADS // B_KV_HEADS
B_HEAD_DIM = 128
B_WIDTH = B_HEADS * B_HEAD_DIM
IDX_HEADS = 16
IDX_DIM = 64
TOPK_MAX = 256
M_HEADS = 4
M_HEAD_DIM = 256
M_WIDTH = M_HEADS * M_HEAD_DIM

SPLIT_SIZES = (
    A_WIDTH, A_WIDTH, A_WIDTH,
    B_WIDTH, B_KV_HEADS * B_HEAD_DIM, B_KV_HEADS * B_HEAD_DIM, B_WIDTH,
    IDX_HEADS * IDX_DIM, IDX_DIM, IDX_HEADS,
    M_WIDTH, M_WIDTH,
    D_MODEL, D_MODEL, D_MODEL,
)

LANES = 128
BLK = 1024
COL_AU, COL_AV, COL_AZ, COL_BQ, COL_BZ, COL_IQ, COL_MQ, COL_MZ = range(8)
COL_GA, COL_GB, COL_GM = 8, 10, 12
NB_MAIN = 14
_OFFS = [int(o) for o in np.concatenate([[0], np.cumsum(SPLIT_SIZES)])]
ROW_ALIGN = 16
MAIN_START = ([_OFFS[i] for i in (0, 1, 2, 3, 6, 7, 10, 11)]
              + [_OFFS[i] + d for i in (12, 13, 14) for d in (0, BLK)])
assert all(s % ROW_ALIGN == 0 for s in MAIN_START)
SMALL_KV_START, SMALL_KV_WIDTH = _OFFS[4], _OFFS[6] - _OFFS[4]
SMALL_IDX_START = _OFFS[8]
assert SMALL_KV_START % SMALL_KV_WIDTH == 0 and SMALL_IDX_START % LANES == 0
assert IDX_DIM + IDX_HEADS <= LANES and _OFFS[9] == SMALL_IDX_START + IDX_DIM

VMEM_LIMIT = 56 * 1024 * 1024
LOG2E = 1.4426950408889634
NEG_BIG = -1e30
F32_MAX = 3.4028234663852886e38

INPROJ_TM, INPROJ_TN = 2048, 1024
KVPREP_TP = 512
TQ = 256
CK = 256
CB = 512
COUNT_WAYS = 8
SEARCH_MARGIN = 0.05
SEARCH_FIXED_PASSES = 11
SEARCH_INTERP_PASSES = 13
MAX_SEARCH_ITERS = 400
MERGE_STAGE_ROWS = 512
MAX_SHIFT_BOUND = 48.0


def _cparams(sem):
    return pltpu.CompilerParams(dimension_semantics=sem, vmem_limit_bytes=VMEM_LIMIT)


def _dot_nt(a, b):
    return lax.dot_general(a, b, (((1,), (1,)), ((), ())), preferred_element_type=F32)


def _rope(x, cos_t, sin_t, half, period):
    lane = lax.broadcasted_iota(jnp.int32, x.shape, 1) % period
    rolled = jnp.where(lane < half, pltpu.roll(x, LANES - half, 1), pltpu.roll(x, half, 1))
    return x * cos_t + rolled * sin_t


def _rms_rows(x, gain):
    ms = jnp.mean(x * x, axis=-1, keepdims=True)
    return (x * lax.rsqrt(ms + EPS) * gain).astype(BF16)


def _inproj_kernel(tab_ref, h_ref, wt_ref, proj_ref):
    del tab_ref
    proj_ref[...] = _dot_nt(h_ref[...], wt_ref[...].astype(BF16)).astype(BF16)


def _inproj(h, w_t):
    m = h.shape[0]
    tm = min(INPROJ_TM, m)
    starts = [s + d for s in MAIN_START for d in range(0, BLK, INPROJ_TN)]
    tab = jnp.asarray(np.array([s // ROW_ALIGN for s in starts], np.int32))
    return pl.pallas_call(
        _inproj_kernel,
        grid_spec=pltpu.PrefetchScalarGridSpec(
            num_scalar_prefetch=1,
            grid=(m // tm, len(starts)),
            in_specs=[
                pl.BlockSpec((tm, D_MODEL), lambda i, n, tab: (i, 0)),
                pl.BlockSpec((pl.Element(INPROJ_TN), pl.Element(D_MODEL)),
                             lambda i, n, tab: (tab[n] * ROW_ALIGN, 0)),
            ],
            out_specs=pl.BlockSpec((tm, INPROJ_TN), lambda i, n, tab: (i, n)),
        ),
        out_shape=jax.ShapeDtypeStruct((m, NB_MAIN * BLK), BF16),
        compiler_params=_cparams(("parallel", "arbitrary")),
        name="inproj",
    )(tab, h, w_t)


def _kvprep_kernel(x_ref, g_ref, wkv_ref, widx_ref, pos_ref, fc_ref, gk_ref, lng_ref, lnb_ref,
                   h_ref, k_ref, vt_ref, iklo_ref, ikhi_ref, cb_ref, sb_ref, ci_ref, si_ref, iwt_ref,
                   wkv_bf_ref, widx_bf_ref):
    @pl.when(jnp.logical_and(pl.program_id(0) == 0, pl.program_id(1) == 0))
    def _():
        wkv_bf_ref[...] = wkv_ref[...].astype(BF16)
        widx_bf_ref[...] = widx_ref[...].astype(BF16)

    h = _rms_rows(x_ref[...], g_ref[...])
    h_ref[...] = h
    kv = _dot_nt(h, wkv_bf_ref[...])
    ikp = _dot_nt(h, widx_bf_ref[...])

    hb, hi = B_HEAD_DIM // 8, IDX_DIM // 8
    ang = pos_ref[0].astype(F32) * fc_ref[...]
    cos_c, sin_c = jnp.cos(ang), jnp.sin(ang)
    lane = lax.broadcasted_iota(jnp.int32, ang.shape, 1)
    cos_b = jnp.where(lane < hb, cos_c, jnp.where(lane < 2 * hb, pltpu.roll(cos_c, hb, 1), 1.0))
    sin_b = jnp.where(lane < hb, -sin_c, jnp.where(lane < 2 * hb, pltpu.roll(sin_c, hb, 1), 0.0))
    cos_i = jnp.ones_like(cos_c)
    sin_i = jnp.zeros_like(sin_c)
    for head in range(LANES // IDX_DIM):
        for part, sign in enumerate((-1.0, 1.0)):
            first = head * IDX_DIM + part * hi
            here = jnp.logical_and(lane >= first, lane < first + hi)
            shift = (first - hb) % LANES
            cos_i = jnp.where(here, pltpu.roll(cos_c, shift, 1), cos_i)
            sin_i = jnp.where(here, sign * pltpu.roll(sin_c, shift, 1), sin_i)
    cb_ref[0] = cos_b
    sb_ref[0] = sin_b
    ci_ref[0] = cos_i
    si_ref[0] = sin_i

    for n in range(B_KV_HEADS):
        kh = kv[:, n * B_HEAD_DIM:(n + 1) * B_HEAD_DIM]
        r = lax.rsqrt(jnp.mean(kh * kh, axis=-1, keepdims=True) + EPS)
        kn = kh * r * gk_ref[...]
        k_ref[0, n] = _rope(kn, cos_b, sin_b, B_HEAD_DIM // 8, LANES).astype(BF16)
        vt = kv[:, (B_KV_HEADS + n) * B_HEAD_DIM:(B_KV_HEADS + n + 1) * B_HEAD_DIM].T
        for c in range(vt.shape[1] // CB):
            vt_ref[0, n, c] = vt[:, c * CB:(c + 1) * CB].astype(BF16)

    lane = lax.broadcasted_iota(jnp.int32, ikp.shape, 1)
    live = lane < IDX_DIM
    mu = jnp.sum(jnp.where(live, ikp, 0.0), axis=-1, keepdims=True) * (1.0 / IDX_DIM)
    d = jnp.where(live, ikp - mu, 0.0)
    var = jnp.sum(d * d, axis=-1, keepdims=True) * (1.0 / IDX_DIM)
    y = d * lax.rsqrt(var + EPS) * lng_ref[...] + lnb_ref[...]
    yr = jnp.where(live, _rope(y, cos_i, sin_i, IDX_DIM // 8, IDX_DIM), 0.0)
    iklo_ref[0] = yr.astype(BF16)
    ikhi_ref[0] = pltpu.roll(yr, IDX_DIM, 1).astype(BF16)

    iw = ikp * (IDX_DIM ** -0.5 * IDX_HEADS ** -0.5)
    iwt_ref[0] = iw.T[IDX_DIM:IDX_DIM + IDX_HEADS, :]


def _kvprep(x3, gain, w_t, pos3, freqs, gk, lng, lnb):
    bsz, s, _ = x3.shape
    tp = min(KVPREP_TP, s)
    row = lambda b, i: (b, i, 0)
    const = lambda b, i: (0, 0)
    tab_spec = pl.BlockSpec((1, LANES), const)
    out_h = pl.BlockSpec((tp, D_MODEL), lambda b, i: (b * (s // tp) + i, 0))
    out_tok = pl.BlockSpec((1, tp, LANES), row)
    out_k = pl.BlockSpec((1, B_KV_HEADS, tp, B_HEAD_DIM), lambda b, i: (b, 0, i, 0))
    out_vt = pl.BlockSpec((1, B_KV_HEADS, tp // CB, B_HEAD_DIM, CB), lambda b, i: (b, 0, i, 0, 0))
    out_iwt = pl.BlockSpec((1, IDX_HEADS, tp), lambda b, i: (b, 0, i))
    return pl.pallas_call(
        _kvprep_kernel,
        grid=(bsz, s // tp),
        in_specs=[pl.BlockSpec((None, tp, D_MODEL), row), pl.BlockSpec((1, D_MODEL), const),
                  pl.BlockSpec((SMALL_KV_WIDTH, D_MODEL), lambda b, i: (SMALL_KV_START // SMALL_KV_WIDTH, 0)),
                  pl.BlockSpec((LANES, D_MODEL), lambda b, i: (SMALL_IDX_START // LANES, 0)),
                  pl.BlockSpec((1, tp, 1), row),
                  tab_spec, tab_spec, tab_spec, tab_spec],
        out_specs=[out_h, out_k, out_vt, out_tok, out_tok, out_tok, out_tok, out_tok, out_tok, out_iwt],
        out_shape=[
            jax.ShapeDtypeStruct((bsz * s, D_MODEL), BF16),
            jax.ShapeDtypeStruct((bsz, B_KV_HEADS, s, B_HEAD_DIM), BF16),
            jax.ShapeDtypeStruct((bsz, B_KV_HEADS, s // CB, B_HEAD_DIM, CB), BF16),
            jax.ShapeDtypeStruct((bsz, s, LANES), BF16),
            jax.ShapeDtypeStruct((bsz, s, LANES), BF16),
            jax.ShapeDtypeStruct((bsz, s, LANES), F32),
            jax.ShapeDtypeStruct((bsz, s, LANES), F32),
            jax.ShapeDtypeStruct((bsz, s, LANES), F32),
            jax.ShapeDtypeStruct((bsz, s, LANES), F32),
            jax.ShapeDtypeStruct((bsz, IDX_HEADS, s), F32),
        ],
        scratch_shapes=[pltpu.VMEM((SMALL_KV_WIDTH, D_MODEL), BF16), pltpu.VMEM((LANES, D_MODEL), BF16)],
        compiler_params=_cparams(("arbitrary", "arbitrary")),
        name="kvprep",
    )(x3, gain, w_t, w_t, pos3, freqs, gk, lng, lnb)


def _gelu_tanh(x):
    c = -2.0 * (2.0 / np.pi) ** 0.5 * LOG2E
    return x / (1.0 + jnp.exp2(x * (x * x * (0.044715 * c) + c)))


def _gmlp_kernel(u_ref, v_ref, z_ref, lng_ref, lnb_ref, ws_ref, sbt_ref, o_ref):
    tm = u_ref.shape[0]
    u = _gelu_tanh(u_ref[...].astype(F32))
    v = _gelu_tanh(v_ref[...].astype(F32))
    mu = jnp.mean(v, axis=-1, keepdims=True)
    d = v - mu
    var = jnp.mean(d * d, axis=-1, keepdims=True)
    vn = (d * lax.rsqrt(var + EPS) * lng_ref[...] + lnb_ref[...]).astype(BF16)
    gate = u * jax.nn.silu(z_ref[...].astype(F32))
    tri = (lax.broadcasted_iota(jnp.int32, (CHUNK, CHUNK), 1)
           <= lax.broadcasted_iota(jnp.int32, (CHUNK, CHUNK), 0))
    for g in range(A_GROUPS):
        wg = jnp.where(tri, ws_ref[g], 0.0).astype(BF16)
        bias = sbt_ref[:, g:g + 1]
        cols = slice(g * A_GROUP_DIM, (g + 1) * A_GROUP_DIM)
        for c in range(tm // CHUNK):
            rows = slice(c * CHUNK, (c + 1) * CHUNK)
            sg = jnp.dot(wg, vn[rows, cols], preferred_element_type=F32) + bias
            o_ref[rows, cols] = (gate[rows, cols] * sg).astype(BF16)


def _gmlp(proj, lng, lnb, ws, sbt):
    m = proj.shape[0]
    tm = min(512, m)
    col = lambda c: pl.BlockSpec((tm, BLK), lambda i: (i, c))
    full = lambda shape: pl.BlockSpec(shape, lambda i: (0,) * len(shape))
    return pl.pallas_call(
        _gmlp_kernel,
        grid=(m // tm,),
        in_specs=[col(COL_AU), col(COL_AV), col(COL_AZ), full((1, A_WIDTH)), full((1, A_WIDTH)),
                  full((A_GROUPS, CHUNK, CHUNK)), full((CHUNK, A_GROUPS))],
        out_specs=pl.BlockSpec((tm, A_WIDTH), lambda i: (i, 0)),
        out_shape=jax.ShapeDtypeStruct((m, A_WIDTH), BF16),
        compiler_params=_cparams(("parallel",)),
        name="gmlp",
    )(proj, proj, proj, lng, lnb, ws, sbt)


def _memkv_kernel(mem_ref, g_ref, w_ref, gk_ref, km_ref, vm_ref, wbf_ref):
    @pl.when(pl.program_id(0) == 0)
    def _():
        wbf_ref[...] = w_ref[...].astype(BF16)

    h = _rms_rows(mem_ref[0], g_ref[...])
    kv = jnp.dot(h, wbf_ref[...], preferred_element_type=F32)
    for hd in range(M_HEADS):
        kh = kv[:, hd * M_HEAD_DIM:(hd + 1) * M_HEAD_DIM]
        r = lax.rsqrt(jnp.mean(kh * kh, axis=-1, keepdims=True) + EPS)
        km_ref[0, hd] = (kh * r * gk_ref[...]).astype(BF16)
        vm_ref[0, hd] = kv[:, M_WIDTH + hd * M_HEAD_DIM:M_WIDTH + (hd + 1) * M_HEAD_DIM].astype(BF16)


def _memkv(mem, gain, w_kv, gk):
    bsz, ml, _ = mem.shape
    out = pl.BlockSpec((1, M_HEADS, ml, M_HEAD_DIM), lambda b: (b, 0, 0, 0))
    shp = jax.ShapeDtypeStruct((bsz, M_HEADS, ml, M_HEAD_DIM), BF16)
    return pl.pallas_call(
        _memkv_kernel,
        grid=(bsz,),
        in_specs=[pl.BlockSpec((1, ml, D_MODEL), lambda b: (b, 0, 0)),
                  pl.BlockSpec((1, D_MODEL), lambda b: (0, 0)),
                  pl.BlockSpec((D_MODEL, 2 * M_WIDTH), lambda b: (0, 0), pipeline_mode=pl.Buffered(1)),
                  pl.BlockSpec((1, M_HEAD_DIM), lambda b: (0, 0))],
        out_specs=[out, out],
        out_shape=[shp, shp],
        scratch_shapes=[pltpu.VMEM((D_MODEL, 2 * M_WIDTH), BF16)],
        compiler_params=_cparams(("arbitrary",)),
        name="memkv",
    )(mem, gain, w_kv, gk)


def _memattn_kernel(q_ref, z_ref, km_ref, vm_ref, gq_ref, o_ref):
    qscale = M_HEAD_DIM ** -0.5 * LOG2E
    for hd in range(M_HEADS):
        cols = slice(hd * M_HEAD_DIM, (hd + 1) * M_HEAD_DIM)
        q = q_ref[:, cols].astype(F32)
        r = lax.rsqrt(jnp.mean(q * q, axis=-1, keepdims=True) + EPS)
        qn = (q * r * gq_ref[...] * qscale).astype(BF16)
        lg = _dot_nt(qn, km_ref[0, hd])
        p = jnp.exp2(lg - jnp.max(lg, axis=-1, keepdims=True))
        l = jnp.sum(p, axis=-1, keepdims=True)
        o = jnp.dot(p.astype(BF16), vm_ref[0, hd], preferred_element_type=F32) / l
        o_ref[:, cols] = (o * jax.nn.silu(z_ref[:, cols].astype(F32))).astype(BF16)


def _memattn(proj, km, vm, gq, s):
    m = proj.shape[0]
    tm = min(512, s)
    per_b = s // tm
    ml = km.shape[2]
    kv_spec = pl.BlockSpec((1, M_HEADS, ml, M_HEAD_DIM), lambda i: (i // per_b, 0, 0, 0))
    return pl.pallas_call(
        _memattn_kernel,
        grid=(m // tm,),
        in_specs=[pl.BlockSpec((tm, BLK), lambda i: (i, COL_MQ)),
                  pl.BlockSpec((tm, BLK), lambda i: (i, COL_MZ)),
                  kv_spec, kv_spec,
                  pl.BlockSpec((1, M_HEAD_DIM), lambda i: (0, 0))],
        out_specs=pl.BlockSpec((tm, M_WIDTH), lambda i: (i, 0)),
        out_shape=jax.ShapeDtypeStruct((m, M_WIDTH), BF16),
        compiler_params=_cparams(("parallel",)),
        name="memattn",
    )(proj, proj, km, vm, gq)


def _dsa_kernel(topk, q_ref, z_ref, iq_ref, cb_ref, sb_ref, ci_ref, si_ref, iwt_ref,
                k_ref, vt_ref, iklo_ref, ikhi_ref, gq_ref, tri_ref, bound_ref, o_ref,
                sc_ref, bias_ref, lg_ref, qs_ref, iqs_ref, acc_ref):
    qb = pl.program_id(1)
    nck = (qb * TQ + TQ + CK - 1) // CK
    nbig = (qb * TQ + TQ + CB - 1) // CB
    t_lane = qb * TQ + lax.broadcasted_iota(jnp.int32, (1, TQ), 1)
    gw = B_GROUP * TQ

    def fold8(a):
        return a.reshape(a.shape[0] // 8, 8, a.shape[1])

    def grouped_loop(trips, body, init, per_step=2, start=0):
        carry, done = init, start
        while per_step >= 1:
            def step(i, c, first=done, n=per_step):
                for u in range(n):
                    c = body(first + n * i + u, c)
                return c

            steps = (trips - done) // per_step
            carry = lax.fori_loop(0, steps, step, carry)
            done = done + steps * per_step
            per_step //= 2
        return carry

    cos_b, sin_b = cb_ref[0], sb_ref[0]
    cos_i, sin_i = ci_ref[0], si_ref[0]
    qscale = B_HEAD_DIM ** -0.5 * LOG2E
    mean_mat = jnp.full((B_HEAD_DIM, B_HEAD_DIM), 1.0 / B_HEAD_DIM, BF16)
    gain = gq_ref[...] * qscale
    for h in range(B_HEADS):
        slab = q_ref[:, h * B_HEAD_DIM:(h + 1) * B_HEAD_DIM].astype(F32)
        sq = slab * slab
        sq_hi = sq.astype(BF16)
        sq_lo = (sq - sq_hi.astype(F32)).astype(BF16)
        msq = (jnp.dot(sq_hi, mean_mat, preferred_element_type=F32)
               + jnp.dot(sq_lo, mean_mat, preferred_element_type=F32))
        qr = _rope(slab * lax.rsqrt(msq + EPS) * gain, cos_b, sin_b, B_HEAD_DIM // 8, LANES)
        g = h % B_GROUP
        qs_ref[h // B_GROUP, g * TQ:(g + 1) * TQ, :] = qr.astype(BF16)
    for j in range(IDX_HEADS // 2):
        slab = iq_ref[:, j * LANES:(j + 1) * LANES].astype(F32)
        iqs_ref[j // 2, (j % 2) * TQ:(j % 2 + 1) * TQ, :] = (
            _rope(slab, cos_i, sin_i, IDX_DIM // 8, IDX_DIM).astype(BF16))
    wt = iwt_ref[0]

    def idx_body(c, carry):
        mn8, mx8, s1, s2 = carry
        off = c * CK if isinstance(c, int) else pl.multiple_of(c * CK, CK)
        acc = jnp.zeros((CK, TQ), F32)
        for jj in range(IDX_HEADS // 4):
            rhs = iqs_ref[jj]
            for half, keys_ref in enumerate((iklo_ref, ikhi_ref)):
                d = _dot_nt(keys_ref[0, pl.ds(off, CK), :], rhs)
                ha, hb = 4 * jj + half, 4 * jj + 2 + half
                acc = (acc + jnp.maximum(d[:, :TQ], 0.0) * wt[ha:ha + 1, :]
                       + jnp.maximum(d[:, TQ:], 0.0) * wt[hb:hb + 1, :])
        key = off + lax.broadcasted_iota(jnp.int32, (CK, TQ), 0)
        causal = key <= t_lane
        sc = jnp.where(causal, acc, -jnp.inf)
        sc_ref[pl.ds(off, CK), :] = sc
        live = jnp.where(causal, acc, 0.0)
        mn8 = jnp.minimum(mn8, jnp.min(fold8(jnp.where(causal, acc, jnp.inf)), axis=0))
        mx8 = jnp.maximum(mx8, jnp.max(fold8(sc), axis=0))
        s1 = s1 + jnp.sum(fold8(live), axis=0)
        s2 = s2 + jnp.sum(fold8(live * live), axis=0)
        return mn8, mx8, s1, s2

    zero8 = jnp.zeros((8, TQ), F32)
    stats = (jnp.full((8, TQ), jnp.inf, F32), jnp.full((8, TQ), -jnp.inf, F32), zero8, zero8)
    mn8, mx8, s1, s2 = grouped_loop(nck, idx_body, idx_body(0, stats), per_step=4, start=1)
    row_min = jnp.min(mn8, axis=0, keepdims=True)
    row_max = jnp.max(mx8, axis=0, keepdims=True)

    def fill_body(c, carry):
        sc_ref[pl.ds(pl.multiple_of(c * CK, CK), CK), :] = jnp.full((CK, TQ), -jnp.inf, F32)
        return carry

    lax.fori_loop(nck, nbig * (CB // CK), fill_body, 0)

    def count_rows(pred):
        rows = 8 * COUNT_WAYS

        def body(c, cnt):
            off = pl.multiple_of(c * CB, CB)
            for r in range(CB // rows):
                cnt = cnt + jnp.where(pred(sc_ref[pl.ds(off + r * rows, rows), :]), 1.0, 0.0)
            return cnt

        cnt = lax.fori_loop(0, nbig, body, jnp.zeros((rows, TQ), F32))
        return jnp.sum(cnt, axis=0, keepdims=True)

    kf = float(topk)
    n_valid = (t_lane + 1).astype(F32)
    all_rows = (t_lane + 1) <= topk

    def search_pass(st, probe, stuck):
        lo, hi, clo, chi, thr, fin, tie = st
        cnt = count_rows(lambda blk: blk >= probe)
        active = fin < 0.5
        hit = cnt == kf
        end_thr = jnp.where(jnp.logical_and(stuck, cnt < kf), lo, probe)
        ends = jnp.logical_and(active, jnp.logical_or(hit, stuck))
        thr = jnp.where(ends, end_thr, thr)
        tie = jnp.where(jnp.logical_and(ends, jnp.logical_not(hit)), 1.0, tie)
        fin = jnp.where(ends, 1.0, fin)
        up = jnp.logical_and(active, cnt > kf)
        dn = jnp.logical_and(active, cnt < kf)
        return (jnp.where(up, probe, lo), jnp.where(dn, probe, hi), jnp.where(up, cnt, clo),
                jnp.where(dn, cnt, chi), thr, fin, tie)

    def next_probe(st, halve=False):
        lo, hi, clo, chi = st[:4]
        frac = jnp.clip((clo - kf + 0.5) / (clo - chi + 1.0), SEARCH_MARGIN, 1.0 - SEARCH_MARGIN)
        probe = lo + (hi - lo) * frac
        outside = jnp.logical_or(jnp.logical_or(probe <= lo, probe >= hi), halve)
        probe = jnp.where(outside, lo * 0.5 + hi * 0.5, probe)
        stuck = jnp.logical_or(probe <= lo, probe >= hi)
        return jnp.where(stuck, hi, probe), stuck

    mean = jnp.sum(s1, axis=0, keepdims=True) / n_valid
    var = jnp.maximum(jnp.sum(s2, axis=0, keepdims=True) / n_valid - mean * mean, 0.0)
    tail = jnp.clip(kf / n_valid, 1e-6, 1.0 - 1e-6)
    tq = jnp.sqrt(-2.0 * jnp.log(jnp.minimum(tail, 1.0 - tail)))
    zq = tq - ((0.010328 * tq + 0.802853) * tq + 2.515517) / (
        ((0.001308 * tq + 0.189269) * tq + 1.432788) * tq + 1.0)
    guess = mean + jnp.where(tail < 0.5, zq, -zq) * jnp.sqrt(var)
    inside = jnp.logical_and(guess >= row_min, guess <= row_max)
    probe0 = jnp.where(inside, guess, row_min * 0.5 + row_max * 0.5)

    ones = jnp.ones((1, TQ), F32)
    st = (row_min, row_max, n_valid, 0.0 * ones, jnp.where(all_rows, -F32_MAX, row_max),
          jnp.where(all_rows, 1.0, 0.0), 0.0 * ones)
    st = search_pass(st, probe0, probe0 < row_min)

    def fixed_body(i, st):
        return search_pass(st, *next_probe(st))

    st = lax.fori_loop(0, SEARCH_FIXED_PASSES, fixed_body, st)

    def more_cond(c):
        return jnp.logical_and(c[0] < MAX_SEARCH_ITERS, jnp.min(c[1][5]) < 0.5)

    def more_body(c):
        return c[0] + 1, search_pass(c[1], *next_probe(c[1], c[0] >= SEARCH_INTERP_PASSES))

    _, st = lax.while_loop(more_cond, more_body, (jnp.int32(0), st))
    thr, tie = st[4], st[6]

    def plain_mask():
        def body(c, carry):
            off = pl.multiple_of(c * CB, CB)
            bias_ref[pl.ds(off, CB), :] = jnp.where(sc_ref[pl.ds(off, CB), :] >= thr, 0.0, NEG_BIG)
            return carry

        lax.fori_loop(0, nbig, body, 0)

    def tied_mask():
        n_gt = count_rows(lambda blk: blk > thr)
        need = jnp.where(all_rows, F32_MAX, kf - n_gt)

        def body(c, seen):
            off = pl.multiple_of(c * CB, CB)
            blk = sc_ref[pl.ds(off, CB), :]
            eq = blk == thr
            pref = jnp.dot(tri_ref[...], jnp.where(eq, 1.0, 0.0).astype(BF16),
                           preferred_element_type=F32) + seen
            keep = jnp.logical_or(blk > thr, jnp.logical_and(eq, pref <= need))
            bias_ref[pl.ds(off, CB), :] = jnp.where(keep, 0.0, NEG_BIG)
            return pref[CB - 1:CB, :]

        lax.fori_loop(0, nbig, body, jnp.zeros((1, TQ), F32))

    lax.cond(jnp.max(tie) > 0.5, tied_mask, plain_mask)

    logit_bound = bound_ref[0, 0]

    def masked_logits(n, off):
        b = bias_ref[pl.ds(off, CB), :]
        lg = _dot_nt(k_ref[0, n, pl.ds(off, CB), :], qs_ref[n])
        return jnp.concatenate([lg[:, g * TQ:(g + 1) * TQ] + b for g in range(B_GROUP)], axis=1)

    for n in range(B_KV_HEADS):
        acc_ref[...] = jnp.zeros(acc_ref.shape, F32)

        def one_pass(n=n):
            def body(c, l8):
                off = pl.multiple_of(c * CB, CB)
                p = jnp.exp2(masked_logits(n, off) - logit_bound)
                acc_ref[...] += jnp.dot(vt_ref[0, n, c], p.astype(BF16), preferred_element_type=F32)
                return l8 + jnp.sum(fold8(p), axis=0)

            return grouped_loop(nbig, body, jnp.zeros((8, gw), F32), per_step=4)

        def two_pass(n=n):
            def logit_body(c, mx8):
                off = pl.multiple_of(c * CB, CB)
                lg = masked_logits(n, off)
                lg_ref[pl.ds(off, CB), :] = lg
                return jnp.maximum(mx8, jnp.max(fold8(lg), axis=0))

            mx8 = grouped_loop(nbig, logit_body, jnp.full((8, gw), NEG_BIG, F32))
            m = jnp.max(mx8, axis=0, keepdims=True)

            def pv_body(c, l8):
                off = pl.multiple_of(c * CB, CB)
                p = jnp.exp2(lg_ref[pl.ds(off, CB), :] - m)
                acc_ref[...] += jnp.dot(vt_ref[0, n, c], p.astype(BF16), preferred_element_type=F32)
                return l8 + jnp.sum(fold8(p), axis=0)

            return grouped_loop(nbig, pv_body, jnp.zeros((8, gw), F32))

        l8 = lax.cond(logit_bound <= MAX_SHIFT_BOUND, one_pass, two_pass)
        o_t = acc_ref[...] / jnp.sum(l8, axis=0, keepdims=True)
        for g in range(B_GROUP):
            cols = slice((n * B_GROUP + g) * B_HEAD_DIM, (n * B_GROUP + g + 1) * B_HEAD_DIM)
            o = o_t[:, g * TQ:(g + 1) * TQ].T
            o_ref[:, cols] = (o * jax.nn.silu(z_ref[:, cols].astype(F32))).astype(BF16)


def _dsa(proj, tabs, iwt, k, vt, iklo, ikhi, gq, gk, bsz, s):
    nq = s // TQ
    bound = (1.02 * B_HEAD_DIM ** 0.5 * LOG2E * jnp.max(jnp.abs(gq)) * jnp.max(jnp.abs(gk))).reshape(1, 1)
    topk = min(TOPK_MAX, s // 4)
    gw = B_GROUP * TQ
    rowblk = lambda c: pl.BlockSpec((TQ, BLK), lambda b, i: (b * nq + i, c))
    tok = pl.BlockSpec((1, TQ, LANES), lambda b, i: (b, i, 0))
    ik = pl.BlockSpec((1, s, LANES), lambda b, i: (b, 0, 0))
    tri = jnp.tril(jnp.ones((CB, CB), BF16))
    return pl.pallas_call(
        functools.partial(_dsa_kernel, topk),
        grid=(bsz, nq),
        in_specs=[rowblk(COL_BQ), rowblk(COL_BZ), rowblk(COL_IQ), tok, tok, tok, tok,
                  pl.BlockSpec((1, IDX_HEADS, TQ), lambda b, i: (b, 0, i)),
                  pl.BlockSpec((1, B_KV_HEADS, s, B_HEAD_DIM), lambda b, i: (b, 0, 0, 0)),
                  pl.BlockSpec((1, B_KV_HEADS, s // CB, B_HEAD_DIM, CB), lambda b, i: (b, 0, 0, 0, 0)),
                  ik, ik,
                  pl.BlockSpec((1, B_HEAD_DIM), lambda b, i: (0, 0)),
                  pl.BlockSpec((CB, CB), lambda b, i: (0, 0)),
                  pl.BlockSpec(memory_space=pltpu.SMEM)],
        out_specs=pl.BlockSpec((TQ, B_WIDTH), lambda b, i: (b * nq + i, 0)),
        out_shape=jax.ShapeDtypeStruct((bsz * s, B_WIDTH), BF16),
        scratch_shapes=[
            pltpu.VMEM((s, TQ), F32),
            pltpu.VMEM((s, TQ), F32),
            pltpu.VMEM((s, gw), F32),
            pltpu.VMEM((B_KV_HEADS, gw, B_HEAD_DIM), BF16),
            pltpu.VMEM((IDX_HEADS // 4, 2 * TQ, LANES), BF16),
            pltpu.VMEM((B_HEAD_DIM, gw), F32),
        ],
        compiler_params=_cparams(("parallel", "arbitrary")),
        name="dsa",
    )(proj, proj, proj, *tabs, iwt, k, vt, iklo, ikhi, gq, tri, bound)


def _merge_kernel(ta_ref, tb_ref, tm_ref, ga_ref, gb_ref, gm_ref, x_ref,
                  wa_hbm, wb_hbm, wm_hbm, wo_hbm, o_ref,
                  wa_ref, wb_ref, wm_ref, wo_ref, stage_ref, sem):
    @pl.when(pl.program_id(0) == 0)
    def _():
        chunks = [(src, dst, r) for src, dst in ((wa_hbm, wa_ref), (wb_hbm, wb_ref), (wm_hbm, wm_ref),
                                                 (wo_hbm, wo_ref))
                  for r in range(0, src.shape[0], MERGE_STAGE_ROWS)]

        def copy(j):
            src, _, r = chunks[j]
            return pltpu.make_async_copy(src.at[pl.ds(r, MERGE_STAGE_ROWS), :], stage_ref.at[j % 2],
                                         sem.at[j % 2])

        copy(0).start()
        for j, (_, dst, r) in enumerate(chunks):
            if j + 1 < len(chunks):
                copy(j + 1).start()
            copy(j).wait()
            dst[pl.ds(r, MERGE_STAGE_ROWS), :] = stage_ref[j % 2].astype(BF16)

    def branch(t_ref, g_ref, w_ref):
        y = jnp.dot(t_ref[...], w_ref[...], preferred_element_type=F32)
        return jax.nn.sigmoid(g_ref[...].astype(F32)) * y

    merged = branch(ta_ref, ga_ref, wa_ref) + branch(tb_ref, gb_ref, wb_ref) + branch(tm_ref, gm_ref, wm_ref)
    o_ref[...] = x_ref[...] + jnp.dot(merged.astype(BF16), wo_ref[...], preferred_element_type=F32)


def _merge(ta, tb, tmem, proj, x2, wa, wb, wm, wo):
    m = x2.shape[0]
    tm = min(256, m)
    act = pl.BlockSpec((tm, BLK), lambda i: (i, 0))
    gate = lambda c: pl.BlockSpec((tm, D_MODEL), lambda i: (i, c // 2))
    wide = pl.BlockSpec((tm, D_MODEL), lambda i: (i, 0))
    hbm = pl.BlockSpec(memory_space=pl.ANY)
    return pl.pallas_call(
        _merge_kernel,
        grid=(m // tm,),
        in_specs=[act, act, act, gate(COL_GA), gate(COL_GB), gate(COL_GM), wide, hbm, hbm, hbm, hbm],
        out_specs=wide,
        out_shape=jax.ShapeDtypeStruct((m, D_MODEL), F32),
        scratch_shapes=[pltpu.VMEM(w.shape, BF16) for w in (wa, wb, wm, wo)]
        + [pltpu.VMEM((2, MERGE_STAGE_ROWS, D_MODEL), F32), pltpu.SemaphoreType.DMA((2,))],
        compiler_params=_cparams(("arbitrary",)),
        name="merge",
    )(ta, tb, tmem, proj, proj, proj, x2, wa, wb, wm, wo)


def _rope_freqs():
    inv = lambda half: ROPE_THETA ** (-np.arange(half, dtype=np.float32) / half)
    freq = np.zeros((1, LANES), np.float32)
    hb, hi = B_HEAD_DIM // 8, IDX_DIM // 8
    freq[0, :hb] = inv(hb)
    freq[0, hb:hb + hi] = inv(hi)
    return jnp.asarray(freq)


def _layer(x, mem, positions, norm_gain, w_in, gmlp_ln_gain, gmlp_ln_bias, spatial_w, spatial_b,
           w_branch_a, q_norm_gain, k_norm_gain, idx_k_ln_gain, idx_k_ln_bias, w_branch_b,
           mem_norm_gain, w_mem_kv, mem_q_norm_gain, mem_k_norm_gain, w_branch_m, w_out):
    bsz, s, _ = x.shape
    m = bsz * s
    row = lambda a: a.reshape(1, -1).astype(F32)
    pad_lanes = lambda a: jnp.pad(a.reshape(1, -1).astype(F32), ((0, 0), (0, LANES - a.shape[-1])))
    x2 = x.reshape(m, D_MODEL)

    w_t = w_in.T
    h, k, vt, iklo, ikhi, cb, sb, ci, si, iwt = _kvprep(
        x, row(norm_gain), w_t, positions.reshape(bsz, s, 1).astype(jnp.int32), _rope_freqs(),
        row(k_norm_gain), pad_lanes(idx_k_ln_gain), pad_lanes(idx_k_ln_bias))
    proj = _inproj(h, w_t)

    t_a = _gmlp(proj, row(gmlp_ln_gain), row(gmlp_ln_bias), spatial_w.astype(F32), spatial_b.T.astype(F32))
    km, vm = _memkv(mem, row(mem_norm_gain), w_mem_kv, row(mem_k_norm_gain))
    t_m = _memattn(proj, km, vm, row(mem_q_norm_gain), s)
    t_b = _dsa(proj, (cb, sb, ci, si), iwt, k, vt, iklo, ikhi, row(q_norm_gain), row(k_norm_gain), bsz, s)

    out = _merge(t_a, t_b, t_m, proj, x2, w_branch_a, w_branch_b, w_branch_m, w_out)
    return out.reshape(bsz, s, D_MODEL)


def kernel(x, mem, positions, norm_gain, w_in, gmlp_ln_gain, gmlp_ln_bias, spatial_w, spatial_b, w_branch_a, q_norm_gain, k_norm_gain, idx_k_ln_gain, idx_k_ln_bias, w_branch_b, mem_norm_gain, w_mem_kv, mem_q_norm_gain, mem_k_norm_gain, w_branch_m, w_out):
    for l in range(norm_gain.shape[0]):
        x = _layer(x, mem, positions, norm_gain[l], w_in[l], gmlp_ln_gain[l], gmlp_ln_bias[l],
                   spatial_w[l], spatial_b[l], w_branch_a[l], q_norm_gain[l], k_norm_gain[l],
                   idx_k_ln_gain[l], idx_k_ln_bias[l], w_branch_b[l], mem_norm_gain[l], w_mem_kv[l],
                   mem_q_norm_gain[l], mem_k_norm_gain[l], w_branch_m[l], w_out[l])
    return x
```

```python
import functools

import numpy as np
import jax
import jax.numpy as jnp
from jax import lax
from jax.experimental import pallas as pl
from jax.experimental.pallas import tpu as pltpu

F32 = jnp.float32
BF16 = jnp.bfloat16

D_MODEL = 2048
ROPE_THETA = 500000.0
EPS = 1e-6
A_GROUPS = 8
A_GROUP_DIM = 128
A_WIDTH = A_GROUPS * A_GROUP_DIM
CHUNK = 128
B_HEADS = 8
B_KV_HEADS = 2
B_GROUP = B_HEADS // B_KV_HEADS
B_HEAD_DIM = 128
B_WIDTH = B_HEADS * B_HEAD_DIM
IDX_HEADS = 16
IDX_DIM = 64
TOPK_MAX = 256
M_HEADS = 4
M_HEAD_DIM = 256
M_WIDTH = M_HEADS * M_HEAD_DIM

SPLIT_SIZES = (
    A_WIDTH, A_WIDTH, A_WIDTH,
    B_WIDTH, B_KV_HEADS * B_HEAD_DIM, B_KV_HEADS * B_HEAD_DIM, B_WIDTH,
    IDX_HEADS * IDX_DIM, IDX_DIM, IDX_HEADS,
    M_WIDTH, M_WIDTH,
    D_MODEL, D_MODEL, D_MODEL,
)

LANES = 128
BLK = 1024
COL_AU, COL_AV, COL_AZ, COL_BQ, COL_BZ, COL_IQ, COL_MQ, COL_MZ = range(8)
COL_GA, COL_GB, COL_GM = 8, 10, 12
NB_MAIN = 14
_OFFS = [int(o) for o in np.concatenate([[0], np.cumsum(SPLIT_SIZES)])]
ROW_ALIGN = 16
MAIN_START = ([_OFFS[i] for i in (0, 1, 2, 3, 6, 7, 10, 11)]
              + [_OFFS[i] + d for i in (12, 13, 14) for d in (0, BLK)])
assert all(s % ROW_ALIGN == 0 for s in MAIN_START)
SMALL_KV_START, SMALL_KV_WIDTH = _OFFS[4], _OFFS[6] - _OFFS[4]
SMALL_IDX_START = _OFFS[8]
assert SMALL_KV_START % SMALL_KV_WIDTH == 0 and SMALL_IDX_START % LANES == 0
assert IDX_DIM + IDX_HEADS <= LANES and _OFFS[9] == SMALL_IDX_START + IDX_DIM

VMEM_LIMIT = 56 * 1024 * 1024
LOG2E = 1.4426950408889634
NEG_BIG = -1e30
F32_MAX = 3.4028234663852886e38

INPROJ_TM, INPROJ_TN = 2048, 1024
KVPREP_TP = 512
TQ = 256
CK = 256
CB = 512
COUNT_WAYS = 8
SEARCH_MARGIN = 0.05
SEARCH_FIXED_PASSES = 11
SEARCH_INTERP_PASSES = 13
MAX_SEARCH_ITERS = 400
MERGE_COLS = 512
MERGE_STAGE_ROWS = 256
MAX_SHIFT_BOUND = 48.0


def _cparams(sem):
    return pltpu.CompilerParams(dimension_semantics=sem, vmem_limit_bytes=VMEM_LIMIT)


def _dot_nt(a, b):
    return lax.dot_general(a, b, (((1,), (1,)), ((), ())), preferred_element_type=F32)


def _rope(x, cos_t, sin_t, half, period):
    lane = lax.broadcasted_iota(jnp.int32, x.shape, 1) % period
    rolled = jnp.where(lane < half, pltpu.roll(x, LANES - half, 1), pltpu.roll(x, half, 1))
    return x * cos_t + rolled * sin_t


def _rms_rows(x, gain):
    ms = jnp.mean(x * x, axis=-1, keepdims=True)
    return (x * lax.rsqrt(ms + EPS) * gain).astype(BF16)


def _inproj_kernel(tab_ref, h_ref, wt_ref, proj_ref):
    del tab_ref
    proj_ref[...] = _dot_nt(h_ref[...], wt_ref[...].astype(BF16)).astype(BF16)


def _inproj(h, w_t):
    m = h.shape[0]
    tm = min(INPROJ_TM, m)
    starts = [s + d for s in MAIN_START for d in range(0, BLK, INPROJ_TN)]
    tab = jnp.asarray(np.array([s // ROW_ALIGN for s in starts], np.int32))
    return pl.pallas_call(
        _inproj_kernel,
        grid_spec=pltpu.PrefetchScalarGridSpec(
            num_scalar_prefetch=1,
            grid=(m // tm, len(starts)),
            in_specs=[
                pl.BlockSpec((tm, D_MODEL), lambda i, n, tab: (i, 0)),
                pl.BlockSpec((pl.Element(INPROJ_TN), pl.Element(D_MODEL)),
                             lambda i, n, tab: (tab[n] * ROW_ALIGN, 0)),
            ],
            out_specs=pl.BlockSpec((tm, INPROJ_TN), lambda i, n, tab: (i, n)),
        ),
        out_shape=jax.ShapeDtypeStruct((m, NB_MAIN * BLK), BF16),
        compiler_params=_cparams(("parallel", "arbitrary")),
        name="inproj",
    )(tab, h, w_t)


def _kvprep_kernel(x_ref, g_ref, wkv_ref, widx_ref, pos_ref, fc_ref, gk_ref, lng_ref, lnb_ref,
                   h_ref, k_ref, vt_ref, iklo_ref, ikhi_ref, cb_ref, sb_ref, ci_ref, si_ref, iwt_ref,
                   wkv_bf_ref, widx_bf_ref):
    @pl.when(jnp.logical_and(pl.program_id(0) == 0, pl.program_id(1) == 0))
    def _():
        wkv_bf_ref[...] = wkv_ref[...].astype(BF16)
        widx_bf_ref[...] = widx_ref[...].astype(BF16)

    h = _rms_rows(x_ref[...], g_ref[...])
    h_ref[...] = h
    kv = _dot_nt(h, wkv_bf_ref[...])
    ikp = _dot_nt(h, widx_bf_ref[...])

    hb, hi = B_HEAD_DIM // 8, IDX_DIM // 8
    ang = pos_ref[0].astype(F32) * fc_ref[...]
    cos_c, sin_c = jnp.cos(ang), jnp.sin(ang)
    lane = lax.broadcasted_iota(jnp.int32, ang.shape, 1)
    cos_b = jnp.where(lane < hb, cos_c, jnp.where(lane < 2 * hb, pltpu.roll(cos_c, hb, 1), 1.0))
    sin_b = jnp.where(lane < hb, -sin_c, jnp.where(lane < 2 * hb, pltpu.roll(sin_c, hb, 1), 0.0))
    cos_i = jnp.ones_like(cos_c)
    sin_i = jnp.zeros_like(sin_c)
    for head in range(LANES // IDX_DIM):
        for part, sign in enumerate((-1.0, 1.0)):
            first = head * IDX_DIM + part * hi
            here = jnp.logical_and(lane >= first, lane < first + hi)
            shift = (first - hb) % LANES
            cos_i = jnp.where(here, pltpu.roll(cos_c, shift, 1), cos_i)
            sin_i = jnp.where(here, sign * pltpu.roll(sin_c, shift, 1), sin_i)
    cb_ref[0] = cos_b
    sb_ref[0] = sin_b
    ci_ref[0] = cos_i
    si_ref[0] = sin_i

    for n in range(B_KV_HEADS):
        kh = kv[:, n * B_HEAD_DIM:(n + 1) * B_HEAD_DIM]
        r = lax.rsqrt(jnp.mean(kh * kh, axis=-1, keepdims=True) + EPS)
        kn = kh * r * gk_ref[...]
        k_ref[0, n] = _rope(kn, cos_b, sin_b, B_HEAD_DIM // 8, LANES).astype(BF16)
        vt = kv[:, (B_KV_HEADS + n) * B_HEAD_DIM:(B_KV_HEADS + n + 1) * B_HEAD_DIM].T
        for c in range(vt.shape[1] // CB):
            vt_ref[0, n, c] = vt[:, c * CB:(c + 1) * CB].astype(BF16)

    lane = lax.broadcasted_iota(jnp.int32, ikp.shape, 1)
    live = lane < IDX_DIM
    mu = jnp.sum(jnp.where(live, ikp, 0.0), axis=-1, keepdims=True) * (1.0 / IDX_DIM)
    d = jnp.where(live, ikp - mu, 0.0)
    var = jnp.sum(d * d, axis=-1, keepdims=True) * (1.0 / IDX_DIM)
    y = d * lax.rsqrt(var + EPS) * lng_ref[...] + lnb_ref[...]
    yr = jnp.where(live, _rope(y, cos_i, sin_i, IDX_DIM // 8, IDX_DIM), 0.0)
    iklo_ref[0] = yr.astype(BF16)
    ikhi_ref[0] = pltpu.roll(yr, IDX_DIM, 1).astype(BF16)

    iw = ikp * (IDX_DIM ** -0.5 * IDX_HEADS ** -0.5)
    iwt_ref[0] = iw.T[IDX_DIM:IDX_DIM + IDX_HEADS, :]


def _kvprep(x3, gain, w_t, pos3, freqs, gk, lng, lnb):
    bsz, s, _ = x3.shape
    tp = min(KVPREP_TP, s)
    row = lambda b, i: (b, i, 0)
    const = lambda b, i: (0, 0)
    tab_spec = pl.BlockSpec((1, LANES), const)
    out_h = pl.BlockSpec((tp, D_MODEL), lambda b, i: (b * (s // tp) + i, 0))
    out_tok = pl.BlockSpec((1, tp, LANES), row)
    out_k = pl.BlockSpec((1, B_KV_HEADS, tp, B_HEAD_DIM), lambda b, i: (b, 0, i, 0))
    out_vt = pl.BlockSpec((1, B_KV_HEADS, tp // CB, B_HEAD_DIM, CB), lambda b, i: (b, 0, i, 0, 0))
    out_iwt = pl.BlockSpec((1, IDX_HEADS, tp), lambda b, i: (b, 0, i))
    return pl.pallas_call(
        _kvprep_kernel,
        grid=(bsz, s // tp),
        in_specs=[pl.BlockSpec((None, tp, D_MODEL), row), pl.BlockSpec((1, D_MODEL), const),
                  pl.BlockSpec((SMALL_KV_WIDTH, D_MODEL), lambda b, i: (SMALL_KV_START // SMALL_KV_WIDTH, 0)),
                  pl.BlockSpec((LANES, D_MODEL), lambda b, i: (SMALL_IDX_START // LANES, 0)),
                  pl.BlockSpec((1, tp, 1), row),
                  tab_spec, tab_spec, tab_spec, tab_spec],
        out_specs=[out_h, out_k, out_vt, out_tok, out_tok, out_tok, out_tok, out_tok, out_tok, out_iwt],
        out_shape=[
            jax.ShapeDtypeStruct((bsz * s, D_MODEL), BF16),
            jax.ShapeDtypeStruct((bsz, B_KV_HEADS, s, B_HEAD_DIM), BF16),
            jax.ShapeDtypeStruct((bsz, B_KV_HEADS, s // CB, B_HEAD_DIM, CB), BF16),
            jax.ShapeDtypeStruct((bsz, s, LANES), BF16),
            jax.ShapeDtypeStruct((bsz, s, LANES), BF16),
            jax.ShapeDtypeStruct((bsz, s, LANES), F32),
            jax.ShapeDtypeStruct((bsz, s, LANES), F32),
            jax.ShapeDtypeStruct((bsz, s, LANES), F32),
            jax.ShapeDtypeStruct((bsz, s, LANES), F32),
            jax.ShapeDtypeStruct((bsz, IDX_HEADS, s), F32),
        ],
        scratch_shapes=[pltpu.VMEM((SMALL_KV_WIDTH, D_MODEL), BF16), pltpu.VMEM((LANES, D_MODEL), BF16)],
        compiler_params=_cparams(("arbitrary", "arbitrary")),
        name="kvprep",
    )(x3, gain, w_t, w_t, pos3, freqs, gk, lng, lnb)


def _gelu_tanh(x):
    c = -2.0 * (2.0 / np.pi) ** 0.5 * LOG2E
    return x / (1.0 + jnp.exp2(x * (x * x * (0.044715 * c) + c)))


def _gmlp_gate(rows, u_ref, z_ref, gate_ref):
    gate_ref[rows, :] = _gelu_tanh(u_ref[rows, :].astype(F32)) * jax.nn.silu(z_ref[rows, :].astype(F32))


def _gmlp_norm(rows, v_ref, lng_ref, lnb_ref, vn_ref):
    v = _gelu_tanh(v_ref[rows, :].astype(F32))
    mu = jnp.mean(v, axis=-1, keepdims=True)
    d = v - mu
    var = jnp.mean(d * d, axis=-1, keepdims=True)
    vn_ref[rows, :] = (d * lax.rsqrt(var + EPS) * lng_ref[...] + lnb_ref[...]).astype(BF16)


def _gmlp_mix(groups, ws_ref, sbt_ref, vn_ref, gate_ref, o_ref):
    tri = (lax.broadcasted_iota(jnp.int32, (CHUNK, CHUNK), 1)
           <= lax.broadcasted_iota(jnp.int32, (CHUNK, CHUNK), 0))
    for g in groups:
        wg = jnp.where(tri, ws_ref[g], 0.0).astype(BF16)
        bias = sbt_ref[:, g:g + 1]
        cols = slice(g * A_GROUP_DIM, (g + 1) * A_GROUP_DIM)
        for c in range(vn_ref.shape[0] // CHUNK):
            rows = slice(c * CHUNK, (c + 1) * CHUNK)
            sg = jnp.dot(wg, vn_ref[rows, cols], preferred_element_type=F32) + bias
            o_ref[rows, cols] = (gate_ref[rows, cols] * sg).astype(BF16)


def _memkv_kernel(mem_ref, g_ref, w_ref, gk_ref, km_ref, vm_ref, wbf_ref):
    @pl.when(pl.program_id(0) == 0)
    def _():
        wbf_ref[...] = w_ref[...].astype(BF16)

    h = _rms_rows(mem_ref[0], g_ref[...])
    kv = jnp.dot(h, wbf_ref[...], preferred_element_type=F32)
    for hd in range(M_HEADS):
        kh = kv[:, hd * M_HEAD_DIM:(hd + 1) * M_HEAD_DIM]
        r = lax.rsqrt(jnp.mean(kh * kh, axis=-1, keepdims=True) + EPS)
        km_ref[0, hd] = (kh * r * gk_ref[...]).astype(BF16)
        vm_ref[0, hd] = kv[:, M_WIDTH + hd * M_HEAD_DIM:M_WIDTH + (hd + 1) * M_HEAD_DIM].astype(BF16)


def _memkv(mem, gain, w_kv, gk):
    bsz, ml, _ = mem.shape
    out = pl.BlockSpec((1, M_HEADS, ml, M_HEAD_DIM), lambda b: (b, 0, 0, 0))
    shp = jax.ShapeDtypeStruct((bsz, M_HEADS, ml, M_HEAD_DIM), BF16)
    return pl.pallas_call(
        _memkv_kernel,
        grid=(bsz,),
        in_specs=[pl.BlockSpec((1, ml, D_MODEL), lambda b: (b, 0, 0)),
                  pl.BlockSpec((1, D_MODEL), lambda b: (0, 0)),
                  pl.BlockSpec((D_MODEL, 2 * M_WIDTH), lambda b: (0, 0), pipeline_mode=pl.Buffered(1)),
                  pl.BlockSpec((1, M_HEAD_DIM), lambda b: (0, 0))],
        out_specs=[out, out],
        out_shape=[shp, shp],
        scratch_shapes=[pltpu.VMEM((D_MODEL, 2 * M_WIDTH), BF16)],
        compiler_params=_cparams(("arbitrary",)),
        name="memkv",
    )(mem, gain, w_kv, gk)


def _memattn_heads(heads, q_ref, z_ref, km_ref, vm_ref, gq_ref, o_ref):
    qscale = M_HEAD_DIM ** -0.5 * LOG2E
    for hd in heads:
        cols = slice(hd * M_HEAD_DIM, (hd + 1) * M_HEAD_DIM)
        q = q_ref[:, cols].astype(F32)
        r = lax.rsqrt(jnp.mean(q * q, axis=-1, keepdims=True) + EPS)
        qn = (q * r * gq_ref[...] * qscale).astype(BF16)
        lg = _dot_nt(qn, km_ref[0, hd])
        p = jnp.exp2(lg - jnp.max(lg, axis=-1, keepdims=True))
        l = jnp.sum(p, axis=-1, keepdims=True)
        o = jnp.dot(p.astype(BF16), vm_ref[0, hd], preferred_element_type=F32) / l
        o_ref[:, cols] = (o * jax.nn.silu(z_ref[:, cols].astype(F32))).astype(BF16)


def _dsa_kernel(topk, q_ref, z_ref, iq_ref, cb_ref, sb_ref, ci_ref, si_ref, iwt_ref,
                k_ref, vt_ref, iklo_ref, ikhi_ref, gq_ref, tri_ref, bound_ref, o_ref,
                sc_ref, bias_ref, lg_ref, qs_ref, iqs_ref, acc_ref):
    qb = pl.program_id(1)
    nck = (qb * TQ + TQ + CK - 1) // CK
    nbig = (qb * TQ + TQ + CB - 1) // CB
    t_lane = qb * TQ + lax.broadcasted_iota(jnp.int32, (1, TQ), 1)
    gw = B_GROUP * TQ

    def fold8(a):
        return a.reshape(a.shape[0] // 8, 8, a.shape[1])

    def grouped_loop(trips, body, init, per_step=2, start=0):
        carry, done = init, start
        while per_step >= 1:
            def step(i, c, first=done, n=per_step):
                for u in range(n):
                    c = body(first + n * i + u, c)
                return c

            steps = (trips - done) // per_step
            carry = lax.fori_loop(0, steps, step, carry)
            done = done + steps * per_step
            per_step //= 2
        return carry

    cos_b, sin_b = cb_ref[0], sb_ref[0]
    cos_i, sin_i = ci_ref[0], si_ref[0]
    qscale = B_HEAD_DIM ** -0.5 * LOG2E
    mean_mat = jnp.full((B_HEAD_DIM, B_HEAD_DIM), 1.0 / B_HEAD_DIM, BF16)
    gain = gq_ref[...] * qscale
    for h in range(B_HEADS):
        slab = q_ref[:, h * B_HEAD_DIM:(h + 1) * B_HEAD_DIM].astype(F32)
        sq = slab * slab
        sq_hi = sq.astype(BF16)
        sq_lo = (sq - sq_hi.astype(F32)).astype(BF16)
        msq = (jnp.dot(sq_hi, mean_mat, preferred_element_type=F32)
               + jnp.dot(sq_lo, mean_mat, preferred_element_type=F32))
        qr = _rope(slab * lax.rsqrt(msq + EPS) * gain, cos_b, sin_b, B_HEAD_DIM // 8, LANES)
        g = h % B_GROUP
        qs_ref[h // B_GROUP, g * TQ:(g + 1) * TQ, :] = qr.astype(BF16)
    for j in range(IDX_HEADS // 2):
        slab = iq_ref[:, j * LANES:(j + 1) * LANES].astype(F32)
        iqs_ref[j // 2, (j % 2) * TQ:(j % 2 + 1) * TQ, :] = (
            _rope(slab, cos_i, sin_i, IDX_DIM // 8, IDX_DIM).astype(BF16))
    wt = iwt_ref[0]

    def idx_body(c, carry):
        mn8, mx8, s1, s2 = carry
        off = c * CK if isinstance(c, int) else pl.multiple_of(c * CK, CK)
        acc = jnp.zeros((CK, TQ), F32)
        for jj in range(IDX_HEADS // 4):
            rhs = iqs_ref[jj]
            for half, keys_ref in enumerate((iklo_ref, ikhi_ref)):
                d = _dot_nt(keys_ref[0, pl.ds(off, CK), :], rhs)
                ha, hb = 4 * jj + half, 4 * jj + 2 + half
                acc = (acc + jnp.maximum(d[:, :TQ], 0.0) * wt[ha:ha + 1, :]
                       + jnp.maximum(d[:, TQ:], 0.0) * wt[hb:hb + 1, :])
        key = off + lax.broadcasted_iota(jnp.int32, (CK, TQ), 0)
        causal = key <= t_lane
        sc = jnp.where(causal, acc, -jnp.inf)
        sc_ref[pl.ds(off, CK), :] = sc
        live = jnp.where(causal, acc, 0.0)
        mn8 = jnp.minimum(mn8, jnp.min(fold8(jnp.where(causal, acc, jnp.inf)), axis=0))
        mx8 = jnp.maximum(mx8, jnp.max(fold8(sc), axis=0))
        s1 = s1 + jnp.sum(fold8(live), axis=0)
        s2 = s2 + jnp.sum(fold8(live * live), axis=0)
        return mn8, mx8, s1, s2

    zero8 = jnp.zeros((8, TQ), F32)
    stats = (jnp.full((8, TQ), jnp.inf, F32), jnp.full((8, TQ), -jnp.inf, F32), zero8, zero8)
    mn8, mx8, s1, s2 = grouped_loop(nck, idx_body, idx_body(0, stats), per_step=4, start=1)
    row_min = jnp.min(mn8, axis=0, keepdims=True)
    row_max = jnp.max(mx8, axis=0, keepdims=True)

    def fill_body(c, carry):
        sc_ref[pl.ds(pl.multiple_of(c * CK, CK), CK), :] = jnp.full((CK, TQ), -jnp.inf, F32)
        return carry

    lax.fori_loop(nck, nbig * (CB // CK), fill_body, 0)

    def count_rows(pred):
        rows = 8 * COUNT_WAYS

        def body(c, cnt):
            off = pl.multiple_of(c * CB, CB)
            for r in range(CB // rows):
                cnt = cnt + jnp.where(pred(sc_ref[pl.ds(off + r * rows, rows), :]), 1.0, 0.0)
            return cnt

        cnt = lax.fori_loop(0, nbig, body, jnp.zeros((rows, TQ), F32))
        return jnp.sum(cnt, axis=0, keepdims=True)

    kf = float(topk)
    n_valid = (t_lane + 1).astype(F32)
    all_rows = (t_lane + 1) <= topk

    def search_pass(st, probe, stuck):
        lo, hi, clo, chi, thr, fin, tie = st
        cnt = count_rows(lambda blk: blk >= probe)
        active = fin < 0.5
        hit = cnt == kf
        end_thr = jnp.where(jnp.logical_and(stuck, cnt < kf), lo, probe)
        ends = jnp.logical_and(active, jnp.logical_or(hit, stuck))
        thr = jnp.where(ends, end_thr, thr)
        tie = jnp.where(jnp.logical_and(ends, jnp.logical_not(hit)), 1.0, tie)
        fin = jnp.where(ends, 1.0, fin)
        up = jnp.logical_and(active, cnt > kf)
        dn = jnp.logical_and(active, cnt < kf)
        return (jnp.where(up, probe, lo), jnp.where(dn, probe, hi), jnp.where(up, cnt, clo),
                jnp.where(dn, cnt, chi), thr, fin, tie)

    def next_probe(st, halve=False):
        lo, hi, clo, chi = st[:4]
        frac = jnp.clip((clo - kf + 0.5) / (clo - chi + 1.0), SEARCH_MARGIN, 1.0 - SEARCH_MARGIN)
        probe = lo + (hi - lo) * frac
        outside = jnp.logical_or(jnp.logical_or(probe <= lo, probe >= hi), halve)
        probe = jnp.where(outside, lo * 0.5 + hi * 0.5, probe)
        stuck = jnp.logical_or(probe <= lo, probe >= hi)
        return jnp.where(stuck, hi, probe), stuck

    mean = jnp.sum(s1, axis=0, keepdims=True) / n_valid
    var = jnp.maximum(jnp.sum(s2, axis=0, keepdims=True) / n_valid - mean * mean, 0.0)
    tail = jnp.clip(kf / n_valid, 1e-6, 1.0 - 1e-6)
    tq = jnp.sqrt(-2.0 * jnp.log(jnp.minimum(tail, 1.0 - tail)))
    zq = tq - ((0.010328 * tq + 0.802853) * tq + 2.515517) / (
        ((0.001308 * tq + 0.189269) * tq + 1.432788) * tq + 1.0)
    guess = mean + jnp.where(tail < 0.5, zq, -zq) * jnp.sqrt(var)
    inside = jnp.logical_and(guess >= row_min, guess <= row_max)
    probe0 = jnp.where(inside, guess, row_min * 0.5 + row_max * 0.5)

    ones = jnp.ones((1, TQ), F32)
    st = (row_min, row_max, n_valid, 0.0 * ones, jnp.where(all_rows, -F32_MAX, row_max),
          jnp.where(all_rows, 1.0, 0.0), 0.0 * ones)
    st = search_pass(st, probe0, probe0 < row_min)

    def fixed_body(i, st):
        return search_pass(st, *next_probe(st))

    st = lax.fori_loop(0, SEARCH_FIXED_PASSES, fixed_body, st)

    def more_cond(c):
        return jnp.logical_and(c[0] < MAX_SEARCH_ITERS, jnp.min(c[1][5]) < 0.5)

    def more_body(c):
        return c[0] + 1, search_pass(c[1], *next_probe(c[1], c[0] >= SEARCH_INTERP_PASSES))

    _, st = lax.while_loop(more_cond, more_body, (jnp.int32(0), st))
    thr, tie = st[4], st[6]

    def plain_mask():
        def body(c, carry):
            off = pl.multiple_of(c * CB, CB)
            bias_ref[pl.ds(off, CB), :] = jnp.where(sc_ref[pl.ds(off, CB), :] >= thr, 0.0, NEG_BIG)
            return carry

        lax.fori_loop(0, nbig, body, 0)

    def tied_mask():
        n_gt = count_rows(lambda blk: blk > thr)
        need = jnp.where(all_rows, F32_MAX, kf - n_gt)

        def body(c, seen):
            off = pl.multiple_of(c * CB, CB)
            blk = sc_ref[pl.ds(off, CB), :]
            eq = blk == thr
            pref = jnp.dot(tri_ref[...], jnp.where(eq, 1.0, 0.0).astype(BF16),
                           preferred_element_type=F32) + seen
            keep = jnp.logical_or(blk > thr, jnp.logical_and(eq, pref <= need))
            bias_ref[pl.ds(off, CB), :] = jnp.where(keep, 0.0, NEG_BIG)
            return pref[CB - 1:CB, :]

        lax.fori_loop(0, nbig, body, jnp.zeros((1, TQ), F32))

    lax.cond(jnp.max(tie) > 0.5, tied_mask, plain_mask)

    logit_bound = bound_ref[0, 0]

    def masked_logits(n, off):
        b = bias_ref[pl.ds(off, CB), :]
        lg = _dot_nt(k_ref[0, n, pl.ds(off, CB), :], qs_ref[n])
        return jnp.concatenate([lg[:, g * TQ:(g + 1) * TQ] + b for g in range(B_GROUP)], axis=1)

    for n in range(B_KV_HEADS):
        acc_ref[...] = jnp.zeros(acc_ref.shape, F32)

        def one_pass(n=n):
            def body(c, l8):
                off = pl.multiple_of(c * CB, CB)
                p = jnp.exp2(masked_logits(n, off) - logit_bound)
                acc_ref[...] += jnp.dot(vt_ref[0, n, c], p.astype(BF16), preferred_element_type=F32)
                return l8 + jnp.sum(fold8(p), axis=0)

            return grouped_loop(nbig, body, jnp.zeros((8, gw), F32), per_step=4)

        def two_pass(n=n):
            def logit_body(c, mx8):
                off = pl.multiple_of(c * CB, CB)
                lg = masked_logits(n, off)
                lg_ref[pl.ds(off, CB), :] = lg
                return jnp.maximum(mx8, jnp.max(fold8(lg), axis=0))

            mx8 = grouped_loop(nbig, logit_body, jnp.full((8, gw), NEG_BIG, F32))
            m = jnp.max(mx8, axis=0, keepdims=True)

            def pv_body(c, l8):
                off = pl.multiple_of(c * CB, CB)
                p = jnp.exp2(lg_ref[pl.ds(off, CB), :] - m)
                acc_ref[...] += jnp.dot(vt_ref[0, n, c], p.astype(BF16), preferred_element_type=F32)
                return l8 + jnp.sum(fold8(p), axis=0)

            return grouped_loop(nbig, pv_body, jnp.zeros((8, gw), F32))

        l8 = lax.cond(logit_bound <= MAX_SHIFT_BOUND, one_pass, two_pass)
        o_t = acc_ref[...] / jnp.sum(l8, axis=0, keepdims=True)
        for g in range(B_GROUP):
            cols = slice((n * B_GROUP + g) * B_HEAD_DIM, (n * B_GROUP + g + 1) * B_HEAD_DIM)
            o = o_t[:, g * TQ:(g + 1) * TQ].T
            o_ref[:, cols] = (o * jax.nn.silu(z_ref[:, cols].astype(F32))).astype(BF16)


def _dsa(proj, tabs, iwt, k, vt, iklo, ikhi, gq, gk, bsz, s):
    nq = s // TQ
    bound = (1.02 * B_HEAD_DIM ** 0.5 * LOG2E * jnp.max(jnp.abs(gq)) * jnp.max(jnp.abs(gk))).reshape(1, 1)
    topk = min(TOPK_MAX, s // 4)
    gw = B_GROUP * TQ
    rowblk = lambda c: pl.BlockSpec((TQ, BLK), lambda b, i: (b * nq + i, c))
    tok = pl.BlockSpec((1, TQ, LANES), lambda b, i: (b, i, 0))
    ik = pl.BlockSpec((1, s, LANES), lambda b, i: (b, 0, 0))
    tri = jnp.tril(jnp.ones((CB, CB), BF16))
    return pl.pallas_call(
        functools.partial(_dsa_kernel, topk),
        grid=(bsz, nq),
        in_specs=[rowblk(COL_BQ), rowblk(COL_BZ), rowblk(COL_IQ), tok, tok, tok, tok,
                  pl.BlockSpec((1, IDX_HEADS, TQ), lambda b, i: (b, 0, i)),
                  pl.BlockSpec((1, B_KV_HEADS, s, B_HEAD_DIM), lambda b, i: (b, 0, 0, 0)),
                  pl.BlockSpec((1, B_KV_HEADS, s // CB, B_HEAD_DIM, CB), lambda b, i: (b, 0, 0, 0, 0)),
                  ik, ik,
                  pl.BlockSpec((1, B_HEAD_DIM), lambda b, i: (0, 0)),
                  pl.BlockSpec((CB, CB), lambda b, i: (0, 0)),
                  pl.BlockSpec(memory_space=pltpu.SMEM)],
        out_specs=pl.BlockSpec((TQ, B_WIDTH), lambda b, i: (b * nq + i, 0)),
        out_shape=jax.ShapeDtypeStruct((bsz * s, B_WIDTH), BF16),
        scratch_shapes=[
            pltpu.VMEM((s, TQ), F32),
            pltpu.VMEM((s, TQ), F32),
            pltpu.VMEM((s, gw), F32),
            pltpu.VMEM((B_KV_HEADS, gw, B_HEAD_DIM), BF16),
            pltpu.VMEM((IDX_HEADS // 4, 2 * TQ, LANES), BF16),
            pltpu.VMEM((B_HEAD_DIM, gw), F32),
        ],
        compiler_params=_cparams(("parallel", "arbitrary")),
        name="dsa",
    )(proj, proj, proj, *tabs, iwt, k, vt, iklo, ikhi, gq, tri, bound)


def _merge_kernel(au_ref, av_ref, az_ref, lng_ref, lnb_ref, ws_ref, sbt_ref,
                  mq_ref, mz_ref, km_ref, vm_ref, gqm_ref,
                  tb_ref, ga_ref, gb_ref, gm_ref, x_ref,
                  wa_hbm, wb_hbm, wm_hbm, wo_hbm, o_ref,
                  wa_ref, wb_ref, wm_ref, wo_ref, stage_ref, sem, ta0_ref, tm0_ref, ta1_ref, tm1_ref,
                  gate_ref, vn_ref, merged_ref):
    step = pl.program_id(0)

    @pl.when(step == 0)
    def _():
        ta1_ref[...] = jnp.zeros(ta1_ref.shape, BF16)
        tm1_ref[...] = jnp.zeros(tm1_ref.shape, BF16)
        chunks = [(src, dst, r) for src, dst in ((wa_hbm, wa_ref), (wb_hbm, wb_ref), (wm_hbm, wm_ref),
                                                 (wo_hbm, wo_ref))
                  for r in range(0, src.shape[0], MERGE_STAGE_ROWS)]

        def copy(j):
            src, _, r = chunks[j]
            return pltpu.make_async_copy(src.at[pl.ds(r, MERGE_STAGE_ROWS), :], stage_ref.at[j % 2],
                                         sem.at[j % 2])

        copy(0).start()
        for j, (_, dst, r) in enumerate(chunks):
            if j + 1 < len(chunks):
                copy(j + 1).start()
            copy(j).wait()
            dst[pl.ds(r, MERGE_STAGE_ROWS), :] = stage_ref[j % 2].astype(BF16)

    nblk = D_MODEL // MERGE_COLS

    def body(ta_new, tm_new, ta_old, tm_old):
        def branch(t_ref, g_ref, w_ref, cols):
            y = jnp.dot(t_ref[...], w_ref[:, cols], preferred_element_type=F32)
            return jax.nn.sigmoid(g_ref[:, cols].astype(F32)) * y

        def merge_block(b):
            cols = slice(b * MERGE_COLS, (b + 1) * MERGE_COLS)
            merged_ref[:, cols] = (branch(ta_old, ga_ref, wa_ref, cols) + branch(tb_ref, gb_ref, wb_ref, cols)
                                   + branch(tm_old, gm_ref, wm_ref, cols)).astype(BF16)

        def out_block(b):
            cols = slice(b * MERGE_COLS, (b + 1) * MERGE_COLS)
            o_ref[:, cols] = x_ref[:, cols] + jnp.dot(merged_ref[...], wo_ref[:, cols],
                                                      preferred_element_type=F32)

        matrix_work = ([functools.partial(merge_block, b) for b in range(nblk)]
                       + [functools.partial(out_block, b) for b in range(nblk)])
        pairs = lambda n: [tuple(range(i, i + 2)) for i in range(0, n, 2)]
        vector_work = (
            [functools.partial(_memattn_heads, h, mq_ref, mz_ref, km_ref, vm_ref, gqm_ref, tm_new)
             for h in pairs(M_HEADS)]
            + [functools.partial(fn, slice(c * CHUNK, (c + 1) * CHUNK), *refs)
               for c in range(gate_ref.shape[0] // CHUNK)
               for fn, refs in ((_gmlp_gate, (au_ref, az_ref, gate_ref)),
                                (_gmlp_norm, (av_ref, lng_ref, lnb_ref, vn_ref)))]
            + [functools.partial(_gmlp_mix, g, ws_ref, sbt_ref, vn_ref, gate_ref, ta_new)
               for g in pairs(A_GROUPS)])
        slots = len(matrix_work) - 1
        for k in range(slots):
            matrix_work[k]()
            for j in range(k * len(vector_work) // slots, (k + 1) * len(vector_work) // slots):
                vector_work[j]()
        matrix_work[-1]()

    lax.cond(step % 2 == 0,
             lambda: body(ta0_ref, tm0_ref, ta1_ref, tm1_ref),
             lambda: body(ta1_ref, tm1_ref, ta0_ref, tm0_ref))


def _merge(proj, tb, x2, lng, lnb, ws, sbt, km, vm, gqm, wa, wb, wm, wo, s):
    m = x2.shape[0]
    tm = min(256, s)
    per_b, n = s // tm, m // tm
    new = lambda i: jnp.minimum(i, n - 1)
    old = lambda i: jnp.maximum(i - 1, 0)
    col = lambda c: pl.BlockSpec((tm, BLK), lambda i: (new(i), c))
    full = lambda shape: pl.BlockSpec(shape, lambda i: (0,) * len(shape))
    kv_spec = pl.BlockSpec((1,) + km.shape[1:], lambda i: (new(i) // per_b, 0, 0, 0))
    gate = lambda c: pl.BlockSpec((tm, D_MODEL), lambda i: (old(i), c // 2))
    wide = pl.BlockSpec((tm, D_MODEL), lambda i: (old(i), 0))
    hbm = pl.BlockSpec(memory_space=pl.ANY)
    return pl.pallas_call(
        _merge_kernel,
        grid=(n + 1,),
        in_specs=[col(COL_AU), col(COL_AV), col(COL_AZ), full((1, A_WIDTH)), full((1, A_WIDTH)),
                  full((A_GROUPS, CHUNK, CHUNK)), full((CHUNK, A_GROUPS)),
                  col(COL_MQ), col(COL_MZ), kv_spec, kv_spec, full((1, M_HEAD_DIM)),
                  pl.BlockSpec((tm, BLK), lambda i: (old(i), 0)),
                  gate(COL_GA), gate(COL_GB), gate(COL_GM), wide, hbm, hbm, hbm, hbm],
        out_specs=wide,
        out_shape=jax.ShapeDtypeStruct((m, D_MODEL), F32),
        scratch_shapes=[pltpu.VMEM(w.shape, BF16) for w in (wa, wb, wm, wo)]
        + [pltpu.VMEM((2, MERGE_STAGE_ROWS, D_MODEL), F32), pltpu.SemaphoreType.DMA((2,))]
        + [pltpu.VMEM((tm, A_WIDTH), BF16), pltpu.VMEM((tm, M_WIDTH), BF16)] * 2
        + [pltpu.VMEM((tm, A_WIDTH), F32), pltpu.VMEM((tm, A_WIDTH), BF16), pltpu.VMEM((tm, D_MODEL), BF16)],
        compiler_params=_cparams(("arbitrary",)),
        name="merge",
    )(proj, proj, proj, lng, lnb, ws, sbt, proj, proj, km, vm, gqm, tb, proj, proj, proj, x2, wa, wb, wm, wo)


def _rope_freqs():
    inv = lambda half: ROPE_THETA ** (-np.arange(half, dtype=np.float32) / half)
    freq = np.zeros((1, LANES), np.float32)
    hb, hi = B_HEAD_DIM // 8, IDX_DIM // 8
    freq[0, :hb] = inv(hb)
    freq[0, hb:hb + hi] = inv(hi)
    return jnp.asarray(freq)


def _layer(x, mem, positions, norm_gain, w_in, gmlp_ln_gain, gmlp_ln_bias, spatial_w, spatial_b,
           w_branch_a, q_norm_gain, k_norm_gain, idx_k_ln_gain, idx_k_ln_bias, w_branch_b,
           mem_norm_gain, w_mem_kv, mem_q_norm_gain, mem_k_norm_gain, w_branch_m, w_out):
    bsz, s, _ = x.shape
    m = bsz * s
    row = lambda a: a.reshape(1, -1).astype(F32)
    pad_lanes = lambda a: jnp.pad(a.reshape(1, -1).astype(F32), ((0, 0), (0, LANES - a.shape[-1])))
    x2 = x.reshape(m, D_MODEL)

    w_t = w_in.T
    h, k, vt, iklo, ikhi, cb, sb, ci, si, iwt = _kvprep(
        x, row(norm_gain), w_t, positions.reshape(bsz, s, 1).astype(jnp.int32), _rope_freqs(),
        row(k_norm_gain), pad_lanes(idx_k_ln_gain), pad_lanes(idx_k_ln_bias))
    proj = _inproj(h, w_t)

    km, vm = _memkv(mem, row(mem_norm_gain), w_mem_kv, row(mem_k_norm_gain))
    t_b = _dsa(proj, (cb, sb, ci, si), iwt, k, vt, iklo, ikhi, row(q_norm_gain), row(k_norm_gain), bsz, s)

    out = _merge(proj, t_b, x2, row(gmlp_ln_gain), row(gmlp_ln_bias), spatial_w.astype(F32),
                 spatial_b.T.astype(F32), km, vm, row(mem_q_norm_gain),
                 w_branch_a, w_branch_b, w_branch_m, w_out, s)
    return out.reshape(bsz, s, D_MODEL)


def kernel(x, mem, positions, norm_gain, w_in, gmlp_ln_gain, gmlp_ln_bias, spatial_w, spatial_b, w_branch_a, q_norm_gain, k_norm_gain, idx_k_ln_gain, idx_k_ln_bias, w_branch_b, mem_norm_gain, w_mem_kv, mem_q_norm_gain, mem_k_norm_gain, w_branch_m, w_out):
    for l in range(norm_gain.shape[0]):
        x = _layer(x, mem, positions, norm_gain[l], w_in[l], gmlp_ln_gain[l], gmlp_ln_bias[l],
                   spatial_w[l], spatial_b[l], w_branch_a[l], q_norm_gain[l], k_norm_gain[l],
                   idx_k_ln_gain[l], idx_k_ln_bias[l], w_branch_b[l], mem_norm_gain[l], w_mem_kv[l],
                   mem_q_norm_gain[l], mem_k_norm_gain[l], w_branch_m[l], w_out[l])
    return x
```

```python
import functools

import numpy as np
import jax
import jax.numpy as jnp
from jax import lax
from jax.experimental import pallas as pl
from jax.experimental.pallas import tpu as pltpu

F32 = jnp.float32
BF16 = jnp.bfloat16

D_MODEL = 2048
ROPE_THETA = 500000.0
EPS = 1e-6
A_GROUPS = 8
A_GROUP_DIM = 128
A_WIDTH = A_GROUPS * A_GROUP_DIM
CHUNK = 128
B_HEADS = 8
B_KV_HEADS = 2
B_GROUP = B_HEADS // B_KV_HEADS
B_HEAD_DIM = 128
B_WIDTH = B_HEADS * B_HEAD_DIM
IDX_HEADS = 16
IDX_DIM = 64
TOPK_MAX = 256
M_HEADS = 4
M_HEAD_DIM = 256
M_WIDTH = M_HEADS * M_HEAD_DIM

SPLIT_SIZES = (
    A_WIDTH, A_WIDTH, A_WIDTH,
    B_WIDTH, B_KV_HEADS * B_HEAD_DIM, B_KV_HEADS * B_HEAD_DIM, B_WIDTH,
    IDX_HEADS * IDX_DIM, IDX_DIM, IDX_HEADS,
    M_WIDTH, M_WIDTH,
    D_MODEL, D_MODEL, D_MODEL,
)

LANES = 128
BLK = 1024
COL_AU, COL_AV, COL_AZ, COL_BZ, COL_MQ, COL_MZ = range(6)
COL_GA, COL_GB, COL_GM = 6, 8, 10
NB_MAIN = 12
_OFFS = [int(o) for o in np.concatenate([[0], np.cumsum(SPLIT_SIZES)])]
ROW_ALIGN = 16
MAIN_START = ([_OFFS[i] for i in (0, 1, 2, 6, 10, 11)]
              + [_OFFS[i] + d for i in (12, 13, 14) for d in (0, BLK)])
assert all(s % ROW_ALIGN == 0 for s in MAIN_START)
Q_START, IQ_START, HALF_BLK = _OFFS[3], _OFFS[7], BLK // 2
SMALL_KV_START, SMALL_KV_WIDTH = _OFFS[4], _OFFS[6] - _OFFS[4]
SMALL_IDX_START = _OFFS[8]
assert Q_START % BLK == 0 and IQ_START % HALF_BLK == 0
assert SMALL_KV_START % SMALL_KV_WIDTH == 0 and SMALL_IDX_START % LANES == 0
assert IDX_DIM + IDX_HEADS <= LANES and _OFFS[9] == SMALL_IDX_START + IDX_DIM

VMEM_LIMIT = 56 * 1024 * 1024
LOG2E = 1.4426950408889634
NEG_BIG = -1e30
F32_MAX = 3.4028234663852886e38

INPROJ_TM, INPROJ_TN = 2048, 1024
KVPREP_TP = 512
TQ = 256
CK = 256
CB = 512
COUNT_WAYS = 8
SEARCH_MARGIN = 0.05
SEARCH_FIXED_PASSES = 11
SEARCH_INTERP_PASSES = 13
MAX_SEARCH_ITERS = 400
MERGE_STAGE_ROWS = 512
MAX_SHIFT_BOUND = 48.0


def _cparams(sem):
    return pltpu.CompilerParams(dimension_semantics=sem, vmem_limit_bytes=VMEM_LIMIT)


def _dot_nt(a, b):
    return lax.dot_general(a, b, (((1,), (1,)), ((), ())), preferred_element_type=F32)


def _rope(x, cos_t, sin_t, half, period):
    lane = lax.broadcasted_iota(jnp.int32, x.shape, 1) % period
    rolled = jnp.where(lane < half, pltpu.roll(x, LANES - half, 1), pltpu.roll(x, half, 1))
    return x * cos_t + rolled * sin_t


def _rms_rows(x, gain):
    ms = jnp.mean(x * x, axis=-1, keepdims=True)
    return (x * lax.rsqrt(ms + EPS) * gain).astype(BF16)


def _inproj_kernel(tab_ref, h_ref, wt_ref, proj_ref):
    del tab_ref
    proj_ref[...] = _dot_nt(h_ref[...], wt_ref[...].astype(BF16)).astype(BF16)


def _inproj(h, w_t):
    m = h.shape[0]
    tm = min(INPROJ_TM, m)
    starts = [s + d for s in MAIN_START for d in range(0, BLK, INPROJ_TN)]
    tab = jnp.asarray(np.array([s // ROW_ALIGN for s in starts], np.int32))
    return pl.pallas_call(
        _inproj_kernel,
        grid_spec=pltpu.PrefetchScalarGridSpec(
            num_scalar_prefetch=1,
            grid=(m // tm, len(starts)),
            in_specs=[
                pl.BlockSpec((tm, D_MODEL), lambda i, n, tab: (i, 0)),
                pl.BlockSpec((pl.Element(INPROJ_TN), pl.Element(D_MODEL)),
                             lambda i, n, tab: (tab[n] * ROW_ALIGN, 0)),
            ],
            out_specs=pl.BlockSpec((tm, INPROJ_TN), lambda i, n, tab: (i, n)),
        ),
        out_shape=jax.ShapeDtypeStruct((m, NB_MAIN * BLK), BF16),
        compiler_params=_cparams(("parallel", "arbitrary")),
        name="inproj",
    )(tab, h, w_t)


def _kvprep_kernel(x_ref, g_ref, wkv_ref, widx_ref, wq_ref, wiqa_ref, wiqb_ref, pos_ref, fc_ref,
                   gk_ref, gq_ref, lng_ref, lnb_ref,
                   h_ref, q_ref, iq_ref, k_ref, vt_ref, iklo_ref, ikhi_ref, iwt_ref,
                   wkv_bf_ref, widx_bf_ref, wq_bf_ref, wiq_bf_ref):
    @pl.when(jnp.logical_and(pl.program_id(0) == 0, pl.program_id(1) == 0))
    def _():
        wkv_bf_ref[...] = wkv_ref[...].astype(BF16)
        widx_bf_ref[...] = widx_ref[...].astype(BF16)
        wq_bf_ref[...] = wq_ref[...].astype(BF16)
        wiq_bf_ref[:HALF_BLK, :] = wiqa_ref[...].astype(BF16)
        wiq_bf_ref[HALF_BLK:, :] = wiqb_ref[...].astype(BF16)

    h = _rms_rows(x_ref[...], g_ref[...])
    h_ref[...] = h
    kv = _dot_nt(h, wkv_bf_ref[...])
    ikp = _dot_nt(h, widx_bf_ref[...])

    hb, hi = B_HEAD_DIM // 8, IDX_DIM // 8
    ang = pos_ref[0].astype(F32) * fc_ref[...]
    cos_c, sin_c = jnp.cos(ang), jnp.sin(ang)
    lane = lax.broadcasted_iota(jnp.int32, ang.shape, 1)
    cos_b = jnp.where(lane < hb, cos_c, jnp.where(lane < 2 * hb, pltpu.roll(cos_c, hb, 1), 1.0))
    sin_b = jnp.where(lane < hb, -sin_c, jnp.where(lane < 2 * hb, pltpu.roll(sin_c, hb, 1), 0.0))
    cos_i = jnp.ones_like(cos_c)
    sin_i = jnp.zeros_like(sin_c)
    for head in range(LANES // IDX_DIM):
        for part, sign in enumerate((-1.0, 1.0)):
            first = head * IDX_DIM + part * hi
            here = jnp.logical_and(lane >= first, lane < first + hi)
            shift = (first - hb) % LANES
            cos_i = jnp.where(here, pltpu.roll(cos_c, shift, 1), cos_i)
            sin_i = jnp.where(here, sign * pltpu.roll(sin_c, shift, 1), sin_i)
    nblk = q_ref.shape[0]
    gain = gq_ref[...] * (B_HEAD_DIM ** -0.5 * LOG2E)
    qp = _dot_nt(h, wq_bf_ref[...])
    for hd in range(B_HEADS):
        slab = qp[:, hd * B_HEAD_DIM:(hd + 1) * B_HEAD_DIM]
        r = lax.rsqrt(jnp.mean(slab * slab, axis=-1, keepdims=True) + EPS)
        qr = _rope(slab * r * gain, cos_b, sin_b, B_HEAD_DIM // 8, LANES).astype(BF16)
        for blk in range(nblk):
            q_ref[blk, hd] = qr[blk * TQ:(blk + 1) * TQ, :]
    iqp = _dot_nt(h, wiq_bf_ref[...])
    for j in range(IDX_HEADS // 2):
        ir = _rope(iqp[:, j * LANES:(j + 1) * LANES], cos_i, sin_i, IDX_DIM // 8, IDX_DIM).astype(BF16)
        for blk in range(nblk):
            iq_ref[blk, j // 2, (j % 2) * TQ:(j % 2 + 1) * TQ, :] = ir[blk * TQ:(blk + 1) * TQ, :]

    for n in range(B_KV_HEADS):
        kh = kv[:, n * B_HEAD_DIM:(n + 1) * B_HEAD_DIM]
        r = lax.rsqrt(jnp.mean(kh * kh, axis=-1, keepdims=True) + EPS)
        kn = kh * r * gk_ref[...]
        k_ref[0, n] = _rope(kn, cos_b, sin_b, B_HEAD_DIM // 8, LANES).astype(BF16)
        vt = kv[:, (B_KV_HEADS + n) * B_HEAD_DIM:(B_KV_HEADS + n + 1) * B_HEAD_DIM].T
        for c in range(vt.shape[1] // CB):
            vt_ref[0, n, c] = vt[:, c * CB:(c + 1) * CB].astype(BF16)

    lane = lax.broadcasted_iota(jnp.int32, ikp.shape, 1)
    live = lane < IDX_DIM
    mu = jnp.sum(jnp.where(live, ikp, 0.0), axis=-1, keepdims=True) * (1.0 / IDX_DIM)
    d = jnp.where(live, ikp - mu, 0.0)
    var = jnp.sum(d * d, axis=-1, keepdims=True) * (1.0 / IDX_DIM)
    y = d * lax.rsqrt(var + EPS) * lng_ref[...] + lnb_ref[...]
    yr = jnp.where(live, _rope(y, cos_i, sin_i, IDX_DIM // 8, IDX_DIM), 0.0)
    iklo_ref[0] = yr.astype(BF16)
    ikhi_ref[0] = pltpu.roll(yr, IDX_DIM, 1).astype(BF16)

    iw = ikp * (IDX_DIM ** -0.5 * IDX_HEADS ** -0.5)
    iwt_ref[0] = iw.T[IDX_DIM:IDX_DIM + IDX_HEADS, :]


def _kvprep(x3, gain, w_t, pos3, freqs, gk, gq, lng, lnb):
    bsz, s, _ = x3.shape
    tp = min(KVPREP_TP, s)
    per_b, nblk = s // tp, tp // TQ
    row = lambda b, i: (b, i, 0)
    const = lambda b, i: (0, 0)
    tab_spec = pl.BlockSpec((1, LANES), const)
    wrows = lambda rows, start: pl.BlockSpec((rows, D_MODEL), lambda b, i: (start // rows, 0),
                                             pipeline_mode=pl.Buffered(1))
    out_h = pl.BlockSpec((tp, D_MODEL), lambda b, i: (b * per_b + i, 0))
    out_q = pl.BlockSpec((nblk, B_HEADS, TQ, B_HEAD_DIM), lambda b, i: (b * per_b + i, 0, 0, 0))
    out_iq = pl.BlockSpec((nblk, IDX_HEADS // 4, 2 * TQ, LANES), lambda b, i: (b * per_b + i, 0, 0, 0))
    out_tok = pl.BlockSpec((1, tp, LANES), row)
    out_k = pl.BlockSpec((1, B_KV_HEADS, tp, B_HEAD_DIM), lambda b, i: (b, 0, i, 0))
    out_vt = pl.BlockSpec((1, B_KV_HEADS, tp // CB, B_HEAD_DIM, CB), lambda b, i: (b, 0, i, 0, 0))
    out_iwt = pl.BlockSpec((1, IDX_HEADS, tp), lambda b, i: (b, 0, i))
    nq = bsz * s // TQ
    return pl.pallas_call(
        _kvprep_kernel,
        grid=(bsz, per_b),
        in_specs=[pl.BlockSpec((None, tp, D_MODEL), row), pl.BlockSpec((1, D_MODEL), const),
                  wrows(SMALL_KV_WIDTH, SMALL_KV_START), wrows(LANES, SMALL_IDX_START),
                  wrows(BLK, Q_START), wrows(HALF_BLK, IQ_START), wrows(HALF_BLK, IQ_START + HALF_BLK),
                  pl.BlockSpec((1, tp, 1), row),
                  tab_spec, tab_spec, tab_spec, tab_spec, tab_spec],
        out_specs=[out_h, out_q, out_iq, out_k, out_vt, out_tok, out_tok, out_iwt],
        out_shape=[
            jax.ShapeDtypeStruct((bsz * s, D_MODEL), BF16),
            jax.ShapeDtypeStruct((nq, B_HEADS, TQ, B_HEAD_DIM), BF16),
            jax.ShapeDtypeStruct((nq, IDX_HEADS // 4, 2 * TQ, LANES), BF16),
            jax.ShapeDtypeStruct((bsz, B_KV_HEADS, s, B_HEAD_DIM), BF16),
            jax.ShapeDtypeStruct((bsz, B_KV_HEADS, s // CB, B_HEAD_DIM, CB), BF16),
            jax.ShapeDtypeStruct((bsz, s, LANES), BF16),
            jax.ShapeDtypeStruct((bsz, s, LANES), BF16),
            jax.ShapeDtypeStruct((bsz, IDX_HEADS, s), F32),
        ],
        scratch_shapes=[pltpu.VMEM((SMALL_KV_WIDTH, D_MODEL), BF16), pltpu.VMEM((LANES, D_MODEL), BF16),
                        pltpu.VMEM((BLK, D_MODEL), BF16), pltpu.VMEM((BLK, D_MODEL), BF16)],
        compiler_params=_cparams(("arbitrary", "arbitrary")),
        name="kvprep",
    )(x3, gain, w_t, w_t, w_t, w_t, w_t, pos3, freqs, gk, gq, lng, lnb)


def _gelu_tanh(x):
    c = -2.0 * (2.0 / np.pi) ** 0.5 * LOG2E
    return x / (1.0 + jnp.exp2(x * (x * x * (0.044715 * c) + c)))


def _gmlp_kernel(u_ref, v_ref, z_ref, lng_ref, lnb_ref, ws_ref, sbt_ref, o_ref):
    tm = u_ref.shape[0]
    u = _gelu_tanh(u_ref[...].astype(F32))
    v = _gelu_tanh(v_ref[...].astype(F32))
    mu = jnp.mean(v, axis=-1, keepdims=True)
    d = v - mu
    var = jnp.mean(d * d, axis=-1, keepdims=True)
    vn = (d * lax.rsqrt(var + EPS) * lng_ref[...] + lnb_ref[...]).astype(BF16)
    gate = u * jax.nn.silu(z_ref[...].astype(F32))
    tri = (lax.broadcasted_iota(jnp.int32, (CHUNK, CHUNK), 1)
           <= lax.broadcasted_iota(jnp.int32, (CHUNK, CHUNK), 0))
    for g in range(A_GROUPS):
        wg = jnp.where(tri, ws_ref[g], 0.0).astype(BF16)
        bias = sbt_ref[:, g:g + 1]
        cols = slice(g * A_GROUP_DIM, (g + 1) * A_GROUP_DIM)
        for c in range(tm // CHUNK):
            rows = slice(c * CHUNK, (c + 1) * CHUNK)
            sg = jnp.dot(wg, vn[rows, cols], preferred_element_type=F32) + bias
            o_ref[rows, cols] = (gate[rows, cols] * sg).astype(BF16)


def _gmlp(proj, lng, lnb, ws, sbt):
    m = proj.shape[0]
    tm = min(512, m)
    col = lambda c: pl.BlockSpec((tm, BLK), lambda i: (i, c))
    full = lambda shape: pl.BlockSpec(shape, lambda i: (0,) * len(shape))
    return pl.pallas_call(
        _gmlp_kernel,
        grid=(m // tm,),
        in_specs=[col(COL_AU), col(COL_AV), col(COL_AZ), full((1, A_WIDTH)), full((1, A_WIDTH)),
                  full((A_GROUPS, CHUNK, CHUNK)), full((CHUNK, A_GROUPS))],
        out_specs=pl.BlockSpec((tm, A_WIDTH), lambda i: (i, 0)),
        out_shape=jax.ShapeDtypeStruct((m, A_WIDTH), BF16),
        compiler_params=_cparams(("parallel",)),
        name="gmlp",
    )(proj, proj, proj, lng, lnb, ws, sbt)


def _memkv_kernel(mem_ref, g_ref, w_ref, gk_ref, km_ref, vm_ref, wbf_ref):
    @pl.when(pl.program_id(0) == 0)
    def _():
        wbf_ref[...] = w_ref[...].astype(BF16)

    h = _rms_rows(mem_ref[0], g_ref[...])
    kv = jnp.dot(h, wbf_ref[...], preferred_element_type=F32)
    for hd in range(M_HEADS):
        kh = kv[:, hd * M_HEAD_DIM:(hd + 1) * M_HEAD_DIM]
        r = lax.rsqrt(jnp.mean(kh * kh, axis=-1, keepdims=True) + EPS)
        km_ref[0, hd] = (kh * r * gk_ref[...]).astype(BF16)
        vm_ref[0, hd] = kv[:, M_WIDTH + hd * M_HEAD_DIM:M_WIDTH + (hd + 1) * M_HEAD_DIM].astype(BF16)


def _memkv(mem, gain, w_kv, gk):
    bsz, ml, _ = mem.shape
    out = pl.BlockSpec((1, M_HEADS, ml, M_HEAD_DIM), lambda b: (b, 0, 0, 0))
    shp = jax.ShapeDtypeStruct((bsz, M_HEADS, ml, M_HEAD_DIM), BF16)
    return pl.pallas_call(
        _memkv_kernel,
        grid=(bsz,),
        in_specs=[pl.BlockSpec((1, ml, D_MODEL), lambda b: (b, 0, 0)),
                  pl.BlockSpec((1, D_MODEL), lambda b: (0, 0)),
                  pl.BlockSpec((D_MODEL, 2 * M_WIDTH), lambda b: (0, 0), pipeline_mode=pl.Buffered(1)),
                  pl.BlockSpec((1, M_HEAD_DIM), lambda b: (0, 0))],
        out_specs=[out, out],
        out_shape=[shp, shp],
        scratch_shapes=[pltpu.VMEM((D_MODEL, 2 * M_WIDTH), BF16)],
        compiler_params=_cparams(("arbitrary",)),
        name="memkv",
    )(mem, gain, w_kv, gk)


def _memattn_kernel(q_ref, z_ref, km_ref, vm_ref, gq_ref, o_ref):
    qscale = M_HEAD_DIM ** -0.5 * LOG2E
    for hd in range(M_HEADS):
        cols = slice(hd * M_HEAD_DIM, (hd + 1) * M_HEAD_DIM)
        q = q_ref[:, cols].astype(F32)
        r = lax.rsqrt(jnp.mean(q * q, axis=-1, keepdims=True) + EPS)
        qn = (q * r * gq_ref[...] * qscale).astype(BF16)
        lg = _dot_nt(qn, km_ref[0, hd])
        p = jnp.exp2(lg - jnp.max(lg, axis=-1, keepdims=True))
        l = jnp.sum(p, axis=-1, keepdims=True)
        o = jnp.dot(p.astype(BF16), vm_ref[0, hd], preferred_element_type=F32) / l
        o_ref[:, cols] = (o * jax.nn.silu(z_ref[:, cols].astype(F32))).astype(BF16)


def _memattn(proj, km, vm, gq, s):
    m = proj.shape[0]
    tm = min(512, s)
    per_b = s // tm
    ml = km.shape[2]
    kv_spec = pl.BlockSpec((1, M_HEADS, ml, M_HEAD_DIM), lambda i: (i // per_b, 0, 0, 0))
    return pl.pallas_call(
        _memattn_kernel,
        grid=(m // tm,),
        in_specs=[pl.BlockSpec((tm, BLK), lambda i: (i, COL_MQ)),
                  pl.BlockSpec((tm, BLK), lambda i: (i, COL_MZ)),
                  kv_spec, kv_spec,
                  pl.BlockSpec((1, M_HEAD_DIM), lambda i: (0, 0))],
        out_specs=pl.BlockSpec((tm, M_WIDTH), lambda i: (i, 0)),
        out_shape=jax.ShapeDtypeStruct((m, M_WIDTH), BF16),
        compiler_params=_cparams(("parallel",)),
        name="memattn",
    )(proj, proj, km, vm, gq)


def _dsa_kernel(topk, q_ref, z_ref, iq_ref, iwt_ref,
                k_ref, vt_ref, iklo_ref, ikhi_ref, tri_ref, bound_ref, o_ref,
                sc_ref, bias_ref, lg_ref, acc_ref):
    qb = pl.program_id(1)
    nck = (qb * TQ + TQ + CK - 1) // CK
    nbig = (qb * TQ + TQ + CB - 1) // CB
    t_lane = qb * TQ + lax.broadcasted_iota(jnp.int32, (1, TQ), 1)
    gw = B_GROUP * TQ

    def fold8(a):
        return a.reshape(a.shape[0] // 8, 8, a.shape[1])

    def grouped_loop(trips, body, init, per_step=2, start=0):
        carry, done = init, start
        while per_step >= 1:
            def step(i, c, first=done, n=per_step):
                for u in range(n):
                    c = body(first + n * i + u, c)
                return c

            steps = (trips - done) // per_step
            carry = lax.fori_loop(0, steps, step, carry)
            done = done + steps * per_step
            per_step //= 2
        return carry

    wt = iwt_ref[0]

    def idx_body(c, carry):
        mn8, mx8, s1, s2 = carry
        off = c * CK if isinstance(c, int) else pl.multiple_of(c * CK, CK)
        acc = jnp.zeros((CK, TQ), F32)
        for jj in range(IDX_HEADS // 4):
            rhs = iq_ref[0, jj]
            for half, keys_ref in enumerate((iklo_ref, ikhi_ref)):
                d = _dot_nt(keys_ref[0, pl.ds(off, CK), :], rhs)
                ha, hb = 4 * jj + half, 4 * jj + 2 + half
                acc = (acc + jnp.maximum(d[:, :TQ], 0.0) * wt[ha:ha + 1, :]
                       + jnp.maximum(d[:, TQ:], 0.0) * wt[hb:hb + 1, :])
        key = off + lax.broadcasted_iota(jnp.int32, (CK, TQ), 0)
        causal = key <= t_lane
        sc = jnp.where(causal, acc, -jnp.inf)
        sc_ref[pl.ds(off, CK), :] = sc
        live = jnp.where(causal, acc, 0.0)
        mn8 = jnp.minimum(mn8, jnp.min(fold8(jnp.where(causal, acc, jnp.inf)), axis=0))
        mx8 = jnp.maximum(mx8, jnp.max(fold8(sc), axis=0))
        s1 = s1 + jnp.sum(fold8(live), axis=0)
        s2 = s2 + jnp.sum(fold8(live * live), axis=0)
        return mn8, mx8, s1, s2

    zero8 = jnp.zeros((8, TQ), F32)
    stats = (jnp.full((8, TQ), jnp.inf, F32), jnp.full((8, TQ), -jnp.inf, F32), zero8, zero8)
    mn8, mx8, s1, s2 = grouped_loop(nck, idx_body, stats, per_step=4)
    row_min = jnp.min(mn8, axis=0, keepdims=True)
    row_max = jnp.max(mx8, axis=0, keepdims=True)

    def fill_body(c, carry):
        sc_ref[pl.ds(pl.multiple_of(c * CK, CK), CK), :] = jnp.full((CK, TQ), -jnp.inf, F32)
        return carry

    lax.fori_loop(nck, nbig * (CB // CK), fill_body, 0)

    def count_rows(pred):
        rows = 8 * COUNT_WAYS

        def body(c, cnt):
            off = pl.multiple_of(c * CB, CB)
            for r in range(CB // rows):
                cnt = cnt + jnp.where(pred(sc_ref[pl.ds(off + r * rows, rows), :]), 1.0, 0.0)
            return cnt

        cnt = lax.fori_loop(0, nbig, body, jnp.zeros((rows, TQ), F32))
        return jnp.sum(cnt, axis=0, keepdims=True)

    kf = float(topk)
    n_valid = (t_lane + 1).astype(F32)
    all_rows = (t_lane + 1) <= topk

    def search_pass(st, probe, stuck):
        lo, hi, clo, chi, thr, fin, tie = st
        cnt = count_rows(lambda blk: blk >= probe)
        active = fin < 0.5
        hit = cnt == kf
        end_thr = jnp.where(jnp.logical_and(stuck, cnt < kf), lo, probe)
        ends = jnp.logical_and(active, jnp.logical_or(hit, stuck))
        thr = jnp.where(ends, end_thr, thr)
        tie = jnp.where(jnp.logical_and(ends, jnp.logical_not(hit)), 1.0, tie)
        fin = jnp.where(ends, 1.0, fin)
        up = jnp.logical_and(active, cnt > kf)
        dn = jnp.logical_and(active, cnt < kf)
        return (jnp.where(up, probe, lo), jnp.where(dn, probe, hi), jnp.where(up, cnt, clo),
                jnp.where(dn, cnt, chi), thr, fin, tie)

    def next_probe(st, halve=False):
        lo, hi, clo, chi = st[:4]
        frac = jnp.clip((clo - kf + 0.5) / (clo - chi + 1.0), SEARCH_MARGIN, 1.0 - SEARCH_MARGIN)
        probe = lo + (hi - lo) * frac
        outside = jnp.logical_or(jnp.logical_or(probe <= lo, probe >= hi), halve)
        probe = jnp.where(outside, lo * 0.5 + hi * 0.5, probe)
        stuck = jnp.logical_or(probe <= lo, probe >= hi)
        return jnp.where(stuck, hi, probe), stuck

    mean = jnp.sum(s1, axis=0, keepdims=True) / n_valid
    var = jnp.maximum(jnp.sum(s2, axis=0, keepdims=True) / n_valid - mean * mean, 0.0)
    tail = jnp.clip(kf / n_valid, 1e-6, 1.0 - 1e-6)
    tq = jnp.sqrt(-2.0 * jnp.log(jnp.minimum(tail, 1.0 - tail)))
    zq = tq - ((0.010328 * tq + 0.802853) * tq + 2.515517) / (
        ((0.001308 * tq + 0.189269) * tq + 1.432788) * tq + 1.0)
    guess = mean + jnp.where(tail < 0.5, zq, -zq) * jnp.sqrt(var)
    inside = jnp.logical_and(guess >= row_min, guess <= row_max)
    probe0 = jnp.where(inside, guess, row_min * 0.5 + row_max * 0.5)

    ones = jnp.ones((1, TQ), F32)
    st = (row_min, row_max, n_valid, 0.0 * ones, jnp.where(all_rows, -F32_MAX, row_max),
          jnp.where(all_rows, 1.0, 0.0), 0.0 * ones)
    st = search_pass(st, probe0, probe0 < row_min)

    def fixed_body(i, st):
        return search_pass(st, *next_probe(st))

    st = lax.fori_loop(0, SEARCH_FIXED_PASSES, fixed_body, st)

    def more_cond(c):
        return jnp.logical_and(c[0] < MAX_SEARCH_ITERS, jnp.min(c[1][5]) < 0.5)

    def more_body(c):
        return c[0] + 1, search_pass(c[1], *next_probe(c[1], c[0] >= SEARCH_INTERP_PASSES))

    _, st = lax.while_loop(more_cond, more_body, (jnp.int32(0), st))
    thr, tie = st[4], st[6]

    def plain_mask():
        def body(c, carry):
            off = pl.multiple_of(c * CB, CB)
            bias_ref[pl.ds(off, CB), :] = jnp.where(sc_ref[pl.ds(off, CB), :] >= thr, 0.0, NEG_BIG)
            return carry

        lax.fori_loop(0, nbig, body, 0)

    def tied_mask():
        n_gt = count_rows(lambda blk: blk > thr)
        need = jnp.where(all_rows, F32_MAX, kf - n_gt)

        def body(c, seen):
            off = pl.multiple_of(c * CB, CB)
            blk = sc_ref[pl.ds(off, CB), :]
            eq = blk == thr
            pref = jnp.dot(tri_ref[...], jnp.where(eq, 1.0, 0.0).astype(BF16),
                           preferred_element_type=F32) + seen
            keep = jnp.logical_or(blk > thr, jnp.logical_and(eq, pref <= need))
            bias_ref[pl.ds(off, CB), :] = jnp.where(keep, 0.0, NEG_BIG)
            return pref[CB - 1:CB, :]

        lax.fori_loop(0, nbig, body, jnp.zeros((1, TQ), F32))

    lax.cond(jnp.max(tie) > 0.5, tied_mask, plain_mask)

    logit_bound = bound_ref[0, 0]

    def masked_logits(n, off):
        b = bias_ref[pl.ds(off, CB), :]
        qn = q_ref[0, n * B_GROUP:(n + 1) * B_GROUP].reshape(gw, B_HEAD_DIM)
        lg = _dot_nt(k_ref[0, n, pl.ds(off, CB), :], qn)
        return jnp.concatenate([lg[:, g * TQ:(g + 1) * TQ] + b for g in range(B_GROUP)], axis=1)

    for n in range(B_KV_HEADS):
        acc_ref[...] = jnp.zeros(acc_ref.shape, F32)

        def one_pass(n=n):
            def body(c, l8):
                off = pl.multiple_of(c * CB, CB)
                p = jnp.exp2(masked_logits(n, off) - logit_bound)
                acc_ref[...] += jnp.dot(vt_ref[0, n, c], p.astype(BF16), preferred_element_type=F32)
                return l8 + jnp.sum(fold8(p), axis=0)

            return grouped_loop(nbig, body, jnp.zeros((8, gw), F32), per_step=4)

        def two_pass(n=n):
            def logit_body(c, mx8):
                off = pl.multiple_of(c * CB, CB)
                lg = masked_logits(n, off)
                lg_ref[pl.ds(off, CB), :] = lg
                return jnp.maximum(mx8, jnp.max(fold8(lg), axis=0))

            mx8 = grouped_loop(nbig, logit_body, jnp.full((8, gw), NEG_BIG, F32))
            m = jnp.max(mx8, axis=0, keepdims=True)

            def pv_body(c, l8):
                off = pl.multiple_of(c * CB, CB)
                p = jnp.exp2(lg_ref[pl.ds(off, CB), :] - m)
                acc_ref[...] += jnp.dot(vt_ref[0, n, c], p.astype(BF16), preferred_element_type=F32)
                return l8 + jnp.sum(fold8(p), axis=0)

            return grouped_loop(nbig, pv_body, jnp.zeros((8, gw), F32))

        l8 = lax.cond(logit_bound <= MAX_SHIFT_BOUND, one_pass, two_pass)
        o_t = acc_ref[...] / jnp.sum(l8, axis=0, keepdims=True)
        for g in range(B_GROUP):
            cols = slice((n * B_GROUP + g) * B_HEAD_DIM, (n * B_GROUP + g + 1) * B_HEAD_DIM)
            o = o_t[:, g * TQ:(g + 1) * TQ].T
            o_ref[:, cols] = (o * jax.nn.silu(z_ref[:, cols].astype(F32))).astype(BF16)


def _dsa(proj, q, iq, iwt, k, vt, iklo, ikhi, gq, gk, bsz, s):
    nq = s // TQ
    bound = (1.02 * B_HEAD_DIM ** 0.5 * LOG2E * jnp.max(jnp.abs(gq)) * jnp.max(jnp.abs(gk))).reshape(1, 1)
    topk = min(TOPK_MAX, s // 4)
    gw = B_GROUP * TQ
    blk4 = lambda a: pl.BlockSpec((1,) + a.shape[1:], lambda b, i: (b * nq + i, 0, 0, 0))
    ik = pl.BlockSpec((1, s, LANES), lambda b, i: (b, 0, 0))
    tri = jnp.tril(jnp.ones((CB, CB), BF16))
    return pl.pallas_call(
        functools.partial(_dsa_kernel, topk),
        grid=(bsz, nq),
        in_specs=[blk4(q), pl.BlockSpec((TQ, BLK), lambda b, i: (b * nq + i, COL_BZ)), blk4(iq),
                  pl.BlockSpec((1, IDX_HEADS, TQ), lambda b, i: (b, 0, i)),
                  pl.BlockSpec((1, B_KV_HEADS, s, B_HEAD_DIM), lambda b, i: (b, 0, 0, 0)),
                  pl.BlockSpec((1, B_KV_HEADS, s // CB, B_HEAD_DIM, CB), lambda b, i: (b, 0, 0, 0, 0)),
                  ik, ik,
                  pl.BlockSpec((CB, CB), lambda b, i: (0, 0)),
                  pl.BlockSpec(memory_space=pltpu.SMEM)],
        out_specs=pl.BlockSpec((TQ, B_WIDTH), lambda b, i: (b * nq + i, 0)),
        out_shape=jax.ShapeDtypeStruct((bsz * s, B_WIDTH), BF16),
        scratch_shapes=[
            pltpu.VMEM((s, TQ), F32),
            pltpu.VMEM((s, TQ), F32),
            pltpu.VMEM((s, gw), F32),
            pltpu.VMEM((B_HEAD_DIM, gw), F32),
        ],
        compiler_params=_cparams(("parallel", "arbitrary")),
        name="dsa",
    )(q, proj, iq, iwt, k, vt, iklo, ikhi, tri, bound)


def _merge_kernel(ta_ref, tb_ref, tm_ref, ga_ref, gb_ref, gm_ref, x_ref,
                  wa_hbm, wb_hbm, wm_hbm, wo_hbm, o_ref,
                  wa_ref, wb_ref, wm_ref, wo_ref, stage_ref, sem):
    @pl.when(pl.program_id(0) == 0)
    def _():
        chunks = [(src, dst, r) for src, dst in ((wa_hbm, wa_ref), (wb_hbm, wb_ref), (wm_hbm, wm_ref),
                                                 (wo_hbm, wo_ref))
                  for r in range(0, src.shape[0], MERGE_STAGE_ROWS)]

        def copy(j):
            src, _, r = chunks[j]
            return pltpu.make_async_copy(src.at[pl.ds(r, MERGE_STAGE_ROWS), :], stage_ref.at[j % 2],
                                         sem.at[j % 2])

        copy(0).start()
        for j, (_, dst, r) in enumerate(chunks):
            if j + 1 < len(chunks):
                copy(j + 1).start()
            copy(j).wait()
            dst[pl.ds(r, MERGE_STAGE_ROWS), :] = stage_ref[j % 2].astype(BF16)

    def branch(t_ref, g_ref, w_ref):
        y = jnp.dot(t_ref[...], w_ref[...], preferred_element_type=F32)
        return jax.nn.sigmoid(g_ref[...].astype(F32)) * y

    merged = branch(ta_ref, ga_ref, wa_ref) + branch(tb_ref, gb_ref, wb_ref) + branch(tm_ref, gm_ref, wm_ref)
    o_ref[...] = x_ref[...] + jnp.dot(merged.astype(BF16), wo_ref[...], preferred_element_type=F32)


def _merge(ta, tb, tmem, proj, x2, wa, wb, wm, wo):
    m = x2.shape[0]
    tm = min(256, m)
    act = pl.BlockSpec((tm, BLK), lambda i: (i, 0))
    gate = lambda c: pl.BlockSpec((tm, D_MODEL), lambda i: (i, c // 2))
    wide = pl.BlockSpec((tm, D_MODEL), lambda i: (i, 0))
    hbm = pl.BlockSpec(memory_space=pl.ANY)
    return pl.pallas_call(
        _merge_kernel,
        grid=(m // tm,),
        in_specs=[act, act, act, gate(COL_GA), gate(COL_GB), gate(COL_GM), wide, hbm, hbm, hbm, hbm],
        out_specs=wide,
        out_shape=jax.ShapeDtypeStruct((m, D_MODEL), F32),
        scratch_shapes=[pltpu.VMEM(w.shape, BF16) for w in (wa, wb, wm, wo)]
        + [pltpu.VMEM((2, MERGE_STAGE_ROWS, D_MODEL), F32), pltpu.SemaphoreType.DMA((2,))],
        compiler_params=_cparams(("arbitrary",)),
        name="merge",
    )(ta, tb, tmem, proj, proj, proj, x2, wa, wb, wm, wo)


def _rope_freqs():
    inv = lambda half: ROPE_THETA ** (-np.arange(half, dtype=np.float32) / half)
    freq = np.zeros((1, LANES), np.float32)
    hb, hi = B_HEAD_DIM // 8, IDX_DIM // 8
    freq[0, :hb] = inv(hb)
    freq[0, hb:hb + hi] = inv(hi)
    return jnp.asarray(freq)


def _layer(x, mem, positions, norm_gain, w_in, gmlp_ln_gain, gmlp_ln_bias, spatial_w, spatial_b,
           w_branch_a, q_norm_gain, k_norm_gain, idx_k_ln_gain, idx_k_ln_bias, w_branch_b,
           mem_norm_gain, w_mem_kv, mem_q_norm_gain, mem_k_norm_gain, w_branch_m, w_out):
    bsz, s, _ = x.shape
    m = bsz * s
    row = lambda a: a.reshape(1, -1).astype(F32)
    pad_lanes = lambda a: jnp.pad(a.reshape(1, -1).astype(F32), ((0, 0), (0, LANES - a.shape[-1])))
    x2 = x.reshape(m, D_MODEL)

    w_t = w_in.T
    h, q, iq, k, vt, iklo, ikhi, iwt = _kvprep(
        x, row(norm_gain), w_t, positions.reshape(bsz, s, 1).astype(jnp.int32), _rope_freqs(),
        row(k_norm_gain), row(q_norm_gain), pad_lanes(idx_k_ln_gain), pad_lanes(idx_k_ln_bias))
    proj = _inproj(h, w_t)

    t_a = _gmlp(proj, row(gmlp_ln_gain), row(gmlp_ln_bias), spatial_w.astype(F32), spatial_b.T.astype(F32))
    km, vm = _memkv(mem, row(mem_norm_gain), w_mem_kv, row(mem_k_norm_gain))
    t_m = _memattn(proj, km, vm, row(mem_q_norm_gain), s)
    t_b = _dsa(proj, q, iq, iwt, k, vt, iklo, ikhi, row(q_norm_gain), row(k_norm_gain), bsz, s)

    out = _merge(t_a, t_b, t_m, proj, x2, w_branch_a, w_branch_b, w_branch_m, w_out)
    return out.reshape(bsz, s, D_MODEL)


def kernel(x, mem, positions, norm_gain, w_in, gmlp_ln_gain, gmlp_ln_bias, spatial_w, spatial_b, w_branch_a, q_norm_gain, k_norm_gain, idx_k_ln_gain, idx_k_ln_bias, w_branch_b, mem_norm_gain, w_mem_kv, mem_q_norm_gain, mem_k_norm_gain, w_branch_m, w_out):
    for l in range(norm_gain.shape[0]):
        x = _layer(x, mem, positions, norm_gain[l], w_in[l], gmlp_ln_gain[l], gmlp_ln_bias[l],
                   spatial_w[l], spatial_b[l], w_branch_a[l], q_norm_gain[l], k_norm_gain[l],
                   idx_k_ln_gain[l], idx_k_ln_bias[l], w_branch_b[l], mem_norm_gain[l], w_mem_kv[l],
                   mem_q_norm_gain[l], mem_k_norm_gain[l], w_branch_m[l], w_out[l])
    return x
```

```python
import functools

import numpy as np
import jax
import jax.numpy as jnp
from jax import lax
from jax.experimental import pallas as pl
from jax.experimental.pallas import tpu as pltpu

F32 = jnp.float32
BF16 = jnp.bfloat16

D_MODEL = 2048
ROPE_THETA = 500000.0
EPS = 1e-6
A_GROUPS = 8
A_GROUP_DIM = 128
A_WIDTH = A_GROUPS * A_GROUP_DIM
CHUNK = 128
B_HEADS = 8
B_KV_HEADS = 2
B_GROUP = B_HEADS // B_KV_HEADS
B_HEAD_DIM = 128
B_WIDTH = B_HEADS * B_HEAD_DIM
IDX_HEADS = 16
IDX_DIM = 64
TOPK_MAX = 256
M_HEADS = 4
M_HEAD_DIM = 256
M_WIDTH = M_HEADS * M_HEAD_DIM

SPLIT_SIZES = (
    A_WIDTH, A_WIDTH, A_WIDTH,
    B_WIDTH, B_KV_HEADS * B_HEAD_DIM, B_KV_HEADS * B_HEAD_DIM, B_WIDTH,
    IDX_HEADS * IDX_DIM, IDX_DIM, IDX_HEADS,
    M_WIDTH, M_WIDTH,
    D_MODEL, D_MODEL, D_MODEL,
)

LANES = 128
BLK = 1024
COL_AU, COL_AV, COL_AZ, COL_BZ, COL_MQ, COL_MZ = range(6)
COL_GA, COL_GB, COL_GM = 6, 8, 10
NB_MAIN = 12
_OFFS = [int(o) for o in np.concatenate([[0], np.cumsum(SPLIT_SIZES)])]
ROW_ALIGN = 16
MAIN_START = ([_OFFS[i] for i in (0, 1, 2, 6, 10, 11)]
              + [_OFFS[i] + d for i in (12, 13, 14) for d in (0, BLK)])
assert all(s % ROW_ALIGN == 0 for s in MAIN_START)
Q_START, IQ_START, HALF_BLK = _OFFS[3], _OFFS[7], BLK // 2
SMALL_KV_START, SMALL_KV_WIDTH = _OFFS[4], _OFFS[6] - _OFFS[4]
SMALL_IDX_START = _OFFS[8]
assert Q_START % BLK == 0 and IQ_START % HALF_BLK == 0
assert SMALL_KV_START % SMALL_KV_WIDTH == 0 and SMALL_IDX_START % LANES == 0
assert IDX_DIM + IDX_HEADS <= LANES and _OFFS[9] == SMALL_IDX_START + IDX_DIM

VMEM_LIMIT = 56 * 1024 * 1024
LOG2E = 1.4426950408889634
NEG_BIG = -1e30
F32_MAX = 3.4028234663852886e38

INPROJ_TM, INPROJ_TN = 2048, 1024
KVPREP_TP = 512
TQ = 256
CK = 256
CB = 512
COUNT_WAYS = 8
SEARCH_MARGIN = 0.02
SEARCH_FIXED_PASSES = 11
SEARCH_INTERP_PASSES = 13
MAX_SEARCH_ITERS = 400
MERGE_STAGE_ROWS = 512
MAX_SHIFT_BOUND = 48.0


def _cparams(sem):
    return pltpu.CompilerParams(dimension_semantics=sem, vmem_limit_bytes=VMEM_LIMIT)


def _dot_nt(a, b):
    return lax.dot_general(a, b, (((1,), (1,)), ((), ())), preferred_element_type=F32)


def _rope(x, cos_t, sin_t, half, period):
    lane = lax.broadcasted_iota(jnp.int32, x.shape, 1) % period
    rolled = jnp.where(lane < half, pltpu.roll(x, LANES - half, 1), pltpu.roll(x, half, 1))
    return x * cos_t + rolled * sin_t


def _rms_rows(x, gain):
    ms = jnp.mean(x * x, axis=-1, keepdims=True)
    return (x * lax.rsqrt(ms + EPS) * gain).astype(BF16)


def _inproj_kernel(tab_ref, h_ref, wt_ref, proj_ref):
    del tab_ref
    proj_ref[...] = _dot_nt(h_ref[...], wt_ref[...].astype(BF16)).astype(BF16)


def _inproj(h, w_t):
    m = h.shape[0]
    tm = min(INPROJ_TM, m)
    starts = [s + d for s in MAIN_START for d in range(0, BLK, INPROJ_TN)]
    tab = jnp.asarray(np.array([s // ROW_ALIGN for s in starts], np.int32))
    return pl.pallas_call(
        _inproj_kernel,
        grid_spec=pltpu.PrefetchScalarGridSpec(
            num_scalar_prefetch=1,
            grid=(m // tm, len(starts)),
            in_specs=[
                pl.BlockSpec((tm, D_MODEL), lambda i, n, tab: (i, 0)),
                pl.BlockSpec((pl.Element(INPROJ_TN), pl.Element(D_MODEL)),
                             lambda i, n, tab: (tab[n] * ROW_ALIGN, 0)),
            ],
            out_specs=pl.BlockSpec((tm, INPROJ_TN), lambda i, n, tab: (i, n)),
        ),
        out_shape=jax.ShapeDtypeStruct((m, NB_MAIN * BLK), BF16),
        compiler_params=_cparams(("parallel", "arbitrary")),
        name="inproj",
    )(tab, h, w_t)


def _kvprep_kernel(x_ref, g_ref, wkv_ref, widx_ref, wq_ref, wiqa_ref, wiqb_ref, pos_ref, fc_ref,
                   gk_ref, gq_ref, lng_ref, lnb_ref,
                   h_ref, q_ref, iq_ref, k_ref, vt_ref, iklo_ref, ikhi_ref, iwt_ref,
                   wkv_bf_ref, widx_bf_ref, wq_bf_ref, wiq_bf_ref):
    @pl.when(jnp.logical_and(pl.program_id(0) == 0, pl.program_id(1) == 0))
    def _():
        wkv_bf_ref[...] = wkv_ref[...].astype(BF16)
        widx_bf_ref[...] = widx_ref[...].astype(BF16)
        wq_bf_ref[...] = wq_ref[...].astype(BF16)
        wiq_bf_ref[:HALF_BLK, :] = wiqa_ref[...].astype(BF16)
        wiq_bf_ref[HALF_BLK:, :] = wiqb_ref[...].astype(BF16)

    h = _rms_rows(x_ref[...], g_ref[...])
    h_ref[...] = h
    kv = _dot_nt(h, wkv_bf_ref[...])
    ikp = _dot_nt(h, widx_bf_ref[...])

    hb, hi = B_HEAD_DIM // 8, IDX_DIM // 8
    ang = pos_ref[0].astype(F32) * fc_ref[...]
    cos_c, sin_c = jnp.cos(ang), jnp.sin(ang)
    lane = lax.broadcasted_iota(jnp.int32, ang.shape, 1)
    cos_b = jnp.where(lane < hb, cos_c, jnp.where(lane < 2 * hb, pltpu.roll(cos_c, hb, 1), 1.0))
    sin_b = jnp.where(lane < hb, -sin_c, jnp.where(lane < 2 * hb, pltpu.roll(sin_c, hb, 1), 0.0))
    cos_i = jnp.ones_like(cos_c)
    sin_i = jnp.zeros_like(sin_c)
    for head in range(LANES // IDX_DIM):
        for part, sign in enumerate((-1.0, 1.0)):
            first = head * IDX_DIM + part * hi
            here = jnp.logical_and(lane >= first, lane < first + hi)
            shift = (first - hb) % LANES
            cos_i = jnp.where(here, pltpu.roll(cos_c, shift, 1), cos_i)
            sin_i = jnp.where(here, sign * pltpu.roll(sin_c, shift, 1), sin_i)
    nblk = q_ref.shape[0]
    gain = gq_ref[...] * (B_HEAD_DIM ** -0.5 * LOG2E)
    qp = _dot_nt(h, wq_bf_ref[...])
    for hd in range(B_HEADS):
        slab = qp[:, hd * B_HEAD_DIM:(hd + 1) * B_HEAD_DIM]
        r = lax.rsqrt(jnp.mean(slab * slab, axis=-1, keepdims=True) + EPS)
        qr = _rope(slab * r * gain, cos_b, sin_b, B_HEAD_DIM // 8, LANES).astype(BF16)
        for blk in range(nblk):
            q_ref[blk, hd] = qr[blk * TQ:(blk + 1) * TQ, :]
    iqp = _dot_nt(h, wiq_bf_ref[...])
    for j in range(IDX_HEADS // 2):
        ir = _rope(iqp[:, j * LANES:(j + 1) * LANES], cos_i, sin_i, IDX_DIM // 8, IDX_DIM).astype(BF16)
        for blk in range(nblk):
            iq_ref[blk, j // 2, (j % 2) * TQ:(j % 2 + 1) * TQ, :] = ir[blk * TQ:(blk + 1) * TQ, :]

    for n in range(B_KV_HEADS):
        kh = kv[:, n * B_HEAD_DIM:(n + 1) * B_HEAD_DIM]
        r = lax.rsqrt(jnp.mean(kh * kh, axis=-1, keepdims=True) + EPS)
        kn = kh * r * gk_ref[...]
        k_ref[0, n] = _rope(kn, cos_b, sin_b, B_HEAD_DIM // 8, LANES).astype(BF16)
        vt = kv[:, (B_KV_HEADS + n) * B_HEAD_DIM:(B_KV_HEADS + n + 1) * B_HEAD_DIM].T
        for c in range(vt.shape[1] // CB):
            vt_ref[0, n, c] = vt[:, c * CB:(c + 1) * CB].astype(BF16)

    lane = lax.broadcasted_iota(jnp.int32, ikp.shape, 1)
    live = lane < IDX_DIM
    mu = jnp.sum(jnp.where(live, ikp, 0.0), axis=-1, keepdims=True) * (1.0 / IDX_DIM)
    d = jnp.where(live, ikp - mu, 0.0)
    var = jnp.sum(d * d, axis=-1, keepdims=True) * (1.0 / IDX_DIM)
    y = d * lax.rsqrt(var + EPS) * lng_ref[...] + lnb_ref[...]
    yr = jnp.where(live, _rope(y, cos_i, sin_i, IDX_DIM // 8, IDX_DIM), 0.0)
    iklo_ref[0] = yr.astype(BF16)
    ikhi_ref[0] = pltpu.roll(yr, IDX_DIM, 1).astype(BF16)

    iw = ikp * (IDX_DIM ** -0.5 * IDX_HEADS ** -0.5)
    iwt_ref[0] = iw.T[IDX_DIM:IDX_DIM + IDX_HEADS, :]


def _kvprep(x3, gain, w_t, pos3, freqs, gk, gq, lng, lnb):
    bsz, s, _ = x3.shape
    tp = min(KVPREP_TP, s)
    per_b, nblk = s // tp, tp // TQ
    row = lambda b, i: (b, i, 0)
    const = lambda b, i: (0, 0)
    tab_spec = pl.BlockSpec((1, LANES), const)
    wrows = lambda rows, start: pl.BlockSpec((rows, D_MODEL), lambda b, i: (start // rows, 0),
                                             pipeline_mode=pl.Buffered(1))
    out_h = pl.BlockSpec((tp, D_MODEL), lambda b, i: (b * per_b + i, 0))
    out_q = pl.BlockSpec((nblk, B_HEADS, TQ, B_HEAD_DIM), lambda b, i: (b * per_b + i, 0, 0, 0))
    out_iq = pl.BlockSpec((nblk, IDX_HEADS // 4, 2 * TQ, LANES), lambda b, i: (b * per_b + i, 0, 0, 0))
    out_tok = pl.BlockSpec((1, tp, LANES), row)
    out_k = pl.BlockSpec((1, B_KV_HEADS, tp, B_HEAD_DIM), lambda b, i: (b, 0, i, 0))
    out_vt = pl.BlockSpec((1, B_KV_HEADS, tp // CB, B_HEAD_DIM, CB), lambda b, i: (b, 0, i, 0, 0))
    out_iwt = pl.BlockSpec((1, IDX_HEADS, tp), lambda b, i: (b, 0, i))
    nq = bsz * s // TQ
    return pl.pallas_call(
        _kvprep_kernel,
        grid=(bsz, per_b),
        in_specs=[pl.BlockSpec((None, tp, D_MODEL), row), pl.BlockSpec((1, D_MODEL), const),
                  wrows(SMALL_KV_WIDTH, SMALL_KV_START), wrows(LANES, SMALL_IDX_START),
                  wrows(BLK, Q_START), wrows(HALF_BLK, IQ_START), wrows(HALF_BLK, IQ_START + HALF_BLK),
                  pl.BlockSpec((1, tp, 1), row),
                  tab_spec, tab_spec, tab_spec, tab_spec, tab_spec],
        out_specs=[out_h, out_q, out_iq, out_k, out_vt, out_tok, out_tok, out_iwt],
        out_shape=[
            jax.ShapeDtypeStruct((bsz * s, D_MODEL), BF16),
            jax.ShapeDtypeStruct((nq, B_HEADS, TQ, B_HEAD_DIM), BF16),
            jax.ShapeDtypeStruct((nq, IDX_HEADS // 4, 2 * TQ, LANES), BF16),
            jax.ShapeDtypeStruct((bsz, B_KV_HEADS, s, B_HEAD_DIM), BF16),
            jax.ShapeDtypeStruct((bsz, B_KV_HEADS, s // CB, B_HEAD_DIM, CB), BF16),
            jax.ShapeDtypeStruct((bsz, s, LANES), BF16),
            jax.ShapeDtypeStruct((bsz, s, LANES), BF16),
            jax.ShapeDtypeStruct((bsz, IDX_HEADS, s), F32),
        ],
        scratch_shapes=[pltpu.VMEM((SMALL_KV_WIDTH, D_MODEL), BF16), pltpu.VMEM((LANES, D_MODEL), BF16),
                        pltpu.VMEM((BLK, D_MODEL), BF16), pltpu.VMEM((BLK, D_MODEL), BF16)],
        compiler_params=_cparams(("arbitrary", "arbitrary")),
        name="kvprep",
    )(x3, gain, w_t, w_t, w_t, w_t, w_t, pos3, freqs, gk, gq, lng, lnb)


def _gelu_tanh(x):
    c = -2.0 * (2.0 / np.pi) ** 0.5 * LOG2E
    return x / (1.0 + jnp.exp2(x * (x * x * (0.044715 * c) + c)))


def _gmlp_kernel(u_ref, v_ref, z_ref, lng_ref, lnb_ref, ws_ref, sbt_ref, o_ref):
    tm = u_ref.shape[0]
    u = _gelu_tanh(u_ref[...].astype(F32))
    v = _gelu_tanh(v_ref[...].astype(F32))
    mu = jnp.mean(v, axis=-1, keepdims=True)
    d = v - mu
    var = jnp.mean(d * d, axis=-1, keepdims=True)
    vn = (d * lax.rsqrt(var + EPS) * lng_ref[...] + lnb_ref[...]).astype(BF16)
    gate = u * jax.nn.silu(z_ref[...].astype(F32))
    tri = (lax.broadcasted_iota(jnp.int32, (CHUNK, CHUNK), 1)
           <= lax.broadcasted_iota(jnp.int32, (CHUNK, CHUNK), 0))
    for g in range(A_GROUPS):
        wg = jnp.where(tri, ws_ref[g], 0.0).astype(BF16)
        bias = sbt_ref[:, g:g + 1]
        cols = slice(g * A_GROUP_DIM, (g + 1) * A_GROUP_DIM)
        for c in range(tm // CHUNK):
            rows = slice(c * CHUNK, (c + 1) * CHUNK)
            sg = jnp.dot(wg, vn[rows, cols], preferred_element_type=F32) + bias
            o_ref[rows, cols] = (gate[rows, cols] * sg).astype(BF16)


def _gmlp(proj, lng, lnb, ws, sbt):
    m = proj.shape[0]
    tm = min(512, m)
    col = lambda c: pl.BlockSpec((tm, BLK), lambda i: (i, c))
    full = lambda shape: pl.BlockSpec(shape, lambda i: (0,) * len(shape))
    return pl.pallas_call(
        _gmlp_kernel,
        grid=(m // tm,),
        in_specs=[col(COL_AU), col(COL_AV), col(COL_AZ), full((1, A_WIDTH)), full((1, A_WIDTH)),
                  full((A_GROUPS, CHUNK, CHUNK)), full((CHUNK, A_GROUPS))],
        out_specs=pl.BlockSpec((tm, A_WIDTH), lambda i: (i, 0)),
        out_shape=jax.ShapeDtypeStruct((m, A_WIDTH), BF16),
        compiler_params=_cparams(("parallel",)),
        name="gmlp",
    )(proj, proj, proj, lng, lnb, ws, sbt)


def _memkv_kernel(mem_ref, g_ref, w_ref, gk_ref, km_ref, vm_ref, wbf_ref):
    @pl.when(pl.program_id(0) == 0)
    def _():
        wbf_ref[...] = w_ref[...].astype(BF16)

    h = _rms_rows(mem_ref[0], g_ref[...])
    kv = jnp.dot(h, wbf_ref[...], preferred_element_type=F32)
    for hd in range(M_HEADS):
        kh = kv[:, hd * M_HEAD_DIM:(hd + 1) * M_HEAD_DIM]
        r = lax.rsqrt(jnp.mean(kh * kh, axis=-1, keepdims=True) + EPS)
        km_ref[0, hd] = (kh * r * gk_ref[...]).astype(BF16)
        vm_ref[0, hd] = kv[:, M_WIDTH + hd * M_HEAD_DIM:M_WIDTH + (hd + 1) * M_HEAD_DIM].astype(BF16)


def _memkv(mem, gain, w_kv, gk):
    bsz, ml, _ = mem.shape
    out = pl.BlockSpec((1, M_HEADS, ml, M_HEAD_DIM), lambda b: (b, 0, 0, 0))
    shp = jax.ShapeDtypeStruct((bsz, M_HEADS, ml, M_HEAD_DIM), BF16)
    return pl.pallas_call(
        _memkv_kernel,
        grid=(bsz,),
        in_specs=[pl.BlockSpec((1, ml, D_MODEL), lambda b: (b, 0, 0)),
                  pl.BlockSpec((1, D_MODEL), lambda b: (0, 0)),
                  pl.BlockSpec((D_MODEL, 2 * M_WIDTH), lambda b: (0, 0), pipeline_mode=pl.Buffered(1)),
                  pl.BlockSpec((1, M_HEAD_DIM), lambda b: (0, 0))],
        out_specs=[out, out],
        out_shape=[shp, shp],
        scratch_shapes=[pltpu.VMEM((D_MODEL, 2 * M_WIDTH), BF16)],
        compiler_params=_cparams(("arbitrary",)),
        name="memkv",
    )(mem, gain, w_kv, gk)


def _memattn_kernel(q_ref, z_ref, km_ref, vm_ref, gq_ref, o_ref):
    qscale = M_HEAD_DIM ** -0.5 * LOG2E
    for hd in range(M_HEADS):
        cols = slice(hd * M_HEAD_DIM, (hd + 1) * M_HEAD_DIM)
        q = q_ref[:, cols].astype(F32)
        r = lax.rsqrt(jnp.mean(q * q, axis=-1, keepdims=True) + EPS)
        qn = (q * r * gq_ref[...] * qscale).astype(BF16)
        lg = _dot_nt(qn, km_ref[0, hd])
        p = jnp.exp2(lg - jnp.max(lg, axis=-1, keepdims=True))
        l = jnp.sum(p, axis=-1, keepdims=True)
        o = jnp.dot(p.astype(BF16), vm_ref[0, hd], preferred_element_type=F32) / l
        o_ref[:, cols] = (o * jax.nn.silu(z_ref[:, cols].astype(F32))).astype(BF16)


def _memattn(proj, km, vm, gq, s):
    m = proj.shape[0]
    tm = min(512, s)
    per_b = s // tm
    ml = km.shape[2]
    kv_spec = pl.BlockSpec((1, M_HEADS, ml, M_HEAD_DIM), lambda i: (i // per_b, 0, 0, 0))
    return pl.pallas_call(
        _memattn_kernel,
        grid=(m // tm,),
        in_specs=[pl.BlockSpec((tm, BLK), lambda i: (i, COL_MQ)),
                  pl.BlockSpec((tm, BLK), lambda i: (i, COL_MZ)),
                  kv_spec, kv_spec,
                  pl.BlockSpec((1, M_HEAD_DIM), lambda i: (0, 0))],
        out_specs=pl.BlockSpec((tm, M_WIDTH), lambda i: (i, 0)),
        out_shape=jax.ShapeDtypeStruct((m, M_WIDTH), BF16),
        compiler_params=_cparams(("parallel",)),
        name="memattn",
    )(proj, proj, km, vm, gq)


def _dsa_kernel(topk, q_ref, z_ref, iq_ref, iwt_ref,
                k_ref, vt_ref, iklo_ref, ikhi_ref, tri_ref, bound_ref, o_ref,
                sc_ref, bias_ref, lg_ref, acc_ref):
    qb = pl.program_id(1)
    nck = (qb * TQ + TQ + CK - 1) // CK
    nbig = (qb * TQ + TQ + CB - 1) // CB
    t_lane = qb * TQ + lax.broadcasted_iota(jnp.int32, (1, TQ), 1)
    gw = B_GROUP * TQ

    def fold8(a):
        return a.reshape(a.shape[0] // 8, 8, a.shape[1])

    def grouped_loop(trips, body, init, per_step=2, start=0):
        carry, done = init, start
        while per_step >= 1:
            def step(i, c, first=done, n=per_step):
                for u in range(n):
                    c = body(first + n * i + u, c)
                return c

            steps = (trips - done) // per_step
            carry = lax.fori_loop(0, steps, step, carry)
            done = done + steps * per_step
            per_step //= 2
        return carry

    wt = iwt_ref[0]

    def idx_body(c, carry):
        mn8, mx8, s1, s2 = carry
        off = pl.multiple_of(c * CK, CK)
        acc = jnp.zeros((CK, TQ), F32)
        for jj in range(IDX_HEADS // 4):
            rhs = iq_ref[0, jj]
            for half, keys_ref in enumerate((iklo_ref, ikhi_ref)):
                d = _dot_nt(keys_ref[0, pl.ds(off, CK), :], rhs)
                ha, hb = 4 * jj + half, 4 * jj + 2 + half
                acc = (acc + jnp.maximum(d[:, :TQ], 0.0) * wt[ha:ha + 1, :]
                       + jnp.maximum(d[:, TQ:], 0.0) * wt[hb:hb + 1, :])
        key = off + lax.broadcasted_iota(jnp.int32, (CK, TQ), 0)
        causal = key <= t_lane
        sc = jnp.where(causal, acc, -jnp.inf)
        sc_ref[pl.ds(off, CK), :] = sc
        live = jnp.where(causal, acc, 0.0)
        mn8 = jnp.minimum(mn8, jnp.min(fold8(jnp.where(causal, acc, jnp.inf)), axis=0))
        mx8 = jnp.maximum(mx8, jnp.max(fold8(sc), axis=0))
        s1 = s1 + jnp.sum(fold8(live), axis=0)
        s2 = s2 + jnp.sum(fold8(live * live), axis=0)
        return mn8, mx8, s1, s2

    zero8 = jnp.zeros((8, TQ), F32)
    stats = (jnp.full((8, TQ), jnp.inf, F32), jnp.full((8, TQ), -jnp.inf, F32), zero8, zero8)
    mn8, mx8, s1, s2 = grouped_loop(nck, idx_body, stats, per_step=4)
    row_min = jnp.min(mn8, axis=0, keepdims=True)
    row_max = jnp.max(mx8, axis=0, keepdims=True)

    def fill_body(c, carry):
        sc_ref[pl.ds(pl.multiple_of(c * CK, CK), CK), :] = jnp.full((CK, TQ), -jnp.inf, F32)
        return carry

    lax.fori_loop(nck, nbig * (CB // CK), fill_body, 0)

    def count_rows(pred):
        rows = 8 * COUNT_WAYS

        def body(c, cnt):
            off = pl.multiple_of(c * CB, CB)
            for r in range(CB // rows):
                cnt = cnt + jnp.where(pred(sc_ref[pl.ds(off + r * rows, rows), :]), 1.0, 0.0)
            return cnt

        cnt = lax.fori_loop(0, nbig, body, jnp.zeros((rows, TQ), F32))
        return jnp.sum(cnt, axis=0, keepdims=True)

    kf = float(topk)
    n_valid = (t_lane + 1).astype(F32)
    all_rows = (t_lane + 1) <= topk

    def search_pass(st, probe, stuck):
        lo, hi, clo, chi, thr, fin, tie = st
        cnt = count_rows(lambda blk: blk >= probe)
        active = fin < 0.5
        hit = cnt == kf
        end_thr = jnp.where(jnp.logical_and(stuck, cnt < kf), lo, probe)
        ends = jnp.logical_and(active, jnp.logical_or(hit, stuck))
        thr = jnp.where(ends, end_thr, thr)
        tie = jnp.where(jnp.logical_and(ends, jnp.logical_not(hit)), 1.0, tie)
        fin = jnp.where(ends, 1.0, fin)
        up = jnp.logical_and(active, cnt > kf)
        dn = jnp.logical_and(active, cnt < kf)
        return (jnp.where(up, probe, lo), jnp.where(dn, probe, hi), jnp.where(up, cnt, clo),
                jnp.where(dn, cnt, chi), thr, fin, tie)

    def next_probe(st, halve=False):
        lo, hi, clo, chi = st[:4]
        frac = jnp.clip((clo - kf + 0.5) / (clo - chi + 1.0), SEARCH_MARGIN, 1.0 - SEARCH_MARGIN)
        probe = lo + (hi - lo) * frac
        outside = jnp.logical_or(jnp.logical_or(probe <= lo, probe >= hi), halve)
        probe = jnp.where(outside, lo * 0.5 + hi * 0.5, probe)
        stuck = jnp.logical_or(probe <= lo, probe >= hi)
        return jnp.where(stuck, hi, probe), stuck

    mean = jnp.sum(s1, axis=0, keepdims=True) / n_valid
    var = jnp.maximum(jnp.sum(s2, axis=0, keepdims=True) / n_valid - mean * mean, 0.0)
    tail = jnp.clip(kf / n_valid, 1e-6, 1.0 - 1e-6)
    tq = jnp.sqrt(-2.0 * jnp.log(jnp.minimum(tail, 1.0 - tail)))
    zq = tq - ((0.010328 * tq + 0.802853) * tq + 2.515517) / (
        ((0.001308 * tq + 0.189269) * tq + 1.432788) * tq + 1.0)
    guess = mean + jnp.where(tail < 0.5, zq, -zq) * jnp.sqrt(var)
    inside = jnp.logical_and(guess >= row_min, guess <= row_max)
    probe0 = jnp.where(inside, guess, row_min * 0.5 + row_max * 0.5)

    ones = jnp.ones((1, TQ), F32)
    st = (row_min, row_max, n_valid, 0.0 * ones, jnp.where(all_rows, -F32_MAX, row_max),
          jnp.where(all_rows, 1.0, 0.0), 0.0 * ones)
    st = search_pass(st, probe0, probe0 < row_min)

    def fixed_body(i, st):
        return search_pass(st, *next_probe(st))

    st = lax.fori_loop(0, SEARCH_FIXED_PASSES, fixed_body, st)

    def more_cond(c):
        return jnp.logical_and(c[0] < MAX_SEARCH_ITERS, jnp.min(c[1][5]) < 0.5)

    def more_body(c):
        return c[0] + 1, search_pass(c[1], *next_probe(c[1], c[0] >= SEARCH_INTERP_PASSES))

    _, st = lax.while_loop(more_cond, more_body, (jnp.int32(0), st))
    thr, tie = st[4], st[6]

    def plain_mask():
        def body(c, carry):
            off = pl.multiple_of(c * CB, CB)
            bias_ref[pl.ds(off, CB), :] = jnp.where(sc_ref[pl.ds(off, CB), :] >= thr, 0.0, NEG_BIG)
            return carry

        lax.fori_loop(0, nbig, body, 0)

    def tied_mask():
        n_gt = count_rows(lambda blk: blk > thr)
        need = jnp.where(all_rows, F32_MAX, kf - n_gt)

        def body(c, seen):
            off = pl.multiple_of(c * CB, CB)
            blk = sc_ref[pl.ds(off, CB), :]
            eq = blk == thr
            pref = jnp.dot(tri_ref[...], jnp.where(eq, 1.0, 0.0).astype(BF16),
                           preferred_element_type=F32) + seen
            keep = jnp.logical_or(blk > thr, jnp.logical_and(eq, pref <= need))
            bias_ref[pl.ds(off, CB), :] = jnp.where(keep, 0.0, NEG_BIG)
            return pref[CB - 1:CB, :]

        lax.fori_loop(0, nbig, body, jnp.zeros((1, TQ), F32))

    lax.cond(jnp.max(tie) > 0.5, tied_mask, plain_mask)

    logit_bound = bound_ref[0, 0]

    def masked_logits(n, off):
        b = bias_ref[pl.ds(off, CB), :]
        qn = q_ref[0, n * B_GROUP:(n + 1) * B_GROUP].reshape(gw, B_HEAD_DIM)
        lg = _dot_nt(k_ref[0, n, pl.ds(off, CB), :], qn)
        return jnp.concatenate([lg[:, g * TQ:(g + 1) * TQ] + b for g in range(B_GROUP)], axis=1)

    for n in range(B_KV_HEADS):
        acc_ref[...] = jnp.zeros(acc_ref.shape, F32)

        def one_pass(n=n):
            def body(c, l8):
                off = pl.multiple_of(c * CB, CB)
                p = jnp.exp2(masked_logits(n, off) - logit_bound)
                acc_ref[...] += jnp.dot(vt_ref[0, n, c], p.astype(BF16), preferred_element_type=F32)
                return l8 + jnp.sum(fold8(p), axis=0)

            return grouped_loop(nbig, body, jnp.zeros((8, gw), F32), per_step=4)

        def two_pass(n=n):
            def logit_body(c, mx8):
                off = pl.multiple_of(c * CB, CB)
                lg = masked_logits(n, off)
                lg_ref[pl.ds(off, CB), :] = lg
                return jnp.maximum(mx8, jnp.max(fold8(lg), axis=0))

            mx8 = grouped_loop(nbig, logit_body, jnp.full((8, gw), NEG_BIG, F32))
            m = jnp.max(mx8, axis=0, keepdims=True)

            def pv_body(c, l8):
                off = pl.multiple_of(c * CB, CB)
                p = jnp.exp2(lg_ref[pl.ds(off, CB), :] - m)
                acc_ref[...] += jnp.dot(vt_ref[0, n, c], p.astype(BF16), preferred_element_type=F32)
                return l8 + jnp.sum(fold8(p), axis=0)

            return grouped_loop(nbig, pv_body, jnp.zeros((8, gw), F32))

        l8 = lax.cond(logit_bound <= MAX_SHIFT_BOUND, one_pass, two_pass)
        o_t = acc_ref[...] / jnp.sum(l8, axis=0, keepdims=True)
        for g in range(B_GROUP):
            cols = slice((n * B_GROUP + g) * B_HEAD_DIM, (n * B_GROUP + g + 1) * B_HEAD_DIM)
            o = o_t[:, g * TQ:(g + 1) * TQ].T
            o_ref[:, cols] = (o * jax.nn.silu(z_ref[:, cols].astype(F32))).astype(BF16)


def _dsa(proj, q, iq, iwt, k, vt, iklo, ikhi, gq, gk, bsz, s):
    nq = s // TQ
    bound = (1.02 * B_HEAD_DIM ** 0.5 * LOG2E * jnp.max(jnp.abs(gq)) * jnp.max(jnp.abs(gk))).reshape(1, 1)
    topk = min(TOPK_MAX, s // 4)
    gw = B_GROUP * TQ
    blk4 = lambda a: pl.BlockSpec((1,) + a.shape[1:], lambda b, i: (b * nq + i, 0, 0, 0))
    ik = pl.BlockSpec((1, s, LANES), lambda b, i: (b, 0, 0))
    tri = jnp.tril(jnp.ones((CB, CB), BF16))
    return pl.pallas_call(
        functools.partial(_dsa_kernel, topk),
        grid=(bsz, nq),
        in_specs=[blk4(q), pl.BlockSpec((TQ, BLK), lambda b, i: (b * nq + i, COL_BZ)), blk4(iq),
                  pl.BlockSpec((1, IDX_HEADS, TQ), lambda b, i: (b, 0, i)),
                  pl.BlockSpec((1, B_KV_HEADS, s, B_HEAD_DIM), lambda b, i: (b, 0, 0, 0)),
                  pl.BlockSpec((1, B_KV_HEADS, s // CB, B_HEAD_DIM, CB), lambda b, i: (b, 0, 0, 0, 0)),
                  ik, ik,
                  pl.BlockSpec((CB, CB), lambda b, i: (0, 0)),
                  pl.BlockSpec(memory_space=pltpu.SMEM)],
        out_specs=pl.BlockSpec((TQ, B_WIDTH), lambda b, i: (b * nq + i, 0)),
        out_shape=jax.ShapeDtypeStruct((bsz * s, B_WIDTH), BF16),
        scratch_shapes=[
            pltpu.VMEM((s, TQ), F32),
            pltpu.VMEM((s, TQ), F32),
            pltpu.VMEM((s, gw), F32),
            pltpu.VMEM((B_HEAD_DIM, gw), F32),
        ],
        compiler_params=_cparams(("parallel", "arbitrary")),
        name="dsa",
    )(q, proj, iq, iwt, k, vt, iklo, ikhi, tri, bound)


def _merge_kernel(ta_ref, tb_ref, tm_ref, ga_ref, gb_ref, gm_ref, x_ref,
                  wa_hbm, wb_hbm, wm_hbm, wo_hbm, o_ref,
                  wa_ref, wb_ref, wm_ref, wo_ref, stage_ref, sem):
    @pl.when(pl.program_id(0) == 0)
    def _():
        chunks = [(src, dst, r) for src, dst in ((wa_hbm, wa_ref), (wb_hbm, wb_ref), (wm_hbm, wm_ref),
                                                 (wo_hbm, wo_ref))
                  for r in range(0, src.shape[0], MERGE_STAGE_ROWS)]

        def copy(j):
            src, _, r = chunks[j]
            return pltpu.make_async_copy(src.at[pl.ds(r, MERGE_STAGE_ROWS), :], stage_ref.at[j % 2],
                                         sem.at[j % 2])

        copy(0).start()
        for j, (_, dst, r) in enumerate(chunks):
            if j + 1 < len(chunks):
                copy(j + 1).start()
            copy(j).wait()
            dst[pl.ds(r, MERGE_STAGE_ROWS), :] = stage_ref[j % 2].astype(BF16)

    def branch(t_ref, g_ref, w_ref):
        y = jnp.dot(t_ref[...], w_ref[...], preferred_element_type=F32)
        return jax.nn.sigmoid(g_ref[...].astype(F32)) * y

    merged = branch(ta_ref, ga_ref, wa_ref) + branch(tb_ref, gb_ref, wb_ref) + branch(tm_ref, gm_ref, wm_ref)
    o_ref[...] = x_ref[...] + jnp.dot(merged.astype(BF16), wo_ref[...], preferred_element_type=F32)


def _merge(ta, tb, tmem, proj, x2, wa, wb, wm, wo):
    m = x2.shape[0]
    tm = min(256, m)
    act = pl.BlockSpec((tm, BLK), lambda i: (i, 0))
    gate = lambda c: pl.BlockSpec((tm, D_MODEL), lambda i: (i, c // 2))
    wide = pl.BlockSpec((tm, D_MODEL), lambda i: (i, 0))
    hbm = pl.BlockSpec(memory_space=pl.ANY)
    return pl.pallas_call(
        _merge_kernel,
        grid=(m // tm,),
        in_specs=[act, act, act, gate(COL_GA), gate(COL_GB), gate(COL_GM), wide, hbm, hbm, hbm, hbm],
        out_specs=wide,
        out_shape=jax.ShapeDtypeStruct((m, D_MODEL), F32),
        scratch_shapes=[pltpu.VMEM(w.shape, BF16) for w in (wa, wb, wm, wo)]
        + [pltpu.VMEM((2, MERGE_STAGE_ROWS, D_MODEL), F32), pltpu.SemaphoreType.DMA((2,))],
        compiler_params=_cparams(("arbitrary",)),
        name="merge",
    )(ta, tb, tmem, proj, proj, proj, x2, wa, wb, wm, wo)


def _rope_freqs():
    inv = lambda half: ROPE_THETA ** (-np.arange(half, dtype=np.float32) / half)
    freq = np.zeros((1, LANES), np.float32)
    hb, hi = B_HEAD_DIM // 8, IDX_DIM // 8
    freq[0, :hb] = inv(hb)
    freq[0, hb:hb + hi] = inv(hi)
    return jnp.asarray(freq)


def _layer(x, mem, positions, norm_gain, w_in, gmlp_ln_gain, gmlp_ln_bias, spatial_w, spatial_b,
           w_branch_a, q_norm_gain, k_norm_gain, idx_k_ln_gain, idx_k_ln_bias, w_branch_b,
           mem_norm_gain, w_mem_kv, mem_q_norm_gain, mem_k_norm_gain, w_branch_m, w_out):
    bsz, s, _ = x.shape
    m = bsz * s
    row = lambda a: a.reshape(1, -1).astype(F32)
    pad_lanes = lambda a: jnp.pad(a.reshape(1, -1).astype(F32), ((0, 0), (0, LANES - a.shape[-1])))
    x2 = x.reshape(m, D_MODEL)

    w_t = w_in.T
    h, q, iq, k, vt, iklo, ikhi, iwt = _kvprep(
        x, row(norm_gain), w_t, positions.reshape(bsz, s, 1).astype(jnp.int32), _rope_freqs(),
        row(k_norm_gain), row(q_norm_gain), pad_lanes(idx_k_ln_gain), pad_lanes(idx_k_ln_bias))
    proj = _inproj(h, w_t)

    t_a = _gmlp(proj, row(gmlp_ln_gain), row(gmlp_ln_bias), spatial_w.astype(F32), spatial_b.T.astype(F32))
    km, vm = _memkv(mem, row(mem_norm_gain), w_mem_kv, row(mem_k_norm_gain))
    t_m = _memattn(proj, km, vm, row(mem_q_norm_gain), s)
    t_b = _dsa(proj, q, iq, iwt, k, vt, iklo, ikhi, row(q_norm_gain), row(k_norm_gain), bsz, s)

    out = _merge(t_a, t_b, t_m, proj, x2, w_branch_a, w_branch_b, w_branch_m, w_out)
    return out.reshape(bsz, s, D_MODEL)


def kernel(x, mem, positions, norm_gain, w_in, gmlp_ln_gain, gmlp_ln_bias, spatial_w, spatial_b, w_branch_a, q_norm_gain, k_norm_gain, idx_k_ln_gain, idx_k_ln_bias, w_branch_b, mem_norm_gain, w_mem_kv, mem_q_norm_gain, mem_k_norm_gain, w_branch_m, w_out):
    for l in range(norm_gain.shape[0]):
        x = _layer(x, mem, positions, norm_gain[l], w_in[l], gmlp_ln_gain[l], gmlp_ln_bias[l],
                   spatial_w[l], spatial_b[l], w_branch_a[l], q_norm_gain[l], k_norm_gain[l],
                   idx_k_ln_gain[l], idx_k_ln_bias[l], w_branch_b[l], mem_norm_gain[l], w_mem_kv[l],
                   mem_q_norm_gain[l], mem_k_norm_gain[l], w_branch_m[l], w_out[l])
    return x
```

```python
import functools

import numpy as np
import jax
import jax.numpy as jnp
from jax import lax
from jax.experimental import pallas as pl
from jax.experimental.pallas import tpu as pltpu

F32 = jnp.float32
BF16 = jnp.bfloat16

D_MODEL = 2048
ROPE_THETA = 500000.0
EPS = 1e-6
A_GROUPS = 8
A_GROUP_DIM = 128
A_WIDTH = A_GROUPS * A_GROUP_DIM
CHUNK = 128
B_HEADS = 8
B_KV_HEADS = 2
B_GROUP = B_HEADS // B_KV_HEADS
B_HEAD_DIM = 128
B_WIDTH = B_HEADS * B_HEAD_DIM
IDX_HEADS = 16
IDX_DIM = 64
TOPK_MAX = 256
M_HEADS = 4
M_HEAD_DIM = 256
M_WIDTH = M_HEADS * M_HEAD_DIM

SPLIT_SIZES = (
    A_WIDTH, A_WIDTH, A_WIDTH,
    B_WIDTH, B_KV_HEADS * B_HEAD_DIM, B_KV_HEADS * B_HEAD_DIM, B_WIDTH,
    IDX_HEADS * IDX_DIM, IDX_DIM, IDX_HEADS,
    M_WIDTH, M_WIDTH,
    D_MODEL, D_MODEL, D_MODEL,
)

LANES = 128
BLK = 1024
COL_AU, COL_AV, COL_AZ, COL_BZ, COL_MQ, COL_MZ = range(6)
COL_GA, COL_GB, COL_GM = 6, 8, 10
NB_MAIN = 12
_OFFS = [int(o) for o in np.concatenate([[0], np.cumsum(SPLIT_SIZES)])]
ROW_ALIGN = 16
MAIN_START = ([_OFFS[i] for i in (0, 1, 2, 6, 10, 11)]
              + [_OFFS[i] + d for i in (12, 13, 14) for d in (0, BLK)])
assert all(s % ROW_ALIGN == 0 for s in MAIN_START)
Q_START, IQ_START, HALF_BLK = _OFFS[3], _OFFS[7], BLK // 2
SMALL_KV_START, SMALL_KV_WIDTH = _OFFS[4], _OFFS[6] - _OFFS[4]
SMALL_IDX_START = _OFFS[8]
assert Q_START % BLK == 0 and IQ_START % HALF_BLK == 0
assert SMALL_KV_START % SMALL_KV_WIDTH == 0 and SMALL_IDX_START % LANES == 0
assert IDX_DIM + IDX_HEADS <= LANES and _OFFS[9] == SMALL_IDX_START + IDX_DIM

VMEM_LIMIT = 56 * 1024 * 1024
LOG2E = 1.4426950408889634
NEG_BIG = -1e30
F32_MAX = 3.4028234663852886e38

INPROJ_TM, INPROJ_TN = 2048, 1024
KVPREP_TP = 512
TQ = 256
CK = 256
CB = 512
COUNT_WAYS = 8
SEARCH_MARGIN = 0.02
SEARCH_FIXED_PASSES = 11
SEARCH_INTERP_PASSES = 13
MAX_SEARCH_ITERS = 400
MERGE_STAGE_ROWS = 512
MAX_SHIFT_BOUND = 48.0


def _cparams(sem):
    return pltpu.CompilerParams(dimension_semantics=sem, vmem_limit_bytes=VMEM_LIMIT)


def _dot_nt(a, b):
    return lax.dot_general(a, b, (((1,), (1,)), ((), ())), preferred_element_type=F32)


def _rope(x, cos_t, sin_t, half, period):
    lane = lax.broadcasted_iota(jnp.int32, x.shape, 1) % period
    rolled = jnp.where(lane < half, pltpu.roll(x, LANES - half, 1), pltpu.roll(x, half, 1))
    return x * cos_t + rolled * sin_t


_HEAD_PERM = np.concatenate([np.arange(0, 16), np.arange(32, 80), np.arange(16, 32), np.arange(80, 128)])
assert B_HEAD_DIM == LANES and B_HEAD_DIM // 8 == 16


def _rope_swapped(x, cos_t, sin_t):
    return x * cos_t + pltpu.roll(x, LANES // 2, 1) * sin_t


def _permute_head_rows(src_ref, dst_ref, hd):
    hb, mid = B_HEAD_DIM // 8, B_HEAD_DIM // 2
    base = hd * B_HEAD_DIM
    for src, dst, n in ((0, 0, hb), (2 * hb, hb, mid - hb), (hb, mid, hb), (mid + hb, mid + hb, mid - hb)):
        dst_ref[base + dst:base + dst + n, :] = src_ref[base + src:base + src + n, :].astype(BF16)


def _rms_rows(x, gain):
    ms = jnp.mean(x * x, axis=-1, keepdims=True)
    return (x * lax.rsqrt(ms + EPS) * gain).astype(BF16)


def _inproj_kernel(tab_ref, h_ref, wt_ref, proj_ref):
    del tab_ref
    proj_ref[...] = _dot_nt(h_ref[...], wt_ref[...].astype(BF16)).astype(BF16)


def _inproj(h, w_t):
    m = h.shape[0]
    tm = min(INPROJ_TM, m)
    starts = [s + d for s in MAIN_START for d in range(0, BLK, INPROJ_TN)]
    tab = jnp.asarray(np.array([s // ROW_ALIGN for s in starts], np.int32))
    return pl.pallas_call(
        _inproj_kernel,
        grid_spec=pltpu.PrefetchScalarGridSpec(
            num_scalar_prefetch=1,
            grid=(m // tm, len(starts)),
            in_specs=[
                pl.BlockSpec((tm, D_MODEL), lambda i, n, tab: (i, 0)),
                pl.BlockSpec((pl.Element(INPROJ_TN), pl.Element(D_MODEL)),
                             lambda i, n, tab: (tab[n] * ROW_ALIGN, 0)),
            ],
            out_specs=pl.BlockSpec((tm, INPROJ_TN), lambda i, n, tab: (i, n)),
        ),
        out_shape=jax.ShapeDtypeStruct((m, NB_MAIN * BLK), BF16),
        compiler_params=_cparams(("parallel", "arbitrary")),
        name="inproj",
    )(tab, h, w_t)


def _kvprep_kernel(x_ref, g_ref, wkv_ref, widx_ref, wq_ref, wiqa_ref, wiqb_ref, pos_ref, fc_ref,
                   gk_ref, gq_ref, lng_ref, lnb_ref,
                   h_ref, q_ref, iq_ref, k_ref, vt_ref, iklo_ref, ikhi_ref, iwt_ref,
                   wkv_bf_ref, widx_bf_ref, wq_bf_ref, wiq_bf_ref):
    @pl.when(jnp.logical_and(pl.program_id(0) == 0, pl.program_id(1) == 0))
    def _():
        wkv_bf_ref[...] = wkv_ref[...].astype(BF16)
        for hd in range(B_KV_HEADS):
            _permute_head_rows(wkv_ref, wkv_bf_ref, hd)
        widx_bf_ref[...] = widx_ref[...].astype(BF16)
        for hd in range(B_HEADS):
            _permute_head_rows(wq_ref, wq_bf_ref, hd)
        wiq_bf_ref[:HALF_BLK, :] = wiqa_ref[...].astype(BF16)
        wiq_bf_ref[HALF_BLK:, :] = wiqb_ref[...].astype(BF16)

    h = _rms_rows(x_ref[...], g_ref[...])
    h_ref[...] = h
    kv = _dot_nt(h, wkv_bf_ref[...])
    ikp = _dot_nt(h, widx_bf_ref[...])

    hb, hi = B_HEAD_DIM // 8, IDX_DIM // 8
    ang = pos_ref[0].astype(F32) * fc_ref[...]
    cos_c, sin_c = jnp.cos(ang), jnp.sin(ang)
    lane = lax.broadcasted_iota(jnp.int32, ang.shape, 1)
    second = jnp.logical_and(lane >= LANES // 2, lane < LANES // 2 + hb)
    cos_b = jnp.where(lane < hb, cos_c, jnp.where(second, pltpu.roll(cos_c, LANES // 2, 1), 1.0))
    sin_b = jnp.where(lane < hb, -sin_c, jnp.where(second, pltpu.roll(sin_c, LANES // 2, 1), 0.0))
    cos_i = jnp.ones_like(cos_c)
    sin_i = jnp.zeros_like(sin_c)
    for head in range(LANES // IDX_DIM):
        for part, sign in enumerate((-1.0, 1.0)):
            first = head * IDX_DIM + part * hi
            here = jnp.logical_and(lane >= first, lane < first + hi)
            shift = (first - hb) % LANES
            cos_i = jnp.where(here, pltpu.roll(cos_c, shift, 1), cos_i)
            sin_i = jnp.where(here, sign * pltpu.roll(sin_c, shift, 1), sin_i)
    nblk = q_ref.shape[0]
    gain = gq_ref[...] * (B_HEAD_DIM ** -0.5 * LOG2E)
    qp = _dot_nt(h, wq_bf_ref[...])
    for hd in range(B_HEADS):
        slab = qp[:, hd * B_HEAD_DIM:(hd + 1) * B_HEAD_DIM]
        r = lax.rsqrt(jnp.mean(slab * slab, axis=-1, keepdims=True) + EPS)
        qr = _rope_swapped(slab * r * gain, cos_b, sin_b).astype(BF16)
        for blk in range(nblk):
            q_ref[blk, hd] = qr[blk * TQ:(blk + 1) * TQ, :]
    iqp = _dot_nt(h, wiq_bf_ref[...])
    for j in range(IDX_HEADS // 2):
        ir = _rope(iqp[:, j * LANES:(j + 1) * LANES], cos_i, sin_i, IDX_DIM // 8, IDX_DIM).astype(BF16)
        for blk in range(nblk):
            iq_ref[blk, j // 2, (j % 2) * TQ:(j % 2 + 1) * TQ, :] = ir[blk * TQ:(blk + 1) * TQ, :]

    for n in range(B_KV_HEADS):
        kh = kv[:, n * B_HEAD_DIM:(n + 1) * B_HEAD_DIM]
        r = lax.rsqrt(jnp.mean(kh * kh, axis=-1, keepdims=True) + EPS)
        kn = kh * r * gk_ref[...]
        k_ref[0, n] = _rope_swapped(kn, cos_b, sin_b).astype(BF16)
        vt = kv[:, (B_KV_HEADS + n) * B_HEAD_DIM:(B_KV_HEADS + n + 1) * B_HEAD_DIM].T
        for c in range(vt.shape[1] // CB):
            vt_ref[0, n, c] = vt[:, c * CB:(c + 1) * CB].astype(BF16)

    lane = lax.broadcasted_iota(jnp.int32, ikp.shape, 1)
    live = lane < IDX_DIM
    mu = jnp.sum(jnp.where(live, ikp, 0.0), axis=-1, keepdims=True) * (1.0 / IDX_DIM)
    d = jnp.where(live, ikp - mu, 0.0)
    var = jnp.sum(d * d, axis=-1, keepdims=True) * (1.0 / IDX_DIM)
    y = d * lax.rsqrt(var + EPS) * lng_ref[...] + lnb_ref[...]
    yr = jnp.where(live, _rope(y, cos_i, sin_i, IDX_DIM // 8, IDX_DIM), 0.0)
    iklo_ref[0] = yr.astype(BF16)
    ikhi_ref[0] = pltpu.roll(yr, IDX_DIM, 1).astype(BF16)

    iw = ikp * (IDX_DIM ** -0.5 * IDX_HEADS ** -0.5)
    iwt_ref[0] = iw.T[IDX_DIM:IDX_DIM + IDX_HEADS, :]


def _kvprep(x3, gain, w_t, pos3, freqs, gk, gq, lng, lnb):
    bsz, s, _ = x3.shape
    tp = min(KVPREP_TP, s)
    per_b, nblk = s // tp, tp // TQ
    row = lambda b, i: (b, i, 0)
    const = lambda b, i: (0, 0)
    tab_spec = pl.BlockSpec((1, LANES), const)
    wrows = lambda rows, start: pl.BlockSpec((rows, D_MODEL), lambda b, i: (start // rows, 0),
                                             pipeline_mode=pl.Buffered(1))
    out_h = pl.BlockSpec((tp, D_MODEL), lambda b, i: (b * per_b + i, 0))
    out_q = pl.BlockSpec((nblk, B_HEADS, TQ, B_HEAD_DIM), lambda b, i: (b * per_b + i, 0, 0, 0))
    out_iq = pl.BlockSpec((nblk, IDX_HEADS // 4, 2 * TQ, LANES), lambda b, i: (b * per_b + i, 0, 0, 0))
    out_tok = pl.BlockSpec((1, tp, LANES), row)
    out_k = pl.BlockSpec((1, B_KV_HEADS, tp, B_HEAD_DIM), lambda b, i: (b, 0, i, 0))
    out_vt = pl.BlockSpec((1, B_KV_HEADS, tp // CB, B_HEAD_DIM, CB), lambda b, i: (b, 0, i, 0, 0))
    out_iwt = pl.BlockSpec((1, IDX_HEADS, tp), lambda b, i: (b, 0, i))
    nq = bsz * s // TQ
    return pl.pallas_call(
        _kvprep_kernel,
        grid=(bsz, per_b),
        in_specs=[pl.BlockSpec((None, tp, D_MODEL), row), pl.BlockSpec((1, D_MODEL), const),
                  wrows(SMALL_KV_WIDTH, SMALL_KV_START), wrows(LANES, SMALL_IDX_START),
                  wrows(BLK, Q_START), wrows(HALF_BLK, IQ_START), wrows(HALF_BLK, IQ_START + HALF_BLK),
                  pl.BlockSpec((1, tp, 1), row),
                  tab_spec, tab_spec, tab_spec, tab_spec, tab_spec],
        out_specs=[out_h, out_q, out_iq, out_k, out_vt, out_tok, out_tok, out_iwt],
        out_shape=[
            jax.ShapeDtypeStruct((bsz * s, D_MODEL), BF16),
            jax.ShapeDtypeStruct((nq, B_HEADS, TQ, B_HEAD_DIM), BF16),
            jax.ShapeDtypeStruct((nq, IDX_HEADS // 4, 2 * TQ, LANES), BF16),
            jax.ShapeDtypeStruct((bsz, B_KV_HEADS, s, B_HEAD_DIM), BF16),
            jax.ShapeDtypeStruct((bsz, B_KV_HEADS, s // CB, B_HEAD_DIM, CB), BF16),
            jax.ShapeDtypeStruct((bsz, s, LANES), BF16),
            jax.ShapeDtypeStruct((bsz, s, LANES), BF16),
            jax.ShapeDtypeStruct((bsz, IDX_HEADS, s), F32),
        ],
        scratch_shapes=[pltpu.VMEM((SMALL_KV_WIDTH, D_MODEL), BF16), pltpu.VMEM((LANES, D_MODEL), BF16),
                        pltpu.VMEM((BLK, D_MODEL), BF16), pltpu.VMEM((BLK, D_MODEL), BF16)],
        compiler_params=_cparams(("arbitrary", "arbitrary")),
        name="kvprep",
    )(x3, gain, w_t, w_t, w_t, w_t, w_t, pos3, freqs, gk, gq, lng, lnb)


def _gelu_tanh(x):
    c = -2.0 * (2.0 / np.pi) ** 0.5 * LOG2E
    return x / (1.0 + jnp.exp2(x * (x * x * (0.044715 * c) + c)))


def _gmlp_kernel(u_ref, v_ref, z_ref, lng_ref, lnb_ref, ws_ref, sbt_ref, o_ref):
    tm = u_ref.shape[0]
    u = _gelu_tanh(u_ref[...].astype(F32))
    v = _gelu_tanh(v_ref[...].astype(F32))
    mu = jnp.mean(v, axis=-1, keepdims=True)
    d = v - mu
    var = jnp.mean(d * d, axis=-1, keepdims=True)
    vn = (d * lax.rsqrt(var + EPS) * lng_ref[...] + lnb_ref[...]).astype(BF16)
    gate = u * jax.nn.silu(z_ref[...].astype(F32))
    tri = (lax.broadcasted_iota(jnp.int32, (CHUNK, CHUNK), 1)
           <= lax.broadcasted_iota(jnp.int32, (CHUNK, CHUNK), 0))
    for g in range(A_GROUPS):
        wg = jnp.where(tri, ws_ref[g], 0.0).astype(BF16)
        bias = sbt_ref[:, g:g + 1]
        cols = slice(g * A_GROUP_DIM, (g + 1) * A_GROUP_DIM)
        for c in range(tm // CHUNK):
            rows = slice(c * CHUNK, (c + 1) * CHUNK)
            sg = jnp.dot(wg, vn[rows, cols], preferred_element_type=F32) + bias
            o_ref[rows, cols] = (gate[rows, cols] * sg).astype(BF16)


def _gmlp(proj, lng, lnb, ws, sbt):
    m = proj.shape[0]
    tm = min(512, m)
    col = lambda c: pl.BlockSpec((tm, BLK), lambda i: (i, c))
    full = lambda shape: pl.BlockSpec(shape, lambda i: (0,) * len(shape))
    return pl.pallas_call(
        _gmlp_kernel,
        grid=(m // tm,),
        in_specs=[col(COL_AU), col(COL_AV), col(COL_AZ), full((1, A_WIDTH)), full((1, A_WIDTH)),
                  full((A_GROUPS, CHUNK, CHUNK)), full((CHUNK, A_GROUPS))],
        out_specs=pl.BlockSpec((tm, A_WIDTH), lambda i: (i, 0)),
        out_shape=jax.ShapeDtypeStruct((m, A_WIDTH), BF16),
        compiler_params=_cparams(("parallel",)),
        name="gmlp",
    )(proj, proj, proj, lng, lnb, ws, sbt)


def _memkv_kernel(mem_ref, g_ref, w_ref, gk_ref, km_ref, vm_ref, wbf_ref):
    @pl.when(pl.program_id(0) == 0)
    def _():
        wbf_ref[...] = w_ref[...].astype(BF16)

    h = _rms_rows(mem_ref[0], g_ref[...])
    kv = jnp.dot(h, wbf_ref[...], preferred_element_type=F32)
    for hd in range(M_HEADS):
        kh = kv[:, hd * M_HEAD_DIM:(hd + 1) * M_HEAD_DIM]
        r = lax.rsqrt(jnp.mean(kh * kh, axis=-1, keepdims=True) + EPS)
        km_ref[0, hd] = (kh * r * gk_ref[...]).astype(BF16)
        vm_ref[0, hd] = kv[:, M_WIDTH + hd * M_HEAD_DIM:M_WIDTH + (hd + 1) * M_HEAD_DIM].astype(BF16)


def _memkv(mem, gain, w_kv, gk):
    bsz, ml, _ = mem.shape
    out = pl.BlockSpec((1, M_HEADS, ml, M_HEAD_DIM), lambda b: (b, 0, 0, 0))
    shp = jax.ShapeDtypeStruct((bsz, M_HEADS, ml, M_HEAD_DIM), BF16)
    return pl.pallas_call(
        _memkv_kernel,
        grid=(bsz,),
        in_specs=[pl.BlockSpec((1, ml, D_MODEL), lambda b: (b, 0, 0)),
                  pl.BlockSpec((1, D_MODEL), lambda b: (0, 0)),
                  pl.BlockSpec((D_MODEL, 2 * M_WIDTH), lambda b: (0, 0), pipeline_mode=pl.Buffered(1)),
                  pl.BlockSpec((1, M_HEAD_DIM), lambda b: (0, 0))],
        out_specs=[out, out],
        out_shape=[shp, shp],
        scratch_shapes=[pltpu.VMEM((D_MODEL, 2 * M_WIDTH), BF16)],
        compiler_params=_cparams(("arbitrary",)),
        name="memkv",
    )(mem, gain, w_kv, gk)


def _memattn_kernel(q_ref, z_ref, km_ref, vm_ref, gq_ref, o_ref):
    qscale = M_HEAD_DIM ** -0.5 * LOG2E
    for hd in range(M_HEADS):
        cols = slice(hd * M_HEAD_DIM, (hd + 1) * M_HEAD_DIM)
        q = q_ref[:, cols].astype(F32)
        r = lax.rsqrt(jnp.mean(q * q, axis=-1, keepdims=True) + EPS)
        qn = (q * r * gq_ref[...] * qscale).astype(BF16)
        lg = _dot_nt(qn, km_ref[0, hd])
        p = jnp.exp2(lg - jnp.max(lg, axis=-1, keepdims=True))
        l = jnp.sum(p, axis=-1, keepdims=True)
        o = jnp.dot(p.astype(BF16), vm_ref[0, hd], preferred_element_type=F32) / l
        o_ref[:, cols] = (o * jax.nn.silu(z_ref[:, cols].astype(F32))).astype(BF16)


def _memattn(proj, km, vm, gq, s):
    m = proj.shape[0]
    tm = min(512, s)
    per_b = s // tm
    ml = km.shape[2]
    kv_spec = pl.BlockSpec((1, M_HEADS, ml, M_HEAD_DIM), lambda i: (i // per_b, 0, 0, 0))
    return pl.pallas_call(
        _memattn_kernel,
        grid=(m // tm,),
        in_specs=[pl.BlockSpec((tm, BLK), lambda i: (i, COL_MQ)),
                  pl.BlockSpec((tm, BLK), lambda i: (i, COL_MZ)),
                  kv_spec, kv_spec,
                  pl.BlockSpec((1, M_HEAD_DIM), lambda i: (0, 0))],
        out_specs=pl.BlockSpec((tm, M_WIDTH), lambda i: (i, 0)),
        out_shape=jax.ShapeDtypeStruct((m, M_WIDTH), BF16),
        compiler_params=_cparams(("parallel",)),
        name="memattn",
    )(proj, proj, km, vm, gq)


def _dsa_kernel(topk, q_ref, z_ref, iq_ref, iwt_ref,
                k_ref, vt_ref, iklo_ref, ikhi_ref, tri_ref, bound_ref, o_ref,
                sc_ref, bias_ref, lg_ref, acc_ref):
    qb = pl.program_id(1)
    nck = (qb * TQ + TQ + CK - 1) // CK
    nbig = (qb * TQ + TQ + CB - 1) // CB
    t_lane = qb * TQ + lax.broadcasted_iota(jnp.int32, (1, TQ), 1)
    gw = B_GROUP * TQ

    def fold8(a):
        return a.reshape(a.shape[0] // 8, 8, a.shape[1])

    def grouped_loop(trips, body, init, per_step=2, start=0):
        carry, done = init, start
        while per_step >= 1:
            def step(i, c, first=done, n=per_step):
                for u in range(n):
                    c = body(first + n * i + u, c)
                return c

            steps = (trips - done) // per_step
            carry = lax.fori_loop(0, steps, step, carry)
            done = done + steps * per_step
            per_step //= 2
        return carry

    wt = iwt_ref[0]

    def idx_body(c, carry):
        mn8, mx8, s1, s2 = carry
        off = pl.multiple_of(c * CK, CK)
        acc = jnp.zeros((CK, TQ), F32)
        for jj in range(IDX_HEADS // 4):
            rhs = iq_ref[0, jj]
            for half, keys_ref in enumerate((iklo_ref, ikhi_ref)):
                d = _dot_nt(keys_ref[0, pl.ds(off, CK), :], rhs)
                ha, hb = 4 * jj + half, 4 * jj + 2 + half
                acc = (acc + jnp.maximum(d[:, :TQ], 0.0) * wt[ha:ha + 1, :]
                       + jnp.maximum(d[:, TQ:], 0.0) * wt[hb:hb + 1, :])
        key = off + lax.broadcasted_iota(jnp.int32, (CK, TQ), 0)
        causal = key <= t_lane
        sc = jnp.where(causal, acc, -jnp.inf)
        sc_ref[pl.ds(off, CK), :] = sc
        live = jnp.where(causal, acc, 0.0)
        mn8 = jnp.minimum(mn8, jnp.min(fold8(jnp.where(causal, acc, jnp.inf)), axis=0))
        mx8 = jnp.maximum(mx8, jnp.max(fold8(sc), axis=0))
        s1 = s1 + jnp.sum(fold8(live), axis=0)
        s2 = s2 + jnp.sum(fold8(live * live), axis=0)
        return mn8, mx8, s1, s2

    zero8 = jnp.zeros((8, TQ), F32)
    stats = (jnp.full((8, TQ), jnp.inf, F32), jnp.full((8, TQ), -jnp.inf, F32), zero8, zero8)
    mn8, mx8, s1, s2 = grouped_loop(nck, idx_body, stats, per_step=4)
    row_min = jnp.min(mn8, axis=0, keepdims=True)
    row_max = jnp.max(mx8, axis=0, keepdims=True)

    def fill_body(c, carry):
        sc_ref[pl.ds(pl.multiple_of(c * CK, CK), CK), :] = jnp.full((CK, TQ), -jnp.inf, F32)
        return carry

    lax.fori_loop(nck, nbig * (CB // CK), fill_body, 0)

    def count_rows(pred):
        rows = 8 * COUNT_WAYS

        def body(c, cnt):
            off = pl.multiple_of(c * CB, CB)
            for r in range(CB // rows):
                cnt = cnt + jnp.where(pred(sc_ref[pl.ds(off + r * rows, rows), :]), 1.0, 0.0)
            return cnt

        cnt = lax.fori_loop(0, nbig, body, jnp.zeros((rows, TQ), F32))
        return jnp.sum(cnt, axis=0, keepdims=True)

    kf = float(topk)
    n_valid = (t_lane + 1).astype(F32)
    all_rows = (t_lane + 1) <= topk

    def search_pass(st, probe, stuck):
        lo, hi, clo, chi, thr, fin, tie = st
        cnt = count_rows(lambda blk: blk >= probe)
        active = fin < 0.5
        hit = cnt == kf
        end_thr = jnp.where(jnp.logical_and(stuck, cnt < kf), lo, probe)
        ends = jnp.logical_and(active, jnp.logical_or(hit, stuck))
        thr = jnp.where(ends, end_thr, thr)
        tie = jnp.where(jnp.logical_and(ends, jnp.logical_not(hit)), 1.0, tie)
        fin = jnp.where(ends, 1.0, fin)
        up = jnp.logical_and(active, cnt > kf)
        dn = jnp.logical_and(active, cnt < kf)
        return (jnp.where(up, probe, lo), jnp.where(dn, probe, hi), jnp.where(up, cnt, clo),
                jnp.where(dn, cnt, chi), thr, fin, tie)

    def next_probe(st, halve=False):
        lo, hi, clo, chi = st[:4]
        frac = jnp.clip((clo - kf + 0.5) / (clo - chi + 1.0), SEARCH_MARGIN, 1.0 - SEARCH_MARGIN)
        probe = lo + (hi - lo) * frac
        outside = jnp.logical_or(jnp.logical_or(probe <= lo, probe >= hi), halve)
        probe = jnp.where(outside, lo * 0.5 + hi * 0.5, probe)
        stuck = jnp.logical_or(probe <= lo, probe >= hi)
        return jnp.where(stuck, hi, probe), stuck

    mean = jnp.sum(s1, axis=0, keepdims=True) / n_valid
    var = jnp.maximum(jnp.sum(s2, axis=0, keepdims=True) / n_valid - mean * mean, 0.0)
    tail = jnp.clip(kf / n_valid, 1e-6, 1.0 - 1e-6)
    tq = jnp.sqrt(-2.0 * jnp.log(jnp.minimum(tail, 1.0 - tail)))
    zq = tq - ((0.010328 * tq + 0.802853) * tq + 2.515517) / (
        ((0.001308 * tq + 0.189269) * tq + 1.432788) * tq + 1.0)
    guess = mean + jnp.where(tail < 0.5, zq, -zq) * jnp.sqrt(var)
    inside = jnp.logical_and(guess >= row_min, guess <= row_max)
    probe0 = jnp.where(inside, guess, row_min * 0.5 + row_max * 0.5)

    ones = jnp.ones((1, TQ), F32)
    st = (row_min, row_max, n_valid, 0.0 * ones, jnp.where(all_rows, -F32_MAX, row_max),
          jnp.where(all_rows, 1.0, 0.0), 0.0 * ones)
    st = search_pass(st, probe0, probe0 < row_min)

    def fixed_body(i, st):
        return search_pass(st, *next_probe(st))

    st = lax.fori_loop(0, SEARCH_FIXED_PASSES, fixed_body, st)

    def more_cond(c):
        return jnp.logical_and(c[0] < MAX_SEARCH_ITERS, jnp.min(c[1][5]) < 0.5)

    def more_body(c):
        return c[0] + 1, search_pass(c[1], *next_probe(c[1], c[0] >= SEARCH_INTERP_PASSES))

    _, st = lax.while_loop(more_cond, more_body, (jnp.int32(0), st))
    thr, tie = st[4], st[6]

    def plain_mask():
        def body(c, carry):
            off = pl.multiple_of(c * CB, CB)
            bias_ref[pl.ds(off, CB), :] = jnp.where(sc_ref[pl.ds(off, CB), :] >= thr, 0.0, NEG_BIG)
            return carry

        lax.fori_loop(0, nbig, body, 0)

    def tied_mask():
        n_gt = count_rows(lambda blk: blk > thr)
        need = jnp.where(all_rows, F32_MAX, kf - n_gt)

        def body(c, seen):
            off = pl.multiple_of(c * CB, CB)
            blk = sc_ref[pl.ds(off, CB), :]
            eq = blk == thr
            pref = jnp.dot(tri_ref[...], jnp.where(eq, 1.0, 0.0).astype(BF16),
                           preferred_element_type=F32) + seen
            keep = jnp.logical_or(blk > thr, jnp.logical_and(eq, pref <= need))
            bias_ref[pl.ds(off, CB), :] = jnp.where(keep, 0.0, NEG_BIG)
            return pref[CB - 1:CB, :]

        lax.fori_loop(0, nbig, body, jnp.zeros((1, TQ), F32))

    lax.cond(jnp.max(tie) > 0.5, tied_mask, plain_mask)

    logit_bound = bound_ref[0, 0]

    def masked_logits(n, off):
        b = bias_ref[pl.ds(off, CB), :]
        qn = q_ref[0, n * B_GROUP:(n + 1) * B_GROUP].reshape(gw, B_HEAD_DIM)
        lg = _dot_nt(k_ref[0, n, pl.ds(off, CB), :], qn)
        return jnp.concatenate([lg[:, g * TQ:(g + 1) * TQ] + b for g in range(B_GROUP)], axis=1)

    for n in range(B_KV_HEADS):
        acc_ref[...] = jnp.zeros(acc_ref.shape, F32)

        def one_pass(n=n):
            def body(c, l8):
                off = pl.multiple_of(c * CB, CB)
                p = jnp.exp2(masked_logits(n, off) - logit_bound)
                acc_ref[...] += jnp.dot(vt_ref[0, n, c], p.astype(BF16), preferred_element_type=F32)
                return l8 + jnp.sum(fold8(p), axis=0)

            return grouped_loop(nbig, body, jnp.zeros((8, gw), F32), per_step=4)

        def two_pass(n=n):
            def logit_body(c, mx8):
                off = pl.multiple_of(c * CB, CB)
                lg = masked_logits(n, off)
                lg_ref[pl.ds(off, CB), :] = lg
                return jnp.maximum(mx8, jnp.max(fold8(lg), axis=0))

            mx8 = grouped_loop(nbig, logit_body, jnp.full((8, gw), NEG_BIG, F32))
            m = jnp.max(mx8, axis=0, keepdims=True)

            def pv_body(c, l8):
                off = pl.multiple_of(c * CB, CB)
                p = jnp.exp2(lg_ref[pl.ds(off, CB), :] - m)
                acc_ref[...] += jnp.dot(vt_ref[0, n, c], p.astype(BF16), preferred_element_type=F32)
                return l8 + jnp.sum(fold8(p), axis=0)

            return grouped_loop(nbig, pv_body, jnp.zeros((8, gw), F32))

        l8 = lax.cond(logit_bound <= MAX_SHIFT_BOUND, one_pass, two_pass)
        o_t = acc_ref[...] / jnp.sum(l8, axis=0, keepdims=True)
        for g in range(B_GROUP):
            cols = slice((n * B_GROUP + g) * B_HEAD_DIM, (n * B_GROUP + g + 1) * B_HEAD_DIM)
            o = o_t[:, g * TQ:(g + 1) * TQ].T
            o_ref[:, cols] = (o * jax.nn.silu(z_ref[:, cols].astype(F32))).astype(BF16)


def _dsa(proj, q, iq, iwt, k, vt, iklo, ikhi, gq, gk, bsz, s):
    nq = s // TQ
    bound = (1.02 * B_HEAD_DIM ** 0.5 * LOG2E * jnp.max(jnp.abs(gq)) * jnp.max(jnp.abs(gk))).reshape(1, 1)
    topk = min(TOPK_MAX, s // 4)
    gw = B_GROUP * TQ
    blk4 = lambda a: pl.BlockSpec((1,) + a.shape[1:], lambda b, i: (b * nq + i, 0, 0, 0))
    ik = pl.BlockSpec((1, s, LANES), lambda b, i: (b, 0, 0))
    tri = jnp.tril(jnp.ones((CB, CB), BF16))
    return pl.pallas_call(
        functools.partial(_dsa_kernel, topk),
        grid=(bsz, nq),
        in_specs=[blk4(q), pl.BlockSpec((TQ, BLK), lambda b, i: (b * nq + i, COL_BZ)), blk4(iq),
                  pl.BlockSpec((1, IDX_HEADS, TQ), lambda b, i: (b, 0, i)),
                  pl.BlockSpec((1, B_KV_HEADS, s, B_HEAD_DIM), lambda b, i: (b, 0, 0, 0)),
                  pl.BlockSpec((1, B_KV_HEADS, s // CB, B_HEAD_DIM, CB), lambda b, i: (b, 0, 0, 0, 0)),
                  ik, ik,
                  pl.BlockSpec((CB, CB), lambda b, i: (0, 0)),
                  pl.BlockSpec(memory_space=pltpu.SMEM)],
        out_specs=pl.BlockSpec((TQ, B_WIDTH), lambda b, i: (b * nq + i, 0)),
        out_shape=jax.ShapeDtypeStruct((bsz * s, B_WIDTH), BF16),
        scratch_shapes=[
            pltpu.VMEM((s, TQ), F32),
            pltpu.VMEM((s, TQ), F32),
            pltpu.VMEM((s, gw), F32),
            pltpu.VMEM((B_HEAD_DIM, gw), F32),
        ],
        compiler_params=_cparams(("parallel", "arbitrary")),
        name="dsa",
    )(q, proj, iq, iwt, k, vt, iklo, ikhi, tri, bound)


def _merge_kernel(ta_ref, tb_ref, tm_ref, ga_ref, gb_ref, gm_ref, x_ref,
                  wa_hbm, wb_hbm, wm_hbm, wo_hbm, o_ref,
                  wa_ref, wb_ref, wm_ref, wo_ref, stage_ref, sem):
    @pl.when(pl.program_id(0) == 0)
    def _():
        chunks = [(src, dst, r) for src, dst in ((wa_hbm, wa_ref), (wb_hbm, wb_ref), (wm_hbm, wm_ref),
                                                 (wo_hbm, wo_ref))
                  for r in range(0, src.shape[0], MERGE_STAGE_ROWS)]

        def copy(j):
            src, _, r = chunks[j]
            return pltpu.make_async_copy(src.at[pl.ds(r, MERGE_STAGE_ROWS), :], stage_ref.at[j % 2],
                                         sem.at[j % 2])

        copy(0).start()
        for j, (_, dst, r) in enumerate(chunks):
            if j + 1 < len(chunks):
                copy(j + 1).start()
            copy(j).wait()
            dst[pl.ds(r, MERGE_STAGE_ROWS), :] = stage_ref[j % 2].astype(BF16)

    def branch(t_ref, g_ref, w_ref):
        y = jnp.dot(t_ref[...], w_ref[...], preferred_element_type=F32)
        return jax.nn.sigmoid(g_ref[...].astype(F32)) * y

    merged = branch(ta_ref, ga_ref, wa_ref) + branch(tb_ref, gb_ref, wb_ref) + branch(tm_ref, gm_ref, wm_ref)
    o_ref[...] = x_ref[...] + jnp.dot(merged.astype(BF16), wo_ref[...], preferred_element_type=F32)


def _merge(ta, tb, tmem, proj, x2, wa, wb, wm, wo):
    m = x2.shape[0]
    tm = min(256, m)
    act = pl.BlockSpec((tm, BLK), lambda i: (i, 0))
    gate = lambda c: pl.BlockSpec((tm, D_MODEL), lambda i: (i, c // 2))
    wide = pl.BlockSpec((tm, D_MODEL), lambda i: (i, 0))
    hbm = pl.BlockSpec(memory_space=pl.ANY)
    return pl.pallas_call(
        _merge_kernel,
        grid=(m // tm,),
        in_specs=[act, act, act, gate(COL_GA), gate(COL_GB), gate(COL_GM), wide, hbm, hbm, hbm, hbm],
        out_specs=wide,
        out_shape=jax.ShapeDtypeStruct((m, D_MODEL), F32),
        scratch_shapes=[pltpu.VMEM(w.shape, BF16) for w in (wa, wb, wm, wo)]
        + [pltpu.VMEM((2, MERGE_STAGE_ROWS, D_MODEL), F32), pltpu.SemaphoreType.DMA((2,))],
        compiler_params=_cparams(("arbitrary",)),
        name="merge",
    )(ta, tb, tmem, proj, proj, proj, x2, wa, wb, wm, wo)


def _rope_freqs():
    inv = lambda half: ROPE_THETA ** (-np.arange(half, dtype=np.float32) / half)
    freq = np.zeros((1, LANES), np.float32)
    hb, hi = B_HEAD_DIM // 8, IDX_DIM // 8
    freq[0, :hb] = inv(hb)
    freq[0, hb:hb + hi] = inv(hi)
    return jnp.asarray(freq)


def _layer(x, mem, positions, norm_gain, w_in, gmlp_ln_gain, gmlp_ln_bias, spatial_w, spatial_b,
           w_branch_a, q_norm_gain, k_norm_gain, idx_k_ln_gain, idx_k_ln_bias, w_branch_b,
           mem_norm_gain, w_mem_kv, mem_q_norm_gain, mem_k_norm_gain, w_branch_m, w_out):
    bsz, s, _ = x.shape
    m = bsz * s
    row = lambda a: a.reshape(1, -1).astype(F32)
    pad_lanes = lambda a: jnp.pad(a.reshape(1, -1).astype(F32), ((0, 0), (0, LANES - a.shape[-1])))
    x2 = x.reshape(m, D_MODEL)

    w_t = w_in.T
    h, q, iq, k, vt, iklo, ikhi, iwt = _kvprep(
        x, row(norm_gain), w_t, positions.reshape(bsz, s, 1).astype(jnp.int32), _rope_freqs(),
        row(k_norm_gain[_HEAD_PERM]), row(q_norm_gain[_HEAD_PERM]),
        pad_lanes(idx_k_ln_gain), pad_lanes(idx_k_ln_bias))
    proj = _inproj(h, w_t)

    t_a = _gmlp(proj, row(gmlp_ln_gain), row(gmlp_ln_bias), spatial_w.astype(F32), spatial_b.T.astype(F32))
    km, vm = _memkv(mem, row(mem_norm_gain), w_mem_kv, row(mem_k_norm_gain))
    t_m = _memattn(proj, km, vm, row(mem_q_norm_gain), s)
    t_b = _dsa(proj, q, iq, iwt, k, vt, iklo, ikhi, row(q_norm_gain), row(k_norm_gain), bsz, s)

    out = _merge(t_a, t_b, t_m, proj, x2, w_branch_a, w_branch_b, w_branch_m, w_out)
    return out.reshape(bsz, s, D_MODEL)


def kernel(x, mem, positions, norm_gain, w_in, gmlp_ln_gain, gmlp_ln_bias, spatial_w, spatial_b, w_branch_a, q_norm_gain, k_norm_gain, idx_k_ln_gain, idx_k_ln_bias, w_branch_b, mem_norm_gain, w_mem_kv, mem_q_norm_gain, mem_k_norm_gain, w_branch_m, w_out):
    for l in range(norm_gain.shape[0]):
        x = _layer(x, mem, positions, norm_gain[l], w_in[l], gmlp_ln_gain[l], gmlp_ln_bias[l],
                   spatial_w[l], spatial_b[l], w_branch_a[l], q_norm_gain[l], k_norm_gain[l],
                   idx_k_ln_gain[l], idx_k_ln_bias[l], w_branch_b[l], mem_norm_gain[l], w_mem_kv[l],
                   mem_q_norm_gain[l], mem_k_norm_gain[l], w_branch_m[l], w_out[l])
    return x
```

```python
import functools

import numpy as np
import jax
import jax.numpy as jnp
from jax import lax
from jax.experimental import pallas as pl
from jax.experimental.pallas import tpu as pltpu

F32 = jnp.float32
BF16 = jnp.bfloat16

D_MODEL = 2048
ROPE_THETA = 500000.0
EPS = 1e-6
A_GROUPS = 8
A_GROUP_DIM = 128
A_WIDTH = A_GROUPS * A_GROUP_DIM
CHUNK = 128
B_HEADS = 8
B_KV_HEADS = 2
B_GROUP = B_HEADS // B_KV_HEADS
B_HEAD_DIM = 128
B_WIDTH = B_HEADS * B_HEAD_DIM
IDX_HEADS = 16
IDX_DIM = 64
TOPK_MAX = 256
M_HEADS = 4
M_HEAD_DIM = 256
M_WIDTH = M_HEADS * M_HEAD_DIM

SPLIT_SIZES = (
    A_WIDTH, A_WIDTH, A_WIDTH,
    B_WIDTH, B_KV_HEADS * B_HEAD_DIM, B_KV_HEADS * B_HEAD_DIM, B_WIDTH,
    IDX_HEADS * IDX_DIM, IDX_DIM, IDX_HEADS,
    M_WIDTH, M_WIDTH,
    D_MODEL, D_MODEL, D_MODEL,
)

LANES = 128
BLK = 1024
COL_AU, COL_AV, COL_AZ, COL_BZ, COL_MQ, COL_MZ = range(6)
COL_GA, COL_GB, COL_GM = 6, 8, 10
NB_MAIN = 12
_OFFS = [int(o) for o in np.concatenate([[0], np.cumsum(SPLIT_SIZES)])]
ROW_ALIGN = 16
MAIN_START = ([_OFFS[i] for i in (0, 1, 2, 6, 10, 11)]
              + [_OFFS[i] + d for i in (12, 13, 14) for d in (0, BLK)])
assert all(s % ROW_ALIGN == 0 for s in MAIN_START)
Q_START, IQ_START, HALF_BLK = _OFFS[3], _OFFS[7], BLK // 2
SMALL_KV_START, SMALL_KV_WIDTH = _OFFS[4], _OFFS[6] - _OFFS[4]
SMALL_IDX_START = _OFFS[8]
assert Q_START % BLK == 0 and IQ_START % HALF_BLK == 0
assert SMALL_KV_START % SMALL_KV_WIDTH == 0 and SMALL_IDX_START % LANES == 0
assert IDX_DIM + IDX_HEADS <= LANES and _OFFS[9] == SMALL_IDX_START + IDX_DIM

VMEM_LIMIT = 56 * 1024 * 1024
LOG2E = 1.4426950408889634
NEG_BIG = -1e30
F32_MAX = 3.4028234663852886e38

INPROJ_TM, INPROJ_TN = 2048, 1024
KVPREP_TP = 512
TQ = 256
CK = 256
CB = 512
COUNT_WAYS = 8
SEARCH_MARGIN = 0.02
SEARCH_FIXED_PASSES = 11
SEARCH_INTERP_PASSES = 13
MAX_SEARCH_ITERS = 400
MERGE_STAGE_ROWS = 512
MAX_SHIFT_BOUND = 48.0


def _cparams(sem):
    return pltpu.CompilerParams(dimension_semantics=sem, vmem_limit_bytes=VMEM_LIMIT)


def _dot_nt(a, b):
    return lax.dot_general(a, b, (((1,), (1,)), ((), ())), preferred_element_type=F32)


_HEAD_PERM = np.concatenate([np.arange(0, 16), np.arange(32, 80), np.arange(16, 32), np.arange(80, 128)])
assert B_HEAD_DIM == LANES and B_HEAD_DIM // 8 == 16
_PAIR_PERM = ((0, 8), (64, 72), (16, 64), (8, 16), (72, 80), (80, 128))
assert 2 * IDX_DIM == LANES and IDX_DIM // 8 == 8


def _rope_swapped(x, cos_t, sin_t):
    return x * cos_t + pltpu.roll(x, LANES // 2, 1) * sin_t


def _permute_head_rows(src_ref, dst_ref, hd):
    hb, mid = B_HEAD_DIM // 8, B_HEAD_DIM // 2
    base = hd * B_HEAD_DIM
    for src, dst, n in ((0, 0, hb), (2 * hb, hb, mid - hb), (hb, mid, hb), (mid + hb, mid + hb, mid - hb)):
        dst_ref[base + dst:base + dst + n, :] = src_ref[base + src:base + src + n, :].astype(BF16)


def _rms_rows(x, gain):
    ms = jnp.mean(x * x, axis=-1, keepdims=True)
    return (x * lax.rsqrt(ms + EPS) * gain).astype(BF16)


def _inproj_kernel(tab_ref, h_ref, wt_ref, proj_ref):
    del tab_ref
    proj_ref[...] = _dot_nt(h_ref[...], wt_ref[...].astype(BF16)).astype(BF16)


def _inproj(h, w_t):
    m = h.shape[0]
    tm = min(INPROJ_TM, m)
    starts = [s + d for s in MAIN_START for d in range(0, BLK, INPROJ_TN)]
    tab = jnp.asarray(np.array([s // ROW_ALIGN for s in starts], np.int32))
    return pl.pallas_call(
        _inproj_kernel,
        grid_spec=pltpu.PrefetchScalarGridSpec(
            num_scalar_prefetch=1,
            grid=(m // tm, len(starts)),
            in_specs=[
                pl.BlockSpec((tm, D_MODEL), lambda i, n, tab: (i, 0)),
                pl.BlockSpec((pl.Element(INPROJ_TN), pl.Element(D_MODEL)),
                             lambda i, n, tab: (tab[n] * ROW_ALIGN, 0)),
            ],
            out_specs=pl.BlockSpec((tm, INPROJ_TN), lambda i, n, tab: (i, n)),
        ),
        out_shape=jax.ShapeDtypeStruct((m, NB_MAIN * BLK), BF16),
        compiler_params=_cparams(("parallel", "arbitrary")),
        name="inproj",
    )(tab, h, w_t)


def _kvprep_kernel(x_ref, g_ref, wkv_ref, widx_ref, wq_ref, wiqa_ref, wiqb_ref, pos_ref, fc_ref,
                   gk_ref, gq_ref, lng_ref, lnb_ref,
                   h_ref, q_ref, iq_ref, k_ref, vt_ref, iklo_ref, ikhi_ref, iwt_ref,
                   wkv_bf_ref, widx_bf_ref, wq_bf_ref, wiq_bf_ref):
    @pl.when(jnp.logical_and(pl.program_id(0) == 0, pl.program_id(1) == 0))
    def _():
        wkv_bf_ref[...] = wkv_ref[...].astype(BF16)
        for hd in range(B_KV_HEADS):
            _permute_head_rows(wkv_ref, wkv_bf_ref, hd)
        widx_bf_ref[...] = widx_ref[...].astype(BF16)
        for hd in range(B_HEADS):
            _permute_head_rows(wq_ref, wq_bf_ref, hd)
        for j in range(IDX_HEADS // 2):
            src = (wiqa_ref, wiqb_ref)[j * LANES // HALF_BLK]
            base = j * LANES % HALF_BLK
            rows = jnp.concatenate([src[base + a:base + b, :] for a, b in _PAIR_PERM], axis=0)
            wiq_bf_ref[j * LANES:(j + 1) * LANES, :] = rows.astype(BF16)

    h = _rms_rows(x_ref[...], g_ref[...])
    h_ref[...] = h
    kv = _dot_nt(h, wkv_bf_ref[...])
    ikp = _dot_nt(h, widx_bf_ref[...])

    hb, hi = B_HEAD_DIM // 8, IDX_DIM // 8
    ang = pos_ref[0].astype(F32) * fc_ref[...]
    cos_c, sin_c = jnp.cos(ang), jnp.sin(ang)
    lane = lax.broadcasted_iota(jnp.int32, ang.shape, 1)
    second = jnp.logical_and(lane >= LANES // 2, lane < LANES // 2 + hb)
    cos_b = jnp.where(lane < hb, cos_c, jnp.where(second, pltpu.roll(cos_c, LANES // 2, 1), 1.0))
    sin_b = jnp.where(lane < hb, -sin_c, jnp.where(second, pltpu.roll(sin_c, LANES // 2, 1), 0.0))
    cos_i = jnp.ones_like(cos_c)
    sin_i = jnp.zeros_like(sin_c)
    for first, sign in ((0, -1.0), (hi, -1.0), (LANES // 2, 1.0), (LANES // 2 + hi, 1.0)):
        here = jnp.logical_and(lane >= first, lane < first + hi)
        shift = (first - hb) % LANES
        cos_i = jnp.where(here, pltpu.roll(cos_c, shift, 1), cos_i)
        sin_i = jnp.where(here, sign * pltpu.roll(sin_c, shift, 1), sin_i)
    nblk = q_ref.shape[0]
    gain = gq_ref[...] * (B_HEAD_DIM ** -0.5 * LOG2E)
    qp = _dot_nt(h, wq_bf_ref[...])
    for hd in range(B_HEADS):
        slab = qp[:, hd * B_HEAD_DIM:(hd + 1) * B_HEAD_DIM]
        r = lax.rsqrt(jnp.mean(slab * slab, axis=-1, keepdims=True) + EPS)
        qr = _rope_swapped(slab * r * gain, cos_b, sin_b).astype(BF16)
        for blk in range(nblk):
            q_ref[blk, hd] = qr[blk * TQ:(blk + 1) * TQ, :]
    iqp = _dot_nt(h, wiq_bf_ref[...])
    for j in range(IDX_HEADS // 2):
        ir = _rope_swapped(iqp[:, j * LANES:(j + 1) * LANES], cos_i, sin_i).astype(BF16)
        for blk in range(nblk):
            iq_ref[blk, j // 2, (j % 2) * TQ:(j % 2 + 1) * TQ, :] = ir[blk * TQ:(blk + 1) * TQ, :]

    for n in range(B_KV_HEADS):
        kh = kv[:, n * B_HEAD_DIM:(n + 1) * B_HEAD_DIM]
        r = lax.rsqrt(jnp.mean(kh * kh, axis=-1, keepdims=True) + EPS)
        kn = kh * r * gk_ref[...]
        k_ref[0, n] = _rope_swapped(kn, cos_b, sin_b).astype(BF16)
        vt = kv[:, (B_KV_HEADS + n) * B_HEAD_DIM:(B_KV_HEADS + n + 1) * B_HEAD_DIM].T
        for c in range(vt.shape[1] // CB):
            vt_ref[0, n, c] = vt[:, c * CB:(c + 1) * CB].astype(BF16)

    lane = lax.broadcasted_iota(jnp.int32, ikp.shape, 1)
    live = lane < IDX_DIM
    mu = jnp.sum(jnp.where(live, ikp, 0.0), axis=-1, keepdims=True) * (1.0 / IDX_DIM)
    d = jnp.where(live, ikp - mu, 0.0)
    var = jnp.sum(d * d, axis=-1, keepdims=True) * (1.0 / IDX_DIM)
    y = d * lax.rsqrt(var + EPS) * lng_ref[...] + lnb_ref[...]
    half = LANES // 2
    x2_a = jnp.logical_and(lane >= half, lane < half + hi)
    keep = jnp.logical_or(lane < hi, jnp.logical_and(lane >= 2 * hi, lane < half))
    lo = _rope_swapped(jnp.where(x2_a, pltpu.roll(y, half - hi, 1), jnp.where(keep, y, 0.0)), cos_i, sin_i)
    x_b = jnp.logical_or(jnp.logical_and(lane >= hi, lane < 2 * hi),
                         jnp.logical_and(lane >= half + hi, lane < half + 2 * hi))
    hi_slots = jnp.where(x_b, pltpu.roll(lo, hi, 1), jnp.where(lane >= half + 2 * hi, pltpu.roll(lo, half, 1), 0.0))
    iklo_ref[0] = lo.astype(BF16)
    ikhi_ref[0] = hi_slots.astype(BF16)

    iw = ikp * (IDX_DIM ** -0.5 * IDX_HEADS ** -0.5)
    iwt_ref[0] = iw.T[IDX_DIM:IDX_DIM + IDX_HEADS, :]


def _kvprep(x3, gain, w_t, pos3, freqs, gk, gq, lng, lnb):
    bsz, s, _ = x3.shape
    tp = min(KVPREP_TP, s)
    per_b, nblk = s // tp, tp // TQ
    row = lambda b, i: (b, i, 0)
    const = lambda b, i: (0, 0)
    tab_spec = pl.BlockSpec((1, LANES), const)
    wrows = lambda rows, start: pl.BlockSpec((rows, D_MODEL), lambda b, i: (start // rows, 0),
                                             pipeline_mode=pl.Buffered(1))
    out_h = pl.BlockSpec((tp, D_MODEL), lambda b, i: (b * per_b + i, 0))
    out_q = pl.BlockSpec((nblk, B_HEADS, TQ, B_HEAD_DIM), lambda b, i: (b * per_b + i, 0, 0, 0))
    out_iq = pl.BlockSpec((nblk, IDX_HEADS // 4, 2 * TQ, LANES), lambda b, i: (b * per_b + i, 0, 0, 0))
    out_tok = pl.BlockSpec((1, tp, LANES), row)
    out_k = pl.BlockSpec((1, B_KV_HEADS, tp, B_HEAD_DIM), lambda b, i: (b, 0, i, 0))
    out_vt = pl.BlockSpec((1, B_KV_HEADS, tp // CB, B_HEAD_DIM, CB), lambda b, i: (b, 0, i, 0, 0))
    out_iwt = pl.BlockSpec((1, IDX_HEADS, tp), lambda b, i: (b, 0, i))
    nq = bsz * s // TQ
    return pl.pallas_call(
        _kvprep_kernel,
        grid=(bsz, per_b),
        in_specs=[pl.BlockSpec((None, tp, D_MODEL), row), pl.BlockSpec((1, D_MODEL), const),
                  wrows(SMALL_KV_WIDTH, SMALL_KV_START), wrows(LANES, SMALL_IDX_START),
                  wrows(BLK, Q_START), wrows(HALF_BLK, IQ_START), wrows(HALF_BLK, IQ_START + HALF_BLK),
                  pl.BlockSpec((1, tp, 1), row),
                  tab_spec, tab_spec, tab_spec, tab_spec, tab_spec],
        out_specs=[out_h, out_q, out_iq, out_k, out_vt, out_tok, out_tok, out_iwt],
        out_shape=[
            jax.ShapeDtypeStruct((bsz * s, D_MODEL), BF16),
            jax.ShapeDtypeStruct((nq, B_HEADS, TQ, B_HEAD_DIM), BF16),
            jax.ShapeDtypeStruct((nq, IDX_HEADS // 4, 2 * TQ, LANES), BF16),
            jax.ShapeDtypeStruct((bsz, B_KV_HEADS, s, B_HEAD_DIM), BF16),
            jax.ShapeDtypeStruct((bsz, B_KV_HEADS, s // CB, B_HEAD_DIM, CB), BF16),
            jax.ShapeDtypeStruct((bsz, s, LANES), BF16),
            jax.ShapeDtypeStruct((bsz, s, LANES), BF16),
            jax.ShapeDtypeStruct((bsz, IDX_HEADS, s), F32),
        ],
        scratch_shapes=[pltpu.VMEM((SMALL_KV_WIDTH, D_MODEL), BF16), pltpu.VMEM((LANES, D_MODEL), BF16),
                        pltpu.VMEM((BLK, D_MODEL), BF16), pltpu.VMEM((BLK, D_MODEL), BF16)],
        compiler_params=_cparams(("arbitrary", "arbitrary")),
        name="kvprep",
    )(x3, gain, w_t, w_t, w_t, w_t, w_t, pos3, freqs, gk, gq, lng, lnb)


def _gelu_tanh(x):
    c = -2.0 * (2.0 / np.pi) ** 0.5 * LOG2E
    return x / (1.0 + jnp.exp2(x * (x * x * (0.044715 * c) + c)))


def _gmlp_kernel(u_ref, v_ref, z_ref, lng_ref, lnb_ref, ws_ref, sbt_ref, o_ref):
    tm = u_ref.shape[0]
    u = _gelu_tanh(u_ref[...].astype(F32))
    v = _gelu_tanh(v_ref[...].astype(F32))
    mu = jnp.mean(v, axis=-1, keepdims=True)
    d = v - mu
    var = jnp.mean(d * d, axis=-1, keepdims=True)
    vn = (d * lax.rsqrt(var + EPS) * lng_ref[...] + lnb_ref[...]).astype(BF16)
    gate = u * jax.nn.silu(z_ref[...].astype(F32))
    tri = (lax.broadcasted_iota(jnp.int32, (CHUNK, CHUNK), 1)
           <= lax.broadcasted_iota(jnp.int32, (CHUNK, CHUNK), 0))
    for g in range(A_GROUPS):
        wg = jnp.where(tri, ws_ref[g], 0.0).astype(BF16)
        bias = sbt_ref[:, g:g + 1]
        cols = slice(g * A_GROUP_DIM, (g + 1) * A_GROUP_DIM)
        for c in range(tm // CHUNK):
            rows = slice(c * CHUNK, (c + 1) * CHUNK)
            sg = jnp.dot(wg, vn[rows, cols], preferred_element_type=F32) + bias
            o_ref[rows, cols] = (gate[rows, cols] * sg).astype(BF16)


def _gmlp(proj, lng, lnb, ws, sbt):
    m = proj.shape[0]
    tm = min(512, m)
    col = lambda c: pl.BlockSpec((tm, BLK), lambda i: (i, c))
    full = lambda shape: pl.BlockSpec(shape, lambda i: (0,) * len(shape))
    return pl.pallas_call(
        _gmlp_kernel,
        grid=(m // tm,),
        in_specs=[col(COL_AU), col(COL_AV), col(COL_AZ), full((1, A_WIDTH)), full((1, A_WIDTH)),
                  full((A_GROUPS, CHUNK, CHUNK)), full((CHUNK, A_GROUPS))],
        out_specs=pl.BlockSpec((tm, A_WIDTH), lambda i: (i, 0)),
        out_shape=jax.ShapeDtypeStruct((m, A_WIDTH), BF16),
        compiler_params=_cparams(("parallel",)),
        name="gmlp",
    )(proj, proj, proj, lng, lnb, ws, sbt)


def _memkv_kernel(mem_ref, g_ref, w_ref, gk_ref, km_ref, vm_ref, wbf_ref):
    @pl.when(pl.program_id(0) == 0)
    def _():
        wbf_ref[...] = w_ref[...].astype(BF16)

    h = _rms_rows(mem_ref[0], g_ref[...])
    kv = jnp.dot(h, wbf_ref[...], preferred_element_type=F32)
    for hd in range(M_HEADS):
        kh = kv[:, hd * M_HEAD_DIM:(hd + 1) * M_HEAD_DIM]
        r = lax.rsqrt(jnp.mean(kh * kh, axis=-1, keepdims=True) + EPS)
        km_ref[0, hd] = (kh * r * gk_ref[...]).astype(BF16)
        vm_ref[0, hd] = kv[:, M_WIDTH + hd * M_HEAD_DIM:M_WIDTH + (hd + 1) * M_HEAD_DIM].astype(BF16)


def _memkv(mem, gain, w_kv, gk):
    bsz, ml, _ = mem.shape
    out = pl.BlockSpec((1, M_HEADS, ml, M_HEAD_DIM), lambda b: (b, 0, 0, 0))
    shp = jax.ShapeDtypeStruct((bsz, M_HEADS, ml, M_HEAD_DIM), BF16)
    return pl.pallas_call(
        _memkv_kernel,
        grid=(bsz,),
        in_specs=[pl.BlockSpec((1, ml, D_MODEL), lambda b: (b, 0, 0)),
                  pl.BlockSpec((1, D_MODEL), lambda b: (0, 0)),
                  pl.BlockSpec((D_MODEL, 2 * M_WIDTH), lambda b: (0, 0), pipeline_mode=pl.Buffered(1)),
                  pl.BlockSpec((1, M_HEAD_DIM), lambda b: (0, 0))],
        out_specs=[out, out],
        out_shape=[shp, shp],
        scratch_shapes=[pltpu.VMEM((D_MODEL, 2 * M_WIDTH), BF16)],
        compiler_params=_cparams(("arbitrary",)),
        name="memkv",
    )(mem, gain, w_kv, gk)


def _memattn_kernel(q_ref, z_ref, km_ref, vm_ref, gq_ref, o_ref):
    qscale = M_HEAD_DIM ** -0.5 * LOG2E
    for hd in range(M_HEADS):
        cols = slice(hd * M_HEAD_DIM, (hd + 1) * M_HEAD_DIM)
        q = q_ref[:, cols].astype(F32)
        r = lax.rsqrt(jnp.mean(q * q, axis=-1, keepdims=True) + EPS)
        qn = (q * r * gq_ref[...] * qscale).astype(BF16)
        lg = _dot_nt(qn, km_ref[0, hd])
        p = jnp.exp2(lg - jnp.max(lg, axis=-1, keepdims=True))
        l = jnp.sum(p, axis=-1, keepdims=True)
        o = jnp.dot(p.astype(BF16), vm_ref[0, hd], preferred_element_type=F32) / l
        o_ref[:, cols] = (o * jax.nn.silu(z_ref[:, cols].astype(F32))).astype(BF16)


def _memattn(proj, km, vm, gq, s):
    m = proj.shape[0]
    tm = min(512, s)
    per_b = s // tm
    ml = km.shape[2]
    kv_spec = pl.BlockSpec((1, M_HEADS, ml, M_HEAD_DIM), lambda i: (i // per_b, 0, 0, 0))
    return pl.pallas_call(
        _memattn_kernel,
        grid=(m // tm,),
        in_specs=[pl.BlockSpec((tm, BLK), lambda i: (i, COL_MQ)),
                  pl.BlockSpec((tm, BLK), lambda i: (i, COL_MZ)),
                  kv_spec, kv_spec,
                  pl.BlockSpec((1, M_HEAD_DIM), lambda i: (0, 0))],
        out_specs=pl.BlockSpec((tm, M_WIDTH), lambda i: (i, 0)),
        out_shape=jax.ShapeDtypeStruct((m, M_WIDTH), BF16),
        compiler_params=_cparams(("parallel",)),
        name="memattn",
    )(proj, proj, km, vm, gq)


def _dsa_kernel(topk, q_ref, z_ref, iq_ref, iwt_ref,
                k_ref, vt_ref, iklo_ref, ikhi_ref, tri_ref, bound_ref, o_ref,
                sc_ref, bias_ref, lg_ref, acc_ref):
    qb = pl.program_id(1)
    nck = (qb * TQ + TQ + CK - 1) // CK
    nbig = (qb * TQ + TQ + CB - 1) // CB
    t_lane = qb * TQ + lax.broadcasted_iota(jnp.int32, (1, TQ), 1)
    gw = B_GROUP * TQ

    def fold8(a):
        return a.reshape(a.shape[0] // 8, 8, a.shape[1])

    def grouped_loop(trips, body, init, per_step=2, start=0):
        carry, done = init, start
        while per_step >= 1:
            def step(i, c, first=done, n=per_step):
                for u in range(n):
                    c = body(first + n * i + u, c)
                return c

            steps = (trips - done) // per_step
            carry = lax.fori_loop(0, steps, step, carry)
            done = done + steps * per_step
            per_step //= 2
        return carry

    wt = iwt_ref[0]

    def idx_body(c, carry):
        mn8, mx8, s1, s2 = carry
        off = pl.multiple_of(c * CK, CK)
        acc = jnp.zeros((CK, TQ), F32)
        for jj in range(IDX_HEADS // 4):
            rhs = iq_ref[0, jj]
            for half, keys_ref in enumerate((iklo_ref, ikhi_ref)):
                d = _dot_nt(keys_ref[0, pl.ds(off, CK), :], rhs)
                ha, hb = 4 * jj + half, 4 * jj + 2 + half
                acc = (acc + jnp.maximum(d[:, :TQ], 0.0) * wt[ha:ha + 1, :]
                       + jnp.maximum(d[:, TQ:], 0.0) * wt[hb:hb + 1, :])
        key = off + lax.broadcasted_iota(jnp.int32, (CK, TQ), 0)
        causal = key <= t_lane
        sc = jnp.where(causal, acc, -jnp.inf)
        sc_ref[pl.ds(off, CK), :] = sc
        live = jnp.where(causal, acc, 0.0)
        mn8 = jnp.minimum(mn8, jnp.min(fold8(jnp.where(causal, acc, jnp.inf)), axis=0))
        mx8 = jnp.maximum(mx8, jnp.max(fold8(sc), axis=0))
        s1 = s1 + jnp.sum(fold8(live), axis=0)
        s2 = s2 + jnp.sum(fold8(live * live), axis=0)
        return mn8, mx8, s1, s2

    zero8 = jnp.zeros((8, TQ), F32)
    stats = (jnp.full((8, TQ), jnp.inf, F32), jnp.full((8, TQ), -jnp.inf, F32), zero8, zero8)
    mn8, mx8, s1, s2 = grouped_loop(nck, idx_body, stats, per_step=4)
    row_min = jnp.min(mn8, axis=0, keepdims=True)
    row_max = jnp.max(mx8, axis=0, keepdims=True)

    def fill_body(c, carry):
        sc_ref[pl.ds(pl.multiple_of(c * CK, CK), CK), :] = jnp.full((CK, TQ), -jnp.inf, F32)
        return carry

    lax.fori_loop(nck, nbig * (CB // CK), fill_body, 0)

    def count_rows(pred):
        rows = 8 * COUNT_WAYS

        def body(c, cnt):
            off = pl.multiple_of(c * CB, CB)
            for r in range(CB // rows):
                cnt = cnt + jnp.where(pred(sc_ref[pl.ds(off + r * rows, rows), :]), 1.0, 0.0)
            return cnt

        cnt = lax.fori_loop(0, nbig, body, jnp.zeros((rows, TQ), F32))
        return jnp.sum(cnt, axis=0, keepdims=True)

    kf = float(topk)
    n_valid = (t_lane + 1).astype(F32)
    all_rows = (t_lane + 1) <= topk

    def search_pass(st, probe, stuck):
        lo, hi, clo, chi, thr, fin, tie = st
        cnt = count_rows(lambda blk: blk >= probe)
        active = fin < 0.5
        hit = cnt == kf
        end_thr = jnp.where(jnp.logical_and(stuck, cnt < kf), lo, probe)
        ends = jnp.logical_and(active, jnp.logical_or(hit, stuck))
        thr = jnp.where(ends, end_thr, thr)
        tie = jnp.where(jnp.logical_and(ends, jnp.logical_not(hit)), 1.0, tie)
        fin = jnp.where(ends, 1.0, fin)
        up = jnp.logical_and(active, cnt > kf)
        dn = jnp.logical_and(active, cnt < kf)
        return (jnp.where(up, probe, lo), jnp.where(dn, probe, hi), jnp.where(up, cnt, clo),
                jnp.where(dn, cnt, chi), thr, fin, tie)

    def next_probe(st, halve=False):
        lo, hi, clo, chi = st[:4]
        frac = jnp.clip((clo - kf + 0.5) / (clo - chi + 1.0), SEARCH_MARGIN, 1.0 - SEARCH_MARGIN)
        probe = lo + (hi - lo) * frac
        outside = jnp.logical_or(jnp.logical_or(probe <= lo, probe >= hi), halve)
        probe = jnp.where(outside, lo * 0.5 + hi * 0.5, probe)
        stuck = jnp.logical_or(probe <= lo, probe >= hi)
        return jnp.where(stuck, hi, probe), stuck

    mean = jnp.sum(s1, axis=0, keepdims=True) / n_valid
    var = jnp.maximum(jnp.sum(s2, axis=0, keepdims=True) / n_valid - mean * mean, 0.0)
    tail = jnp.clip(kf / n_valid, 1e-6, 1.0 - 1e-6)
    tq = jnp.sqrt(-2.0 * jnp.log(jnp.minimum(tail, 1.0 - tail)))
    zq = tq - ((0.010328 * tq + 0.802853) * tq + 2.515517) / (
        ((0.001308 * tq + 0.189269) * tq + 1.432788) * tq + 1.0)
    guess = mean + jnp.where(tail < 0.5, zq, -zq) * jnp.sqrt(var)
    inside = jnp.logical_and(guess >= row_min, guess <= row_max)
    probe0 = jnp.where(inside, guess, row_min * 0.5 + row_max * 0.5)

    ones = jnp.ones((1, TQ), F32)
    st = (row_min, row_max, n_valid, 0.0 * ones, jnp.where(all_rows, -F32_MAX, row_max),
          jnp.where(all_rows, 1.0, 0.0), 0.0 * ones)
    st = search_pass(st, probe0, probe0 < row_min)

    def fixed_body(i, st):
        return search_pass(st, *next_probe(st))

    st = lax.fori_loop(0, SEARCH_FIXED_PASSES, fixed_body, st)

    def more_cond(c):
        return jnp.logical_and(c[0] < MAX_SEARCH_ITERS, jnp.min(c[1][5]) < 0.5)

    def more_body(c):
        return c[0] + 1, search_pass(c[1], *next_probe(c[1], c[0] >= SEARCH_INTERP_PASSES))

    _, st = lax.while_loop(more_cond, more_body, (jnp.int32(0), st))
    thr, tie = st[4], st[6]

    def plain_mask():
        def body(c, carry):
            off = pl.multiple_of(c * CB, CB)
            bias_ref[pl.ds(off, CB), :] = jnp.where(sc_ref[pl.ds(off, CB), :] >= thr, 0.0, NEG_BIG)
            return carry

        lax.fori_loop(0, nbig, body, 0)

    def tied_mask():
        n_gt = count_rows(lambda blk: blk > thr)
        need = jnp.where(all_rows, F32_MAX, kf - n_gt)

        def body(c, seen):
            off = pl.multiple_of(c * CB, CB)
            blk = sc_ref[pl.ds(off, CB), :]
            eq = blk == thr
            pref = jnp.dot(tri_ref[...], jnp.where(eq, 1.0, 0.0).astype(BF16),
                           preferred_element_type=F32) + seen
            keep = jnp.logical_or(blk > thr, jnp.logical_and(eq, pref <= need))
            bias_ref[pl.ds(off, CB), :] = jnp.where(keep, 0.0, NEG_BIG)
            return pref[CB - 1:CB, :]

        lax.fori_loop(0, nbig, body, jnp.zeros((1, TQ), F32))

    lax.cond(jnp.max(tie) > 0.5, tied_mask, plain_mask)

    logit_bound = bound_ref[0, 0]

    def masked_logits(n, off):
        b = bias_ref[pl.ds(off, CB), :]
        qn = q_ref[0, n * B_GROUP:(n + 1) * B_GROUP].reshape(gw, B_HEAD_DIM)
        lg = _dot_nt(k_ref[0, n, pl.ds(off, CB), :], qn)
        return jnp.concatenate([lg[:, g * TQ:(g + 1) * TQ] + b for g in range(B_GROUP)], axis=1)

    for n in range(B_KV_HEADS):
        acc_ref[...] = jnp.zeros(acc_ref.shape, F32)

        def one_pass(n=n):
            def body(c, l8):
                off = pl.multiple_of(c * CB, CB)
                p = jnp.exp2(masked_logits(n, off) - logit_bound)
                acc_ref[...] += jnp.dot(vt_ref[0, n, c], p.astype(BF16), preferred_element_type=F32)
                return l8 + jnp.sum(fold8(p), axis=0)

            return grouped_loop(nbig, body, jnp.zeros((8, gw), F32), per_step=4)

        def two_pass(n=n):
            def logit_body(c, mx8):
                off = pl.multiple_of(c * CB, CB)
                lg = masked_logits(n, off)
                lg_ref[pl.ds(off, CB), :] = lg
                return jnp.maximum(mx8, jnp.max(fold8(lg), axis=0))

            mx8 = grouped_loop(nbig, logit_body, jnp.full((8, gw), NEG_BIG, F32))
            m = jnp.max(mx8, axis=0, keepdims=True)

            def pv_body(c, l8):
                off = pl.multiple_of(c * CB, CB)
                p = jnp.exp2(lg_ref[pl.ds(off, CB), :] - m)
                acc_ref[...] += jnp.dot(vt_ref[0, n, c], p.astype(BF16), preferred_element_type=F32)
                return l8 + jnp.sum(fold8(p), axis=0)

            return grouped_loop(nbig, pv_body, jnp.zeros((8, gw), F32))

        l8 = lax.cond(logit_bound <= MAX_SHIFT_BOUND, one_pass, two_pass)
        o_t = acc_ref[...] / jnp.sum(l8, axis=0, keepdims=True)
        for g in range(B_GROUP):
            cols = slice((n * B_GROUP + g) * B_HEAD_DIM, (n * B_GROUP + g + 1) * B_HEAD_DIM)
            o = o_t[:, g * TQ:(g + 1) * TQ].T
            o_ref[:, cols] = (o * jax.nn.silu(z_ref[:, cols].astype(F32))).astype(BF16)


def _dsa(proj, q, iq, iwt, k, vt, iklo, ikhi, gq, gk, bsz, s):
    nq = s // TQ
    bound = (1.02 * B_HEAD_DIM ** 0.5 * LOG2E * jnp.max(jnp.abs(gq)) * jnp.max(jnp.abs(gk))).reshape(1, 1)
    topk = min(TOPK_MAX, s // 4)
    gw = B_GROUP * TQ
    blk4 = lambda a: pl.BlockSpec((1,) + a.shape[1:], lambda b, i: (b * nq + i, 0, 0, 0))
    ik = pl.BlockSpec((1, s, LANES), lambda b, i: (b, 0, 0))
    tri = jnp.tril(jnp.ones((CB, CB), BF16))
    return pl.pallas_call(
        functools.partial(_dsa_kernel, topk),
        grid=(bsz, nq),
        in_specs=[blk4(q), pl.BlockSpec((TQ, BLK), lambda b, i: (b * nq + i, COL_BZ)), blk4(iq),
                  pl.BlockSpec((1, IDX_HEADS, TQ), lambda b, i: (b, 0, i)),
                  pl.BlockSpec((1, B_KV_HEADS, s, B_HEAD_DIM), lambda b, i: (b, 0, 0, 0)),
                  pl.BlockSpec((1, B_KV_HEADS, s // CB, B_HEAD_DIM, CB), lambda b, i: (b, 0, 0, 0, 0)),
                  ik, ik,
                  pl.BlockSpec((CB, CB), lambda b, i: (0, 0)),
                  pl.BlockSpec(memory_space=pltpu.SMEM)],
        out_specs=pl.BlockSpec((TQ, B_WIDTH), lambda b, i: (b * nq + i, 0)),
        out_shape=jax.ShapeDtypeStruct((bsz * s, B_WIDTH), BF16),
        scratch_shapes=[
            pltpu.VMEM((s, TQ), F32),
            pltpu.VMEM((s, TQ), F32),
            pltpu.VMEM((s, gw), F32),
            pltpu.VMEM((B_HEAD_DIM, gw), F32),
        ],
        compiler_params=_cparams(("parallel", "arbitrary")),
        name="dsa",
    )(q, proj, iq, iwt, k, vt, iklo, ikhi, tri, bound)


def _merge_kernel(ta_ref, tb_ref, tm_ref, ga_ref, gb_ref, gm_ref, x_ref,
                  wa_hbm, wb_hbm, wm_hbm, wo_hbm, o_ref,
                  wa_ref, wb_ref, wm_ref, wo_ref, stage_ref, sem):
    @pl.when(pl.program_id(0) == 0)
    def _():
        chunks = [(src, dst, r) for src, dst in ((wa_hbm, wa_ref), (wb_hbm, wb_ref), (wm_hbm, wm_ref),
                                                 (wo_hbm, wo_ref))
                  for r in range(0, src.shape[0], MERGE_STAGE_ROWS)]

        def copy(j):
            src, _, r = chunks[j]
            return pltpu.make_async_copy(src.at[pl.ds(r, MERGE_STAGE_ROWS), :], stage_ref.at[j % 2],
                                         sem.at[j % 2])

        copy(0).start()
        for j, (_, dst, r) in enumerate(chunks):
            if j + 1 < len(chunks):
                copy(j + 1).start()
            copy(j).wait()
            dst[pl.ds(r, MERGE_STAGE_ROWS), :] = stage_ref[j % 2].astype(BF16)

    def branch(t_ref, g_ref, w_ref):
        y = jnp.dot(t_ref[...], w_ref[...], preferred_element_type=F32)
        return jax.nn.sigmoid(g_ref[...].astype(F32)) * y

    merged = branch(ta_ref, ga_ref, wa_ref) + branch(tb_ref, gb_ref, wb_ref) + branch(tm_ref, gm_ref, wm_ref)
    o_ref[...] = x_ref[...] + jnp.dot(merged.astype(BF16), wo_ref[...], preferred_element_type=F32)


def _merge(ta, tb, tmem, proj, x2, wa, wb, wm, wo):
    m = x2.shape[0]
    tm = min(256, m)
    act = pl.BlockSpec((tm, BLK), lambda i: (i, 0))
    gate = lambda c: pl.BlockSpec((tm, D_MODEL), lambda i: (i, c // 2))
    wide = pl.BlockSpec((tm, D_MODEL), lambda i: (i, 0))
    hbm = pl.BlockSpec(memory_space=pl.ANY)
    return pl.pallas_call(
        _merge_kernel,
        grid=(m // tm,),
        in_specs=[act, act, act, gate(COL_GA), gate(COL_GB), gate(COL_GM), wide, hbm, hbm, hbm, hbm],
        out_specs=wide,
        out_shape=jax.ShapeDtypeStruct((m, D_MODEL), F32),
        scratch_shapes=[pltpu.VMEM(w.shape, BF16) for w in (wa, wb, wm, wo)]
        + [pltpu.VMEM((2, MERGE_STAGE_ROWS, D_MODEL), F32), pltpu.SemaphoreType.DMA((2,))],
        compiler_params=_cparams(("arbitrary",)),
        name="merge",
    )(ta, tb, tmem, proj, proj, proj, x2, wa, wb, wm, wo)


def _rope_freqs():
    inv = lambda half: ROPE_THETA ** (-np.arange(half, dtype=np.float32) / half)
    freq = np.zeros((1, LANES), np.float32)
    hb, hi = B_HEAD_DIM // 8, IDX_DIM // 8
    freq[0, :hb] = inv(hb)
    freq[0, hb:hb + hi] = inv(hi)
    return jnp.asarray(freq)


def _layer(x, mem, positions, norm_gain, w_in, gmlp_ln_gain, gmlp_ln_bias, spatial_w, spatial_b,
           w_branch_a, q_norm_gain, k_norm_gain, idx_k_ln_gain, idx_k_ln_bias, w_branch_b,
           mem_norm_gain, w_mem_kv, mem_q_norm_gain, mem_k_norm_gain, w_branch_m, w_out):
    bsz, s, _ = x.shape
    m = bsz * s
    row = lambda a: a.reshape(1, -1).astype(F32)
    pad_lanes = lambda a: jnp.pad(a.reshape(1, -1).astype(F32), ((0, 0), (0, LANES - a.shape[-1])))
    x2 = x.reshape(m, D_MODEL)

    w_t = w_in.T
    h, q, iq, k, vt, iklo, ikhi, iwt = _kvprep(
        x, row(norm_gain), w_t, positions.reshape(bsz, s, 1).astype(jnp.int32), _rope_freqs(),
        row(k_norm_gain[_HEAD_PERM]), row(q_norm_gain[_HEAD_PERM]),
        pad_lanes(idx_k_ln_gain), pad_lanes(idx_k_ln_bias))
    proj = _inproj(h, w_t)

    t_a = _gmlp(proj, row(gmlp_ln_gain), row(gmlp_ln_bias), spatial_w.astype(F32), spatial_b.T.astype(F32))
    km, vm = _memkv(mem, row(mem_norm_gain), w_mem_kv, row(mem_k_norm_gain))
    t_m = _memattn(proj, km, vm, row(mem_q_norm_gain), s)
    t_b = _dsa(proj, q, iq, iwt, k, vt, iklo, ikhi, row(q_norm_gain), row(k_norm_gain), bsz, s)

    out = _merge(t_a, t_b, t_m, proj, x2, w_branch_a, w_branch_b, w_branch_m, w_out)
    return out.reshape(bsz, s, D_MODEL)


def kernel(x, mem, positions, norm_gain, w_in, gmlp_ln_gain, gmlp_ln_bias, spatial_w, spatial_b, w_branch_a, q_norm_gain, k_norm_gain, idx_k_ln_gain, idx_k_ln_bias, w_branch_b, mem_norm_gain, w_mem_kv, mem_q_norm_gain, mem_k_norm_gain, w_branch_m, w_out):
    for l in range(norm_gain.shape[0]):
        x = _layer(x, mem, positions, norm_gain[l], w_in[l], gmlp_ln_gain[l], gmlp_ln_bias[l],
                   spatial_w[l], spatial_b[l], w_branch_a[l], q_norm_gain[l], k_norm_gain[l],
                   idx_k_ln_gain[l], idx_k_ln_bias[l], w_branch_b[l], mem_norm_gain[l], w_mem_kv[l],
                   mem_q_norm_gain[l], mem_k_norm_gain[l], w_branch_m[l], w_out[l])
    return x
```

```python
import functools

import numpy as np
import jax
import jax.numpy as jnp
from jax import lax
from jax.experimental import pallas as pl
from jax.experimental.pallas import tpu as pltpu

F32 = jnp.float32
BF16 = jnp.bfloat16

D_MODEL = 2048
ROPE_THETA = 500000.0
EPS = 1e-6
A_GROUPS = 8
A_GROUP_DIM = 128
A_WIDTH = A_GROUPS * A_GROUP_DIM
CHUNK = 128
B_HEADS = 8
B_KV_HEADS = 2
B_GROUP = B_HEADS // B_KV_HEADS
B_HEAD_DIM = 128
B_WIDTH = B_HEADS * B_HEAD_DIM
IDX_HEADS = 16
IDX_DIM = 64
TOPK_MAX = 256
M_HEADS = 4
M_HEAD_DIM = 256
M_WIDTH = M_HEADS * M_HEAD_DIM

SPLIT_SIZES = (
    A_WIDTH, A_WIDTH, A_WIDTH,
    B_WIDTH, B_KV_HEADS * B_HEAD_DIM, B_KV_HEADS * B_HEAD_DIM, B_WIDTH,
    IDX_HEADS * IDX_DIM, IDX_DIM, IDX_HEADS,
    M_WIDTH, M_WIDTH,
    D_MODEL, D_MODEL, D_MODEL,
)

LANES = 128
BLK = 1024
COL_AU, COL_AV, COL_AZ, COL_BZ, COL_MQ, COL_MZ = range(6)
COL_GA, COL_GB, COL_GM = 6, 8, 10
NB_MAIN = 12
_OFFS = [int(o) for o in np.concatenate([[0], np.cumsum(SPLIT_SIZES)])]
ROW_ALIGN = 16
MAIN_START = ([_OFFS[i] for i in (0, 1, 2, 6, 10, 11)]
              + [_OFFS[i] + d for i in (12, 13, 14) for d in (0, BLK)])
assert all(s % ROW_ALIGN == 0 for s in MAIN_START)
Q_START, IQ_START, HALF_BLK = _OFFS[3], _OFFS[7], BLK // 2
SMALL_KV_START, SMALL_KV_WIDTH = _OFFS[4], _OFFS[6] - _OFFS[4]
SMALL_IDX_START = _OFFS[8]
assert Q_START % BLK == 0 and IQ_START % HALF_BLK == 0
assert SMALL_KV_START % SMALL_KV_WIDTH == 0 and SMALL_IDX_START % LANES == 0
assert IDX_DIM + IDX_HEADS <= LANES and _OFFS[9] == SMALL_IDX_START + IDX_DIM

VMEM_LIMIT = 56 * 1024 * 1024
LOG2E = 1.4426950408889634
NEG_BIG = -1e30
F32_MAX = 3.4028234663852886e38

INPROJ_TM, INPROJ_TN = 2048, 1024
KVPREP_TP = 512
TQ = 256
CK = 256
CB = 512
COUNT_WAYS = 8
SEARCH_MARGIN = 0.02
SEARCH_FIXED_PASSES = 11
SEARCH_INTERP_PASSES = 13
MAX_SEARCH_ITERS = 400
MERGE_STAGE_ROWS = 512
MAX_SHIFT_BOUND = 48.0


def _cparams(sem):
    return pltpu.CompilerParams(dimension_semantics=sem, vmem_limit_bytes=VMEM_LIMIT)


def _dot_nt(a, b):
    return lax.dot_general(a, b, (((1,), (1,)), ((), ())), preferred_element_type=F32)


_HEAD_PERM = np.concatenate([np.arange(0, 16), np.arange(32, 80), np.arange(16, 32), np.arange(80, 128)])
assert B_HEAD_DIM == LANES and B_HEAD_DIM // 8 == 16
_PAIR_PERM = ((0, 8), (64, 72), (16, 64), (8, 16), (72, 80), (80, 128))
assert 2 * IDX_DIM == LANES and IDX_DIM // 8 == 8


def _rope_swapped(x, cos_t, sin_t):
    return x * cos_t + pltpu.roll(x, LANES // 2, 1) * sin_t


def _permute_head_rows(src_ref, dst_ref, hd):
    hb, mid = B_HEAD_DIM // 8, B_HEAD_DIM // 2
    base = hd * B_HEAD_DIM
    for src, dst, n in ((0, 0, hb), (2 * hb, hb, mid - hb), (hb, mid, hb), (mid + hb, mid + hb, mid - hb)):
        dst_ref[base + dst:base + dst + n, :] = src_ref[base + src:base + src + n, :].astype(BF16)


def _rms_rows(x, gain):
    ms = jnp.mean(x * x, axis=-1, keepdims=True)
    return (x * lax.rsqrt(ms + EPS) * gain).astype(BF16)


def _inproj_kernel(tab_ref, h_ref, wt_ref, proj_ref):
    del tab_ref
    proj_ref[...] = _dot_nt(h_ref[...], wt_ref[...].astype(BF16)).astype(BF16)


def _inproj(h, w_t):
    m = h.shape[0]
    tm = min(INPROJ_TM, m)
    starts = [s + d for s in MAIN_START for d in range(0, BLK, INPROJ_TN)]
    tab = jnp.asarray(np.array([s // ROW_ALIGN for s in starts], np.int32))
    return pl.pallas_call(
        _inproj_kernel,
        grid_spec=pltpu.PrefetchScalarGridSpec(
            num_scalar_prefetch=1,
            grid=(m // tm, len(starts)),
            in_specs=[
                pl.BlockSpec((tm, D_MODEL), lambda i, n, tab: (i, 0)),
                pl.BlockSpec((pl.Element(INPROJ_TN), pl.Element(D_MODEL)),
                             lambda i, n, tab: (tab[n] * ROW_ALIGN, 0)),
            ],
            out_specs=pl.BlockSpec((tm, INPROJ_TN), lambda i, n, tab: (i, n)),
        ),
        out_shape=jax.ShapeDtypeStruct((m, NB_MAIN * BLK), BF16),
        compiler_params=_cparams(("parallel", "arbitrary")),
        name="inproj",
    )(tab, h, w_t)


def _kvprep_kernel(x_ref, g_ref, wkv_ref, widx_ref, wq_ref, wiqa_ref, wiqb_ref, pos_ref, fc_ref,
                   gk_ref, gq_ref, lng_ref, lnb_ref,
                   h_ref, q_ref, iq_ref, k_ref, vt_ref, iklo_ref, ikhi_ref, iwt_ref,
                   wkv_bf_ref, widx_bf_ref, wq_bf_ref, wiq_bf_ref):
    @pl.when(jnp.logical_and(pl.program_id(0) == 0, pl.program_id(1) == 0))
    def _():
        wkv_bf_ref[...] = wkv_ref[...].astype(BF16)
        for hd in range(B_KV_HEADS):
            _permute_head_rows(wkv_ref, wkv_bf_ref, hd)
        widx_bf_ref[...] = widx_ref[...].astype(BF16)
        for hd in range(B_HEADS):
            _permute_head_rows(wq_ref, wq_bf_ref, hd)
        for j in range(IDX_HEADS // 2):
            src = (wiqa_ref, wiqb_ref)[j * LANES // HALF_BLK]
            base = j * LANES % HALF_BLK
            rows = jnp.concatenate([src[base + a:base + b, :] for a, b in _PAIR_PERM], axis=0)
            wiq_bf_ref[j * LANES:(j + 1) * LANES, :] = rows.astype(BF16)

    h = _rms_rows(x_ref[...], g_ref[...])
    h_ref[...] = h
    kv = _dot_nt(h, wkv_bf_ref[...])
    ikp = _dot_nt(h, widx_bf_ref[...])

    hb, hi = B_HEAD_DIM // 8, IDX_DIM // 8
    ang = pos_ref[0].astype(F32) * fc_ref[...]
    cos_c, sin_c = jnp.cos(ang), jnp.sin(ang)
    lane = lax.broadcasted_iota(jnp.int32, ang.shape, 1)
    second = jnp.logical_and(lane >= LANES // 2, lane < LANES // 2 + hb)
    cos_b = jnp.where(lane < hb, cos_c, jnp.where(second, pltpu.roll(cos_c, LANES // 2, 1), 1.0))
    sin_b = jnp.where(lane < hb, -sin_c, jnp.where(second, pltpu.roll(sin_c, LANES // 2, 1), 0.0))
    cos_i = jnp.ones_like(cos_c)
    sin_i = jnp.zeros_like(sin_c)
    for first, sign in ((0, -1.0), (hi, -1.0), (LANES // 2, 1.0), (LANES // 2 + hi, 1.0)):
        here = jnp.logical_and(lane >= first, lane < first + hi)
        shift = (first - hb) % LANES
        cos_i = jnp.where(here, pltpu.roll(cos_c, shift, 1), cos_i)
        sin_i = jnp.where(here, sign * pltpu.roll(sin_c, shift, 1), sin_i)
    nblk = q_ref.shape[0]
    gain = gq_ref[...] * (B_HEAD_DIM ** -0.5 * LOG2E)
    qp = _dot_nt(h, wq_bf_ref[...])
    for hd in range(B_HEADS):
        slab = qp[:, hd * B_HEAD_DIM:(hd + 1) * B_HEAD_DIM]
        r = lax.rsqrt(jnp.mean(slab * slab, axis=-1, keepdims=True) + EPS)
        qr = _rope_swapped(slab * r * gain, cos_b, sin_b).astype(BF16)
        for blk in range(nblk):
            q_ref[blk, hd] = qr[blk * TQ:(blk + 1) * TQ, :]
    iqp = _dot_nt(h, wiq_bf_ref[...])
    for j in range(IDX_HEADS // 2):
        ir = _rope_swapped(iqp[:, j * LANES:(j + 1) * LANES], cos_i, sin_i).astype(BF16)
        for blk in range(nblk):
            iq_ref[blk, j // 2, (j % 2) * TQ:(j % 2 + 1) * TQ, :] = ir[blk * TQ:(blk + 1) * TQ, :]

    for n in range(B_KV_HEADS):
        kh = kv[:, n * B_HEAD_DIM:(n + 1) * B_HEAD_DIM]
        r = lax.rsqrt(jnp.mean(kh * kh, axis=-1, keepdims=True) + EPS)
        kn = kh * r * gk_ref[...]
        k_ref[0, n] = _rope_swapped(kn, cos_b, sin_b).astype(BF16)
        vt = kv[:, (B_KV_HEADS + n) * B_HEAD_DIM:(B_KV_HEADS + n + 1) * B_HEAD_DIM].T
        for c in range(vt.shape[1] // CB):
            vt_ref[0, n, c] = vt[:, c * CB:(c + 1) * CB].astype(BF16)

    lane = lax.broadcasted_iota(jnp.int32, ikp.shape, 1)
    live = lane < IDX_DIM
    mu = jnp.sum(jnp.where(live, ikp, 0.0), axis=-1, keepdims=True) * (1.0 / IDX_DIM)
    d = jnp.where(live, ikp - mu, 0.0)
    var = jnp.sum(d * d, axis=-1, keepdims=True) * (1.0 / IDX_DIM)
    y = d * lax.rsqrt(var + EPS) * lng_ref[...] + lnb_ref[...]
    half = LANES // 2
    x2_a = jnp.logical_and(lane >= half, lane < half + hi)
    keep = jnp.logical_or(lane < hi, jnp.logical_and(lane >= 2 * hi, lane < half))
    lo = _rope_swapped(jnp.where(x2_a, pltpu.roll(y, half - hi, 1), jnp.where(keep, y, 0.0)), cos_i, sin_i)
    x_b = jnp.logical_or(jnp.logical_and(lane >= hi, lane < 2 * hi),
                         jnp.logical_and(lane >= half + hi, lane < half + 2 * hi))
    hi_slots = jnp.where(x_b, pltpu.roll(lo, hi, 1), jnp.where(lane >= half + 2 * hi, pltpu.roll(lo, half, 1), 0.0))
    iklo_ref[0] = lo.astype(BF16)
    ikhi_ref[0] = hi_slots.astype(BF16)

    iw = ikp * (IDX_DIM ** -0.5 * IDX_HEADS ** -0.5)
    iwt_ref[0] = iw.T[IDX_DIM:IDX_DIM + IDX_HEADS, :]


def _kvprep(x3, gain, w_t, pos3, freqs, gk, gq, lng, lnb):
    bsz, s, _ = x3.shape
    tp = min(KVPREP_TP, s)
    per_b, nblk = s // tp, tp // TQ
    row = lambda b, i: (b, i, 0)
    const = lambda b, i: (0, 0)
    tab_spec = pl.BlockSpec((1, LANES), const)
    wrows = lambda rows, start: pl.BlockSpec((rows, D_MODEL), lambda b, i: (start // rows, 0),
                                             pipeline_mode=pl.Buffered(1))
    out_h = pl.BlockSpec((tp, D_MODEL), lambda b, i: (b * per_b + i, 0))
    out_q = pl.BlockSpec((nblk, B_HEADS, TQ, B_HEAD_DIM), lambda b, i: (b * per_b + i, 0, 0, 0))
    out_iq = pl.BlockSpec((nblk, IDX_HEADS // 4, 2 * TQ, LANES), lambda b, i: (b * per_b + i, 0, 0, 0))
    out_tok = pl.BlockSpec((1, tp, LANES), row)
    out_k = pl.BlockSpec((1, B_KV_HEADS, tp, B_HEAD_DIM), lambda b, i: (b, 0, i, 0))
    out_vt = pl.BlockSpec((1, B_KV_HEADS, tp // CB, B_HEAD_DIM, CB), lambda b, i: (b, 0, i, 0, 0))
    out_iwt = pl.BlockSpec((1, IDX_HEADS, tp), lambda b, i: (b, 0, i))
    nq = bsz * s // TQ
    return pl.pallas_call(
        _kvprep_kernel,
        grid=(bsz, per_b),
        in_specs=[pl.BlockSpec((None, tp, D_MODEL), row), pl.BlockSpec((1, D_MODEL), const),
                  wrows(SMALL_KV_WIDTH, SMALL_KV_START), wrows(LANES, SMALL_IDX_START),
                  wrows(BLK, Q_START), wrows(HALF_BLK, IQ_START), wrows(HALF_BLK, IQ_START + HALF_BLK),
                  pl.BlockSpec((1, tp, 1), row),
                  tab_spec, tab_spec, tab_spec, tab_spec, tab_spec],
        out_specs=[out_h, out_q, out_iq, out_k, out_vt, out_tok, out_tok, out_iwt],
        out_shape=[
            jax.ShapeDtypeStruct((bsz * s, D_MODEL), BF16),
            jax.ShapeDtypeStruct((nq, B_HEADS, TQ, B_HEAD_DIM), BF16),
            jax.ShapeDtypeStruct((nq, IDX_HEADS // 4, 2 * TQ, LANES), BF16),
            jax.ShapeDtypeStruct((bsz, B_KV_HEADS, s, B_HEAD_DIM), BF16),
            jax.ShapeDtypeStruct((bsz, B_KV_HEADS, s // CB, B_HEAD_DIM, CB), BF16),
            jax.ShapeDtypeStruct((bsz, s, LANES), BF16),
            jax.ShapeDtypeStruct((bsz, s, LANES), BF16),
            jax.ShapeDtypeStruct((bsz, IDX_HEADS, s), F32),
        ],
        scratch_shapes=[pltpu.VMEM((SMALL_KV_WIDTH, D_MODEL), BF16), pltpu.VMEM((LANES, D_MODEL), BF16),
                        pltpu.VMEM((BLK, D_MODEL), BF16), pltpu.VMEM((BLK, D_MODEL), BF16)],
        compiler_params=_cparams(("arbitrary", "arbitrary")),
        name="kvprep",
    )(x3, gain, w_t, w_t, w_t, w_t, w_t, pos3, freqs, gk, gq, lng, lnb)


def _gelu_tanh(x):
    c = -2.0 * (2.0 / np.pi) ** 0.5 * LOG2E
    return x / (1.0 + jnp.exp2(x * (x * x * (0.044715 * c) + c)))


def _gmlp_kernel(u_ref, v_ref, z_ref, lng_ref, lnb_ref, ws_ref, sbt_ref, o_ref):
    tm = u_ref.shape[0]
    u = _gelu_tanh(u_ref[...].astype(F32))
    v = _gelu_tanh(v_ref[...].astype(F32))
    mu = jnp.mean(v, axis=-1, keepdims=True)
    d = v - mu
    var = jnp.mean(d * d, axis=-1, keepdims=True)
    vn = (d * lax.rsqrt(var + EPS) * lng_ref[...] + lnb_ref[...]).astype(BF16)
    gate = u * jax.nn.silu(z_ref[...].astype(F32))
    tri = (lax.broadcasted_iota(jnp.int32, (CHUNK, CHUNK), 1)
           <= lax.broadcasted_iota(jnp.int32, (CHUNK, CHUNK), 0))
    for g in range(A_GROUPS):
        wg = jnp.where(tri, ws_ref[g], 0.0).astype(BF16)
        bias = sbt_ref[:, g:g + 1]
        cols = slice(g * A_GROUP_DIM, (g + 1) * A_GROUP_DIM)
        for c in range(tm // CHUNK):
            rows = slice(c * CHUNK, (c + 1) * CHUNK)
            sg = jnp.dot(wg, vn[rows, cols], preferred_element_type=F32) + bias
            o_ref[rows, cols] = (gate[rows, cols] * sg).astype(BF16)


def _gmlp(proj, lng, lnb, ws, sbt):
    m = proj.shape[0]
    tm = min(512, m)
    col = lambda c: pl.BlockSpec((tm, BLK), lambda i: (i, c))
    full = lambda shape: pl.BlockSpec(shape, lambda i: (0,) * len(shape))
    return pl.pallas_call(
        _gmlp_kernel,
        grid=(m // tm,),
        in_specs=[col(COL_AU), col(COL_AV), col(COL_AZ), full((1, A_WIDTH)), full((1, A_WIDTH)),
                  full((A_GROUPS, CHUNK, CHUNK)), full((CHUNK, A_GROUPS))],
        out_specs=pl.BlockSpec((tm, A_WIDTH), lambda i: (i, 0)),
        out_shape=jax.ShapeDtypeStruct((m, A_WIDTH), BF16),
        compiler_params=_cparams(("parallel",)),
        name="gmlp",
    )(proj, proj, proj, lng, lnb, ws, sbt)


def _memkv_kernel(mem_ref, g_ref, w_ref, gk_ref, km_ref, vm_ref, wbf_ref):
    @pl.when(pl.program_id(0) == 0)
    def _():
        wbf_ref[...] = w_ref[...].astype(BF16)

    h = _rms_rows(mem_ref[0], g_ref[...])
    kv = jnp.dot(h, wbf_ref[...], preferred_element_type=F32)
    for hd in range(M_HEADS):
        kh = kv[:, hd * M_HEAD_DIM:(hd + 1) * M_HEAD_DIM]
        r = lax.rsqrt(jnp.mean(kh * kh, axis=-1, keepdims=True) + EPS)
        km_ref[0, hd] = (kh * r * gk_ref[...]).astype(BF16)
        vm_ref[0, hd] = kv[:, M_WIDTH + hd * M_HEAD_DIM:M_WIDTH + (hd + 1) * M_HEAD_DIM].astype(BF16)


def _memkv(mem, gain, w_kv, gk):
    bsz, ml, _ = mem.shape
    out = pl.BlockSpec((1, M_HEADS, ml, M_HEAD_DIM), lambda b: (b, 0, 0, 0))
    shp = jax.ShapeDtypeStruct((bsz, M_HEADS, ml, M_HEAD_DIM), BF16)
    return pl.pallas_call(
        _memkv_kernel,
        grid=(bsz,),
        in_specs=[pl.BlockSpec((1, ml, D_MODEL), lambda b: (b, 0, 0)),
                  pl.BlockSpec((1, D_MODEL), lambda b: (0, 0)),
                  pl.BlockSpec((D_MODEL, 2 * M_WIDTH), lambda b: (0, 0), pipeline_mode=pl.Buffered(1)),
                  pl.BlockSpec((1, M_HEAD_DIM), lambda b: (0, 0))],
        out_specs=[out, out],
        out_shape=[shp, shp],
        scratch_shapes=[pltpu.VMEM((D_MODEL, 2 * M_WIDTH), BF16)],
        compiler_params=_cparams(("arbitrary",)),
        name="memkv",
    )(mem, gain, w_kv, gk)


def _memattn_kernel(q_ref, z_ref, km_ref, vm_ref, gq_ref, o_ref):
    qscale = M_HEAD_DIM ** -0.5 * LOG2E
    for hd in range(M_HEADS):
        cols = slice(hd * M_HEAD_DIM, (hd + 1) * M_HEAD_DIM)
        q = q_ref[:, cols].astype(F32)
        r = lax.rsqrt(jnp.mean(q * q, axis=-1, keepdims=True) + EPS)
        qn = (q * r * gq_ref[...] * qscale).astype(BF16)
        lg = _dot_nt(qn, km_ref[0, hd])
        p = jnp.exp2(lg - jnp.max(lg, axis=-1, keepdims=True))
        l = jnp.sum(p, axis=-1, keepdims=True)
        o = jnp.dot(p.astype(BF16), vm_ref[0, hd], preferred_element_type=F32) / l
        o_ref[:, cols] = (o * jax.nn.silu(z_ref[:, cols].astype(F32))).astype(BF16)


def _memattn(proj, km, vm, gq, s):
    m = proj.shape[0]
    tm = min(512, s)
    per_b = s // tm
    ml = km.shape[2]
    kv_spec = pl.BlockSpec((1, M_HEADS, ml, M_HEAD_DIM), lambda i: (i // per_b, 0, 0, 0))
    return pl.pallas_call(
        _memattn_kernel,
        grid=(m // tm,),
        in_specs=[pl.BlockSpec((tm, BLK), lambda i: (i, COL_MQ)),
                  pl.BlockSpec((tm, BLK), lambda i: (i, COL_MZ)),
                  kv_spec, kv_spec,
                  pl.BlockSpec((1, M_HEAD_DIM), lambda i: (0, 0))],
        out_specs=pl.BlockSpec((tm, M_WIDTH), lambda i: (i, 0)),
        out_shape=jax.ShapeDtypeStruct((m, M_WIDTH), BF16),
        compiler_params=_cparams(("parallel",)),
        name="memattn",
    )(proj, proj, km, vm, gq)


def _dsa_kernel(topk, q_ref, z_ref, iq_ref, iwt_ref,
                k_ref, vt_ref, iklo_ref, ikhi_ref, tri_ref, bound_ref, o_ref,
                sc_ref, bias_ref, lg_ref, acc_ref):
    qb = pl.program_id(1)
    nck = (qb * TQ + TQ + CK - 1) // CK
    nbig = (qb * TQ + TQ + CB - 1) // CB
    t_lane = qb * TQ + lax.broadcasted_iota(jnp.int32, (1, TQ), 1)
    gw = B_GROUP * TQ

    def fold8(a):
        return a.reshape(a.shape[0] // 8, 8, a.shape[1])

    def grouped_loop(trips, body, init, per_step=2, start=0):
        carry, done = init, start
        while per_step >= 1:
            def step(i, c, first=done, n=per_step):
                for u in range(n):
                    c = body(first + n * i + u, c)
                return c

            steps = (trips - done) // per_step
            carry = lax.fori_loop(0, steps, step, carry)
            done = done + steps * per_step
            per_step //= 2
        return carry

    wt = iwt_ref[0]

    def idx_body(c, carry):
        mn8, mx8, s1, s2 = carry
        off = pl.multiple_of(c * CK, CK)
        acc = jnp.zeros((CK, TQ), F32)
        for jj in range(IDX_HEADS // 4):
            rhs = iq_ref[0, jj]
            for half, keys_ref in enumerate((iklo_ref, ikhi_ref)):
                d = _dot_nt(keys_ref[0, pl.ds(off, CK), :], rhs)
                ha, hb = 4 * jj + half, 4 * jj + 2 + half
                acc = (acc + jnp.maximum(d[:, :TQ], 0.0) * wt[ha:ha + 1, :]
                       + jnp.maximum(d[:, TQ:], 0.0) * wt[hb:hb + 1, :])
        key = off + lax.broadcasted_iota(jnp.int32, (CK, TQ), 0)
        causal = key <= t_lane
        sc = jnp.where(causal, acc, -jnp.inf)
        sc_ref[pl.ds(off, CK), :] = sc
        live = jnp.where(causal, acc, 0.0)
        mn8 = jnp.minimum(mn8, jnp.min(fold8(jnp.where(causal, acc, jnp.inf)), axis=0))
        mx8 = jnp.maximum(mx8, jnp.max(fold8(sc), axis=0))
        s1 = s1 + jnp.sum(fold8(live), axis=0)
        s2 = s2 + jnp.sum(fold8(live * live), axis=0)
        return mn8, mx8, s1, s2

    zero8 = jnp.zeros((8, TQ), F32)
    stats = (jnp.full((8, TQ), jnp.inf, F32), jnp.full((8, TQ), -jnp.inf, F32), zero8, zero8)
    mn8, mx8, s1, s2 = grouped_loop(nck, idx_body, stats, per_step=4)
    row_min = jnp.min(mn8, axis=0, keepdims=True)
    row_max = jnp.max(mx8, axis=0, keepdims=True)

    def fill_body(c, carry):
        sc_ref[pl.ds(pl.multiple_of(c * CK, CK), CK), :] = jnp.full((CK, TQ), -jnp.inf, F32)
        return carry

    lax.fori_loop(nck, nbig * (CB // CK), fill_body, 0)

    def count_rows(pred):
        rows = 8 * COUNT_WAYS

        def body(c, cnt):
            off = pl.multiple_of(c * CB, CB)
            for r in range(CB // rows):
                cnt = cnt + jnp.where(pred(sc_ref[pl.ds(off + r * rows, rows), :]), 1.0, 0.0)
            return cnt

        cnt = lax.fori_loop(0, nbig, body, jnp.zeros((rows, TQ), F32))
        return jnp.sum(cnt, axis=0, keepdims=True)

    kf = float(topk)
    n_valid = (t_lane + 1).astype(F32)
    all_rows = (t_lane + 1) <= topk

    def search_pass(st, probe, stuck):
        lo, hi, clo, chi, thr, fin, tie = st
        cnt = count_rows(lambda blk: blk >= probe)
        active = fin < 0.5
        hit = cnt == kf
        end_thr = jnp.where(jnp.logical_and(stuck, cnt < kf), lo, probe)
        ends = jnp.logical_and(active, jnp.logical_or(hit, stuck))
        thr = jnp.where(ends, end_thr, thr)
        tie = jnp.where(jnp.logical_and(ends, jnp.logical_not(hit)), 1.0, tie)
        fin = jnp.where(ends, 1.0, fin)
        up = jnp.logical_and(active, cnt > kf)
        dn = jnp.logical_and(active, cnt < kf)
        return (jnp.where(up, probe, lo), jnp.where(dn, probe, hi), jnp.where(up, cnt, clo),
                jnp.where(dn, cnt, chi), thr, fin, tie)

    def next_probe(st, halve=False):
        lo, hi, clo, chi = st[:4]
        frac = jnp.clip((clo - kf + 0.5) / (clo - chi + 1.0), SEARCH_MARGIN, 1.0 - SEARCH_MARGIN)
        probe = lo + (hi - lo) * frac
        outside = jnp.logical_or(jnp.logical_or(probe <= lo, probe >= hi), halve)
        probe = jnp.where(outside, lo * 0.5 + hi * 0.5, probe)
        stuck = jnp.logical_or(probe <= lo, probe >= hi)
        return jnp.where(stuck, hi, probe), stuck

    mean = jnp.sum(s1, axis=0, keepdims=True) / n_valid
    var = jnp.maximum(jnp.sum(s2, axis=0, keepdims=True) / n_valid - mean * mean, 0.0)
    tail = jnp.clip(kf / n_valid, 1e-6, 1.0 - 1e-6)
    tq = jnp.sqrt(-2.0 * jnp.log(jnp.minimum(tail, 1.0 - tail)))
    zq = tq - ((0.010328 * tq + 0.802853) * tq + 2.515517) / (
        ((0.001308 * tq + 0.189269) * tq + 1.432788) * tq + 1.0)
    guess = mean + jnp.where(tail < 0.5, zq, -zq) * jnp.sqrt(var)
    inside = jnp.logical_and(guess >= row_min, guess <= row_max)
    probe0 = jnp.where(inside, guess, row_min * 0.5 + row_max * 0.5)

    ones = jnp.ones((1, TQ), F32)
    st = (row_min, row_max, n_valid, 0.0 * ones, jnp.where(all_rows, -F32_MAX, row_max),
          jnp.where(all_rows, 1.0, 0.0), 0.0 * ones)
    st = search_pass(st, probe0, probe0 < row_min)

    def fixed_body(i, st):
        return search_pass(st, *next_probe(st))

    st = lax.fori_loop(0, SEARCH_FIXED_PASSES, fixed_body, st)

    def more_cond(c):
        return jnp.logical_and(c[0] < MAX_SEARCH_ITERS, jnp.min(c[1][5]) < 0.5)

    def more_body(c):
        return c[0] + 1, search_pass(c[1], *next_probe(c[1], c[0] >= SEARCH_INTERP_PASSES))

    _, st = lax.while_loop(more_cond, more_body, (jnp.int32(0), st))
    thr, tie = st[4], st[6]

    logit_bound = bound_ref[0, 0]
    sel_bias = jnp.where(logit_bound <= MAX_SHIFT_BOUND, -logit_bound, 0.0)

    def plain_mask():
        def body(c, carry):
            off = pl.multiple_of(c * CB, CB)
            bias_ref[pl.ds(off, CB), :] = jnp.where(sc_ref[pl.ds(off, CB), :] >= thr, sel_bias, NEG_BIG)
            return carry

        lax.fori_loop(0, nbig, body, 0)

    def tied_mask():
        n_gt = count_rows(lambda blk: blk > thr)
        need = jnp.where(all_rows, F32_MAX, kf - n_gt)

        def body(c, seen):
            off = pl.multiple_of(c * CB, CB)
            blk = sc_ref[pl.ds(off, CB), :]
            eq = blk == thr
            pref = jnp.dot(tri_ref[...], jnp.where(eq, 1.0, 0.0).astype(BF16),
                           preferred_element_type=F32) + seen
            keep = jnp.logical_or(blk > thr, jnp.logical_and(eq, pref <= need))
            bias_ref[pl.ds(off, CB), :] = jnp.where(keep, sel_bias, NEG_BIG)
            return pref[CB - 1:CB, :]

        lax.fori_loop(0, nbig, body, jnp.zeros((1, TQ), F32))

    lax.cond(jnp.max(tie) > 0.5, tied_mask, plain_mask)

    def masked_logits(n, off):
        b = bias_ref[pl.ds(off, CB), :]
        qn = q_ref[0, n * B_GROUP:(n + 1) * B_GROUP].reshape(gw, B_HEAD_DIM)
        lg = _dot_nt(k_ref[0, n, pl.ds(off, CB), :], qn)
        return jnp.concatenate([lg[:, g * TQ:(g + 1) * TQ] + b for g in range(B_GROUP)], axis=1)

    for n in range(B_KV_HEADS):
        acc_ref[...] = jnp.zeros(acc_ref.shape, F32)

        def one_pass(n=n):
            def body(c, l8):
                off = pl.multiple_of(c * CB, CB)
                p = jnp.exp2(masked_logits(n, off))
                acc_ref[...] += jnp.dot(vt_ref[0, n, c], p.astype(BF16), preferred_element_type=F32)
                return l8 + jnp.sum(fold8(p), axis=0)

            return grouped_loop(nbig, body, jnp.zeros((8, gw), F32), per_step=4)

        def two_pass(n=n):
            def logit_body(c, mx8):
                off = pl.multiple_of(c * CB, CB)
                lg = masked_logits(n, off)
                lg_ref[pl.ds(off, CB), :] = lg
                return jnp.maximum(mx8, jnp.max(fold8(lg), axis=0))

            mx8 = grouped_loop(nbig, logit_body, jnp.full((8, gw), NEG_BIG, F32))
            m = jnp.max(mx8, axis=0, keepdims=True)

            def pv_body(c, l8):
                off = pl.multiple_of(c * CB, CB)
                p = jnp.exp2(lg_ref[pl.ds(off, CB), :] - m)
                acc_ref[...] += jnp.dot(vt_ref[0, n, c], p.astype(BF16), preferred_element_type=F32)
                return l8 + jnp.sum(fold8(p), axis=0)

            return grouped_loop(nbig, pv_body, jnp.zeros((8, gw), F32))

        l8 = lax.cond(logit_bound <= MAX_SHIFT_BOUND, one_pass, two_pass)
        o_t = acc_ref[...] / jnp.sum(l8, axis=0, keepdims=True)
        for g in range(B_GROUP):
            cols = slice((n * B_GROUP + g) * B_HEAD_DIM, (n * B_GROUP + g + 1) * B_HEAD_DIM)
            o = o_t[:, g * TQ:(g + 1) * TQ].T
            o_ref[:, cols] = (o * jax.nn.silu(z_ref[:, cols].astype(F32))).astype(BF16)


def _dsa(proj, q, iq, iwt, k, vt, iklo, ikhi, gq, gk, bsz, s):
    nq = s // TQ
    bound = (1.02 * B_HEAD_DIM ** 0.5 * LOG2E * jnp.max(jnp.abs(gq)) * jnp.max(jnp.abs(gk))).reshape(1, 1)
    topk = min(TOPK_MAX, s // 4)
    gw = B_GROUP * TQ
    blk4 = lambda a: pl.BlockSpec((1,) + a.shape[1:], lambda b, i: (b * nq + i, 0, 0, 0))
    ik = pl.BlockSpec((1, s, LANES), lambda b, i: (b, 0, 0))
    tri = jnp.tril(jnp.ones((CB, CB), BF16))
    return pl.pallas_call(
        functools.partial(_dsa_kernel, topk),
        grid=(bsz, nq),
        in_specs=[blk4(q), pl.BlockSpec((TQ, BLK), lambda b, i: (b * nq + i, COL_BZ)), blk4(iq),
                  pl.BlockSpec((1, IDX_HEADS, TQ), lambda b, i: (b, 0, i)),
                  pl.BlockSpec((1, B_KV_HEADS, s, B_HEAD_DIM), lambda b, i: (b, 0, 0, 0)),
                  pl.BlockSpec((1, B_KV_HEADS, s // CB, B_HEAD_DIM, CB), lambda b, i: (b, 0, 0, 0, 0)),
                  ik, ik,
                  pl.BlockSpec((CB, CB), lambda b, i: (0, 0)),
                  pl.BlockSpec(memory_space=pltpu.SMEM)],
        out_specs=pl.BlockSpec((TQ, B_WIDTH), lambda b, i: (b * nq + i, 0)),
        out_shape=jax.ShapeDtypeStruct((bsz * s, B_WIDTH), BF16),
        scratch_shapes=[
            pltpu.VMEM((s, TQ), F32),
            pltpu.VMEM((s, TQ), F32),
            pltpu.VMEM((s, gw), F32),
            pltpu.VMEM((B_HEAD_DIM, gw), F32),
        ],
        compiler_params=_cparams(("parallel", "arbitrary")),
        name="dsa",
    )(q, proj, iq, iwt, k, vt, iklo, ikhi, tri, bound)


def _merge_kernel(ta_ref, tb_ref, tm_ref, ga_ref, gb_ref, gm_ref, x_ref,
                  wa_hbm, wb_hbm, wm_hbm, wo_hbm, o_ref,
                  wa_ref, wb_ref, wm_ref, wo_ref, stage_ref, sem):
    @pl.when(pl.program_id(0) == 0)
    def _():
        chunks = [(src, dst, r) for src, dst in ((wa_hbm, wa_ref), (wb_hbm, wb_ref), (wm_hbm, wm_ref),
                                                 (wo_hbm, wo_ref))
                  for r in range(0, src.shape[0], MERGE_STAGE_ROWS)]

        def copy(j):
            src, _, r = chunks[j]
            return pltpu.make_async_copy(src.at[pl.ds(r, MERGE_STAGE_ROWS), :], stage_ref.at[j % 2],
                                         sem.at[j % 2])

        copy(0).start()
        for j, (_, dst, r) in enumerate(chunks):
            if j + 1 < len(chunks):
                copy(j + 1).start()
            copy(j).wait()
            dst[pl.ds(r, MERGE_STAGE_ROWS), :] = stage_ref[j % 2].astype(BF16)

    def branch(t_ref, g_ref, w_ref):
        y = jnp.dot(t_ref[...], w_ref[...], preferred_element_type=F32)
        return jax.nn.sigmoid(g_ref[...].astype(F32)) * y

    merged = branch(ta_ref, ga_ref, wa_ref) + branch(tb_ref, gb_ref, wb_ref) + branch(tm_ref, gm_ref, wm_ref)
    o_ref[...] = x_ref[...] + jnp.dot(merged.astype(BF16), wo_ref[...], preferred_element_type=F32)


def _merge(ta, tb, tmem, proj, x2, wa, wb, wm, wo):
    m = x2.shape[0]
    tm = min(256, m)
    act = pl.BlockSpec((tm, BLK), lambda i: (i, 0))
    gate = lambda c: pl.BlockSpec((tm, D_MODEL), lambda i: (i, c // 2))
    wide = pl.BlockSpec((tm, D_MODEL), lambda i: (i, 0))
    hbm = pl.BlockSpec(memory_space=pl.ANY)
    return pl.pallas_call(
        _merge_kernel,
        grid=(m // tm,),
        in_specs=[act, act, act, gate(COL_GA), gate(COL_GB), gate(COL_GM), wide, hbm, hbm, hbm, hbm],
        out_specs=wide,
        out_shape=jax.ShapeDtypeStruct((m, D_MODEL), F32),
        scratch_shapes=[pltpu.VMEM(w.shape, BF16) for w in (wa, wb, wm, wo)]
        + [pltpu.VMEM((2, MERGE_STAGE_ROWS, D_MODEL), F32), pltpu.SemaphoreType.DMA((2,))],
        compiler_params=_cparams(("arbitrary",)),
        name="merge",
    )(ta, tb, tmem, proj, proj, proj, x2, wa, wb, wm, wo)


def _rope_freqs():
    inv = lambda half: ROPE_THETA ** (-np.arange(half, dtype=np.float32) / half)
    freq = np.zeros((1, LANES), np.float32)
    hb, hi = B_HEAD_DIM // 8, IDX_DIM // 8
    freq[0, :hb] = inv(hb)
    freq[0, hb:hb + hi] = inv(hi)
    return jnp.asarray(freq)


def _layer(x, mem, positions, norm_gain, w_in, gmlp_ln_gain, gmlp_ln_bias, spatial_w, spatial_b,
           w_branch_a, q_norm_gain, k_norm_gain, idx_k_ln_gain, idx_k_ln_bias, w_branch_b,
           mem_norm_gain, w_mem_kv, mem_q_norm_gain, mem_k_norm_gain, w_branch_m, w_out):
    bsz, s, _ = x.shape
    m = bsz * s
    row = lambda a: a.reshape(1, -1).astype(F32)
    pad_lanes = lambda a: jnp.pad(a.reshape(1, -1).astype(F32), ((0, 0), (0, LANES - a.shape[-1])))
    x2 = x.reshape(m, D_MODEL)

    w_t = w_in.T
    h, q, iq, k, vt, iklo, ikhi, iwt = _kvprep(
        x, row(norm_gain), w_t, positions.reshape(bsz, s, 1).astype(jnp.int32), _rope_freqs(),
        row(k_norm_gain[_HEAD_PERM]), row(q_norm_gain[_HEAD_PERM]),
        pad_lanes(idx_k_ln_gain), pad_lanes(idx_k_ln_bias))
    proj = _inproj(h, w_t)

    t_a = _gmlp(proj, row(gmlp_ln_gain), row(gmlp_ln_bias), spatial_w.astype(F32), spatial_b.T.astype(F32))
    km, vm = _memkv(mem, row(mem_norm_gain), w_mem_kv, row(mem_k_norm_gain))
    t_m = _memattn(proj, km, vm, row(mem_q_norm_gain), s)
    t_b = _dsa(proj, q, iq, iwt, k, vt, iklo, ikhi, row(q_norm_gain), row(k_norm_gain), bsz, s)

    out = _merge(t_a, t_b, t_m, proj, x2, w_branch_a, w_branch_b, w_branch_m, w_out)
    return out.reshape(bsz, s, D_MODEL)


def kernel(x, mem, positions, norm_gain, w_in, gmlp_ln_gain, gmlp_ln_bias, spatial_w, spatial_b, w_branch_a, q_norm_gain, k_norm_gain, idx_k_ln_gain, idx_k_ln_bias, w_branch_b, mem_norm_gain, w_mem_kv, mem_q_norm_gain, mem_k_norm_gain, w_branch_m, w_out):
    for l in range(norm_gain.shape[0]):
        x = _layer(x, mem, positions, norm_gain[l], w_in[l], gmlp_ln_gain[l], gmlp_ln_bias[l],
                   spatial_w[l], spatial_b[l], w_branch_a[l], q_norm_gain[l], k_norm_gain[l],
                   idx_k_ln_gain[l], idx_k_ln_bias[l], w_branch_b[l], mem_norm_gain[l], w_mem_kv[l],
                   mem_q_norm_gain[l], mem_k_norm_gain[l], w_branch_m[l], w_out[l])
    return x
```

```python
import functools

import numpy as np
import jax
import jax.numpy as jnp
from jax import lax
from jax.experimental import pallas as pl
from jax.experimental.pallas import tpu as pltpu

F32 = jnp.float32
BF16 = jnp.bfloat16

D_MODEL = 2048
ROPE_THETA = 500000.0
EPS = 1e-6
A_GROUPS = 8
A_GROUP_DIM = 128
A_WIDTH = A_GROUPS * A_GROUP_DIM
CHUNK = 128
B_HEADS = 8
B_KV_HEADS = 2
B_GROUP = B_HEADS // B_KV_HEADS
B_HEAD_DIM = 128
B_WIDTH = B_HEADS * B_HEAD_DIM
IDX_HEADS = 16
IDX_DIM = 64
TOPK_MAX = 256
M_HEADS = 4
M_HEAD_DIM = 256
M_WIDTH = M_HEADS * M_HEAD_DIM

SPLIT_SIZES = (
    A_WIDTH, A_WIDTH, A_WIDTH,
    B_WIDTH, B_KV_HEADS * B_HEAD_DIM, B_KV_HEADS * B_HEAD_DIM, B_WIDTH,
    IDX_HEADS * IDX_DIM, IDX_DIM, IDX_HEADS,
    M_WIDTH, M_WIDTH,
    D_MODEL, D_MODEL, D_MODEL,
)

LANES = 128
BLK = 1024
COL_AU, COL_AV, COL_AZ, COL_BZ, COL_MQ, COL_MZ = range(6)
COL_GA, COL_GB, COL_GM = 6, 8, 10
NB_MAIN = 12
_OFFS = [int(o) for o in np.concatenate([[0], np.cumsum(SPLIT_SIZES)])]
ROW_ALIGN = 16
MAIN_START = ([_OFFS[i] for i in (0, 1, 2, 6, 10, 11)]
              + [_OFFS[i] + d for i in (12, 13, 14) for d in (0, BLK)])
assert all(s % ROW_ALIGN == 0 for s in MAIN_START)
Q_START, IQ_START, HALF_BLK = _OFFS[3], _OFFS[7], BLK // 2
SMALL_KV_START, SMALL_KV_WIDTH = _OFFS[4], _OFFS[6] - _OFFS[4]
SMALL_IDX_START = _OFFS[8]
assert Q_START % BLK == 0 and IQ_START % HALF_BLK == 0
assert SMALL_KV_START % SMALL_KV_WIDTH == 0 and SMALL_IDX_START % LANES == 0
assert IDX_DIM + IDX_HEADS <= LANES and _OFFS[9] == SMALL_IDX_START + IDX_DIM

VMEM_LIMIT = 56 * 1024 * 1024
LOG2E = 1.4426950408889634
NEG_BIG = -1e30
F32_MAX = 3.4028234663852886e38

INPROJ_TM, INPROJ_TN = 2048, 1024
KVPREP_TP = 512
TQ = 256
CK = 256
CB = 512
COUNT_WAYS = 8
SEARCH_MARGIN = 0.02
SEARCH_FIXED_PASSES = 11
SEARCH_INTERP_PASSES = 13
MAX_SEARCH_ITERS = 400
MERGE_STAGE_ROWS = 512
MAX_SHIFT_BOUND = 48.0


def _cparams(sem):
    return pltpu.CompilerParams(dimension_semantics=sem, vmem_limit_bytes=VMEM_LIMIT)


def _dot_nt(a, b):
    return lax.dot_general(a, b, (((1,), (1,)), ((), ())), preferred_element_type=F32)


_HEAD_PERM = np.concatenate([np.arange(0, 16), np.arange(32, 80), np.arange(16, 32), np.arange(80, 128)])
assert B_HEAD_DIM == LANES and B_HEAD_DIM // 8 == 16
_PAIR_PERM = ((0, 8), (64, 72), (16, 64), (8, 16), (72, 80), (80, 128))
assert 2 * IDX_DIM == LANES and IDX_DIM // 8 == 8


def _rope_swapped(x, cos_t, sin_t):
    return x * cos_t + pltpu.roll(x, LANES // 2, 1) * sin_t


def _permute_head_rows(src_ref, dst_ref, hd):
    hb, mid = B_HEAD_DIM // 8, B_HEAD_DIM // 2
    base = hd * B_HEAD_DIM
    for src, dst, n in ((0, 0, hb), (2 * hb, hb, mid - hb), (hb, mid, hb), (mid + hb, mid + hb, mid - hb)):
        dst_ref[base + dst:base + dst + n, :] = src_ref[base + src:base + src + n, :].astype(BF16)


def _rms_rows(x, gain):
    ms = jnp.mean(x * x, axis=-1, keepdims=True)
    return (x * lax.rsqrt(ms + EPS) * gain).astype(BF16)


def _inproj_kernel(tab_ref, h_ref, wt_ref, proj_ref):
    del tab_ref
    proj_ref[...] = _dot_nt(h_ref[...], wt_ref[...].astype(BF16)).astype(BF16)


def _inproj(h, w_t):
    m = h.shape[0]
    tm = min(INPROJ_TM, m)
    starts = [s + d for s in MAIN_START for d in range(0, BLK, INPROJ_TN)]
    tab = jnp.asarray(np.array([s // ROW_ALIGN for s in starts], np.int32))
    return pl.pallas_call(
        _inproj_kernel,
        grid_spec=pltpu.PrefetchScalarGridSpec(
            num_scalar_prefetch=1,
            grid=(m // tm, len(starts)),
            in_specs=[
                pl.BlockSpec((tm, D_MODEL), lambda i, n, tab: (i, 0)),
                pl.BlockSpec((pl.Element(INPROJ_TN), pl.Element(D_MODEL)),
                             lambda i, n, tab: (tab[n] * ROW_ALIGN, 0)),
            ],
            out_specs=pl.BlockSpec((tm, INPROJ_TN), lambda i, n, tab: (i, n)),
        ),
        out_shape=jax.ShapeDtypeStruct((m, NB_MAIN * BLK), BF16),
        compiler_params=_cparams(("parallel", "arbitrary")),
        name="inproj",
    )(tab, h, w_t)


def _kvprep_kernel(x_ref, g_ref, wkv_ref, widx_ref, wq_ref, wiqa_ref, wiqb_ref, pos_ref, fc_ref,
                   gk_ref, gq_ref, lng_ref, lnb_ref,
                   h_ref, q_ref, iq_ref, k_ref, vt_ref, iklo_ref, ikhi_ref, iwt_ref,
                   wkv_bf_ref, widx_bf_ref, wq_bf_ref, wiq_bf_ref):
    @pl.when(jnp.logical_and(pl.program_id(0) == 0, pl.program_id(1) == 0))
    def _():
        wkv_bf_ref[...] = wkv_ref[...].astype(BF16)
        for hd in range(B_KV_HEADS):
            _permute_head_rows(wkv_ref, wkv_bf_ref, hd)
        widx_bf_ref[...] = widx_ref[...].astype(BF16)
        for hd in range(B_HEADS):
            _permute_head_rows(wq_ref, wq_bf_ref, hd)
        for j in range(IDX_HEADS // 2):
            src = (wiqa_ref, wiqb_ref)[j * LANES // HALF_BLK]
            base = j * LANES % HALF_BLK
            rows = jnp.concatenate([src[base + a:base + b, :] for a, b in _PAIR_PERM], axis=0)
            wiq_bf_ref[j * LANES:(j + 1) * LANES, :] = rows.astype(BF16)

    h = _rms_rows(x_ref[...], g_ref[...])
    h_ref[...] = h
    kv = _dot_nt(h, wkv_bf_ref[...])
    ikp = _dot_nt(h, widx_bf_ref[...])

    hb, hi = B_HEAD_DIM // 8, IDX_DIM // 8
    ang = pos_ref[0].astype(F32) * fc_ref[...]
    cos_c, sin_c = jnp.cos(ang), jnp.sin(ang)
    lane = lax.broadcasted_iota(jnp.int32, ang.shape, 1)
    second = jnp.logical_and(lane >= LANES // 2, lane < LANES // 2 + hb)
    cos_b = jnp.where(lane < hb, cos_c, jnp.where(second, pltpu.roll(cos_c, LANES // 2, 1), 1.0))
    sin_b = jnp.where(lane < hb, -sin_c, jnp.where(second, pltpu.roll(sin_c, LANES // 2, 1), 0.0))
    cos_i = jnp.ones_like(cos_c)
    sin_i = jnp.zeros_like(sin_c)
    for first, sign in ((0, -1.0), (hi, -1.0), (LANES // 2, 1.0), (LANES // 2 + hi, 1.0)):
        here = jnp.logical_and(lane >= first, lane < first + hi)
        shift = (first - hb) % LANES
        cos_i = jnp.where(here, pltpu.roll(cos_c, shift, 1), cos_i)
        sin_i = jnp.where(here, sign * pltpu.roll(sin_c, shift, 1), sin_i)
    nblk = q_ref.shape[0]
    gain = gq_ref[...] * (B_HEAD_DIM ** -0.5 * LOG2E)
    qp = _dot_nt(h, wq_bf_ref[...])
    for hd in range(B_HEADS):
        slab = qp[:, hd * B_HEAD_DIM:(hd + 1) * B_HEAD_DIM]
        r = lax.rsqrt(jnp.mean(slab * slab, axis=-1, keepdims=True) + EPS)
        qr = _rope_swapped(slab * r * gain, cos_b, sin_b).astype(BF16)
        for blk in range(nblk):
            q_ref[blk, hd] = qr[blk * TQ:(blk + 1) * TQ, :]
    iqp = _dot_nt(h, wiq_bf_ref[...])
    for j in range(IDX_HEADS // 2):
        ir = _rope_swapped(iqp[:, j * LANES:(j + 1) * LANES], cos_i, sin_i).astype(BF16)
        for blk in range(nblk):
            iq_ref[blk, j // 2, (j % 2) * TQ:(j % 2 + 1) * TQ, :] = ir[blk * TQ:(blk + 1) * TQ, :]

    for n in range(B_KV_HEADS):
        kh = kv[:, n * B_HEAD_DIM:(n + 1) * B_HEAD_DIM]
        r = lax.rsqrt(jnp.mean(kh * kh, axis=-1, keepdims=True) + EPS)
        kn = kh * r * gk_ref[...]
        k_ref[0, n] = _rope_swapped(kn, cos_b, sin_b).astype(BF16)
        vt = kv[:, (B_KV_HEADS + n) * B_HEAD_DIM:(B_KV_HEADS + n + 1) * B_HEAD_DIM].T
        for c in range(vt.shape[1] // CB):
            vt_ref[0, n, c] = vt[:, c * CB:(c + 1) * CB].astype(BF16)

    lane = lax.broadcasted_iota(jnp.int32, ikp.shape, 1)
    live = lane < IDX_DIM
    mu = jnp.sum(jnp.where(live, ikp, 0.0), axis=-1, keepdims=True) * (1.0 / IDX_DIM)
    d = jnp.where(live, ikp - mu, 0.0)
    var = jnp.sum(d * d, axis=-1, keepdims=True) * (1.0 / IDX_DIM)
    y = d * lax.rsqrt(var + EPS) * lng_ref[...] + lnb_ref[...]
    half = LANES // 2
    x2_a = jnp.logical_and(lane >= half, lane < half + hi)
    keep = jnp.logical_or(lane < hi, jnp.logical_and(lane >= 2 * hi, lane < half))
    lo = _rope_swapped(jnp.where(x2_a, pltpu.roll(y, half - hi, 1), jnp.where(keep, y, 0.0)), cos_i, sin_i)
    x_b = jnp.logical_or(jnp.logical_and(lane >= hi, lane < 2 * hi),
                         jnp.logical_and(lane >= half + hi, lane < half + 2 * hi))
    hi_slots = jnp.where(x_b, pltpu.roll(lo, hi, 1), jnp.where(lane >= half + 2 * hi, pltpu.roll(lo, half, 1), 0.0))
    iklo_ref[0] = lo.astype(BF16)
    ikhi_ref[0] = hi_slots.astype(BF16)

    iw = ikp * (IDX_DIM ** -0.5 * IDX_HEADS ** -0.5)
    iwt_ref[0] = iw.T[IDX_DIM:IDX_DIM + IDX_HEADS, :]


def _kvprep(x3, gain, w_t, pos3, freqs, gk, gq, lng, lnb):
    bsz, s, _ = x3.shape
    tp = min(KVPREP_TP, s)
    per_b, nblk = s // tp, tp // TQ
    row = lambda b, i: (b, i, 0)
    const = lambda b, i: (0, 0)
    tab_spec = pl.BlockSpec((1, LANES), const)
    wrows = lambda rows, start: pl.BlockSpec((rows, D_MODEL), lambda b, i: (start // rows, 0),
                                             pipeline_mode=pl.Buffered(1))
    out_h = pl.BlockSpec((tp, D_MODEL), lambda b, i: (b * per_b + i, 0))
    out_q = pl.BlockSpec((nblk, B_HEADS, TQ, B_HEAD_DIM), lambda b, i: (b * per_b + i, 0, 0, 0))
    out_iq = pl.BlockSpec((nblk, IDX_HEADS // 4, 2 * TQ, LANES), lambda b, i: (b * per_b + i, 0, 0, 0))
    out_tok = pl.BlockSpec((1, tp, LANES), row)
    out_k = pl.BlockSpec((1, B_KV_HEADS, tp, B_HEAD_DIM), lambda b, i: (b, 0, i, 0))
    out_vt = pl.BlockSpec((1, B_KV_HEADS, tp // CB, B_HEAD_DIM, CB), lambda b, i: (b, 0, i, 0, 0))
    out_iwt = pl.BlockSpec((1, IDX_HEADS, tp), lambda b, i: (b, 0, i))
    nq = bsz * s // TQ
    return pl.pallas_call(
        _kvprep_kernel,
        grid=(bsz, per_b),
        in_specs=[pl.BlockSpec((None, tp, D_MODEL), row), pl.BlockSpec((1, D_MODEL), const),
                  wrows(SMALL_KV_WIDTH, SMALL_KV_START), wrows(LANES, SMALL_IDX_START),
                  wrows(BLK, Q_START), wrows(HALF_BLK, IQ_START), wrows(HALF_BLK, IQ_START + HALF_BLK),
                  pl.BlockSpec((1, tp, 1), row),
                  tab_spec, tab_spec, tab_spec, tab_spec, tab_spec],
        out_specs=[out_h, out_q, out_iq, out_k, out_vt, out_tok, out_tok, out_iwt],
        out_shape=[
            jax.ShapeDtypeStruct((bsz * s, D_MODEL), BF16),
            jax.ShapeDtypeStruct((nq, B_HEADS, TQ, B_HEAD_DIM), BF16),
            jax.ShapeDtypeStruct((nq, IDX_HEADS // 4, 2 * TQ, LANES), BF16),
            jax.ShapeDtypeStruct((bsz, B_KV_HEADS, s, B_HEAD_DIM), BF16),
            jax.ShapeDtypeStruct((bsz, B_KV_HEADS, s // CB, B_HEAD_DIM, CB), BF16),
            jax.ShapeDtypeStruct((bsz, s, LANES), BF16),
            jax.ShapeDtypeStruct((bsz, s, LANES), BF16),
            jax.ShapeDtypeStruct((bsz, IDX_HEADS, s), F32),
        ],
        scratch_shapes=[pltpu.VMEM((SMALL_KV_WIDTH, D_MODEL), BF16), pltpu.VMEM((LANES, D_MODEL), BF16),
                        pltpu.VMEM((BLK, D_MODEL), BF16), pltpu.VMEM((BLK, D_MODEL), BF16)],
        compiler_params=_cparams(("arbitrary", "arbitrary")),
        name="kvprep",
    )(x3, gain, w_t, w_t, w_t, w_t, w_t, pos3, freqs, gk, gq, lng, lnb)


def _gelu_tanh(x):
    c = -2.0 * (2.0 / np.pi) ** 0.5 * LOG2E
    return x / (1.0 + jnp.exp2(x * (x * x * (0.044715 * c) + c)))


def _gmlp_kernel(u_ref, v_ref, z_ref, lng_ref, lnb_ref, ws_ref, sbt_ref, o_ref):
    tm = u_ref.shape[0]
    u = _gelu_tanh(u_ref[...].astype(F32))
    v = _gelu_tanh(v_ref[...].astype(F32))
    mu = jnp.mean(v, axis=-1, keepdims=True)
    d = v - mu
    var = jnp.mean(d * d, axis=-1, keepdims=True)
    vn = (d * lax.rsqrt(var + EPS) * lng_ref[...] + lnb_ref[...]).astype(BF16)
    gate = u * jax.nn.silu(z_ref[...].astype(F32))
    tri = (lax.broadcasted_iota(jnp.int32, (CHUNK, CHUNK), 1)
           <= lax.broadcasted_iota(jnp.int32, (CHUNK, CHUNK), 0))
    for g in range(A_GROUPS):
        wg = jnp.where(tri, ws_ref[g], 0.0).astype(BF16)
        bias = sbt_ref[:, g:g + 1]
        cols = slice(g * A_GROUP_DIM, (g + 1) * A_GROUP_DIM)
        for c in range(tm // CHUNK):
            rows = slice(c * CHUNK, (c + 1) * CHUNK)
            sg = jnp.dot(wg, vn[rows, cols], preferred_element_type=F32) + bias
            o_ref[rows, cols] = (gate[rows, cols] * sg).astype(BF16)


def _gmlp(proj, lng, lnb, ws, sbt):
    m = proj.shape[0]
    tm = min(512, m)
    col = lambda c: pl.BlockSpec((tm, BLK), lambda i: (i, c))
    full = lambda shape: pl.BlockSpec(shape, lambda i: (0,) * len(shape))
    return pl.pallas_call(
        _gmlp_kernel,
        grid=(m // tm,),
        in_specs=[col(COL_AU), col(COL_AV), col(COL_AZ), full((1, A_WIDTH)), full((1, A_WIDTH)),
                  full((A_GROUPS, CHUNK, CHUNK)), full((CHUNK, A_GROUPS))],
        out_specs=pl.BlockSpec((tm, A_WIDTH), lambda i: (i, 0)),
        out_shape=jax.ShapeDtypeStruct((m, A_WIDTH), BF16),
        compiler_params=_cparams(("parallel",)),
        name="gmlp",
    )(proj, proj, proj, lng, lnb, ws, sbt)


def _memkv_kernel(mem_ref, g_ref, w_ref, gk_ref, km_ref, vm_ref, wbf_ref):
    @pl.when(pl.program_id(0) == 0)
    def _():
        wbf_ref[...] = w_ref[...].astype(BF16)

    h = _rms_rows(mem_ref[0], g_ref[...])
    kv = jnp.dot(h, wbf_ref[...], preferred_element_type=F32)
    for hd in range(M_HEADS):
        kh = kv[:, hd * M_HEAD_DIM:(hd + 1) * M_HEAD_DIM]
        r = lax.rsqrt(jnp.mean(kh * kh, axis=-1, keepdims=True) + EPS)
        km_ref[0, hd] = (kh * r * gk_ref[...]).astype(BF16)
        vm_ref[0, hd] = kv[:, M_WIDTH + hd * M_HEAD_DIM:M_WIDTH + (hd + 1) * M_HEAD_DIM].astype(BF16)


def _memkv(mem, gain, w_kv, gk):
    bsz, ml, _ = mem.shape
    out = pl.BlockSpec((1, M_HEADS, ml, M_HEAD_DIM), lambda b: (b, 0, 0, 0))
    shp = jax.ShapeDtypeStruct((bsz, M_HEADS, ml, M_HEAD_DIM), BF16)
    return pl.pallas_call(
        _memkv_kernel,
        grid=(bsz,),
        in_specs=[pl.BlockSpec((1, ml, D_MODEL), lambda b: (b, 0, 0)),
                  pl.BlockSpec((1, D_MODEL), lambda b: (0, 0)),
                  pl.BlockSpec((D_MODEL, 2 * M_WIDTH), lambda b: (0, 0), pipeline_mode=pl.Buffered(1)),
                  pl.BlockSpec((1, M_HEAD_DIM), lambda b: (0, 0))],
        out_specs=[out, out],
        out_shape=[shp, shp],
        scratch_shapes=[pltpu.VMEM((D_MODEL, 2 * M_WIDTH), BF16)],
        compiler_params=_cparams(("arbitrary",)),
        name="memkv",
    )(mem, gain, w_kv, gk)


def _memattn_kernel(q_ref, z_ref, km_ref, vm_ref, gq_ref, o_ref):
    qscale = M_HEAD_DIM ** -0.5 * LOG2E
    for hd in range(M_HEADS):
        cols = slice(hd * M_HEAD_DIM, (hd + 1) * M_HEAD_DIM)
        q = q_ref[:, cols].astype(F32)
        r = lax.rsqrt(jnp.mean(q * q, axis=-1, keepdims=True) + EPS)
        qn = (q * r * gq_ref[...] * qscale).astype(BF16)
        lg = _dot_nt(qn, km_ref[0, hd])
        p = jnp.exp2(lg - jnp.max(lg, axis=-1, keepdims=True))
        l = jnp.sum(p, axis=-1, keepdims=True)
        o = jnp.dot(p.astype(BF16), vm_ref[0, hd], preferred_element_type=F32) / l
        o_ref[:, cols] = (o * jax.nn.silu(z_ref[:, cols].astype(F32))).astype(BF16)


def _memattn(proj, km, vm, gq, s):
    m = proj.shape[0]
    tm = min(512, s)
    per_b = s // tm
    ml = km.shape[2]
    kv_spec = pl.BlockSpec((1, M_HEADS, ml, M_HEAD_DIM), lambda i: (i // per_b, 0, 0, 0))
    return pl.pallas_call(
        _memattn_kernel,
        grid=(m // tm,),
        in_specs=[pl.BlockSpec((tm, BLK), lambda i: (i, COL_MQ)),
                  pl.BlockSpec((tm, BLK), lambda i: (i, COL_MZ)),
                  kv_spec, kv_spec,
                  pl.BlockSpec((1, M_HEAD_DIM), lambda i: (0, 0))],
        out_specs=pl.BlockSpec((tm, M_WIDTH), lambda i: (i, 0)),
        out_shape=jax.ShapeDtypeStruct((m, M_WIDTH), BF16),
        compiler_params=_cparams(("parallel",)),
        name="memattn",
    )(proj, proj, km, vm, gq)


def _dsa_kernel(topk, q_ref, z_ref, iq_ref, iwt_ref,
                k_ref, vt_ref, iklo_ref, ikhi_ref, tri_ref, bound_ref, o_ref,
                sc_ref, bias_ref, lg_ref, acc_ref):
    qb = pl.program_id(1)
    nck = (qb * TQ + TQ + CK - 1) // CK
    nbig = (qb * TQ + TQ + CB - 1) // CB
    t_lane = qb * TQ + lax.broadcasted_iota(jnp.int32, (1, TQ), 1)
    gw = B_GROUP * TQ

    def fold8(a):
        return a.reshape(a.shape[0] // 8, 8, a.shape[1])

    def grouped_loop(trips, body, init, per_step=2, start=0, whole_groups=False):
        carry, done = init, start
        while per_step >= 1:
            def step(i, c, first=done, n=per_step):
                if whole_groups:
                    return body(first + n * i, n, c)
                for u in range(n):
                    c = body(first + n * i + u, c)
                return c

            steps = (trips - done) // per_step
            carry = lax.fori_loop(0, steps, step, carry)
            done = done + steps * per_step
            per_step //= 2
        return carry

    wt = iwt_ref[0]

    def idx_body(c, carry):
        mn8, mx8, s1, s2 = carry
        off = pl.multiple_of(c * CK, CK)
        acc = jnp.zeros((CK, TQ), F32)
        for jj in range(IDX_HEADS // 4):
            rhs = iq_ref[0, jj]
            for half, keys_ref in enumerate((iklo_ref, ikhi_ref)):
                d = _dot_nt(keys_ref[0, pl.ds(off, CK), :], rhs)
                ha, hb = 4 * jj + half, 4 * jj + 2 + half
                acc = (acc + jnp.maximum(d[:, :TQ], 0.0) * wt[ha:ha + 1, :]
                       + jnp.maximum(d[:, TQ:], 0.0) * wt[hb:hb + 1, :])
        key = off + lax.broadcasted_iota(jnp.int32, (CK, TQ), 0)
        causal = key <= t_lane
        sc = jnp.where(causal, acc, -jnp.inf)
        sc_ref[pl.ds(off, CK), :] = sc
        live = jnp.where(causal, acc, 0.0)
        mn8 = jnp.minimum(mn8, jnp.min(fold8(jnp.where(causal, acc, jnp.inf)), axis=0))
        mx8 = jnp.maximum(mx8, jnp.max(fold8(sc), axis=0))
        s1 = s1 + jnp.sum(fold8(live), axis=0)
        s2 = s2 + jnp.sum(fold8(live * live), axis=0)
        return mn8, mx8, s1, s2

    zero8 = jnp.zeros((8, TQ), F32)
    stats = (jnp.full((8, TQ), jnp.inf, F32), jnp.full((8, TQ), -jnp.inf, F32), zero8, zero8)
    mn8, mx8, s1, s2 = grouped_loop(nck, idx_body, stats, per_step=4)
    row_min = jnp.min(mn8, axis=0, keepdims=True)
    row_max = jnp.max(mx8, axis=0, keepdims=True)

    def fill_body(c, carry):
        sc_ref[pl.ds(pl.multiple_of(c * CK, CK), CK), :] = jnp.full((CK, TQ), -jnp.inf, F32)
        return carry

    lax.fori_loop(nck, nbig * (CB // CK), fill_body, 0)

    def count_rows(pred):
        rows = 8 * COUNT_WAYS

        def body(c, cnt):
            off = pl.multiple_of(c * CB, CB)
            for r in range(CB // rows):
                cnt = cnt + jnp.where(pred(sc_ref[pl.ds(off + r * rows, rows), :]), 1.0, 0.0)
            return cnt

        cnt = lax.fori_loop(0, nbig, body, jnp.zeros((rows, TQ), F32))
        return jnp.sum(cnt, axis=0, keepdims=True)

    kf = float(topk)
    n_valid = (t_lane + 1).astype(F32)
    all_rows = (t_lane + 1) <= topk

    def search_pass(st, probe, stuck):
        lo, hi, clo, chi, thr, fin, tie = st
        cnt = count_rows(lambda blk: blk >= probe)
        active = fin < 0.5
        hit = cnt == kf
        end_thr = jnp.where(jnp.logical_and(stuck, cnt < kf), lo, probe)
        ends = jnp.logical_and(active, jnp.logical_or(hit, stuck))
        thr = jnp.where(ends, end_thr, thr)
        tie = jnp.where(jnp.logical_and(ends, jnp.logical_not(hit)), 1.0, tie)
        fin = jnp.where(ends, 1.0, fin)
        up = jnp.logical_and(active, cnt > kf)
        dn = jnp.logical_and(active, cnt < kf)
        return (jnp.where(up, probe, lo), jnp.where(dn, probe, hi), jnp.where(up, cnt, clo),
                jnp.where(dn, cnt, chi), thr, fin, tie)

    def next_probe(st, halve=False):
        lo, hi, clo, chi = st[:4]
        frac = jnp.clip((clo - kf + 0.5) / (clo - chi + 1.0), SEARCH_MARGIN, 1.0 - SEARCH_MARGIN)
        probe = lo + (hi - lo) * frac
        outside = jnp.logical_or(jnp.logical_or(probe <= lo, probe >= hi), halve)
        probe = jnp.where(outside, lo * 0.5 + hi * 0.5, probe)
        stuck = jnp.logical_or(probe <= lo, probe >= hi)
        return jnp.where(stuck, hi, probe), stuck

    mean = jnp.sum(s1, axis=0, keepdims=True) / n_valid
    var = jnp.maximum(jnp.sum(s2, axis=0, keepdims=True) / n_valid - mean * mean, 0.0)
    tail = jnp.clip(kf / n_valid, 1e-6, 1.0 - 1e-6)
    tq = jnp.sqrt(-2.0 * jnp.log(jnp.minimum(tail, 1.0 - tail)))
    zq = tq - ((0.010328 * tq + 0.802853) * tq + 2.515517) / (
        ((0.001308 * tq + 0.189269) * tq + 1.432788) * tq + 1.0)
    guess = mean + jnp.where(tail < 0.5, zq, -zq) * jnp.sqrt(var)
    inside = jnp.logical_and(guess >= row_min, guess <= row_max)
    probe0 = jnp.where(inside, guess, row_min * 0.5 + row_max * 0.5)

    ones = jnp.ones((1, TQ), F32)
    st = (row_min, row_max, n_valid, 0.0 * ones, jnp.where(all_rows, -F32_MAX, row_max),
          jnp.where(all_rows, 1.0, 0.0), 0.0 * ones)
    st = search_pass(st, probe0, probe0 < row_min)

    def fixed_body(i, st):
        return search_pass(st, *next_probe(st))

    st = lax.fori_loop(0, SEARCH_FIXED_PASSES, fixed_body, st)

    def more_cond(c):
        return jnp.logical_and(c[0] < MAX_SEARCH_ITERS, jnp.min(c[1][5]) < 0.5)

    def more_body(c):
        return c[0] + 1, search_pass(c[1], *next_probe(c[1], c[0] >= SEARCH_INTERP_PASSES))

    _, st = lax.while_loop(more_cond, more_body, (jnp.int32(0), st))
    thr, tie = st[4], st[6]

    logit_bound = bound_ref[0, 0]
    sel_bias = jnp.where(logit_bound <= MAX_SHIFT_BOUND, -logit_bound, 0.0)

    def plain_mask():
        def body(c, carry):
            off = pl.multiple_of(c * CB, CB)
            bias_ref[pl.ds(off, CB), :] = jnp.where(sc_ref[pl.ds(off, CB), :] >= thr, sel_bias, NEG_BIG)
            return carry

        lax.fori_loop(0, nbig, body, 0)

    def tied_mask():
        n_gt = count_rows(lambda blk: blk > thr)
        need = jnp.where(all_rows, F32_MAX, kf - n_gt)

        def body(c, seen):
            off = pl.multiple_of(c * CB, CB)
            blk = sc_ref[pl.ds(off, CB), :]
            eq = blk == thr
            pref = jnp.dot(tri_ref[...], jnp.where(eq, 1.0, 0.0).astype(BF16),
                           preferred_element_type=F32) + seen
            keep = jnp.logical_or(blk > thr, jnp.logical_and(eq, pref <= need))
            bias_ref[pl.ds(off, CB), :] = jnp.where(keep, sel_bias, NEG_BIG)
            return pref[CB - 1:CB, :]

        lax.fori_loop(0, nbig, body, jnp.zeros((1, TQ), F32))

    lax.cond(jnp.max(tie) > 0.5, tied_mask, plain_mask)

    def masked_logits(n, off):
        b = bias_ref[pl.ds(off, CB), :]
        qn = q_ref[0, n * B_GROUP:(n + 1) * B_GROUP].reshape(gw, B_HEAD_DIM)
        lg = _dot_nt(k_ref[0, n, pl.ds(off, CB), :], qn)
        return jnp.concatenate([lg[:, g * TQ:(g + 1) * TQ] + b for g in range(B_GROUP)], axis=1)

    for n in range(B_KV_HEADS):
        acc_ref[...] = jnp.zeros(acc_ref.shape, F32)

        def one_pass(n=n):
            def body(first, count, l8):
                pv = None
                for u in range(count):
                    c = first + u
                    p = jnp.exp2(masked_logits(n, pl.multiple_of(c * CB, CB)))
                    d = jnp.dot(vt_ref[0, n, c], p.astype(BF16), preferred_element_type=F32)
                    pv = d if pv is None else pv + d
                    l8 = l8 + jnp.sum(fold8(p), axis=0)
                acc_ref[...] += pv
                return l8

            return grouped_loop(nbig, body, jnp.zeros((8, gw), F32), per_step=4, whole_groups=True)

        def two_pass(n=n):
            def logit_body(c, mx8):
                off = pl.multiple_of(c * CB, CB)
                lg = masked_logits(n, off)
                lg_ref[pl.ds(off, CB), :] = lg
                return jnp.maximum(mx8, jnp.max(fold8(lg), axis=0))

            mx8 = grouped_loop(nbig, logit_body, jnp.full((8, gw), NEG_BIG, F32))
            m = jnp.max(mx8, axis=0, keepdims=True)

            def pv_body(c, l8):
                off = pl.multiple_of(c * CB, CB)
                p = jnp.exp2(lg_ref[pl.ds(off, CB), :] - m)
                acc_ref[...] += jnp.dot(vt_ref[0, n, c], p.astype(BF16), preferred_element_type=F32)
                return l8 + jnp.sum(fold8(p), axis=0)

            return grouped_loop(nbig, pv_body, jnp.zeros((8, gw), F32))

        l8 = lax.cond(logit_bound <= MAX_SHIFT_BOUND, one_pass, two_pass)
        o_t = acc_ref[...] / jnp.sum(l8, axis=0, keepdims=True)
        for g in range(B_GROUP):
            cols = slice((n * B_GROUP + g) * B_HEAD_DIM, (n * B_GROUP + g + 1) * B_HEAD_DIM)
            o = o_t[:, g * TQ:(g + 1) * TQ].T
            o_ref[:, cols] = (o * jax.nn.silu(z_ref[:, cols].astype(F32))).astype(BF16)


def _dsa(proj, q, iq, iwt, k, vt, iklo, ikhi, gq, gk, bsz, s):
    nq = s // TQ
    bound = (1.02 * B_HEAD_DIM ** 0.5 * LOG2E * jnp.max(jnp.abs(gq)) * jnp.max(jnp.abs(gk))).reshape(1, 1)
    topk = min(TOPK_MAX, s // 4)
    gw = B_GROUP * TQ
    blk4 = lambda a: pl.BlockSpec((1,) + a.shape[1:], lambda b, i: (b * nq + i, 0, 0, 0))
    ik = pl.BlockSpec((1, s, LANES), lambda b, i: (b, 0, 0))
    tri = jnp.tril(jnp.ones((CB, CB), BF16))
    return pl.pallas_call(
        functools.partial(_dsa_kernel, topk),
        grid=(bsz, nq),
        in_specs=[blk4(q), pl.BlockSpec((TQ, BLK), lambda b, i: (b * nq + i, COL_BZ)), blk4(iq),
                  pl.BlockSpec((1, IDX_HEADS, TQ), lambda b, i: (b, 0, i)),
                  pl.BlockSpec((1, B_KV_HEADS, s, B_HEAD_DIM), lambda b, i: (b, 0, 0, 0)),
                  pl.BlockSpec((1, B_KV_HEADS, s // CB, B_HEAD_DIM, CB), lambda b, i: (b, 0, 0, 0, 0)),
                  ik, ik,
                  pl.BlockSpec((CB, CB), lambda b, i: (0, 0)),
                  pl.BlockSpec(memory_space=pltpu.SMEM)],
        out_specs=pl.BlockSpec((TQ, B_WIDTH), lambda b, i: (b * nq + i, 0)),
        out_shape=jax.ShapeDtypeStruct((bsz * s, B_WIDTH), BF16),
        scratch_shapes=[
            pltpu.VMEM((s, TQ), F32),
            pltpu.VMEM((s, TQ), F32),
            pltpu.VMEM((s, gw), F32),
            pltpu.VMEM((B_HEAD_DIM, gw), F32),
        ],
        compiler_params=_cparams(("parallel", "arbitrary")),
        name="dsa",
    )(q, proj, iq, iwt, k, vt, iklo, ikhi, tri, bound)


def _merge_kernel(ta_ref, tb_ref, tm_ref, ga_ref, gb_ref, gm_ref, x_ref,
                  wa_hbm, wb_hbm, wm_hbm, wo_hbm, o_ref,
                  wa_ref, wb_ref, wm_ref, wo_ref, stage_ref, sem):
    @pl.when(pl.program_id(0) == 0)
    def _():
        chunks = [(src, dst, r) for src, dst in ((wa_hbm, wa_ref), (wb_hbm, wb_ref), (wm_hbm, wm_ref),
                                                 (wo_hbm, wo_ref))
                  for r in range(0, src.shape[0], MERGE_STAGE_ROWS)]

        def copy(j):
            src, _, r = chunks[j]
            return pltpu.make_async_copy(src.at[pl.ds(r, MERGE_STAGE_ROWS), :], stage_ref.at[j % 2],
                                         sem.at[j % 2])

        copy(0).start()
        for j, (_, dst, r) in enumerate(chunks):
            if j + 1 < len(chunks):
                copy(j + 1).start()
            copy(j).wait()
            dst[pl.ds(r, MERGE_STAGE_ROWS), :] = stage_ref[j % 2].astype(BF16)

    def branch(t_ref, g_ref, w_ref):
        y = jnp.dot(t_ref[...], w_ref[...], preferred_element_type=F32)
        return jax.nn.sigmoid(g_ref[...].astype(F32)) * y

    merged = branch(ta_ref, ga_ref, wa_ref) + branch(tb_ref, gb_ref, wb_ref) + branch(tm_ref, gm_ref, wm_ref)
    o_ref[...] = x_ref[...] + jnp.dot(merged.astype(BF16), wo_ref[...], preferred_element_type=F32)


def _merge(ta, tb, tmem, proj, x2, wa, wb, wm, wo):
    m = x2.shape[0]
    tm = min(256, m)
    act = pl.BlockSpec((tm, BLK), lambda i: (i, 0))
    gate = lambda c: pl.BlockSpec((tm, D_MODEL), lambda i: (i, c // 2))
    wide = pl.BlockSpec((tm, D_MODEL), lambda i: (i, 0))
    hbm = pl.BlockSpec(memory_space=pl.ANY)
    return pl.pallas_call(
        _merge_kernel,
        grid=(m // tm,),
        in_specs=[act, act, act, gate(COL_GA), gate(COL_GB), gate(COL_GM), wide, hbm, hbm, hbm, hbm],
        out_specs=wide,
        out_shape=jax.ShapeDtypeStruct((m, D_MODEL), F32),
        scratch_shapes=[pltpu.VMEM(w.shape, BF16) for w in (wa, wb, wm, wo)]
        + [pltpu.VMEM((2, MERGE_STAGE_ROWS, D_MODEL), F32), pltpu.SemaphoreType.DMA((2,))],
        compiler_params=_cparams(("arbitrary",)),
        name="merge",
    )(ta, tb, tmem, proj, proj, proj, x2, wa, wb, wm, wo)


def _rope_freqs():
    inv = lambda half: ROPE_THETA ** (-np.arange(half, dtype=np.float32) / half)
    freq = np.zeros((1, LANES), np.float32)
    hb, hi = B_HEAD_DIM // 8, IDX_DIM // 8
    freq[0, :hb] = inv(hb)
    freq[0, hb:hb + hi] = inv(hi)
    return jnp.asarray(freq)


def _layer(x, mem, positions, norm_gain, w_in, gmlp_ln_gain, gmlp_ln_bias, spatial_w, spatial_b,
           w_branch_a, q_norm_gain, k_norm_gain, idx_k_ln_gain, idx_k_ln_bias, w_branch_b,
           mem_norm_gain, w_mem_kv, mem_q_norm_gain, mem_k_norm_gain, w_branch_m, w_out):
    bsz, s, _ = x.shape
    m = bsz * s
    row = lambda a: a.reshape(1, -1).astype(F32)
    pad_lanes = lambda a: jnp.pad(a.reshape(1, -1).astype(F32), ((0, 0), (0, LANES - a.shape[-1])))
    x2 = x.reshape(m, D_MODEL)

    w_t = w_in.T
    h, q, iq, k, vt, iklo, ikhi, iwt = _kvprep(
        x, row(norm_gain), w_t, positions.reshape(bsz, s, 1).astype(jnp.int32), _rope_freqs(),
        row(k_norm_gain[_HEAD_PERM]), row(q_norm_gain[_HEAD_PERM]),
        pad_lanes(idx_k_ln_gain), pad_lanes(idx_k_ln_bias))
    proj = _inproj(h, w_t)

    t_a = _gmlp(proj, row(gmlp_ln_gain), row(gmlp_ln_bias), spatial_w.astype(F32), spatial_b.T.astype(F32))
    km, vm = _memkv(mem, row(mem_norm_gain), w_mem_kv, row(mem_k_norm_gain))
    t_m = _memattn(proj, km, vm, row(mem_q_norm_gain), s)
    t_b = _dsa(proj, q, iq, iwt, k, vt, iklo, ikhi, row(q_norm_gain), row(k_norm_gain), bsz, s)

    out = _merge(t_a, t_b, t_m, proj, x2, w_branch_a, w_branch_b, w_branch_m, w_out)
    return out.reshape(bsz, s, D_MODEL)


def kernel(x, mem, positions, norm_gain, w_in, gmlp_ln_gain, gmlp_ln_bias, spatial_w, spatial_b, w_branch_a, q_norm_gain, k_norm_gain, idx_k_ln_gain, idx_k_ln_bias, w_branch_b, mem_norm_gain, w_mem_kv, mem_q_norm_gain, mem_k_norm_gain, w_branch_m, w_out):
    for l in range(norm_gain.shape[0]):
        x = _layer(x, mem, positions, norm_gain[l], w_in[l], gmlp_ln_gain[l], gmlp_ln_bias[l],
                   spatial_w[l], spatial_b[l], w_branch_a[l], q_norm_gain[l], k_norm_gain[l],
                   idx_k_ln_gain[l], idx_k_ln_bias[l], w_branch_b[l], mem_norm_gain[l], w_mem_kv[l],
                   mem_q_norm_gain[l], mem_k_norm_gain[l], w_branch_m[l], w_out[l])
    return x
```

```python
import functools

import numpy as np
import jax
import jax.numpy as jnp
from jax import lax
from jax.experimental import pallas as pl
from jax.experimental.pallas import tpu as pltpu

F32 = jnp.float32
BF16 = jnp.bfloat16

D_MODEL = 2048
ROPE_THETA = 500000.0
EPS = 1e-6
A_GROUPS = 8
A_GROUP_DIM = 128
A_WIDTH = A_GROUPS * A_GROUP_DIM
CHUNK = 128
B_HEADS = 8
B_KV_HEADS = 2
B_GROUP = B_HEADS // B_KV_HEADS
B_HEAD_DIM = 128
B_WIDTH = B_HEADS * B_HEAD_DIM
IDX_HEADS = 16
IDX_DIM = 64
TOPK_MAX = 256
M_HEADS = 4
M_HEAD_DIM = 256
M_WIDTH = M_HEADS * M_HEAD_DIM

SPLIT_SIZES = (
    A_WIDTH, A_WIDTH, A_WIDTH,
    B_WIDTH, B_KV_HEADS * B_HEAD_DIM, B_KV_HEADS * B_HEAD_DIM, B_WIDTH,
    IDX_HEADS * IDX_DIM, IDX_DIM, IDX_HEADS,
    M_WIDTH, M_WIDTH,
    D_MODEL, D_MODEL, D_MODEL,
)

LANES = 128
BLK = 1024
COL_AU, COL_AV, COL_AZ, COL_BZ, COL_MQ, COL_MZ = range(6)
COL_GA, COL_GB, COL_GM = 6, 8, 10
NB_MAIN = 12
_OFFS = [int(o) for o in np.concatenate([[0], np.cumsum(SPLIT_SIZES)])]
ROW_ALIGN = 16
MAIN_START = ([_OFFS[i] for i in (0, 1, 2, 6, 10, 11)]
              + [_OFFS[i] + d for i in (12, 13, 14) for d in (0, BLK)])
assert all(s % ROW_ALIGN == 0 for s in MAIN_START)
Q_START, IQ_START, HALF_BLK = _OFFS[3], _OFFS[7], BLK // 2
SMALL_KV_START, SMALL_KV_WIDTH = _OFFS[4], _OFFS[6] - _OFFS[4]
SMALL_IDX_START = _OFFS[8]
assert Q_START % BLK == 0 and IQ_START % HALF_BLK == 0
assert SMALL_KV_START % SMALL_KV_WIDTH == 0 and SMALL_IDX_START % LANES == 0
assert IDX_DIM + IDX_HEADS <= LANES and _OFFS[9] == SMALL_IDX_START + IDX_DIM

VMEM_LIMIT = 56 * 1024 * 1024
LOG2E = 1.4426950408889634
NEG_BIG = -1e30
F32_MAX = 3.4028234663852886e38

INPROJ_TM, INPROJ_TN = 2048, 1024
KVPREP_TP = 512
TQ = 256
CK = 256
CB = 512
COUNT_WAYS = 8
SEARCH_MARGIN = 0.02
SEARCH_FIXED_PASSES = 11
SEARCH_INTERP_PASSES = 13
MAX_SEARCH_ITERS = 400
MERGE_STAGE_ROWS = 256
MERGE_STAGE_SLOTS = 4
MAX_SHIFT_BOUND = 48.0


def _cparams(sem):
    return pltpu.CompilerParams(dimension_semantics=sem, vmem_limit_bytes=VMEM_LIMIT)


def _dot_nt(a, b):
    return lax.dot_general(a, b, (((1,), (1,)), ((), ())), preferred_element_type=F32)


_HEAD_PERM = np.concatenate([np.arange(0, 16), np.arange(32, 80), np.arange(16, 32), np.arange(80, 128)])
assert B_HEAD_DIM == LANES and B_HEAD_DIM // 8 == 16
_PAIR_PERM = ((0, 8), (64, 72), (16, 64), (8, 16), (72, 80), (80, 128))
assert 2 * IDX_DIM == LANES and IDX_DIM // 8 == 8


def _rope_swapped(x, cos_t, sin_t):
    return x * cos_t + pltpu.roll(x, LANES // 2, 1) * sin_t


def _permute_head_rows(src_ref, dst_ref, hd):
    hb, mid = B_HEAD_DIM // 8, B_HEAD_DIM // 2
    base = hd * B_HEAD_DIM
    for src, dst, n in ((0, 0, hb), (2 * hb, hb, mid - hb), (hb, mid, hb), (mid + hb, mid + hb, mid - hb)):
        dst_ref[base + dst:base + dst + n, :] = src_ref[base + src:base + src + n, :].astype(BF16)


def _rms_rows(x, gain):
    ms = jnp.mean(x * x, axis=-1, keepdims=True)
    return (x * lax.rsqrt(ms + EPS) * gain).astype(BF16)


def _inproj_kernel(tab_ref, h_ref, wt_ref, proj_ref):
    del tab_ref
    proj_ref[...] = _dot_nt(h_ref[...], wt_ref[...].astype(BF16)).astype(BF16)


def _inproj(h, w_t):
    m = h.shape[0]
    tm = min(INPROJ_TM, m)
    starts = [s + d for s in MAIN_START for d in range(0, BLK, INPROJ_TN)]
    tab = jnp.asarray(np.array([s // ROW_ALIGN for s in starts], np.int32))
    return pl.pallas_call(
        _inproj_kernel,
        grid_spec=pltpu.PrefetchScalarGridSpec(
            num_scalar_prefetch=1,
            grid=(m // tm, len(starts)),
            in_specs=[
                pl.BlockSpec((tm, D_MODEL), lambda i, n, tab: (i, 0)),
                pl.BlockSpec((pl.Element(INPROJ_TN), pl.Element(D_MODEL)),
                             lambda i, n, tab: (tab[n] * ROW_ALIGN, 0)),
            ],
            out_specs=pl.BlockSpec((tm, INPROJ_TN), lambda i, n, tab: (i, n)),
        ),
        out_shape=jax.ShapeDtypeStruct((m, NB_MAIN * BLK), BF16),
        compiler_params=_cparams(("parallel", "arbitrary")),
        name="inproj",
    )(tab, h, w_t)


def _kvprep_kernel(x_ref, g_ref, wkv_ref, widx_ref, wq_ref, wiqa_ref, wiqb_ref, pos_ref, fc_ref,
                   gk_ref, gq_ref, lng_ref, lnb_ref,
                   h_ref, q_ref, iq_ref, k_ref, vt_ref, iklo_ref, ikhi_ref, iwt_ref,
                   wkv_bf_ref, widx_bf_ref, wq_bf_ref, wiq_bf_ref):
    @pl.when(jnp.logical_and(pl.program_id(0) == 0, pl.program_id(1) == 0))
    def _():
        wkv_bf_ref[...] = wkv_ref[...].astype(BF16)
        for hd in range(B_KV_HEADS):
            _permute_head_rows(wkv_ref, wkv_bf_ref, hd)
        widx_bf_ref[...] = widx_ref[...].astype(BF16)
        for hd in range(B_HEADS):
            _permute_head_rows(wq_ref, wq_bf_ref, hd)
        for j in range(IDX_HEADS // 2):
            src = (wiqa_ref, wiqb_ref)[j * LANES // HALF_BLK]
            base = j * LANES % HALF_BLK
            rows = jnp.concatenate([src[base + a:base + b, :] for a, b in _PAIR_PERM], axis=0)
            wiq_bf_ref[j * LANES:(j + 1) * LANES, :] = rows.astype(BF16)

    h = _rms_rows(x_ref[...], g_ref[...])
    h_ref[...] = h
    kv = _dot_nt(h, wkv_bf_ref[...])
    ikp = _dot_nt(h, widx_bf_ref[...])

    hb, hi = B_HEAD_DIM // 8, IDX_DIM // 8
    ang = pos_ref[0].astype(F32) * fc_ref[...]
    cos_c, sin_c = jnp.cos(ang), jnp.sin(ang)
    lane = lax.broadcasted_iota(jnp.int32, ang.shape, 1)
    second = jnp.logical_and(lane >= LANES // 2, lane < LANES // 2 + hb)
    cos_b = jnp.where(lane < hb, cos_c, jnp.where(second, pltpu.roll(cos_c, LANES // 2, 1), 1.0))
    sin_b = jnp.where(lane < hb, -sin_c, jnp.where(second, pltpu.roll(sin_c, LANES // 2, 1), 0.0))
    cos_i = jnp.ones_like(cos_c)
    sin_i = jnp.zeros_like(sin_c)
    for first, sign in ((0, -1.0), (hi, -1.0), (LANES // 2, 1.0), (LANES // 2 + hi, 1.0)):
        here = jnp.logical_and(lane >= first, lane < first + hi)
        shift = (first - hb) % LANES
        cos_i = jnp.where(here, pltpu.roll(cos_c, shift, 1), cos_i)
        sin_i = jnp.where(here, sign * pltpu.roll(sin_c, shift, 1), sin_i)
    nblk = q_ref.shape[0]
    gain = gq_ref[...] * (B_HEAD_DIM ** -0.5 * LOG2E)
    qp = _dot_nt(h, wq_bf_ref[...])
    for hd in range(B_HEADS):
        slab = qp[:, hd * B_HEAD_DIM:(hd + 1) * B_HEAD_DIM]
        r = lax.rsqrt(jnp.mean(slab * slab, axis=-1, keepdims=True) + EPS)
        qr = _rope_swapped(slab * r * gain, cos_b, sin_b).astype(BF16)
        for blk in range(nblk):
            q_ref[blk, hd] = qr[blk * TQ:(blk + 1) * TQ, :]
    iqp = _dot_nt(h, wiq_bf_ref[...])
    for j in range(IDX_HEADS // 2):
        ir = _rope_swapped(iqp[:, j * LANES:(j + 1) * LANES], cos_i, sin_i).astype(BF16)
        for blk in range(nblk):
            iq_ref[blk, j // 2, (j % 2) * TQ:(j % 2 + 1) * TQ, :] = ir[blk * TQ:(blk + 1) * TQ, :]

    for n in range(B_KV_HEADS):
        kh = kv[:, n * B_HEAD_DIM:(n + 1) * B_HEAD_DIM]
        r = lax.rsqrt(jnp.mean(kh * kh, axis=-1, keepdims=True) + EPS)
        kn = kh * r * gk_ref[...]
        k_ref[0, n] = _rope_swapped(kn, cos_b, sin_b).astype(BF16)
        vt = kv[:, (B_KV_HEADS + n) * B_HEAD_DIM:(B_KV_HEADS + n + 1) * B_HEAD_DIM].T
        for c in range(vt.shape[1] // CB):
            vt_ref[0, n, c] = vt[:, c * CB:(c + 1) * CB].astype(BF16)

    lane = lax.broadcasted_iota(jnp.int32, ikp.shape, 1)
    live = lane < IDX_DIM
    mu = jnp.sum(jnp.where(live, ikp, 0.0), axis=-1, keepdims=True) * (1.0 / IDX_DIM)
    d = jnp.where(live, ikp - mu, 0.0)
    var = jnp.sum(d * d, axis=-1, keepdims=True) * (1.0 / IDX_DIM)
    y = d * lax.rsqrt(var + EPS) * lng_ref[...] + lnb_ref[...]
    half = LANES // 2
    x2_a = jnp.logical_and(lane >= half, lane < half + hi)
    keep = jnp.logical_or(lane < hi, jnp.logical_and(lane >= 2 * hi, lane < half))
    lo = _rope_swapped(jnp.where(x2_a, pltpu.roll(y, half - hi, 1), jnp.where(keep, y, 0.0)), cos_i, sin_i)
    x_b = jnp.logical_or(jnp.logical_and(lane >= hi, lane < 2 * hi),
                         jnp.logical_and(lane >= half + hi, lane < half + 2 * hi))
    hi_slots = jnp.where(x_b, pltpu.roll(lo, hi, 1), jnp.where(lane >= half + 2 * hi, pltpu.roll(lo, half, 1), 0.0))
    iklo_ref[0] = lo.astype(BF16)
    ikhi_ref[0] = hi_slots.astype(BF16)

    iw = ikp * (IDX_DIM ** -0.5 * IDX_HEADS ** -0.5)
    iwt_ref[0] = iw.T[IDX_DIM:IDX_DIM + IDX_HEADS, :]


def _kvprep(x3, gain, w_t, pos3, freqs, gk, gq, lng, lnb):
    bsz, s, _ = x3.shape
    tp = min(KVPREP_TP, s)
    per_b, nblk = s // tp, tp // TQ
    row = lambda b, i: (b, i, 0)
    const = lambda b, i: (0, 0)
    tab_spec = pl.BlockSpec((1, LANES), const)
    wrows = lambda rows, start: pl.BlockSpec((rows, D_MODEL), lambda b, i: (start // rows, 0),
                                             pipeline_mode=pl.Buffered(1))
    out_h = pl.BlockSpec((tp, D_MODEL), lambda b, i: (b * per_b + i, 0))
    out_q = pl.BlockSpec((nblk, B_HEADS, TQ, B_HEAD_DIM), lambda b, i: (b * per_b + i, 0, 0, 0))
    out_iq = pl.BlockSpec((nblk, IDX_HEADS // 4, 2 * TQ, LANES), lambda b, i: (b * per_b + i, 0, 0, 0))
    out_tok = pl.BlockSpec((1, tp, LANES), row)
    out_k = pl.BlockSpec((1, B_KV_HEADS, tp, B_HEAD_DIM), lambda b, i: (b, 0, i, 0))
    out_vt = pl.BlockSpec((1, B_KV_HEADS, tp // CB, B_HEAD_DIM, CB), lambda b, i: (b, 0, i, 0, 0))
    out_iwt = pl.BlockSpec((1, IDX_HEADS, tp), lambda b, i: (b, 0, i))
    nq = bsz * s // TQ
    return pl.pallas_call(
        _kvprep_kernel,
        grid=(bsz, per_b),
        in_specs=[pl.BlockSpec((None, tp, D_MODEL), row), pl.BlockSpec((1, D_MODEL), const),
                  wrows(SMALL_KV_WIDTH, SMALL_KV_START), wrows(LANES, SMALL_IDX_START),
                  wrows(BLK, Q_START), wrows(HALF_BLK, IQ_START), wrows(HALF_BLK, IQ_START + HALF_BLK),
                  pl.BlockSpec((1, tp, 1), row),
                  tab_spec, tab_spec, tab_spec, tab_spec, tab_spec],
        out_specs=[out_h, out_q, out_iq, out_k, out_vt, out_tok, out_tok, out_iwt],
        out_shape=[
            jax.ShapeDtypeStruct((bsz * s, D_MODEL), BF16),
            jax.ShapeDtypeStruct((nq, B_HEADS, TQ, B_HEAD_DIM), BF16),
            jax.ShapeDtypeStruct((nq, IDX_HEADS // 4, 2 * TQ, LANES), BF16),
            jax.ShapeDtypeStruct((bsz, B_KV_HEADS, s, B_HEAD_DIM), BF16),
            jax.ShapeDtypeStruct((bsz, B_KV_HEADS, s // CB, B_HEAD_DIM, CB), BF16),
            jax.ShapeDtypeStruct((bsz, s, LANES), BF16),
            jax.ShapeDtypeStruct((bsz, s, LANES), BF16),
            jax.ShapeDtypeStruct((bsz, IDX_HEADS, s), F32),
        ],
        scratch_shapes=[pltpu.VMEM((SMALL_KV_WIDTH, D_MODEL), BF16), pltpu.VMEM((LANES, D_MODEL), BF16),
                        pltpu.VMEM((BLK, D_MODEL), BF16), pltpu.VMEM((BLK, D_MODEL), BF16)],
        compiler_params=_cparams(("arbitrary", "arbitrary")),
        name="kvprep",
    )(x3, gain, w_t, w_t, w_t, w_t, w_t, pos3, freqs, gk, gq, lng, lnb)


def _gelu_tanh(x):
    c = -2.0 * (2.0 / np.pi) ** 0.5 * LOG2E
    return x / (1.0 + jnp.exp2(x * (x * x * (0.044715 * c) + c)))


def _gmlp_kernel(u_ref, v_ref, z_ref, lng_ref, lnb_ref, ws_ref, sbt_ref, o_ref):
    tm = u_ref.shape[0]
    u = _gelu_tanh(u_ref[...].astype(F32))
    v = _gelu_tanh(v_ref[...].astype(F32))
    mu = jnp.mean(v, axis=-1, keepdims=True)
    d = v - mu
    var = jnp.mean(d * d, axis=-1, keepdims=True)
    vn = (d * lax.rsqrt(var + EPS) * lng_ref[...] + lnb_ref[...]).astype(BF16)
    gate = u * jax.nn.silu(z_ref[...].astype(F32))
    tri = (lax.broadcasted_iota(jnp.int32, (CHUNK, CHUNK), 1)
           <= lax.broadcasted_iota(jnp.int32, (CHUNK, CHUNK), 0))
    for g in range(A_GROUPS):
        wg = jnp.where(tri, ws_ref[g], 0.0).astype(BF16)
        bias = sbt_ref[:, g:g + 1]
        cols = slice(g * A_GROUP_DIM, (g + 1) * A_GROUP_DIM)
        for c in range(tm // CHUNK):
            rows = slice(c * CHUNK, (c + 1) * CHUNK)
            sg = jnp.dot(wg, vn[rows, cols], preferred_element_type=F32) + bias
            o_ref[rows, cols] = (gate[rows, cols] * sg).astype(BF16)


def _gmlp(proj, lng, lnb, ws, sbt):
    m = proj.shape[0]
    tm = min(512, m)
    col = lambda c: pl.BlockSpec((tm, BLK), lambda i: (i, c))
    full = lambda shape: pl.BlockSpec(shape, lambda i: (0,) * len(shape))
    return pl.pallas_call(
        _gmlp_kernel,
        grid=(m // tm,),
        in_specs=[col(COL_AU), col(COL_AV), col(COL_AZ), full((1, A_WIDTH)), full((1, A_WIDTH)),
                  full((A_GROUPS, CHUNK, CHUNK)), full((CHUNK, A_GROUPS))],
        out_specs=pl.BlockSpec((tm, A_WIDTH), lambda i: (i, 0)),
        out_shape=jax.ShapeDtypeStruct((m, A_WIDTH), BF16),
        compiler_params=_cparams(("parallel",)),
        name="gmlp",
    )(proj, proj, proj, lng, lnb, ws, sbt)


def _memkv_kernel(mem_ref, g_ref, w_ref, gk_ref, km_ref, vm_ref, wbf_ref):
    @pl.when(pl.program_id(0) == 0)
    def _():
        wbf_ref[...] = w_ref[...].astype(BF16)

    h = _rms_rows(mem_ref[0], g_ref[...])
    kv = jnp.dot(h, wbf_ref[...], preferred_element_type=F32)
    for hd in range(M_HEADS):
        kh = kv[:, hd * M_HEAD_DIM:(hd + 1) * M_HEAD_DIM]
        r = lax.rsqrt(jnp.mean(kh * kh, axis=-1, keepdims=True) + EPS)
        km_ref[0, hd] = (kh * r * gk_ref[...]).astype(BF16)
        vm_ref[0, hd] = kv[:, M_WIDTH + hd * M_HEAD_DIM:M_WIDTH + (hd + 1) * M_HEAD_DIM].astype(BF16)


def _memkv(mem, gain, w_kv, gk):
    bsz, ml, _ = mem.shape
    out = pl.BlockSpec((1, M_HEADS, ml, M_HEAD_DIM), lambda b: (b, 0, 0, 0))
    shp = jax.ShapeDtypeStruct((bsz, M_HEADS, ml, M_HEAD_DIM), BF16)
    return pl.pallas_call(
        _memkv_kernel,
        grid=(bsz,),
        in_specs=[pl.BlockSpec((1, ml, D_MODEL), lambda b: (b, 0, 0)),
                  pl.BlockSpec((1, D_MODEL), lambda b: (0, 0)),
                  pl.BlockSpec((D_MODEL, 2 * M_WIDTH), lambda b: (0, 0), pipeline_mode=pl.Buffered(1)),
                  pl.BlockSpec((1, M_HEAD_DIM), lambda b: (0, 0))],
        out_specs=[out, out],
        out_shape=[shp, shp],
        scratch_shapes=[pltpu.VMEM((D_MODEL, 2 * M_WIDTH), BF16)],
        compiler_params=_cparams(("arbitrary",)),
        name="memkv",
    )(mem, gain, w_kv, gk)


def _memattn_kernel(q_ref, z_ref, km_ref, vm_ref, gq_ref, o_ref):
    qscale = M_HEAD_DIM ** -0.5 * LOG2E
    for hd in range(M_HEADS):
        cols = slice(hd * M_HEAD_DIM, (hd + 1) * M_HEAD_DIM)
        q = q_ref[:, cols].astype(F32)
        r = lax.rsqrt(jnp.mean(q * q, axis=-1, keepdims=True) + EPS)
        qn = (q * r * gq_ref[...] * qscale).astype(BF16)
        lg = _dot_nt(qn, km_ref[0, hd])
        p = jnp.exp2(lg - jnp.max(lg, axis=-1, keepdims=True))
        l = jnp.sum(p, axis=-1, keepdims=True)
        o = jnp.dot(p.astype(BF16), vm_ref[0, hd], preferred_element_type=F32) / l
        o_ref[:, cols] = (o * jax.nn.silu(z_ref[:, cols].astype(F32))).astype(BF16)


def _memattn(proj, km, vm, gq, s):
    m = proj.shape[0]
    tm = min(512, s)
    per_b = s // tm
    ml = km.shape[2]
    kv_spec = pl.BlockSpec((1, M_HEADS, ml, M_HEAD_DIM), lambda i: (i // per_b, 0, 0, 0))
    return pl.pallas_call(
        _memattn_kernel,
        grid=(m // tm,),
        in_specs=[pl.BlockSpec((tm, BLK), lambda i: (i, COL_MQ)),
                  pl.BlockSpec((tm, BLK), lambda i: (i, COL_MZ)),
                  kv_spec, kv_spec,
                  pl.BlockSpec((1, M_HEAD_DIM), lambda i: (0, 0))],
        out_specs=pl.BlockSpec((tm, M_WIDTH), lambda i: (i, 0)),
        out_shape=jax.ShapeDtypeStruct((m, M_WIDTH), BF16),
        compiler_params=_cparams(("parallel",)),
        name="memattn",
    )(proj, proj, km, vm, gq)


def _dsa_kernel(topk, q_ref, z_ref, iq_ref, iwt_ref,
                k_ref, vt_ref, iklo_ref, ikhi_ref, tri_ref, bound_ref, o_ref,
                sc_ref, bias_ref, lg_ref, acc_ref):
    qb = pl.program_id(1)
    nck = (qb * TQ + TQ + CK - 1) // CK
    nbig = (qb * TQ + TQ + CB - 1) // CB
    t_lane = qb * TQ + lax.broadcasted_iota(jnp.int32, (1, TQ), 1)
    gw = B_GROUP * TQ

    def fold8(a):
        return a.reshape(a.shape[0] // 8, 8, a.shape[1])

    def grouped_loop(trips, body, init, per_step=2, start=0, whole_groups=False):
        carry, done = init, start
        while per_step >= 1:
            def step(i, c, first=done, n=per_step):
                if whole_groups:
                    return body(first + n * i, n, c)
                for u in range(n):
                    c = body(first + n * i + u, c)
                return c

            steps = (trips - done) // per_step
            carry = lax.fori_loop(0, steps, step, carry)
            done = done + steps * per_step
            per_step //= 2
        return carry

    wt = iwt_ref[0]

    def idx_body(c, carry):
        mn8, mx8, s1, s2 = carry
        off = pl.multiple_of(c * CK, CK)
        acc = jnp.zeros((CK, TQ), F32)
        for jj in range(IDX_HEADS // 4):
            rhs = iq_ref[0, jj]
            for half, keys_ref in enumerate((iklo_ref, ikhi_ref)):
                d = _dot_nt(keys_ref[0, pl.ds(off, CK), :], rhs)
                ha, hb = 4 * jj + half, 4 * jj + 2 + half
                acc = (acc + jnp.maximum(d[:, :TQ], 0.0) * wt[ha:ha + 1, :]
                       + jnp.maximum(d[:, TQ:], 0.0) * wt[hb:hb + 1, :])
        key = off + lax.broadcasted_iota(jnp.int32, (CK, TQ), 0)
        causal = key <= t_lane
        sc = jnp.where(causal, acc, -jnp.inf)
        sc_ref[pl.ds(off, CK), :] = sc
        live = jnp.where(causal, acc, 0.0)
        mn8 = jnp.minimum(mn8, jnp.min(fold8(jnp.where(causal, acc, jnp.inf)), axis=0))
        mx8 = jnp.maximum(mx8, jnp.max(fold8(sc), axis=0))
        s1 = s1 + jnp.sum(fold8(live), axis=0)
        s2 = s2 + jnp.sum(fold8(live * live), axis=0)
        return mn8, mx8, s1, s2

    zero8 = jnp.zeros((8, TQ), F32)
    stats = (jnp.full((8, TQ), jnp.inf, F32), jnp.full((8, TQ), -jnp.inf, F32), zero8, zero8)
    mn8, mx8, s1, s2 = grouped_loop(nck, idx_body, stats, per_step=4)
    row_min = jnp.min(mn8, axis=0, keepdims=True)
    row_max = jnp.max(mx8, axis=0, keepdims=True)

    def fill_body(c, carry):
        sc_ref[pl.ds(pl.multiple_of(c * CK, CK), CK), :] = jnp.full((CK, TQ), -jnp.inf, F32)
        return carry

    lax.fori_loop(nck, nbig * (CB // CK), fill_body, 0)

    def count_rows(pred):
        rows = 8 * COUNT_WAYS

        def body(c, cnt):
            off = pl.multiple_of(c * CB, CB)
            for r in range(CB // rows):
                cnt = cnt + jnp.where(pred(sc_ref[pl.ds(off + r * rows, rows), :]), 1.0, 0.0)
            return cnt

        cnt = lax.fori_loop(0, nbig, body, jnp.zeros((rows, TQ), F32))
        return jnp.sum(cnt, axis=0, keepdims=True)

    kf = float(topk)
    n_valid = (t_lane + 1).astype(F32)
    all_rows = (t_lane + 1) <= topk

    def search_pass(st, probe, stuck):
        lo, hi, clo, chi, thr, fin, tie = st
        cnt = count_rows(lambda blk: blk >= probe)
        active = fin < 0.5
        hit = cnt == kf
        end_thr = jnp.where(jnp.logical_and(stuck, cnt < kf), lo, probe)
        ends = jnp.logical_and(active, jnp.logical_or(hit, stuck))
        thr = jnp.where(ends, end_thr, thr)
        tie = jnp.where(jnp.logical_and(ends, jnp.logical_not(hit)), 1.0, tie)
        fin = jnp.where(ends, 1.0, fin)
        up = jnp.logical_and(active, cnt > kf)
        dn = jnp.logical_and(active, cnt < kf)
        return (jnp.where(up, probe, lo), jnp.where(dn, probe, hi), jnp.where(up, cnt, clo),
                jnp.where(dn, cnt, chi), thr, fin, tie)

    def next_probe(st, halve=False):
        lo, hi, clo, chi = st[:4]
        frac = jnp.clip((clo - kf + 0.5) / (clo - chi + 1.0), SEARCH_MARGIN, 1.0 - SEARCH_MARGIN)
        probe = lo + (hi - lo) * frac
        outside = jnp.logical_or(jnp.logical_or(probe <= lo, probe >= hi), halve)
        probe = jnp.where(outside, lo * 0.5 + hi * 0.5, probe)
        stuck = jnp.logical_or(probe <= lo, probe >= hi)
        return jnp.where(stuck, hi, probe), stuck

    mean = jnp.sum(s1, axis=0, keepdims=True) / n_valid
    var = jnp.maximum(jnp.sum(s2, axis=0, keepdims=True) / n_valid - mean * mean, 0.0)
    tail = jnp.clip(kf / n_valid, 1e-6, 1.0 - 1e-6)
    tq = jnp.sqrt(-2.0 * jnp.log(jnp.minimum(tail, 1.0 - tail)))
    zq = tq - ((0.010328 * tq + 0.802853) * tq + 2.515517) / (
        ((0.001308 * tq + 0.189269) * tq + 1.432788) * tq + 1.0)
    guess = mean + jnp.where(tail < 0.5, zq, -zq) * jnp.sqrt(var)
    inside = jnp.logical_and(guess >= row_min, guess <= row_max)
    probe0 = jnp.where(inside, guess, row_min * 0.5 + row_max * 0.5)

    ones = jnp.ones((1, TQ), F32)
    st = (row_min, row_max, n_valid, 0.0 * ones, jnp.where(all_rows, -F32_MAX, row_max),
          jnp.where(all_rows, 1.0, 0.0), 0.0 * ones)
    st = search_pass(st, probe0, probe0 < row_min)

    def fixed_body(i, st):
        return search_pass(st, *next_probe(st))

    st = lax.fori_loop(0, SEARCH_FIXED_PASSES, fixed_body, st)

    def more_cond(c):
        return jnp.logical_and(c[0] < MAX_SEARCH_ITERS, jnp.min(c[1][5]) < 0.5)

    def more_body(c):
        return c[0] + 1, search_pass(c[1], *next_probe(c[1], c[0] >= SEARCH_INTERP_PASSES))

    _, st = lax.while_loop(more_cond, more_body, (jnp.int32(0), st))
    thr, tie = st[4], st[6]

    logit_bound = bound_ref[0, 0]
    sel_bias = jnp.where(logit_bound <= MAX_SHIFT_BOUND, -logit_bound, 0.0)

    def plain_mask():
        def body(c, carry):
            off = pl.multiple_of(c * CB, CB)
            bias_ref[pl.ds(off, CB), :] = jnp.where(sc_ref[pl.ds(off, CB), :] >= thr, sel_bias, NEG_BIG)
            return carry

        lax.fori_loop(0, nbig, body, 0)

    def tied_mask():
        n_gt = count_rows(lambda blk: blk > thr)
        need = jnp.where(all_rows, F32_MAX, kf - n_gt)

        def body(c, seen):
            off = pl.multiple_of(c * CB, CB)
            blk = sc_ref[pl.ds(off, CB), :]
            eq = blk == thr
            pref = jnp.dot(tri_ref[...], jnp.where(eq, 1.0, 0.0).astype(BF16),
                           preferred_element_type=F32) + seen
            keep = jnp.logical_or(blk > thr, jnp.logical_and(eq, pref <= need))
            bias_ref[pl.ds(off, CB), :] = jnp.where(keep, sel_bias, NEG_BIG)
            return pref[CB - 1:CB, :]

        lax.fori_loop(0, nbig, body, jnp.zeros((1, TQ), F32))

    lax.cond(jnp.max(tie) > 0.5, tied_mask, plain_mask)

    def masked_logits(n, off):
        b = bias_ref[pl.ds(off, CB), :]
        qn = q_ref[0, n * B_GROUP:(n + 1) * B_GROUP].reshape(gw, B_HEAD_DIM)
        lg = _dot_nt(k_ref[0, n, pl.ds(off, CB), :], qn)
        return jnp.concatenate([lg[:, g * TQ:(g + 1) * TQ] + b for g in range(B_GROUP)], axis=1)

    for n in range(B_KV_HEADS):
        acc_ref[...] = jnp.zeros(acc_ref.shape, F32)

        def one_pass(n=n):
            def body(first, count, l8):
                pv = None
                for u in range(count):
                    c = first + u
                    p = jnp.exp2(masked_logits(n, pl.multiple_of(c * CB, CB)))
                    d = jnp.dot(vt_ref[0, n, c], p.astype(BF16), preferred_element_type=F32)
                    pv = d if pv is None else pv + d
                    l8 = l8 + jnp.sum(fold8(p), axis=0)
                acc_ref[...] += pv
                return l8

            return grouped_loop(nbig, body, jnp.zeros((8, gw), F32), per_step=4, whole_groups=True)

        def two_pass(n=n):
            def logit_body(c, mx8):
                off = pl.multiple_of(c * CB, CB)
                lg = masked_logits(n, off)
                lg_ref[pl.ds(off, CB), :] = lg
                return jnp.maximum(mx8, jnp.max(fold8(lg), axis=0))

            mx8 = grouped_loop(nbig, logit_body, jnp.full((8, gw), NEG_BIG, F32))
            m = jnp.max(mx8, axis=0, keepdims=True)

            def pv_body(c, l8):
                off = pl.multiple_of(c * CB, CB)
                p = jnp.exp2(lg_ref[pl.ds(off, CB), :] - m)
                acc_ref[...] += jnp.dot(vt_ref[0, n, c], p.astype(BF16), preferred_element_type=F32)
                return l8 + jnp.sum(fold8(p), axis=0)

            return grouped_loop(nbig, pv_body, jnp.zeros((8, gw), F32))

        l8 = lax.cond(logit_bound <= MAX_SHIFT_BOUND, one_pass, two_pass)
        o_t = acc_ref[...] / jnp.sum(l8, axis=0, keepdims=True)
        for g in range(B_GROUP):
            cols = slice((n * B_GROUP + g) * B_HEAD_DIM, (n * B_GROUP + g + 1) * B_HEAD_DIM)
            o = o_t[:, g * TQ:(g + 1) * TQ].T
            o_ref[:, cols] = (o * jax.nn.silu(z_ref[:, cols].astype(F32))).astype(BF16)


def _dsa(proj, q, iq, iwt, k, vt, iklo, ikhi, gq, gk, bsz, s):
    nq = s // TQ
    bound = (1.02 * B_HEAD_DIM ** 0.5 * LOG2E * jnp.max(jnp.abs(gq)) * jnp.max(jnp.abs(gk))).reshape(1, 1)
    topk = min(TOPK_MAX, s // 4)
    gw = B_GROUP * TQ
    blk4 = lambda a: pl.BlockSpec((1,) + a.shape[1:], lambda b, i: (b * nq + i, 0, 0, 0))
    ik = pl.BlockSpec((1, s, LANES), lambda b, i: (b, 0, 0))
    tri = jnp.tril(jnp.ones((CB, CB), BF16))
    return pl.pallas_call(
        functools.partial(_dsa_kernel, topk),
        grid=(bsz, nq),
        in_specs=[blk4(q), pl.BlockSpec((TQ, BLK), lambda b, i: (b * nq + i, COL_BZ)), blk4(iq),
                  pl.BlockSpec((1, IDX_HEADS, TQ), lambda b, i: (b, 0, i)),
                  pl.BlockSpec((1, B_KV_HEADS, s, B_HEAD_DIM), lambda b, i: (b, 0, 0, 0)),
                  pl.BlockSpec((1, B_KV_HEADS, s // CB, B_HEAD_DIM, CB), lambda b, i: (b, 0, 0, 0, 0)),
                  ik, ik,
                  pl.BlockSpec((CB, CB), lambda b, i: (0, 0)),
                  pl.BlockSpec(memory_space=pltpu.SMEM)],
        out_specs=pl.BlockSpec((TQ, B_WIDTH), lambda b, i: (b * nq + i, 0)),
        out_shape=jax.ShapeDtypeStruct((bsz * s, B_WIDTH), BF16),
        scratch_shapes=[
            pltpu.VMEM((s, TQ), F32),
            pltpu.VMEM((s, TQ), F32),
            pltpu.VMEM((s, gw), F32),
            pltpu.VMEM((B_HEAD_DIM, gw), F32),
        ],
        compiler_params=_cparams(("parallel", "arbitrary")),
        name="dsa",
    )(q, proj, iq, iwt, k, vt, iklo, ikhi, tri, bound)


def _merge_kernel(ta_ref, tb_ref, tm_ref, ga_ref, gb_ref, gm_ref, x_ref,
                  wa_hbm, wb_hbm, wm_hbm, wo_hbm, o_ref,
                  wa_ref, wb_ref, wm_ref, wo_ref, stage_ref, sem):
    @pl.when(pl.program_id(0) == 0)
    def _():
        chunks = [(src, dst, r) for src, dst in ((wa_hbm, wa_ref), (wb_hbm, wb_ref), (wm_hbm, wm_ref),
                                                 (wo_hbm, wo_ref))
                  for r in range(0, src.shape[0], MERGE_STAGE_ROWS)]

        slots = stage_ref.shape[0]

        def copy(j):
            src, _, r = chunks[j]
            return pltpu.make_async_copy(src.at[pl.ds(r, MERGE_STAGE_ROWS), :], stage_ref.at[j % slots],
                                         sem.at[j % slots])

        for j in range(min(slots - 1, len(chunks))):
            copy(j).start()
        for j, (_, dst, r) in enumerate(chunks):
            if j + slots - 1 < len(chunks):
                copy(j + slots - 1).start()
            copy(j).wait()
            dst[pl.ds(r, MERGE_STAGE_ROWS), :] = stage_ref[j % slots].astype(BF16)

    def branch(t_ref, g_ref, w_ref):
        y = jnp.dot(t_ref[...], w_ref[...], preferred_element_type=F32)
        return jax.nn.sigmoid(g_ref[...].astype(F32)) * y

    merged = branch(ta_ref, ga_ref, wa_ref) + branch(tb_ref, gb_ref, wb_ref) + branch(tm_ref, gm_ref, wm_ref)
    o_ref[...] = x_ref[...] + jnp.dot(merged.astype(BF16), wo_ref[...], preferred_element_type=F32)


def _merge(ta, tb, tmem, proj, x2, wa, wb, wm, wo):
    m = x2.shape[0]
    tm = min(256, m)
    act = pl.BlockSpec((tm, BLK), lambda i: (i, 0))
    gate = lambda c: pl.BlockSpec((tm, D_MODEL), lambda i: (i, c // 2))
    wide = pl.BlockSpec((tm, D_MODEL), lambda i: (i, 0))
    hbm = pl.BlockSpec(memory_space=pl.ANY)
    return pl.pallas_call(
        _merge_kernel,
        grid=(m // tm,),
        in_specs=[act, act, act, gate(COL_GA), gate(COL_GB), gate(COL_GM), wide, hbm, hbm, hbm, hbm],
        out_specs=wide,
        out_shape=jax.ShapeDtypeStruct((m, D_MODEL), F32),
        scratch_shapes=[pltpu.VMEM(w.shape, BF16) for w in (wa, wb, wm, wo)]
        + [pltpu.VMEM((MERGE_STAGE_SLOTS, MERGE_STAGE_ROWS, D_MODEL), F32),
           pltpu.SemaphoreType.DMA((MERGE_STAGE_SLOTS,))],
        compiler_params=_cparams(("arbitrary",)),
        name="merge",
    )(ta, tb, tmem, proj, proj, proj, x2, wa, wb, wm, wo)


def _rope_freqs():
    inv = lambda half: ROPE_THETA ** (-np.arange(half, dtype=np.float32) / half)
    freq = np.zeros((1, LANES), np.float32)
    hb, hi = B_HEAD_DIM // 8, IDX_DIM // 8
    freq[0, :hb] = inv(hb)
    freq[0, hb:hb + hi] = inv(hi)
    return jnp.asarray(freq)


def _layer(x, mem, positions, norm_gain, w_in, gmlp_ln_gain, gmlp_ln_bias, spatial_w, spatial_b,
           w_branch_a, q_norm_gain, k_norm_gain, idx_k_ln_gain, idx_k_ln_bias, w_branch_b,
           mem_norm_gain, w_mem_kv, mem_q_norm_gain, mem_k_norm_gain, w_branch_m, w_out):
    bsz, s, _ = x.shape
    m = bsz * s
    row = lambda a: a.reshape(1, -1).astype(F32)
    pad_lanes = lambda a: jnp.pad(a.reshape(1, -1).astype(F32), ((0, 0), (0, LANES - a.shape[-1])))
    x2 = x.reshape(m, D_MODEL)

    w_t = w_in.T
    h, q, iq, k, vt, iklo, ikhi, iwt = _kvprep(
        x, row(norm_gain), w_t, positions.reshape(bsz, s, 1).astype(jnp.int32), _rope_freqs(),
        row(k_norm_gain[_HEAD_PERM]), row(q_norm_gain[_HEAD_PERM]),
        pad_lanes(idx_k_ln_gain), pad_lanes(idx_k_ln_bias))
    proj = _inproj(h, w_t)

    t_a = _gmlp(proj, row(gmlp_ln_gain), row(gmlp_ln_bias), spatial_w.astype(F32), spatial_b.T.astype(F32))
    km, vm = _memkv(mem, row(mem_norm_gain), w_mem_kv, row(mem_k_norm_gain))
    t_m = _memattn(proj, km, vm, row(mem_q_norm_gain), s)
    t_b = _dsa(proj, q, iq, iwt, k, vt, iklo, ikhi, row(q_norm_gain), row(k_norm_gain), bsz, s)

    out = _merge(t_a, t_b, t_m, proj, x2, w_branch_a, w_branch_b, w_branch_m, w_out)
    return out.reshape(bsz, s, D_MODEL)


def kernel(x, mem, positions, norm_gain, w_in, gmlp_ln_gain, gmlp_ln_bias, spatial_w, spatial_b, w_branch_a, q_norm_gain, k_norm_gain, idx_k_ln_gain, idx_k_ln_bias, w_branch_b, mem_norm_gain, w_mem_kv, mem_q_norm_gain, mem_k_norm_gain, w_branch_m, w_out):
    for l in range(norm_gain.shape[0]):
        x = _layer(x, mem, positions, norm_gain[l], w_in[l], gmlp_ln_gain[l], gmlp_ln_bias[l],
                   spatial_w[l], spatial_b[l], w_branch_a[l], q_norm_gain[l], k_norm_gain[l],
                   idx_k_ln_gain[l], idx_k_ln_bias[l], w_branch_b[l], mem_norm_gain[l], w_mem_kv[l],
                   mem_q_norm_gain[l], mem_k_norm_gain[l], w_branch_m[l], w_out[l])
    return x
```
